```python
import math
import jax, jax.numpy as jnp
from jax import lax
import numpy as np

D_MODEL = 1024
BATCH = 4
SEQ = 4096
DEPTH = 4

N_META = 16
RWKV_HEAD_DIM = 64
RWKV_WIDTH = D_MODEL // 4
RWKV_HEADS = RWKV_WIDTH // RWKV_HEAD_DIM
DECAY_LORA = 32
AAA_LORA = 32
GATE_LORA = 64
DIFF_QK_DIM = 64
DIFF_V_DIM = 2 * DIFF_QK_DIM
DIFF_WIDTH = D_MODEL // 2
DIFF_HEADS = DIFF_WIDTH // DIFF_V_DIM
CONV_WIDTH = D_MODEL - RWKV_WIDTH - DIFF_WIDTH
CONV_K = 3
ROPE_THETA = 500000.0
ROPE_DIM = DIFF_QK_DIM // 4
Q_BLOCK = 128
D_FF = 2816
N_EXPERTS = 8
TOP_K = 2
MOE_BLOCK = 256
N_DENSE = (DEPTH + 1) // 2
N_MOE = DEPTH // 2
NORM_EPS = 1e-6
RWKV_GN_EPS = 64e-5
SUBLN_EPS = 1e-5
NEG_INF = -1e30

RWKV_COLS = 3 * RWKV_WIDTH + DECAY_LORA + AAA_LORA + GATE_LORA
DIFF_COLS = 2 * DIFF_HEADS * 2 * DIFF_QK_DIM + DIFF_HEADS * DIFF_V_DIM
CONV_COLS = 3 * CONV_WIDTH
IN_COLS = RWKV_COLS + DIFF_COLS + CONV_COLS

kernel_name = "hymba_rwkv7_diffattn_shortconv_moe"


def rms_norm(t, g, eps):
    tf = t.astype(jnp.float32)
    tf = tf * lax.rsqrt(jnp.mean(tf * tf, axis=-1, keepdims=True) + eps)
    return (tf * g.astype(jnp.float32)).astype(t.dtype)


def swiglu(t, w_gate, w_up, w_down):
    return (jax.nn.silu(t @ w_gate) * (t @ w_up)) @ w_down


def rope_tables(length):
    inv_freq = ROPE_THETA ** (-jnp.arange(0, ROPE_DIM, 2, dtype=jnp.float32) / ROPE_DIM)
    ang = jnp.arange(length, dtype=jnp.float32)[:, None] * inv_freq[None, :]
    return jnp.cos(ang), jnp.sin(ang)


def apply_partial_rope(t, cos, sin):
    half = ROPE_DIM // 2
    c = cos[None, :, None, None, :].astype(t.dtype)
    s = sin[None, :, None, None, :].astype(t.dtype)
    t1, t2 = t[..., :half], t[..., half:ROPE_DIM]
    return jnp.concatenate([t1 * c - t2 * s, t2 * c + t1 * s, t[..., ROPE_DIM:]], axis=-1)


def rwkv7_time_mix(hx, mu, w0, w_decay_up, a0, w_a_up, w_g_up, k_k, k_a, r_k, gn_g, gn_b):
    b, l, _ = hx.shape
    H, N = RWKV_HEADS, RWKV_HEAD_DIM
    prev = jnp.pad(hx, ((0, 0), (1, 0), (0, 0)))[:, :-1]
    hx = hx + (prev - hx) * mu
    s1 = RWKV_WIDTH
    r, k, v, hw, ha, hg = jnp.split(
        hx, [s1, 2 * s1, 3 * s1, 3 * s1 + DECAY_LORA, 3 * s1 + DECAY_LORA + AAA_LORA], axis=-1)
    w = -jax.nn.softplus(-(w0 + jnp.tanh(hw) @ w_decay_up)) - 0.5
    decay = jnp.exp(-jnp.exp(w.astype(jnp.float32)))
    a = jax.nn.sigmoid(a0 + ha @ w_a_up)
    g = jax.nn.sigmoid(hg) @ w_g_up
    heads = lambda t: t.reshape(b, l, H, N)
    kk = heads(k * k_k).astype(jnp.float32)
    kk = kk / jnp.maximum(jnp.sqrt(jnp.sum(kk * kk, axis=-1, keepdims=True)), 1e-12)
    k = k * (1.0 + (a - 1.0) * k_a)
    r_h, k_h, v_h = heads(r), heads(k), heads(v)
    a_h = heads(a).astype(jnp.float32)
    tm = lambda t: jnp.swapaxes(t.astype(jnp.float32), 0, 1)
    xs = (tm(r_h), tm(heads(decay)), tm(k_h), tm(v_h), tm(-kk), tm(kk * a_h))

    def step(S, inp):
        r_t, w_t, k_t, v_t, a_t, b_t = inp
        sa = jnp.einsum('bhvk,bhk->bhv', S, a_t)
        S = S * w_t[:, :, None, :] + sa[..., None] * b_t[:, :, None, :] + v_t[..., None] * k_t[:, :, None, :]
        return S, jnp.einsum('bhvk,bhk->bhv', S, r_t)

    S0 = jnp.zeros((b, H, N, N), jnp.float32)
    _, y = lax.scan(step, S0, xs)
    y = jnp.swapaxes(y, 0, 1)
    mean = jnp.mean(y, axis=-1, keepdims=True)
    var = jnp.mean(jnp.square(y - mean), axis=-1, keepdims=True)
    y = ((y - mean) * lax.rsqrt(var + RWKV_GN_EPS)).reshape(b, l, RWKV_WIDTH)
    y = (y * gn_g.astype(jnp.float32) + gn_b.astype(jnp.float32)).astype(hx.dtype)
    bonus = jnp.sum(r_h * k_h * r_k, axis=-1, keepdims=True) * v_h
    return (y + bonus.reshape(b, l, RWKV_WIDTH)) * g


def diff_attention(hx, cos, sin, q_norm_g, k_norm_g, lq1, lk1, lq2, lk2, subln_g, lam_init):
    b, l, _ = hx.shape
    H, dk, dv = DIFF_HEADS, DIFF_QK_DIM, DIFF_V_DIM
    qw = H * 2 * dk
    q, k, v = jnp.split(hx, [qw, 2 * qw], axis=-1)
    q = q.reshape(b, l, H, 2, dk)
    k = k.reshape(b, l, H, 2, dk)
    v = v.reshape(b, l, H, dv)
    q = apply_partial_rope(rms_norm(q, q_norm_g, NORM_EPS), cos, sin) * (dk ** -0.5)
    k = apply_partial_rope(rms_norm(k, k_norm_g, NORM_EPS), cos, sin)
    lam = (jnp.exp(jnp.sum(lq1.astype(jnp.float32) * lk1.astype(jnp.float32)))
           - jnp.exp(jnp.sum(lq2.astype(jnp.float32) * lk2.astype(jnp.float32))) + lam_init)
    pad = (-l) % Q_BLOCK
    lp = l + pad
    nb = lp // Q_BLOCK
    qp = jnp.pad(q, ((0, 0), (pad, 0), (0, 0), (0, 0), (0, 0)))
    kp = jnp.pad(k, ((0, 0), (pad, 0), (0, 0), (0, 0), (0, 0)))
    vp = jnp.pad(v, ((0, 0), (pad, 0), (0, 0), (0, 0)))
    qb = jnp.moveaxis(qp.reshape(b, nb, Q_BLOCK, H, 2, dk), 1, 0)
    kpos = jnp.arange(lp)

    def one_block(args):
        qi, i = args
        s = jnp.einsum('bqhcd,bkhcd->bhcqk', qi, kp).astype(jnp.float32)
        qpos = i * Q_BLOCK + jnp.arange(Q_BLOCK)
        mask = (kpos[None, :] <= qpos[:, None]) & (kpos[None, :] >= pad)
        p = jax.nn.softmax(jnp.where(mask, s, NEG_INF), axis=-1)
        pd = p[:, :, 0] - lam * p[:, :, 1]
        return jnp.einsum('bhqk,bkhe->bqhe', pd.astype(vp.dtype), vp)

    o = lax.map(one_block, (qb, jnp.arange(nb)))
    o = jnp.moveaxis(o, 0, 1).reshape(b, lp, H, dv)[:, pad:]
    o = rms_norm(o, subln_g, SUBLN_EPS) * (1.0 - lam_init)
    return o.reshape(b, l, DIFF_WIDTH)


def short_conv(hx, conv_w):
    l = hx.shape[1]
    bg, cg, u = jnp.split(hx, [CONV_WIDTH, 2 * CONV_WIDTH], axis=-1)
    z = jnp.pad(cg * u, ((0, 0), (CONV_K - 1, 0), (0, 0)))
    y = sum(conv_w[j] * z[:, j:j + l] for j in range(CONV_K))
    return bg * y


def moe_swiglu(xt, router_w, w_gate, w_up, w_down):
    T, D = xt.shape
    logits = (xt @ router_w).astype(jnp.float32)
    top_logit, top_idx = lax.top_k(logits, TOP_K)
    gates = jax.nn.softmax(top_logit, axis=-1).astype(xt.dtype)
    TK = T * TOP_K
    flat_e = top_idx.reshape(TK).astype(jnp.int32)
    flat_tok = jnp.repeat(jnp.arange(T, dtype=jnp.int32), TOP_K)
    flat_g = gates.reshape(TK)
    order = jnp.argsort(flat_e * TK + jnp.arange(TK, dtype=jnp.int32))
    se, stok, sg = flat_e[order], flat_tok[order], flat_g[order]
    counts = jnp.bincount(flat_e, length=N_EXPERTS)
    padded = (counts + MOE_BLOCK - 1) // MOE_BLOCK * MOE_BLOCK
    start = jnp.cumsum(counts) - counts
    pend = jnp.cumsum(padded)
    pstart = pend - padded
    dest = pstart[se] + jnp.arange(TK, dtype=jnp.int32) - start[se]
    NB = -(-TK // MOE_BLOCK) + N_EXPERTS
    P = NB * MOE_BLOCK
    row_tok = jnp.full((P,), T, jnp.int32).at[dest].set(stok)
    row_gate = jnp.zeros((P,), xt.dtype).at[dest].set(sg)
    block_expert = jnp.minimum(
        jnp.searchsorted(pend, jnp.arange(NB, dtype=pend.dtype) * MOE_BLOCK, side='right'), N_EXPERTS - 1)
    x_rows = jnp.concatenate([xt, jnp.zeros((1, D), xt.dtype)], axis=0)[row_tok].reshape(NB, MOE_BLOCK, D)

    def expert_block(args):
        xb, e = args
        return swiglu(xb, w_gate[e], w_up[e], w_down[e])

    y_rows = lax.map(expert_block, (x_rows, block_expert)).reshape(P, D)
    return jax.ops.segment_sum(y_rows * row_gate[:, None], row_tok, num_segments=T + 1)[:T]


def setup_inputs(seed: int = 0) -> dict:
    key = jax.random.key(seed)
    keys = iter(jax.random.split(key, 40))
    f32 = jnp.float32

    def normal(shape, scale):
        return jax.random.normal(next(keys), shape, f32) * scale

    def gain(shape):
        return 1.0 + normal(shape, 0.02)

    def uniform(shape, lo, hi):
        return jax.random.uniform(next(keys), shape, f32, lo, hi)

    return {
        "x": normal((BATCH, SEQ, D_MODEL), 1.0),
        "meta_tokens": normal((N_META, D_MODEL), 1.0),
        "mix_norm_g": gain((DEPTH, D_MODEL)),
        "w_in": normal((DEPTH, D_MODEL, IN_COLS), D_MODEL ** -0.5),
        "tm_mu": uniform((DEPTH, RWKV_COLS), 0.0, 1.0),
        "tm_w0": uniform((DEPTH, RWKV_WIDTH), -6.0, 1.0),
        "tm_w_decay_up": normal((DEPTH, DECAY_LORA, RWKV_WIDTH), 0.1),
        "tm_a0": normal((DEPTH, RWKV_WIDTH), 0.1),
        "tm_w_a_up": normal((DEPTH, AAA_LORA, RWKV_WIDTH), 0.1),
        "tm_w_g_up": normal((DEPTH, GATE_LORA, RWKV_WIDTH), GATE_LORA ** -0.5),
        "tm_k_k": 0.85 + normal((DEPTH, RWKV_WIDTH), 0.02),
        "tm_k_a": gain((DEPTH, RWKV_WIDTH)),
        "tm_r_k": normal((DEPTH, RWKV_HEADS, RWKV_HEAD_DIM), 0.1),
        "tm_gn_g": gain((DEPTH, RWKV_WIDTH)),
        "tm_gn_b": normal((DEPTH, RWKV_WIDTH), 0.02),
        "da_q_norm_g": gain((DEPTH, DIFF_QK_DIM)),
        "da_k_norm_g": gain((DEPTH, DIFF_QK_DIM)),
        "da_lambda_q1": normal((DEPTH, DIFF_QK_DIM), 0.1),
        "da_lambda_k1": normal((DEPTH, DIFF_QK_DIM), 0.1),
        "da_lambda_q2": normal((DEPTH, DIFF_QK_DIM), 0.1),
        "da_lambda_k2": normal((DEPTH, DIFF_QK_DIM), 0.1),
        "da_subln_g": gain((DEPTH, DIFF_V_DIM)),
        "sc_conv_w": normal((DEPTH, CONV_K, CONV_WIDTH), CONV_K ** -0.5),
        "w_out": normal((DEPTH, D_MODEL, D_MODEL), D_MODEL ** -0.5),
        "ffn_norm_g": gain((DEPTH, D_MODEL)),
        "ffn_w_gate": normal((N_DENSE, D_MODEL, D_FF), D_MODEL ** -0.5),
        "ffn_w_up": normal((N_DENSE, D_MODEL, D_FF), D_MODEL ** -0.5),
        "ffn_w_down": normal((N_DENSE, D_FF, D_MODEL), D_FF ** -0.5),
        "router_w": normal((N_MOE, D_MODEL, N_EXPERTS), D_MODEL ** -0.5),
        "moe_w_gate": normal((N_MOE, N_EXPERTS, D_MODEL, D_FF), D_MODEL ** -0.5),
        "moe_w_up": normal((N_MOE, N_EXPERTS, D_MODEL, D_FF), D_MODEL ** -0.5),
        "moe_w_down": normal((N_MOE, N_EXPERTS, D_FF, D_MODEL), D_FF ** -0.5),
    }


def reference(x, meta_tokens, mix_norm_g, w_in, tm_mu, tm_w0, tm_w_decay_up, tm_a0, tm_w_a_up,
              tm_w_g_up, tm_k_k, tm_k_a, tm_r_k, tm_gn_g, tm_gn_b, da_q_norm_g, da_k_norm_g,
              da_lambda_q1, da_lambda_k1, da_lambda_q2, da_lambda_k2, da_subln_g, sc_conv_w,
              w_out, ffn_norm_g, ffn_w_gate, ffn_w_up, ffn_w_down, router_w, moe_w_gate,
              moe_w_up, moe_w_down):
    b = x.shape[0]
    meta = jnp.broadcast_to(meta_tokens.astype(x.dtype)[None], (b, N_META, D_MODEL))
    h = jnp.concatenate([meta, x], axis=1)
    l = h.shape[1]
    cos, sin = rope_tables(l)
    for i in range(DEPTH):
        lam_init = 0.8 - 0.6 * math.exp(-0.3 * i)
        proj = rms_norm(h, mix_norm_g[i], NORM_EPS) @ w_in[i]
        p_a = proj[..., :RWKV_COLS]
        p_b = proj[..., RWKV_COLS:RWKV_COLS + DIFF_COLS]
        p_c = proj[..., RWKV_COLS + DIFF_COLS:]
        o_a = rwkv7_time_mix(p_a, tm_mu[i], tm_w0[i], tm_w_decay_up[i], tm_a0[i], tm_w_a_up[i],
                             tm_w_g_up[i], tm_k_k[i], tm_k_a[i], tm_r_k[i], tm_gn_g[i], tm_gn_b[i])
        o_b = diff_attention(p_b, cos, sin, da_q_norm_g[i], da_k_norm_g[i], da_lambda_q1[i],
                             da_lambda_k1[i], da_lambda_q2[i], da_lambda_k2[i], da_subln_g[i], lam_init)
        o_c = short_conv(p_c, sc_conv_w[i])
        h = h + jnp.concatenate([o_a, o_b, o_c], axis=-1) @ w_out[i]
        hn = rms_norm(h, ffn_norm_g[i], NORM_EPS)
        j = i // 2
        if i % 2 == 0:
            f = swiglu(hn, ffn_w_gate[j], ffn_w_up[j], ffn_w_down[j])
        else:
            f = moe_swiglu(hn.reshape(b * l, D_MODEL), router_w[j], moe_w_gate[j], moe_w_up[j],
                           moe_w_down[j]).reshape(b, l, D_MODEL)
        h = h + f
    return h[:, N_META:]
```

```python
import functools
import math

import jax
import jax.numpy as jnp
from jax import lax
from jax.experimental import pallas as pl
from jax.experimental.pallas import tpu as pltpu

N_META = 16
RWKV_HEAD_DIM = 64
RWKV_HEADS = 4
RWKV_WIDTH = RWKV_HEADS * RWKV_HEAD_DIM
DECAY_LORA = 32
AAA_LORA = 32
GATE_LORA = 64
LORA_COLS = DECAY_LORA + AAA_LORA + GATE_LORA
DIFF_QK_DIM = 64
DIFF_V_DIM = 128
DIFF_HEADS = 4
DIFF_WIDTH = DIFF_HEADS * DIFF_V_DIM
CONV_WIDTH = 256
CONV_K = 3
ROPE_THETA = 500000.0
ROPE_DIM = DIFF_QK_DIM // 4
N_EXPERTS = 8
TOP_K = 2
MOE_BLOCK = 256
NORM_EPS = 1e-6
RWKV_GN_EPS = 64e-5
SUBLN_EPS = 1e-5
NEG_INF = -1e30

LANES = 128
SUBLANES = 8
MXU_DIM = 256
VMEM_LIMIT_BYTES = 56 * 1024 * 1024

HEAD_PAD = LANES
RWKV_PAD = RWKV_HEADS * HEAD_PAD
PA_COLS = 3 * RWKV_PAD + LORA_COLS
PB_COLS = 3 * DIFF_WIDTH
PC_COLS = 3 * CONV_WIDTH
CHUNK = 64

F32 = jnp.float32
BF16 = jnp.bfloat16
HIGHEST = lax.Precision.HIGHEST


def _dot(a, b):
    return jnp.dot(a.astype(BF16), b.astype(BF16), preferred_element_type=F32)


def _dot_nt(a, b):
    return lax.dot_general(a.astype(BF16), b.astype(BF16), (((1,), (1,)), ((), ())),
                           preferred_element_type=F32)


def _dot_f32(a, b):
    return jnp.dot(a, b, preferred_element_type=F32, precision=HIGHEST)


def _row_tile(lp):
    for t in (384, 256, 128):
        if lp % t == 0:
            return t
    raise ValueError(f"padded length {lp} is not a multiple of {LANES}")


def _params(sem):
    return pltpu.CompilerParams(dimension_semantics=sem, vmem_limit_bytes=VMEM_LIMIT_BYTES)


def _inproj_kernel(h_ref, g_ref, w_ref, rc_ref, rs1_ref, rs2_ref, qg_ref, kg_ref,
                   pa_ref, qs_ref, kh_ref, vv_ref, pc_ref):
    tm = h_ref.shape[0]
    x = h_ref[...]
    ms = jnp.mean(x * x, axis=-1, keepdims=True)
    xn = (x * lax.rsqrt(ms + NORM_EPS) * g_ref[...]).astype(BF16)
    pa_ref[...] = jnp.dot(xn, w_ref[:, :PA_COLS], preferred_element_type=F32)
    pc_ref[...] = jnp.dot(xn, w_ref[:, PA_COLS + PB_COLS:], preferred_element_type=F32)
    pb = jnp.dot(xn, w_ref[:, PA_COLS:PA_COLS + PB_COLS], preferred_element_type=F32)

    lane = lax.broadcasted_iota(jnp.int32, (tm, LANES), 1)
    lo = lane < DIFF_QK_DIM
    rc, rs1, rs2 = rc_ref[...], rs1_ref[...], rs2_ref[...]

    def norm_rope(t, g):
        t2 = t * t
        ss_lo = jnp.sum(jnp.where(lo, t2, 0.0), axis=-1, keepdims=True)
        ss_hi = jnp.sum(jnp.where(lo, 0.0, t2), axis=-1, keepdims=True)
        inv = jnp.where(lo, lax.rsqrt(ss_lo / DIFF_QK_DIM + NORM_EPS),
                        lax.rsqrt(ss_hi / DIFF_QK_DIM + NORM_EPS))
        tn = t * inv * g
        half = ROPE_DIM // 2
        return tn * rc + pltpu.roll(tn, LANES - half, 1) * rs1 + pltpu.roll(tn, half, 1) * rs2

    for hd in range(DIFF_HEADS):
        cs = slice(hd * LANES, (hd + 1) * LANES)
        q = norm_rope(pb[:, cs], qg_ref[...]) * (DIFF_QK_DIM ** -0.5)
        qs_ref[0, :, cs] = jnp.where(lo, q, 0.0).astype(BF16)
        qs_ref[1, :, cs] = jnp.where(lo, 0.0, q).astype(BF16)
        kcs = slice(DIFF_WIDTH + hd * LANES, DIFF_WIDTH + (hd + 1) * LANES)
        kh_ref[:, cs] = norm_rope(pb[:, kcs], kg_ref[...]).astype(BF16)
    vv_ref[...] = pb[:, 2 * DIFF_WIDTH:].astype(BF16)


def _inproj(h, g, w, rc, rs1, rs2, qg, kg, *, tm, lp):
    r, d = h.shape
    nt_b = lp // tm
    row = lambda i: (i, 0)
    const = lambda i: (0, 0)
    rope = lambda i: (i % nt_b, 0)
    return pl.pallas_call(
        _inproj_kernel,
        grid=(r // tm,),
        in_specs=[
            pl.BlockSpec((tm, d), row),
            pl.BlockSpec((1, d), const),
            pl.BlockSpec(w.shape, const),
            pl.BlockSpec((tm, LANES), rope),
            pl.BlockSpec((tm, LANES), rope),
            pl.BlockSpec((tm, LANES), rope),
            pl.BlockSpec((1, LANES), const),
            pl.BlockSpec((1, LANES), const),
        ],
        out_specs=[
            pl.BlockSpec((tm, PA_COLS), row),
            pl.BlockSpec((2, tm, DIFF_WIDTH), lambda i: (0, i, 0)),
            pl.BlockSpec((tm, DIFF_WIDTH), row),
            pl.BlockSpec((tm, DIFF_WIDTH), row),
            pl.BlockSpec((tm, PC_COLS), row),
        ],
        out_shape=[
            jax.ShapeDtypeStruct((r, PA_COLS), F32),
            jax.ShapeDtypeStruct((2, r, DIFF_WIDTH), BF16),
            jax.ShapeDtypeStruct((r, DIFF_WIDTH), BF16),
            jax.ShapeDtypeStruct((r, DIFF_WIDTH), BF16),
            jax.ShapeDtypeStruct((r, PC_COLS), F32),
        ],
        compiler_params=_params(("arbitrary",)),
        name="inproj",
    )(h, g, w, rc, rs1, rs2, qg, kg)


def _rwkv_kernel(pa_ref, halo_ref, mu_ref, vec_ref, wd_ref, wa_ref, wg_ref, oa_ref,
                 r_s, k_s, v_s, a_s, b_s, wl_s, g_s, bonus_s, y_s, hstate_s):
    tb = pa_ref.shape[0]
    i = pl.program_id(1)

    @pl.when(i == 0)
    def _():
        hstate_s[...] = jnp.zeros_like(hstate_s)

    x = pa_ref[...]
    prev_row = jnp.where(i == 0, 0.0, halo_ref[SUBLANES - 1:SUBLANES, :])
    row = lax.broadcasted_iota(jnp.int32, (tb, 1), 0)
    prev = jnp.where(row == 0, prev_row, pltpu.roll(x, 1, 0))
    hx = x + (prev - x) * mu_ref[...]
    r = hx[:, 0:RWKV_PAD]
    k = hx[:, RWKV_PAD:2 * RWKV_PAD]
    v = hx[:, 2 * RWKV_PAD:3 * RWKV_PAD]
    lora = hx[:, 3 * RWKV_PAD:]
    w0, a0, k_k, k_a = vec_ref[0:1, :], vec_ref[1:2, :], vec_ref[2:3, :], vec_ref[3:4, :]
    r_k, gn_g, gn_b = vec_ref[4:5, :], vec_ref[5:6, :], vec_ref[6:7, :]

    w = -jax.nn.softplus(-(w0 + _dot_f32(jnp.tanh(lora), wd_ref[...]))) - 0.5
    wl_s[...] = -jnp.exp(w)
    alr = jax.nn.sigmoid(a0 + _dot_f32(lora, wa_ref[...]))
    g_s[...] = _dot_f32(jax.nn.sigmoid(lora), wg_ref[...])
    kk = k * k_k
    k2 = k * (1.0 + (alr - 1.0) * k_a)
    rkr = r * k2 * r_k
    for hd in range(RWKV_HEADS):
        cs = slice(hd * HEAD_PAD, (hd + 1) * HEAD_PAD)
        kk_h = kk[:, cs]
        nrm = jnp.sqrt(jnp.sum(kk_h * kk_h, axis=-1, keepdims=True))
        kk_h = kk_h / jnp.maximum(nrm, 1e-12)
        a_s[:, cs] = -kk_h
        b_s[:, cs] = kk_h * alr[:, cs]
        bonus_s[:, cs] = jnp.sum(rkr[:, cs], axis=-1, keepdims=True) * v[:, cs]
    r_s[...] = r
    k_s[...] = k2
    v_s[...] = v

    ri = lax.broadcasted_iota(jnp.int32, (CHUNK, CHUNK), 0)
    ci = lax.broadcasted_iota(jnp.int32, (CHUNK, CHUNK), 1)
    low_incl = ci <= ri
    low_strict = ci < ri
    tril_f = jnp.where(low_incl, 1.0, 0.0).astype(F32)
    eye_k = (lax.broadcasted_iota(jnp.int32, (HEAD_PAD, HEAD_PAD), 0)
             == lax.broadcasted_iota(jnp.int32, (HEAD_PAD, HEAD_PAD), 1))
    n_sq = int(math.log2(CHUNK))

    def chunk_body(c, carry):
        rows = pl.ds(pl.multiple_of(c * CHUNK, CHUNK), CHUNK)
        for hd in range(RWKV_HEADS):
            cs = slice(hd * HEAD_PAD, (hd + 1) * HEAD_PAD)
            wl = wl_s[rows, cs]
            cum = _dot_f32(tril_f, wl)
            last = cum[CHUNK - 1:CHUNK, :]
            e_neg = jnp.exp(-cum)
            e_last = jnp.exp(last - cum)
            rr, kc, vc, ac, bc = r_s[rows, cs], k_s[rows, cs], v_s[rows, cs], a_s[rows, cs], b_s[rows, cs]
            ah = ac * jnp.exp(cum - wl)
            rh = rr * jnp.exp(cum)
            ar = jnp.concatenate([ah, rh], axis=0)
            gb = _dot_nt(ar, bc * e_neg)
            gk = _dot_nt(ar, kc * e_neg)
            a_ab = jnp.where(low_strict, gb[:CHUNK], 0.0)
            r_b = jnp.where(low_incl, gb[CHUNK:], 0.0)
            a_ak = jnp.where(low_strict, gk[:CHUNK], 0.0)
            r_kk = jnp.where(low_incl, gk[CHUNK:], 0.0)
            xx = jnp.concatenate([_dot(a_ak, vc), ah], axis=1)
            pp = a_ab
            for j in range(n_sq):
                xx = xx + _dot(pp, xx)
                if j + 1 < n_sq:
                    pp = _dot(pp, pp)
            w1 = xx[:, :HEAD_PAD]
            at = xx[:, HEAD_PAD:]
            rbx = _dot(r_b, xx)
            q = rh + rbx[:, HEAD_PAD:]
            y0 = rbx[:, :HEAD_PAD] + _dot(r_kk, vc)
            btx = _dot((bc * e_last).T, xx)
            m = jnp.where(eye_k, jnp.exp(last), 0.0) + btx[:, HEAD_PAD:]
            n0 = btx[:, :HEAD_PAD] + _dot((kc * e_last).T, vc)
            hs = hstate_s[hd]
            y_s[rows, cs] = _dot(q, hs) + y0
            hstate_s[hd] = _dot(m, hs) + n0
        return carry

    lax.fori_loop(0, tb // CHUNK, chunk_body, 0)

    lane = lax.broadcasted_iota(jnp.int32, (tb, HEAD_PAD), 1)
    real = lane < RWKV_HEAD_DIM
    for hd in range(RWKV_HEADS):
        cs = slice(hd * HEAD_PAD, (hd + 1) * HEAD_PAD)
        y = y_s[:, cs]
        mean = jnp.sum(y, axis=-1, keepdims=True) / RWKV_HEAD_DIM
        dlt = jnp.where(real, y - mean, 0.0)
        var = jnp.sum(dlt * dlt, axis=-1, keepdims=True) / RWKV_HEAD_DIM
        yn = dlt * lax.rsqrt(var + RWKV_GN_EPS) * gn_g[:, cs] + gn_b[:, cs]
        oa_ref[:, cs] = ((yn + bonus_s[:, cs]) * g_s[:, cs]).astype(BF16)


def _rwkv(pa, mu, vecs, wd, wa, wg, *, tb):
    b, lp, _ = pa.shape
    tile = lambda bi, i: (bi, i, 0)
    const = lambda bi, i: (0, 0)
    halo = lambda bi, i: (bi, jnp.maximum(i * (tb // SUBLANES) - 1, 0), 0)
    big = pltpu.VMEM((tb, RWKV_PAD), F32)
    return pl.pallas_call(
        _rwkv_kernel,
        grid=(b, lp // tb),
        in_specs=[
            pl.BlockSpec((None, tb, PA_COLS), tile),
            pl.BlockSpec((None, SUBLANES, PA_COLS), halo),
            pl.BlockSpec((1, PA_COLS), const),
            pl.BlockSpec((SUBLANES, RWKV_PAD), const),
            pl.BlockSpec((LORA_COLS, RWKV_PAD), const),
            pl.BlockSpec((LORA_COLS, RWKV_PAD), const),
            pl.BlockSpec((LORA_COLS, RWKV_PAD), const),
        ],
        out_specs=pl.BlockSpec((None, tb, RWKV_PAD), tile),
        out_shape=jax.ShapeDtypeStruct((b, lp, RWKV_PAD), BF16),
        scratch_shapes=[big] * 9 + [pltpu.VMEM((RWKV_HEADS, HEAD_PAD, HEAD_PAD), F32)],
        compiler_params=_params(("arbitrary", "arbitrary")),
        name="rwkv7",
    )(pa, pa, mu, vecs, wd, wa, wg)


def _attn_kernel(q_ref, k_ref, v_ref, lamv_ref, sg_ref, o_ref, m_s, l_s, acc_s, *, lam_init):
    tq = o_ref.shape[0]
    tk = k_ref.shape[0]
    qi = pl.program_id(2)
    ki = pl.program_id(3)

    @pl.when(ki == 0)
    def _():
        m_s[...] = jnp.full_like(m_s, NEG_INF)
        l_s[...] = jnp.zeros_like(l_s)
        acc_s[...] = jnp.zeros_like(acc_s)

    def step(masked):
        q = q_ref[...].reshape(2 * tq, LANES)
        s = lax.dot_general(q, k_ref[...], (((1,), (1,)), ((), ())), preferred_element_type=F32)
        if masked:
            rr = lax.broadcasted_iota(jnp.int32, (2 * tq, tk), 0)
            cc = lax.broadcasted_iota(jnp.int32, (2 * tq, tk), 1)
            rr = jnp.where(rr >= tq, rr - tq, rr)
            s = jnp.where(cc <= rr, s, NEG_INF)
        m_prev = m_s[...]
        m_new = jnp.maximum(m_prev, jnp.max(s, axis=-1, keepdims=True))
        alpha = jnp.exp(m_prev - m_new)
        p = jnp.exp(s - m_new)
        l_s[...] = alpha * l_s[...] + jnp.sum(p, axis=-1, keepdims=True)
        acc_s[...] = alpha * acc_s[...] + jnp.dot(p.astype(BF16), v_ref[...],
                                                  preferred_element_type=F32)
        m_s[...] = m_new

    @pl.when(ki < qi)
    def _():
        step(False)

    @pl.when(ki == qi)
    def _():
        step(True)
        o = acc_s[...] / l_s[...]
        lv = lamv_ref[...]
        lam = (jnp.exp(jnp.sum(lv[0:1] * lv[1:2], axis=-1, keepdims=True))
               - jnp.exp(jnp.sum(lv[2:3] * lv[3:4], axis=-1, keepdims=True)) + lam_init)
        od = o[:tq] - lam * o[tq:]
        ms = jnp.mean(od * od, axis=-1, keepdims=True)
        o_ref[...] = (od * lax.rsqrt(ms + SUBLN_EPS) * sg_ref[...] * (1.0 - lam_init)).astype(BF16)


def _attention(qs, kh, vv, lamv, sg, *, tq, lam_init):
    _, b, lp, _ = qs.shape
    nq = lp // tq
    kv_map = lambda bi, hd, qi, ki: (bi, jnp.minimum(ki, qi), hd)
    const = lambda bi, hd, qi, ki: (0, 0)
    return pl.pallas_call(
        functools.partial(_attn_kernel, lam_init=lam_init),
        grid=(b, DIFF_HEADS, nq, nq),
        in_specs=[
            pl.BlockSpec((2, None, tq, LANES), lambda bi, hd, qi, ki: (0, bi, qi, hd)),
            pl.BlockSpec((None, tq, LANES), kv_map),
            pl.BlockSpec((None, tq, LANES), kv_map),
            pl.BlockSpec((SUBLANES, LANES), const),
            pl.BlockSpec((1, LANES), const),
        ],
        out_specs=pl.BlockSpec((None, tq, LANES), lambda bi, hd, qi, ki: (bi, qi, hd)),
        out_shape=jax.ShapeDtypeStruct((b, lp, DIFF_WIDTH), BF16),
        scratch_shapes=[pltpu.VMEM((2 * tq, 1), F32), pltpu.VMEM((2 * tq, 1), F32),
                        pltpu.VMEM((2 * tq, LANES), F32)],
        compiler_params=_params(("arbitrary", "arbitrary", "arbitrary", "arbitrary")),
        name="diffattn",
    )(qs, kh, vv, lamv, sg)


def _outproj_kernel(h_ref, oa_ref, ob_ref, pc_ref, pch_ref, cw_ref, wa_ref, wb_ref, wc_ref,
                    o_ref, *, tiles_per_batch):
    tm = h_ref.shape[0]
    first = (pl.program_id(0) % tiles_per_batch) == 0
    pc = pc_ref[...]
    bg = pc[:, :CONV_WIDTH]
    z = pc[:, CONV_WIDTH:2 * CONV_WIDTH] * pc[:, 2 * CONV_WIDTH:]
    ph = pch_ref[...]
    zh = jnp.where(first, 0.0, ph[:, CONV_WIDTH:2 * CONV_WIDTH] * ph[:, 2 * CONV_WIDTH:])
    row = lax.broadcasted_iota(jnp.int32, (tm, 1), 0)
    z1 = jnp.where(row == 0, zh[SUBLANES - 1:SUBLANES], pltpu.roll(z, 1, 0))
    z2 = jnp.where(row == 0, zh[SUBLANES - 2:SUBLANES - 1],
                   jnp.where(row == 1, zh[SUBLANES - 1:SUBLANES], pltpu.roll(z, 2, 0)))
    cw = cw_ref[...]
    y = cw[0:1] * z2 + cw[1:2] * z1 + cw[2:3] * z
    oc = (bg * y).astype(BF16)
    o_ref[...] = (h_ref[...]
                  + jnp.dot(oa_ref[...], wa_ref[...], preferred_element_type=F32)
                  + jnp.dot(ob_ref[...], wb_ref[...], preferred_element_type=F32)
                  + jnp.dot(oc, wc_ref[...], preferred_element_type=F32))


def _outproj(h, oa, ob, pc, cw, wa, wb, wc, *, tm, lp):
    r, d = h.shape
    row = lambda i: (i, 0)
    const = lambda i: (0, 0)
    halo = lambda i: (jnp.maximum(i * (tm // SUBLANES) - 1, 0), 0)
    return pl.pallas_call(
        functools.partial(_outproj_kernel, tiles_per_batch=lp // tm),
        grid=(r // tm,),
        in_specs=[
            pl.BlockSpec((tm, d), row),
            pl.BlockSpec((tm, RWKV_PAD), row),
            pl.BlockSpec((tm, DIFF_WIDTH), row),
            pl.BlockSpec((tm, PC_COLS), row),
            pl.BlockSpec((SUBLANES, PC_COLS), halo),
            pl.BlockSpec((SUBLANES, CONV_WIDTH), const),
            pl.BlockSpec(wa.shape, const),
            pl.BlockSpec(wb.shape, const),
            pl.BlockSpec(wc.shape, const),
        ],
        out_specs=pl.BlockSpec((tm, d), row),
        out_shape=jax.ShapeDtypeStruct((r, d), F32),
        compiler_params=_params(("arbitrary",)),
        name="outproj",
    )(h, oa, ob, pc, pc, cw, wa, wb, wc)


def _swiglu_block(xb, wg_ref, wu_ref, wd_ref):
    d_ff = wg_ref.shape[1]
    acc = jnp.zeros((xb.shape[0], wd_ref.shape[1]), F32)
    for c in range(d_ff // MXU_DIM):
        cs = slice(c * MXU_DIM, (c + 1) * MXU_DIM)
        gt = jnp.dot(xb, wg_ref[:, cs], preferred_element_type=F32)
        up = jnp.dot(xb, wu_ref[:, cs], preferred_element_type=F32)
        act = (jax.nn.silu(gt) * up).astype(BF16)
        acc = acc + jnp.dot(act, wd_ref[cs, :], preferred_element_type=F32)
    return acc


def _rms(x, g):
    ms = jnp.mean(x * x, axis=-1, keepdims=True)
    return x * lax.rsqrt(ms + NORM_EPS) * g


def _ffn_kernel(h_ref, g_ref, wg_ref, wu_ref, wd_ref, o_ref):
    x = h_ref[...]
    xn = _rms(x, g_ref[...]).astype(BF16)
    o_ref[...] = x + _swiglu_block(xn, wg_ref, wu_ref, wd_ref)


def _dense_ffn(h, g, wg, wu, wd, *, tm):
    r, d = h.shape
    row = lambda i: (i, 0)
    const = lambda i: (0, 0)
    return pl.pallas_call(
        _ffn_kernel,
        grid=(r // tm,),
        in_specs=[
            pl.BlockSpec((tm, d), row),
            pl.BlockSpec((1, d), const),
            pl.BlockSpec(wg.shape, const),
            pl.BlockSpec(wu.shape, const),
            pl.BlockSpec(wd.shape, const),
        ],
        out_specs=pl.BlockSpec((tm, d), row),
        out_shape=jax.ShapeDtypeStruct((r, d), F32),
        compiler_params=_params(("arbitrary",)),
        name="dense_ffn",
    )(h, g, wg, wu, wd)


def _router_kernel(h_ref, g_ref, rw_ref, hn_ref, route_ref):
    tm = h_ref.shape[0]
    xn = _rms(h_ref[...], g_ref[...])
    hn_ref[...] = xn
    logits = _dot_f32(xn, rw_ref[...])
    lane = lax.broadcasted_iota(jnp.int32, (tm, LANES), 1)
    logits = jnp.where(lane < N_EXPERTS, logits, -jnp.inf)
    m1 = jnp.max(logits, axis=-1, keepdims=True)
    i1 = jnp.min(jnp.where(logits == m1, lane, LANES), axis=-1, keepdims=True)
    rest = jnp.where(lane == i1, -jnp.inf, logits)
    m2 = jnp.max(rest, axis=-1, keepdims=True)
    i2 = jnp.min(jnp.where(rest == m2, lane, LANES), axis=-1, keepdims=True)
    e2 = jnp.exp(m2 - m1)
    den = 1.0 + e2
    route_ref[...] = jnp.where(lane == 0, i1.astype(F32),
                     jnp.where(lane == 1, i2.astype(F32),
                     jnp.where(lane == 2, 1.0 / den,
                     jnp.where(lane == 3, e2 / den, 0.0))))


def _router(h, g, rw, *, tm):
    r, d = h.shape
    row = lambda i: (i, 0)
    const = lambda i: (0, 0)
    return pl.pallas_call(
        _router_kernel,
        grid=(r // tm,),
        in_specs=[pl.BlockSpec((tm, d), row), pl.BlockSpec((1, d), const),
                  pl.BlockSpec((d, LANES), const)],
        out_specs=[pl.BlockSpec((tm, d), row), pl.BlockSpec((tm, LANES), row)],
        out_shape=[jax.ShapeDtypeStruct((r, d), F32), jax.ShapeDtypeStruct((r, LANES), F32)],
        compiler_params=_params(("arbitrary",)),
        name="router",
    )(h, g, rw)


def _row_copy(src_hbm, dst_vmem, sem, src_row, dst_row):
    return pltpu.make_async_copy(src_hbm.at[pl.ds(src_row, 1), :],
                                 dst_vmem.at[pl.ds(dst_row, 1), :], sem)


def _moe_kernel(bexp_ref, rsrc_ref, hn_hbm, wg_ref, wu_ref, wd_ref, y_ref, xbuf, sem):
    del bexp_ref
    blk = xbuf.shape[0]
    base = pl.program_id(0) * blk

    def issue(j, c):
        _row_copy(hn_hbm, xbuf, sem, rsrc_ref[base + j], j).start()
        return c

    lax.fori_loop(0, blk, issue, 0)

    def drain(j, c):
        _row_copy(hn_hbm, xbuf, sem, 0, j).wait()
        return c

    lax.fori_loop(0, blk, drain, 0)
    y_ref[...] = _swiglu_block(xbuf[...].astype(BF16), wg_ref, wu_ref, wd_ref)


def _moe_experts(block_expert, row_src, hn, wg, wu, wd):
    nb = block_expert.shape[0]
    d = hn.shape[1]
    d_ff = wg.shape[2]
    wmap = lambda i, be, rs: (be[i], 0, 0)
    grid_spec = pltpu.PrefetchScalarGridSpec(
        num_scalar_prefetch=2,
        grid=(nb,),
        in_specs=[
            pl.BlockSpec(memory_space=pl.ANY),
            pl.BlockSpec((None, d, d_ff), wmap),
            pl.BlockSpec((None, d, d_ff), wmap),
            pl.BlockSpec((None, d_ff, d), wmap),
        ],
        out_specs=pl.BlockSpec((MOE_BLOCK, d), lambda i, be, rs: (i, 0)),
        scratch_shapes=[pltpu.VMEM((MOE_BLOCK, d), F32), pltpu.SemaphoreType.DMA(())],
    )
    return pl.pallas_call(
        _moe_kernel,
        grid_spec=grid_spec,
        out_shape=jax.ShapeDtypeStruct((nb * MOE_BLOCK, d), F32),
        compiler_params=_params(("arbitrary",)),
        name="moe_experts",
    )(block_expert, row_src, hn, wg, wu, wd)


def _combine_kernel(pos_ref, h_ref, route_ref, y_hbm, o_ref, ybuf, sem):
    tm = h_ref.shape[0]
    base = pl.program_id(0) * tm

    def issue(j, c):
        for kk in range(TOP_K):
            _row_copy(y_hbm, ybuf.at[kk], sem, pos_ref[TOP_K * (base + j) + kk], j).start()
        return c

    lax.fori_loop(0, tm, issue, 0)

    def drain(j, c):
        for kk in range(TOP_K):
            _row_copy(y_hbm, ybuf.at[kk], sem, 0, j).wait()
        return c

    lax.fori_loop(0, tm, drain, 0)
    route = route_ref[...]
    o_ref[...] = h_ref[...] + route[:, 2:3] * ybuf[0] + route[:, 3:4] * ybuf[1]


def _moe_combine(pos, h, route, y_rows, *, tm):
    r, d = h.shape
    row = lambda i, p: (i, 0)
    grid_spec = pltpu.PrefetchScalarGridSpec(
        num_scalar_prefetch=1,
        grid=(r // tm,),
        in_specs=[
            pl.BlockSpec((tm, d), row),
            pl.BlockSpec((tm, LANES), row),
            pl.BlockSpec(memory_space=pl.ANY),
        ],
        out_specs=pl.BlockSpec((tm, d), row),
        scratch_shapes=[pltpu.VMEM((TOP_K, tm, d), F32), pltpu.SemaphoreType.DMA(())],
    )
    return pl.pallas_call(
        _combine_kernel,
        grid_spec=grid_spec,
        out_shape=jax.ShapeDtypeStruct((r, d), F32),
        compiler_params=_params(("arbitrary",)),
        name="moe_combine",
    )(pos, h, route, y_rows)


def _routing_tables(route, b, l, lp):
    t = b * l
    tk = t * TOP_K
    rt = route.reshape(b, lp, LANES)[:, :l, :]
    flat_e = rt[:, :, :TOP_K].reshape(tk).astype(jnp.int32)
    tok_row = (jnp.arange(b, dtype=jnp.int32)[:, None] * lp
               + jnp.arange(l, dtype=jnp.int32)[None, :]).reshape(t)
    flat_row = jnp.repeat(tok_row, TOP_K)
    onehot = (flat_e[:, None] == jnp.arange(N_EXPERTS, dtype=jnp.int32)[None, :]).astype(jnp.int32)
    csum = jnp.cumsum(onehot, axis=0)
    rank = jnp.sum(csum * onehot, axis=1) - 1
    counts = csum[-1]
    padded = (counts + MOE_BLOCK - 1) // MOE_BLOCK * MOE_BLOCK
    pend = jnp.cumsum(padded)
    pstart = pend - padded
    dest = pstart[flat_e] + rank
    nb = -(-tk // MOE_BLOCK) + N_EXPERTS
    row_src = jnp.zeros((nb * MOE_BLOCK,), jnp.int32).at[dest].set(flat_row, unique_indices=True)
    block_expert = jnp.minimum(
        jnp.searchsorted(pend, jnp.arange(nb, dtype=pend.dtype) * MOE_BLOCK, side='right'),
        N_EXPERTS - 1).astype(jnp.int32)
    pos = jnp.zeros((b, lp, TOP_K), jnp.int32).at[:, :l, :].set(dest.reshape(b, l, TOP_K))
    return block_expert, row_src, pos.reshape(b * lp * TOP_K)


def _pad_heads(t, axis):
    axis = axis % t.ndim
    shp = t.shape
    t = t.reshape(shp[:axis] + (RWKV_HEADS, RWKV_HEAD_DIM) + shp[axis + 1:])
    pad = [(0, 0)] * t.ndim
    pad[axis + 1] = (0, HEAD_PAD - RWKV_HEAD_DIM)
    t = jnp.pad(t, pad)
    return t.reshape(shp[:axis] + (RWKV_PAD,) + shp[axis + 1:])


def _rope_tables(lp):
    half = ROPE_DIM // 2
    inv_freq = ROPE_THETA ** (-jnp.arange(0, ROPE_DIM, 2, dtype=F32) / ROPE_DIM)
    ang = jnp.arange(lp, dtype=F32)[:, None] * inv_freq[None, :]
    cos, sin = jnp.cos(ang), jnp.sin(ang)
    ones = jnp.ones((lp, DIFF_QK_DIM - ROPE_DIM), F32)
    zeros = jnp.zeros((lp, DIFF_QK_DIM - ROPE_DIM), F32)
    zh = jnp.zeros((lp, half), F32)
    rc = jnp.concatenate([cos, cos, ones], axis=1)
    rs1 = jnp.concatenate([-sin, zh, zeros], axis=1)
    rs2 = jnp.concatenate([zh, sin, zeros], axis=1)
    tile2 = lambda a: jnp.concatenate([a, a], axis=1)
    return tile2(rc), tile2(rs1), tile2(rs2)


def kernel(x, meta_tokens, mix_norm_g, w_in, tm_mu, tm_w0, tm_w_decay_up, tm_a0, tm_w_a_up, tm_w_g_up, tm_k_k, tm_k_a, tm_r_k, tm_gn_g, tm_gn_b, da_q_norm_g, da_k_norm_g, da_lambda_q1, da_lambda_k1, da_lambda_q2, da_lambda_k2, da_subln_g, sc_conv_w, w_out, ffn_norm_g, ffn_w_gate, ffn_w_up, ffn_w_down, router_w, moe_w_gate, moe_w_up, moe_w_down):
    b, seq, d = x.shape
    depth = w_in.shape[0]
    l = N_META + seq
    lp = -(-l // LANES) * LANES
    tm = _row_tile(lp)
    r = b * lp
    s1 = RWKV_WIDTH

    meta = jnp.broadcast_to(meta_tokens.astype(x.dtype)[None], (b, N_META, d))
    h = jnp.concatenate([meta, x, jnp.zeros((b, lp - l, d), x.dtype)], axis=1).reshape(r, d)
    rc, rs1, rs2 = _rope_tables(lp)

    for i in range(depth):
        lam_init = 0.8 - 0.6 * math.exp(-0.3 * i)
        wi = w_in[i]
        w_cat = jnp.concatenate(
            [_pad_heads(wi[:, 0:s1], 1), _pad_heads(wi[:, s1:2 * s1], 1),
             _pad_heads(wi[:, 2 * s1:3 * s1], 1), wi[:, 3 * s1:]], axis=1).astype(BF16)
        qg = jnp.tile(da_q_norm_g[i], 2)[None]
        kg = jnp.tile(da_k_norm_g[i], 2)[None]
        pa, qs, kh, vv, pc = _inproj(h, mix_norm_g[i][None], w_cat, rc, rs1, rs2, qg, kg,
                                     tm=tm, lp=lp)

        mu = tm_mu[i]
        mu_p = jnp.concatenate([_pad_heads(mu[0:s1], 0), _pad_heads(mu[s1:2 * s1], 0),
                                _pad_heads(mu[2 * s1:3 * s1], 0), mu[3 * s1:]])[None]
        vecs = jnp.stack([_pad_heads(tm_w0[i], 0), _pad_heads(tm_a0[i], 0),
                          _pad_heads(tm_k_k[i], 0), _pad_heads(tm_k_a[i], 0),
                          _pad_heads(tm_r_k[i].reshape(s1), 0), _pad_heads(tm_gn_g[i], 0),
                          _pad_heads(tm_gn_b[i], 0), jnp.zeros((RWKV_PAD,), F32)])
        zl = lambda n: jnp.zeros((n, RWKV_PAD), F32)
        wd_p = jnp.concatenate([_pad_heads(tm_w_decay_up[i], 1), zl(AAA_LORA + GATE_LORA)], axis=0)
        wa_p = jnp.concatenate([zl(DECAY_LORA), _pad_heads(tm_w_a_up[i], 1), zl(GATE_LORA)], axis=0)
        wg_p = jnp.concatenate([zl(DECAY_LORA + AAA_LORA), _pad_heads(tm_w_g_up[i], 1)], axis=0)
        oa = _rwkv(pa.reshape(b, lp, PA_COLS), mu_p, vecs, wd_p, wa_p, wg_p, tb=tm)

        lamv = jnp.zeros((SUBLANES, LANES), F32).at[0:4, :DIFF_QK_DIM].set(
            jnp.stack([da_lambda_q1[i], da_lambda_k1[i], da_lambda_q2[i], da_lambda_k2[i]]))
        ob = _attention(qs.reshape(2, b, lp, DIFF_WIDTH), kh.reshape(b, lp, DIFF_WIDTH),
                        vv.reshape(b, lp, DIFF_WIDTH), lamv, da_subln_g[i][None],
                        tq=tm, lam_init=lam_init)

        wo = w_out[i]
        cw = jnp.zeros((SUBLANES, CONV_WIDTH), F32).at[:CONV_K].set(sc_conv_w[i])
        h = _outproj(h, oa.reshape(r, RWKV_PAD), ob.reshape(r, DIFF_WIDTH), pc, cw,
                     _pad_heads(wo[:s1], 0).astype(BF16), wo[s1:s1 + DIFF_WIDTH].astype(BF16),
                     wo[s1 + DIFF_WIDTH:].astype(BF16), tm=tm, lp=lp)

        j = i // 2
        if i % 2 == 0:
            h = _dense_ffn(h, ffn_norm_g[i][None], ffn_w_gate[j].astype(BF16),
                           ffn_w_up[j].astype(BF16), ffn_w_down[j].astype(BF16), tm=tm)
        else:
            rw = jnp.zeros((d, LANES), F32).at[:, :N_EXPERTS].set(router_w[j])
            hn, route = _router(h, ffn_norm_g[i][None], rw, tm=tm)
            block_expert, row_src, pos = _routing_tables(route, b, l, lp)
            y_rows = _moe_experts(block_expert, row_src, hn, moe_w_gate[j].astype(BF16),
                                  moe_w_up[j].astype(BF16), moe_w_down[j].astype(BF16))
            h = _moe_combine(pos, h, route, y_rows, tm=tm)

    return h.reshape(b, lp, d)[:, N_META:l]
```

```python
import functools
import math

import jax
import jax.numpy as jnp
from jax import lax
from jax.experimental import pallas as pl
from jax.experimental.pallas import tpu as pltpu

N_META = 16
RWKV_HEAD_DIM = 64
RWKV_HEADS = 4
RWKV_WIDTH = RWKV_HEADS * RWKV_HEAD_DIM
DECAY_LORA = 32
AAA_LORA = 32
GATE_LORA = 64
LORA_COLS = DECAY_LORA + AAA_LORA + GATE_LORA
DIFF_QK_DIM = 64
DIFF_V_DIM = 128
DIFF_HEADS = 4
DIFF_WIDTH = DIFF_HEADS * DIFF_V_DIM
CONV_WIDTH = 256
CONV_K = 3
ROPE_THETA = 500000.0
ROPE_DIM = DIFF_QK_DIM // 4
N_EXPERTS = 8
TOP_K = 2
MOE_BLOCK = 256
NORM_EPS = 1e-6
RWKV_GN_EPS = 64e-5
SUBLN_EPS = 1e-5
NEG_INF = -1e30

LANES = 128
SUBLANES = 8
MXU_DIM = 256
VMEM_LIMIT_BYTES = 56 * 1024 * 1024

HEAD_PAD = LANES
RWKV_PAD = RWKV_HEADS * HEAD_PAD
PA_COLS = 3 * RWKV_PAD + LORA_COLS
PB_COLS = 3 * DIFF_WIDTH
PC_COLS = 3 * CONV_WIDTH
CHUNK = 64

F32 = jnp.float32
BF16 = jnp.bfloat16
HIGHEST = lax.Precision.HIGHEST


def _dot(a, b):
    return jnp.dot(a.astype(BF16), b.astype(BF16), preferred_element_type=F32)


def _dot_nt(a, b):
    return lax.dot_general(a.astype(BF16), b.astype(BF16), (((1,), (1,)), ((), ())),
                           preferred_element_type=F32)


def _dot_f32(a, b):
    return jnp.dot(a, b, preferred_element_type=F32, precision=HIGHEST)


def _row_tile(lp):
    for t in (384, 256, 128):
        if lp % t == 0:
            return t
    raise ValueError(f"padded length {lp} is not a multiple of {LANES}")


def _params(sem):
    return pltpu.CompilerParams(dimension_semantics=sem, vmem_limit_bytes=VMEM_LIMIT_BYTES)


def _inproj_kernel(h_ref, g_ref, w_ref, rc_ref, rs1_ref, rs2_ref, qg_ref, kg_ref,
                   pa_ref, qs_ref, kh_ref, vt_ref, pc_ref):
    tm = h_ref.shape[0]
    x = h_ref[...]
    ms = jnp.mean(x * x, axis=-1, keepdims=True)
    xn = (x * lax.rsqrt(ms + NORM_EPS) * g_ref[...]).astype(BF16)
    pa_ref[...] = jnp.dot(xn, w_ref[:, :PA_COLS], preferred_element_type=F32)
    pc_ref[...] = jnp.dot(xn, w_ref[:, PA_COLS + PB_COLS:], preferred_element_type=F32)
    pb = jnp.dot(xn, w_ref[:, PA_COLS:PA_COLS + PB_COLS], preferred_element_type=F32)

    lane = lax.broadcasted_iota(jnp.int32, (tm, LANES), 1)
    lo = lane < DIFF_QK_DIM
    rc, rs1, rs2 = rc_ref[...], rs1_ref[...], rs2_ref[...]

    def norm_rope(t, g):
        t2 = t * t
        ss_lo = jnp.sum(jnp.where(lo, t2, 0.0), axis=-1, keepdims=True)
        ss_hi = jnp.sum(jnp.where(lo, 0.0, t2), axis=-1, keepdims=True)
        inv = jnp.where(lo, lax.rsqrt(ss_lo / DIFF_QK_DIM + NORM_EPS),
                        lax.rsqrt(ss_hi / DIFF_QK_DIM + NORM_EPS))
        tn = t * inv * g
        half = ROPE_DIM // 2
        return tn * rc + pltpu.roll(tn, LANES - half, 1) * rs1 + pltpu.roll(tn, half, 1) * rs2

    for hd in range(DIFF_HEADS):
        cs = slice(hd * LANES, (hd + 1) * LANES)
        q = norm_rope(pb[:, cs], qg_ref[...]) * (DIFF_QK_DIM ** -0.5)
        qs_ref[0, :, cs] = jnp.where(lo, q, 0.0).astype(BF16)
        qs_ref[1, :, cs] = jnp.where(lo, 0.0, q).astype(BF16)
        kcs = slice(DIFF_WIDTH + hd * LANES, DIFF_WIDTH + (hd + 1) * LANES)
        kh_ref[:, cs] = norm_rope(pb[:, kcs], kg_ref[...]).astype(BF16)
        vcs = slice(2 * DIFF_WIDTH + hd * LANES, 2 * DIFF_WIDTH + (hd + 1) * LANES)
        vt_ref[0, hd] = pb[:, vcs].T.astype(BF16)


def _inproj(h, g, w, rc, rs1, rs2, qg, kg, *, tm, lp):
    r, d = h.shape
    nt_b = lp // tm
    row = lambda i: (i, 0)
    const = lambda i: (0, 0)
    rope = lambda i: (i % nt_b, 0)
    return pl.pallas_call(
        _inproj_kernel,
        grid=(r // tm,),
        in_specs=[
            pl.BlockSpec((tm, d), row),
            pl.BlockSpec((1, d), const),
            pl.BlockSpec(w.shape, const),
            pl.BlockSpec((tm, LANES), rope),
            pl.BlockSpec((tm, LANES), rope),
            pl.BlockSpec((tm, LANES), rope),
            pl.BlockSpec((1, LANES), const),
            pl.BlockSpec((1, LANES), const),
        ],
        out_specs=[
            pl.BlockSpec((tm, PA_COLS), row),
            pl.BlockSpec((2, tm, DIFF_WIDTH), lambda i: (0, i, 0)),
            pl.BlockSpec((tm, DIFF_WIDTH), row),
            pl.BlockSpec((1, DIFF_HEADS, LANES, tm), lambda i: (i, 0, 0, 0)),
            pl.BlockSpec((tm, PC_COLS), row),
        ],
        out_shape=[
            jax.ShapeDtypeStruct((r, PA_COLS), F32),
            jax.ShapeDtypeStruct((2, r, DIFF_WIDTH), BF16),
            jax.ShapeDtypeStruct((r, DIFF_WIDTH), BF16),
            jax.ShapeDtypeStruct((r // tm, DIFF_HEADS, LANES, tm), BF16),
            jax.ShapeDtypeStruct((r, PC_COLS), F32),
        ],
        compiler_params=_params(("arbitrary",)),
        name="inproj",
    )(h, g, w, rc, rs1, rs2, qg, kg)


def _split3(x):
    hi = x.astype(BF16)
    r1 = x - hi.astype(F32)
    mid = r1.astype(BF16)
    lo = (r1 - mid.astype(F32)).astype(BF16)
    return hi, mid, lo


def _dot_hi(a, b):
    ah = a.astype(BF16)
    al = (a - ah.astype(F32)).astype(BF16)
    bh = b.astype(BF16)
    bl = (b - bh.astype(F32)).astype(BF16)
    d = lambda p, q: jnp.dot(p, q, preferred_element_type=F32)
    return d(ah, bh) + d(ah, bl) + d(al, bh)


def _bmm(a, b):
    return jnp.einsum('cik,ckj->cij', a.astype(BF16), b.astype(BF16), preferred_element_type=F32)


def _bmm_nt(a, b):
    return jnp.einsum('cik,cjk->cij', a.astype(BF16), b.astype(BF16), preferred_element_type=F32)


def _rwkv_kernel(pa_ref, halo_ref, mu_ref, vec_ref, wd_ref, wa_ref, wg_ref, oa_ref, hstate_s):
    tb = pa_ref.shape[0]
    nc = tb // CHUNK
    i = pl.program_id(1)

    @pl.when(i == 0)
    def _():
        hstate_s[...] = jnp.zeros_like(hstate_s)

    x = pa_ref[...]
    prev_row = jnp.where(i == 0, 0.0, halo_ref[SUBLANES - 1:SUBLANES, :])
    row = lax.broadcasted_iota(jnp.int32, (tb, 1), 0)
    prev = jnp.where(row == 0, prev_row, pltpu.roll(x, 1, 0))
    hx = x + (prev - x) * mu_ref[...]
    r = hx[:, 0:RWKV_PAD]
    k = hx[:, RWKV_PAD:2 * RWKV_PAD]
    v = hx[:, 2 * RWKV_PAD:3 * RWKV_PAD]
    lora = hx[:, 3 * RWKV_PAD:]
    w0, a0, k_k, k_a = vec_ref[0:1, :], vec_ref[1:2, :], vec_ref[2:3, :], vec_ref[3:4, :]
    r_k, gn_g, gn_b = vec_ref[4:5, :], vec_ref[5:6, :], vec_ref[6:7, :]

    w = -jax.nn.softplus(-(w0 + _dot_hi(jnp.tanh(lora), wd_ref[...]))) - 0.5
    wl = -jnp.exp(w)
    alr = jax.nn.sigmoid(a0 + _dot(lora, wa_ref[...]))
    gate = _dot(jax.nn.sigmoid(lora), wg_ref[...])
    kk = k * k_k
    k2 = k * (1.0 + (alr - 1.0) * k_a)
    rkr = r * k2 * r_k

    ri = lax.broadcasted_iota(jnp.int32, (CHUNK, CHUNK), 0)
    ci = lax.broadcasted_iota(jnp.int32, (CHUNK, CHUNK), 1)
    low_incl = (ci <= ri)[None]
    low_strict = (ci < ri)[None]
    tril_b = jnp.where(ci <= ri, 1.0, 0.0).astype(BF16)
    eye_k = (lax.broadcasted_iota(jnp.int32, (HEAD_PAD, HEAD_PAD), 0)
             == lax.broadcasted_iota(jnp.int32, (HEAD_PAD, HEAD_PAD), 1))[None]
    n_sq = int(math.log2(CHUNK))

    parts = _split3(wl)
    cums = []
    for c in range(nc):
        rows = slice(c * CHUNK, (c + 1) * CHUNK)
        cums.append(sum(jnp.dot(tril_b, p[rows], preferred_element_type=F32) for p in parts))
    cum_all = jnp.concatenate(cums, axis=0)

    lane = lax.broadcasted_iota(jnp.int32, (tb, HEAD_PAD), 1)
    real = lane < RWKV_HEAD_DIM
    to3 = lambda t: t.reshape(nc, CHUNK, HEAD_PAD)
    for hd in range(RWKV_HEADS):
        cs = slice(hd * HEAD_PAD, (hd + 1) * HEAD_PAD)
        kk_h = kk[:, cs]
        nrm = jnp.sqrt(jnp.sum(kk_h * kk_h, axis=-1, keepdims=True))
        kk_h = kk_h / jnp.maximum(nrm, 1e-12)
        v_h = v[:, cs]
        bonus = jnp.sum(rkr[:, cs], axis=-1, keepdims=True) * v_h

        wl3, cum = to3(wl[:, cs]), to3(cum_all[:, cs])
        rr, kc, vc = to3(r[:, cs]), to3(k2[:, cs]), to3(v_h)
        ac, bc = to3(-kk_h), to3(kk_h * alr[:, cs])
        last = cum[:, CHUNK - 1:CHUNK, :]
        e_neg = jnp.exp(-cum)
        e_last = jnp.exp(last - cum)
        ah = ac * jnp.exp(cum - wl3)
        rh = rr * jnp.exp(cum)
        ar = jnp.concatenate([ah, rh], axis=1)
        gb = _bmm_nt(ar, bc * e_neg)
        gk = _bmm_nt(ar, kc * e_neg)
        a_ab = jnp.where(low_strict, gb[:, :CHUNK], 0.0)
        r_b = jnp.where(low_incl, gb[:, CHUNK:], 0.0)
        a_ak = jnp.where(low_strict, gk[:, :CHUNK], 0.0)
        r_kk = jnp.where(low_incl, gk[:, CHUNK:], 0.0)
        xx = jnp.concatenate([_bmm(a_ak, vc), ah], axis=2)
        pp = a_ab
        for j in range(n_sq):
            xx = xx + _bmm(pp, xx)
            if j + 1 < n_sq:
                pp = _bmm(pp, pp)
        rbx = _bmm(r_b, xx)
        q = rh + rbx[:, :, HEAD_PAD:]
        y0 = rbx[:, :, :HEAD_PAD] + _bmm(r_kk, vc)
        btx = _bmm(jnp.swapaxes(bc * e_last, 1, 2), xx)
        m = jnp.where(eye_k, jnp.exp(last), 0.0) + btx[:, :, HEAD_PAD:]
        n0 = btx[:, :, :HEAD_PAD] + _bmm(jnp.swapaxes(kc * e_last, 1, 2), vc)
        qm = jnp.concatenate([q, m], axis=1).astype(BF16)

        hs = hstate_s[hd]
        ys = []
        for c in range(nc):
            res = jnp.dot(qm[c], hs.astype(BF16), preferred_element_type=F32)
            ys.append(res[:CHUNK] + y0[c])
            hs = res[CHUNK:] + n0[c]
        hstate_s[hd] = hs
        y = jnp.concatenate(ys, axis=0)

        mean = jnp.sum(y, axis=-1, keepdims=True) / RWKV_HEAD_DIM
        dlt = jnp.where(real, y - mean, 0.0)
        var = jnp.sum(dlt * dlt, axis=-1, keepdims=True) / RWKV_HEAD_DIM
        yn = dlt * lax.rsqrt(var + RWKV_GN_EPS) * gn_g[:, cs] + gn_b[:, cs]
        oa_ref[:, cs] = ((yn + bonus) * gate[:, cs]).astype(BF16)


def _rwkv(pa, mu, vecs, wd, wa, wg, *, tb):
    b, lp, _ = pa.shape
    tile = lambda bi, i: (bi, i, 0)
    const = lambda bi, i: (0, 0)
    halo = lambda bi, i: (bi, jnp.maximum(i * (tb // SUBLANES) - 1, 0), 0)
    return pl.pallas_call(
        _rwkv_kernel,
        grid=(b, lp // tb),
        in_specs=[
            pl.BlockSpec((None, tb, PA_COLS), tile),
            pl.BlockSpec((None, SUBLANES, PA_COLS), halo),
            pl.BlockSpec((1, PA_COLS), const),
            pl.BlockSpec((SUBLANES, RWKV_PAD), const),
            pl.BlockSpec((LORA_COLS, RWKV_PAD), const),
            pl.BlockSpec((LORA_COLS, RWKV_PAD), const),
            pl.BlockSpec((LORA_COLS, RWKV_PAD), const),
        ],
        out_specs=pl.BlockSpec((None, tb, RWKV_PAD), tile),
        out_shape=jax.ShapeDtypeStruct((b, lp, RWKV_PAD), BF16),
        scratch_shapes=[pltpu.VMEM((RWKV_HEADS, HEAD_PAD, HEAD_PAD), F32)],
        compiler_params=_params(("arbitrary", "arbitrary")),
        name="rwkv7",
    )(pa, pa, mu, vecs, wd, wa, wg)


def _attn_kernel(q_ref, k_ref, vt_ref, lamv_ref, sg_ref, o_ref, acc_s, *, lam_init):
    tq = o_ref.shape[0]
    tk = vt_ref.shape[2]
    qi = pl.program_id(2)
    q = q_ref[...].reshape(2 * tq, LANES)
    acc_s[...] = jnp.zeros_like(acc_s)

    def tile(ki, carry, masked):
        m_prev, l_prev = carry
        k = k_ref[pl.ds(pl.multiple_of(ki * tk, tk), tk), :]
        st = lax.dot_general(k, q, (((1,), (1,)), ((), ())), preferred_element_type=F32)
        if masked:
            kr = lax.broadcasted_iota(jnp.int32, (tk, 2 * tq), 0)
            qc = lax.broadcasted_iota(jnp.int32, (tk, 2 * tq), 1)
            qc = jnp.where(qc >= tq, qc - tq, qc)
            st = jnp.where(kr <= qc, st, NEG_INF)
        m_new = jnp.maximum(m_prev, jnp.max(st, axis=0, keepdims=True))
        alpha = jnp.exp(m_prev - m_new)
        p = jnp.exp(st - m_new)
        l_new = alpha * l_prev + jnp.sum(p, axis=0, keepdims=True)
        acc_s[...] = alpha * acc_s[...] + jnp.dot(vt_ref[ki], p.astype(BF16),
                                                  preferred_element_type=F32)
        return m_new, l_new

    init = (jnp.full((1, 2 * tq), NEG_INF, F32), jnp.zeros((1, 2 * tq), F32))
    carry = lax.fori_loop(0, qi, lambda ki, c: tile(ki, c, False), init)
    _, l_fin = tile(qi, carry, True)

    ot = acc_s[...] / l_fin
    lv = lamv_ref[...]
    lam = (jnp.exp(jnp.sum(lv[0:1] * lv[1:2], axis=-1, keepdims=True))
           - jnp.exp(jnp.sum(lv[2:3] * lv[3:4], axis=-1, keepdims=True)) + lam_init)
    od = (ot[:, :tq] - lam * ot[:, tq:]).T
    ms = jnp.mean(od * od, axis=-1, keepdims=True)
    o_ref[...] = (od * lax.rsqrt(ms + SUBLN_EPS) * sg_ref[...] * (1.0 - lam_init)).astype(BF16)


def _attention(qs, kh, vt, lamv, sg, *, tq, lam_init):
    _, b, lp, _ = qs.shape
    nq = lp // tq
    const = lambda bi, hd, qi: (0, 0)
    return pl.pallas_call(
        functools.partial(_attn_kernel, lam_init=lam_init),
        grid=(b, DIFF_HEADS, nq),
        in_specs=[
            pl.BlockSpec((2, None, tq, LANES), lambda bi, hd, qi: (0, bi, qi, hd)),
            pl.BlockSpec((None, lp, LANES), lambda bi, hd, qi: (bi, 0, hd)),
            pl.BlockSpec((None, nq, None, LANES, tq), lambda bi, hd, qi: (bi, 0, hd, 0, 0)),
            pl.BlockSpec((SUBLANES, LANES), const),
            pl.BlockSpec((1, LANES), const),
        ],
        out_specs=pl.BlockSpec((None, tq, LANES), lambda bi, hd, qi: (bi, qi, hd)),
        out_shape=jax.ShapeDtypeStruct((b, lp, DIFF_WIDTH), BF16),
        scratch_shapes=[pltpu.VMEM((LANES, 2 * tq), F32)],
        compiler_params=_params(("arbitrary", "arbitrary", "arbitrary")),
        name="diffattn",
    )(qs, kh, vt, lamv, sg)


def _outproj_kernel(h_ref, oa_ref, ob_ref, pc_ref, pch_ref, cw_ref, wa_ref, wb_ref, wc_ref,
                    o_ref, *, tiles_per_batch):
    tm = h_ref.shape[0]
    first = (pl.program_id(0) % tiles_per_batch) == 0
    pc = pc_ref[...]
    bg = pc[:, :CONV_WIDTH]
    z = pc[:, CONV_WIDTH:2 * CONV_WIDTH] * pc[:, 2 * CONV_WIDTH:]
    ph = pch_ref[...]
    zh = jnp.where(first, 0.0, ph[:, CONV_WIDTH:2 * CONV_WIDTH] * ph[:, 2 * CONV_WIDTH:])
    row = lax.broadcasted_iota(jnp.int32, (tm, 1), 0)
    z1 = jnp.where(row == 0, zh[SUBLANES - 1:SUBLANES], pltpu.roll(z, 1, 0))
    z2 = jnp.where(row == 0, zh[SUBLANES - 2:SUBLANES - 1],
                   jnp.where(row == 1, zh[SUBLANES - 1:SUBLANES], pltpu.roll(z, 2, 0)))
    cw = cw_ref[...]
    y = cw[0:1] * z2 + cw[1:2] * z1 + cw[2:3] * z
    oc = (bg * y).astype(BF16)
    o_ref[...] = (h_ref[...]
                  + jnp.dot(oa_ref[...], wa_ref[...], preferred_element_type=F32)
                  + jnp.dot(ob_ref[...], wb_ref[...], preferred_element_type=F32)
                  + jnp.dot(oc, wc_ref[...], preferred_element_type=F32))


def _outproj(h, oa, ob, pc, cw, wa, wb, wc, *, tm, lp):
    r, d = h.shape
    row = lambda i: (i, 0)
    const = lambda i: (0, 0)
    halo = lambda i: (jnp.maximum(i * (tm // SUBLANES) - 1, 0), 0)
    return pl.pallas_call(
        functools.partial(_outproj_kernel, tiles_per_batch=lp // tm),
        grid=(r // tm,),
        in_specs=[
            pl.BlockSpec((tm, d), row),
            pl.BlockSpec((tm, RWKV_PAD), row),
            pl.BlockSpec((tm, DIFF_WIDTH), row),
            pl.BlockSpec((tm, PC_COLS), row),
            pl.BlockSpec((SUBLANES, PC_COLS), halo),
            pl.BlockSpec((SUBLANES, CONV_WIDTH), const),
            pl.BlockSpec(wa.shape, const),
            pl.BlockSpec(wb.shape, const),
            pl.BlockSpec(wc.shape, const),
        ],
        out_specs=pl.BlockSpec((tm, d), row),
        out_shape=jax.ShapeDtypeStruct((r, d), F32),
        compiler_params=_params(("arbitrary",)),
        name="outproj",
    )(h, oa, ob, pc, pc, cw, wa, wb, wc)


def _swiglu_block(xb, wg_ref, wu_ref, wd_ref):
    d_ff = wg_ref.shape[1]
    acc = jnp.zeros((xb.shape[0], wd_ref.shape[1]), F32)
    for c in range(d_ff // MXU_DIM):
        cs = slice(c * MXU_DIM, (c + 1) * MXU_DIM)
        gt = jnp.dot(xb, wg_ref[:, cs], preferred_element_type=F32)
        up = jnp.dot(xb, wu_ref[:, cs], preferred_element_type=F32)
        act = (jax.nn.silu(gt) * up).astype(BF16)
        acc = acc + jnp.dot(act, wd_ref[cs, :], preferred_element_type=F32)
    return acc


def _rms(x, g):
    ms = jnp.mean(x * x, axis=-1, keepdims=True)
    return x * lax.rsqrt(ms + NORM_EPS) * g


def _ffn_kernel(h_ref, g_ref, wg_ref, wu_ref, wd_ref, o_ref):
    x = h_ref[...]
    xn = _rms(x, g_ref[...]).astype(BF16)
    o_ref[...] = x + _swiglu_block(xn, wg_ref, wu_ref, wd_ref)


def _dense_ffn(h, g, wg, wu, wd, *, tm):
    r, d = h.shape
    row = lambda i: (i, 0)
    const = lambda i: (0, 0)
    return pl.pallas_call(
        _ffn_kernel,
        grid=(r // tm,),
        in_specs=[
            pl.BlockSpec((tm, d), row),
            pl.BlockSpec((1, d), const),
            pl.BlockSpec(wg.shape, const),
            pl.BlockSpec(wu.shape, const),
            pl.BlockSpec(wd.shape, const),
        ],
        out_specs=pl.BlockSpec((tm, d), row),
        out_shape=jax.ShapeDtypeStruct((r, d), F32),
        compiler_params=_params(("arbitrary",)),
        name="dense_ffn",
    )(h, g, wg, wu, wd)


def _router_kernel(h_ref, g_ref, rw_ref, hn_ref, route_ref):
    tm = h_ref.shape[0]
    xn = _rms(h_ref[...], g_ref[...])
    hn_ref[...] = xn
    logits = _dot_f32(xn, rw_ref[...])
    lane = lax.broadcasted_iota(jnp.int32, (tm, LANES), 1)
    logits = jnp.where(lane < N_EXPERTS, logits, -jnp.inf)
    m1 = jnp.max(logits, axis=-1, keepdims=True)
    i1 = jnp.min(jnp.where(logits == m1, lane, LANES), axis=-1, keepdims=True)
    rest = jnp.where(lane == i1, -jnp.inf, logits)
    m2 = jnp.max(rest, axis=-1, keepdims=True)
    i2 = jnp.min(jnp.where(rest == m2, lane, LANES), axis=-1, keepdims=True)
    e2 = jnp.exp(m2 - m1)
    den = 1.0 + e2
    route_ref[...] = jnp.where(lane == 0, i1.astype(F32),
                     jnp.where(lane == 1, i2.astype(F32),
                     jnp.where(lane == 2, 1.0 / den,
                     jnp.where(lane == 3, e2 / den, 0.0))))


def _router(h, g, rw, *, tm):
    r, d = h.shape
    row = lambda i: (i, 0)
    const = lambda i: (0, 0)
    return pl.pallas_call(
        _router_kernel,
        grid=(r // tm,),
        in_specs=[pl.BlockSpec((tm, d), row), pl.BlockSpec((1, d), const),
                  pl.BlockSpec((d, LANES), const)],
        out_specs=[pl.BlockSpec((tm, d), row), pl.BlockSpec((tm, LANES), row)],
        out_shape=[jax.ShapeDtypeStruct((r, d), F32), jax.ShapeDtypeStruct((r, LANES), F32)],
        compiler_params=_params(("arbitrary",)),
        name="router",
    )(h, g, rw)


def _row_copy(src_hbm, dst_vmem, sem, src_row, dst_row):
    return pltpu.make_async_copy(src_hbm.at[pl.ds(src_row, 1), :],
                                 dst_vmem.at[pl.ds(dst_row, 1), :], sem)


def _moe_kernel(bexp_ref, rsrc_ref, hn_hbm, wg_ref, wu_ref, wd_ref, y_ref, xbuf, sem):
    del bexp_ref
    blk = xbuf.shape[0]
    base = pl.program_id(0) * blk

    def issue(j, c):
        _row_copy(hn_hbm, xbuf, sem, rsrc_ref[base + j], j).start()
        return c

    lax.fori_loop(0, blk, issue, 0)

    def drain(j, c):
        _row_copy(hn_hbm, xbuf, sem, 0, j).wait()
        return c

    lax.fori_loop(0, blk, drain, 0)
    y_ref[...] = _swiglu_block(xbuf[...].astype(BF16), wg_ref, wu_ref, wd_ref)


def _moe_experts(block_expert, row_src, hn, wg, wu, wd):
    nb = block_expert.shape[0]
    d = hn.shape[1]
    d_ff = wg.shape[2]
    wmap = lambda i, be, rs: (be[i], 0, 0)
    grid_spec = pltpu.PrefetchScalarGridSpec(
        num_scalar_prefetch=2,
        grid=(nb,),
        in_specs=[
            pl.BlockSpec(memory_space=pl.ANY),
            pl.BlockSpec((None, d, d_ff), wmap),
            pl.BlockSpec((None, d, d_ff), wmap),
            pl.BlockSpec((None, d_ff, d), wmap),
        ],
        out_specs=pl.BlockSpec((MOE_BLOCK, d), lambda i, be, rs: (i, 0)),
        scratch_shapes=[pltpu.VMEM((MOE_BLOCK, d), F32), pltpu.SemaphoreType.DMA(())],
    )
    return pl.pallas_call(
        _moe_kernel,
        grid_spec=grid_spec,
        out_shape=jax.ShapeDtypeStruct((nb * MOE_BLOCK, d), F32),
        compiler_params=_params(("arbitrary",)),
        name="moe_experts",
    )(block_expert, row_src, hn, wg, wu, wd)


def _combine_kernel(pos_ref, h_ref, route_ref, y_hbm, o_ref, ybuf, sem):
    tm = h_ref.shape[0]
    base = pl.program_id(0) * tm

    def issue(j, c):
        for kk in range(TOP_K):
            _row_copy(y_hbm, ybuf.at[kk], sem, pos_ref[TOP_K * (base + j) + kk], j).start()
        return c

    lax.fori_loop(0, tm, issue, 0)

    def drain(j, c):
        for kk in range(TOP_K):
            _row_copy(y_hbm, ybuf.at[kk], sem, 0, j).wait()
        return c

    lax.fori_loop(0, tm, drain, 0)
    route = route_ref[...]
    o_ref[...] = h_ref[...] + route[:, 2:3] * ybuf[0] + route[:, 3:4] * ybuf[1]


def _moe_combine(pos, h, route, y_rows, *, tm):
    r, d = h.shape
    row = lambda i, p: (i, 0)
    grid_spec = pltpu.PrefetchScalarGridSpec(
        num_scalar_prefetch=1,
        grid=(r // tm,),
        in_specs=[
            pl.BlockSpec((tm, d), row),
            pl.BlockSpec((tm, LANES), row),
            pl.BlockSpec(memory_space=pl.ANY),
        ],
        out_specs=pl.BlockSpec((tm, d), row),
        scratch_shapes=[pltpu.VMEM((TOP_K, tm, d), F32), pltpu.SemaphoreType.DMA(())],
    )
    return pl.pallas_call(
        _combine_kernel,
        grid_spec=grid_spec,
        out_shape=jax.ShapeDtypeStruct((r, d), F32),
        compiler_params=_params(("arbitrary",)),
        name="moe_combine",
    )(pos, h, route, y_rows)


def _routing_tables(route, b, l, lp):
    t = b * l
    tk = t * TOP_K
    rt = route.reshape(b, lp, LANES)[:, :l, :]
    flat_e = rt[:, :, :TOP_K].reshape(tk).astype(jnp.int32)
    tok_row = (jnp.arange(b, dtype=jnp.int32)[:, None] * lp
               + jnp.arange(l, dtype=jnp.int32)[None, :]).reshape(t)
    flat_row = jnp.repeat(tok_row, TOP_K)
    onehot = (flat_e[:, None] == jnp.arange(N_EXPERTS, dtype=jnp.int32)[None, :]).astype(jnp.int32)
    csum = jnp.cumsum(onehot, axis=0)
    rank = jnp.sum(csum * onehot, axis=1) - 1
    counts = csum[-1]
    padded = (counts + MOE_BLOCK - 1) // MOE_BLOCK * MOE_BLOCK
    pend = jnp.cumsum(padded)
    pstart = pend - padded
    dest = pstart[flat_e] + rank
    nb = -(-tk // MOE_BLOCK) + N_EXPERTS
    row_src = jnp.zeros((nb * MOE_BLOCK,), jnp.int32).at[dest].set(flat_row, unique_indices=True)
    block_expert = jnp.minimum(
        jnp.searchsorted(pend, jnp.arange(nb, dtype=pend.dtype) * MOE_BLOCK, side='right'),
        N_EXPERTS - 1).astype(jnp.int32)
    pos = jnp.zeros((b, lp, TOP_K), jnp.int32).at[:, :l, :].set(dest.reshape(b, l, TOP_K))
    return block_expert, row_src, pos.reshape(b * lp * TOP_K)


def _pad_heads(t, axis):
    axis = axis % t.ndim
    shp = t.shape
    t = t.reshape(shp[:axis] + (RWKV_HEADS, RWKV_HEAD_DIM) + shp[axis + 1:])
    pad = [(0, 0)] * t.ndim
    pad[axis + 1] = (0, HEAD_PAD - RWKV_HEAD_DIM)
    t = jnp.pad(t, pad)
    return t.reshape(shp[:axis] + (RWKV_PAD,) + shp[axis + 1:])


def _rope_tables(lp):
    half = ROPE_DIM // 2
    inv_freq = ROPE_THETA ** (-jnp.arange(0, ROPE_DIM, 2, dtype=F32) / ROPE_DIM)
    ang = jnp.arange(lp, dtype=F32)[:, None] * inv_freq[None, :]
    cos, sin = jnp.cos(ang), jnp.sin(ang)
    ones = jnp.ones((lp, DIFF_QK_DIM - ROPE_DIM), F32)
    zeros = jnp.zeros((lp, DIFF_QK_DIM - ROPE_DIM), F32)
    zh = jnp.zeros((lp, half), F32)
    rc = jnp.concatenate([cos, cos, ones], axis=1)
    rs1 = jnp.concatenate([-sin, zh, zeros], axis=1)
    rs2 = jnp.concatenate([zh, sin, zeros], axis=1)
    tile2 = lambda a: jnp.concatenate([a, a], axis=1)
    return tile2(rc), tile2(rs1), tile2(rs2)


def kernel(x, meta_tokens, mix_norm_g, w_in, tm_mu, tm_w0, tm_w_decay_up, tm_a0, tm_w_a_up, tm_w_g_up, tm_k_k, tm_k_a, tm_r_k, tm_gn_g, tm_gn_b, da_q_norm_g, da_k_norm_g, da_lambda_q1, da_lambda_k1, da_lambda_q2, da_lambda_k2, da_subln_g, sc_conv_w, w_out, ffn_norm_g, ffn_w_gate, ffn_w_up, ffn_w_down, router_w, moe_w_gate, moe_w_up, moe_w_down):
    b, seq, d = x.shape
    depth = w_in.shape[0]
    l = N_META + seq
    lp = -(-l // LANES) * LANES
    tm = _row_tile(lp)
    r = b * lp
    s1 = RWKV_WIDTH

    meta = jnp.broadcast_to(meta_tokens.astype(x.dtype)[None], (b, N_META, d))
    h = jnp.concatenate([meta, x, jnp.zeros((b, lp - l, d), x.dtype)], axis=1).reshape(r, d)
    rc, rs1, rs2 = _rope_tables(lp)

    for i in range(depth):
        lam_init = 0.8 - 0.6 * math.exp(-0.3 * i)
        wi = w_in[i]
        w_cat = jnp.concatenate(
            [_pad_heads(wi[:, 0:s1], 1), _pad_heads(wi[:, s1:2 * s1], 1),
             _pad_heads(wi[:, 2 * s1:3 * s1], 1), wi[:, 3 * s1:]], axis=1).astype(BF16)
        qg = jnp.tile(da_q_norm_g[i], 2)[None]
        kg = jnp.tile(da_k_norm_g[i], 2)[None]
        pa, qs, kh, vt, pc = _inproj(h, mix_norm_g[i][None], w_cat, rc, rs1, rs2, qg, kg,
                                     tm=tm, lp=lp)

        mu = tm_mu[i]
        mu_p = jnp.concatenate([_pad_heads(mu[0:s1], 0), _pad_heads(mu[s1:2 * s1], 0),
                                _pad_heads(mu[2 * s1:3 * s1], 0), mu[3 * s1:]])[None]
        vecs = jnp.stack([_pad_heads(tm_w0[i], 0), _pad_heads(tm_a0[i], 0),
                          _pad_heads(tm_k_k[i], 0), _pad_heads(tm_k_a[i], 0),
                          _pad_heads(tm_r_k[i].reshape(s1), 0), _pad_heads(tm_gn_g[i], 0),
                          _pad_heads(tm_gn_b[i], 0), jnp.zeros((RWKV_PAD,), F32)])
        zl = lambda n: jnp.zeros((n, RWKV_PAD), F32)
        wd_p = jnp.concatenate([_pad_heads(tm_w_decay_up[i], 1), zl(AAA_LORA + GATE_LORA)], axis=0)
        wa_p = jnp.concatenate([zl(DECAY_LORA), _pad_heads(tm_w_a_up[i], 1), zl(GATE_LORA)], axis=0)
        wg_p = jnp.concatenate([zl(DECAY_LORA + AAA_LORA), _pad_heads(tm_w_g_up[i], 1)], axis=0)
        oa = _rwkv(pa.reshape(b, lp, PA_COLS), mu_p, vecs, wd_p, wa_p, wg_p, tb=tm)

        lamv = jnp.zeros((SUBLANES, LANES), F32).at[0:4, :DIFF_QK_DIM].set(
            jnp.stack([da_lambda_q1[i], da_lambda_k1[i], da_lambda_q2[i], da_lambda_k2[i]]))
        ob = _attention(qs.reshape(2, b, lp, DIFF_WIDTH), kh.reshape(b, lp, DIFF_WIDTH),
                        vt.reshape(b, lp // tm, DIFF_HEADS, LANES, tm), lamv, da_subln_g[i][None],
                        tq=tm, lam_init=lam_init)

        wo = w_out[i]
        cw = jnp.zeros((SUBLANES, CONV_WIDTH), F32).at[:CONV_K].set(sc_conv_w[i])
        h = _outproj(h, oa.reshape(r, RWKV_PAD), ob.reshape(r, DIFF_WIDTH), pc, cw,
                     _pad_heads(wo[:s1], 0).astype(BF16), wo[s1:s1 + DIFF_WIDTH].astype(BF16),
                     wo[s1 + DIFF_WIDTH:].astype(BF16), tm=tm, lp=lp)

        j = i // 2
        if i % 2 == 0:
            h = _dense_ffn(h, ffn_norm_g[i][None], ffn_w_gate[j].astype(BF16),
                           ffn_w_up[j].astype(BF16), ffn_w_down[j].astype(BF16), tm=tm)
        else:
            rw = jnp.zeros((d, LANES), F32).at[:, :N_EXPERTS].set(router_w[j])
            hn, route = _router(h, ffn_norm_g[i][None], rw, tm=tm)
            block_expert, row_src, pos = _routing_tables(route, b, l, lp)
            y_rows = _moe_experts(block_expert, row_src, hn, moe_w_gate[j].astype(BF16),
                                  moe_w_up[j].astype(BF16), moe_w_down[j].astype(BF16))
            h = _moe_combine(pos, h, route, y_rows, tm=tm)

    return h.reshape(b, lp, d)[:, N_META:l]
```

```python
import functools
import math

import jax
import jax.numpy as jnp
from jax import lax
from jax.experimental import pallas as pl
from jax.experimental.pallas import tpu as pltpu

N_META = 16
RWKV_HEAD_DIM = 64
RWKV_HEADS = 4
RWKV_WIDTH = RWKV_HEADS * RWKV_HEAD_DIM
DECAY_LORA = 32
AAA_LORA = 32
GATE_LORA = 64
LORA_COLS = DECAY_LORA + AAA_LORA + GATE_LORA
DIFF_QK_DIM = 64
DIFF_V_DIM = 128
DIFF_HEADS = 4
DIFF_WIDTH = DIFF_HEADS * DIFF_V_DIM
CONV_WIDTH = 256
CONV_K = 3
ROPE_THETA = 500000.0
ROPE_DIM = DIFF_QK_DIM // 4
N_EXPERTS = 8
TOP_K = 2
MOE_BLOCK = 256
NORM_EPS = 1e-6
RWKV_GN_EPS = 64e-5
SUBLN_EPS = 1e-5
NEG_INF = -1e30

LANES = 128
SUBLANES = 8
MXU_DIM = 256
VMEM_LIMIT_BYTES = 56 * 1024 * 1024

HEAD_PAD = LANES
RWKV_PAD = RWKV_HEADS * HEAD_PAD
PA_COLS = 3 * RWKV_PAD + LORA_COLS
PB_COLS = 3 * DIFF_WIDTH
PC_COLS = 3 * CONV_WIDTH
CHUNK = 64

F32 = jnp.float32
BF16 = jnp.bfloat16
HIGHEST = lax.Precision.HIGHEST


def _dot(a, b):
    return jnp.dot(a.astype(BF16), b.astype(BF16), preferred_element_type=F32)


def _dot_nt(a, b):
    return lax.dot_general(a.astype(BF16), b.astype(BF16), (((1,), (1,)), ((), ())),
                           preferred_element_type=F32)


def _dot_f32(a, b):
    return jnp.dot(a, b, preferred_element_type=F32, precision=HIGHEST)


def _row_tile(lp):
    for t in (384, 256, 128):
        if lp % t == 0:
            return t
    raise ValueError(f"padded length {lp} is not a multiple of {LANES}")


def _params(sem):
    return pltpu.CompilerParams(dimension_semantics=sem, vmem_limit_bytes=VMEM_LIMIT_BYTES)


def _inproj_kernel(h_ref, g_ref, w_ref, rc_ref, rs1_ref, rs2_ref, qg_ref, kg_ref,
                   pa_ref, qs_ref, kh_ref, vt_ref, pc_ref):
    tm = h_ref.shape[0]
    x = h_ref[...]
    ms = jnp.mean(x * x, axis=-1, keepdims=True)
    xn = (x * lax.rsqrt(ms + NORM_EPS) * g_ref[...]).astype(BF16)
    pa_ref[...] = jnp.dot(xn, w_ref[:, :PA_COLS], preferred_element_type=F32)
    pc_ref[...] = jnp.dot(xn, w_ref[:, PA_COLS + PB_COLS:], preferred_element_type=F32)
    pb = jnp.dot(xn, w_ref[:, PA_COLS:PA_COLS + PB_COLS], preferred_element_type=F32)

    lane = lax.broadcasted_iota(jnp.int32, (tm, LANES), 1)
    lo = lane < DIFF_QK_DIM
    rc, rs1, rs2 = rc_ref[...], rs1_ref[...], rs2_ref[...]

    def norm_rope(t, g):
        t2 = t * t
        ss_lo = jnp.sum(jnp.where(lo, t2, 0.0), axis=-1, keepdims=True)
        ss_hi = jnp.sum(jnp.where(lo, 0.0, t2), axis=-1, keepdims=True)
        inv = jnp.where(lo, lax.rsqrt(ss_lo / DIFF_QK_DIM + NORM_EPS),
                        lax.rsqrt(ss_hi / DIFF_QK_DIM + NORM_EPS))
        tn = t * inv * g
        half = ROPE_DIM // 2
        return tn * rc + pltpu.roll(tn, LANES - half, 1) * rs1 + pltpu.roll(tn, half, 1) * rs2

    for hd in range(DIFF_HEADS):
        cs = slice(hd * LANES, (hd + 1) * LANES)
        q = norm_rope(pb[:, cs], qg_ref[...]) * (DIFF_QK_DIM ** -0.5)
        qs_ref[0, :, cs] = jnp.where(lo, q, 0.0).astype(BF16)
        qs_ref[1, :, cs] = jnp.where(lo, 0.0, q).astype(BF16)
        kcs = slice(DIFF_WIDTH + hd * LANES, DIFF_WIDTH + (hd + 1) * LANES)
        kh_ref[:, cs] = norm_rope(pb[:, kcs], kg_ref[...]).astype(BF16)
        vcs = slice(2 * DIFF_WIDTH + hd * LANES, 2 * DIFF_WIDTH + (hd + 1) * LANES)
        vt_ref[0, hd] = pb[:, vcs].T.astype(BF16)


def _inproj(h, g, w, rc, rs1, rs2, qg, kg, *, tm, lp):
    r, d = h.shape
    nt_b = lp // tm
    row = lambda i: (i, 0)
    const = lambda i: (0, 0)
    rope = lambda i: (i % nt_b, 0)
    return pl.pallas_call(
        _inproj_kernel,
        grid=(r // tm,),
        in_specs=[
            pl.BlockSpec((tm, d), row),
            pl.BlockSpec((1, d), const),
            pl.BlockSpec(w.shape, const),
            pl.BlockSpec((tm, LANES), rope),
            pl.BlockSpec((tm, LANES), rope),
            pl.BlockSpec((tm, LANES), rope),
            pl.BlockSpec((1, LANES), const),
            pl.BlockSpec((1, LANES), const),
        ],
        out_specs=[
            pl.BlockSpec((tm, PA_COLS), row),
            pl.BlockSpec((2, tm, DIFF_WIDTH), lambda i: (0, i, 0)),
            pl.BlockSpec((tm, DIFF_WIDTH), row),
            pl.BlockSpec((1, DIFF_HEADS, LANES, tm), lambda i: (i, 0, 0, 0)),
            pl.BlockSpec((tm, PC_COLS), row),
        ],
        out_shape=[
            jax.ShapeDtypeStruct((r, PA_COLS), F32),
            jax.ShapeDtypeStruct((2, r, DIFF_WIDTH), BF16),
            jax.ShapeDtypeStruct((r, DIFF_WIDTH), BF16),
            jax.ShapeDtypeStruct((r // tm, DIFF_HEADS, LANES, tm), BF16),
            jax.ShapeDtypeStruct((r, PC_COLS), F32),
        ],
        compiler_params=_params(("arbitrary",)),
        name="inproj",
    )(h, g, w, rc, rs1, rs2, qg, kg)


def _split3(x):
    hi = x.astype(BF16)
    r1 = x - hi.astype(F32)
    mid = r1.astype(BF16)
    lo = (r1 - mid.astype(F32)).astype(BF16)
    return hi, mid, lo


def _dot_hi(a, b):
    ah = a.astype(BF16)
    al = (a - ah.astype(F32)).astype(BF16)
    bh = b.astype(BF16)
    bl = (b - bh.astype(F32)).astype(BF16)
    d = lambda p, q: jnp.dot(p, q, preferred_element_type=F32)
    return d(ah, bh) + d(ah, bl) + d(al, bh)


def _bmm(a, b):
    return jnp.einsum('cik,ckj->cij', a.astype(BF16), b.astype(BF16), preferred_element_type=F32)


def _bmm_nt(a, b):
    return jnp.einsum('cik,cjk->cij', a.astype(BF16), b.astype(BF16), preferred_element_type=F32)


def _rwkv_kernel(pa_ref, halo_ref, mu_ref, vec_ref, wd_ref, wa_ref, wg_ref, oa_ref, hstate_s):
    tb = pa_ref.shape[0]
    nc = tb // CHUNK
    i = pl.program_id(1)

    @pl.when(i == 0)
    def _():
        hstate_s[...] = jnp.zeros_like(hstate_s)

    x = pa_ref[...]
    prev_row = jnp.where(i == 0, 0.0, halo_ref[SUBLANES - 1:SUBLANES, :])
    row = lax.broadcasted_iota(jnp.int32, (tb, 1), 0)
    prev = jnp.where(row == 0, prev_row, pltpu.roll(x, 1, 0))
    hx = x + (prev - x) * mu_ref[...]
    r = hx[:, 0:RWKV_PAD]
    k = hx[:, RWKV_PAD:2 * RWKV_PAD]
    v = hx[:, 2 * RWKV_PAD:3 * RWKV_PAD]
    lora = hx[:, 3 * RWKV_PAD:]
    w0, a0, k_k, k_a = vec_ref[0:1, :], vec_ref[1:2, :], vec_ref[2:3, :], vec_ref[3:4, :]
    r_k, gn_g, gn_b = vec_ref[4:5, :], vec_ref[5:6, :], vec_ref[6:7, :]

    w = -jax.nn.softplus(-(w0 + _dot_hi(jnp.tanh(lora), wd_ref[...]))) - 0.5
    wl = -jnp.exp(w)
    alr = jax.nn.sigmoid(a0 + _dot(lora, wa_ref[...]))
    gate = _dot(jax.nn.sigmoid(lora), wg_ref[...])
    kk = k * k_k
    k2 = k * (1.0 + (alr - 1.0) * k_a)
    rkr = r * k2 * r_k

    ri = lax.broadcasted_iota(jnp.int32, (CHUNK, CHUNK), 0)
    ci = lax.broadcasted_iota(jnp.int32, (CHUNK, CHUNK), 1)
    low_incl = (ci <= ri)[None]
    low_strict = (ci < ri)[None]
    tril_b = jnp.where(ci <= ri, 1.0, 0.0).astype(BF16)
    eye_k = (lax.broadcasted_iota(jnp.int32, (HEAD_PAD, HEAD_PAD), 0)
             == lax.broadcasted_iota(jnp.int32, (HEAD_PAD, HEAD_PAD), 1))[None]
    n_sq = int(math.log2(CHUNK))

    parts = _split3(wl)
    cums = []
    for c in range(nc):
        rows = slice(c * CHUNK, (c + 1) * CHUNK)
        cums.append(sum(jnp.dot(tril_b, p[rows], preferred_element_type=F32) for p in parts))
    cum_all = jnp.concatenate(cums, axis=0)

    lane = lax.broadcasted_iota(jnp.int32, (tb, HEAD_PAD), 1)
    real = lane < RWKV_HEAD_DIM
    to3 = lambda t: t.reshape(nc, CHUNK, HEAD_PAD)
    for hd in range(RWKV_HEADS):
        cs = slice(hd * HEAD_PAD, (hd + 1) * HEAD_PAD)
        kk_h = kk[:, cs]
        nrm = jnp.sqrt(jnp.sum(kk_h * kk_h, axis=-1, keepdims=True))
        kk_h = kk_h / jnp.maximum(nrm, 1e-12)
        v_h = v[:, cs]
        bonus = jnp.sum(rkr[:, cs], axis=-1, keepdims=True) * v_h

        wl3, cum = to3(wl[:, cs]), to3(cum_all[:, cs])
        rr, kc, vc = to3(r[:, cs]), to3(k2[:, cs]), to3(v_h)
        ac, bc = to3(-kk_h), to3(kk_h * alr[:, cs])
        last = cum[:, CHUNK - 1:CHUNK, :]
        e_neg = jnp.exp(-cum)
        e_last = jnp.exp(last - cum)
        ah = ac * jnp.exp(cum - wl3)
        rh = rr * jnp.exp(cum)
        ar = jnp.concatenate([ah, rh], axis=1)
        gb = _bmm_nt(ar, bc * e_neg)
        gk = _bmm_nt(ar, kc * e_neg)
        a_ab = jnp.where(low_strict, gb[:, :CHUNK], 0.0)
        r_b = jnp.where(low_incl, gb[:, CHUNK:], 0.0)
        a_ak = jnp.where(low_strict, gk[:, :CHUNK], 0.0)
        r_kk = jnp.where(low_incl, gk[:, CHUNK:], 0.0)
        xx = jnp.concatenate([_bmm(a_ak, vc), ah], axis=2)
        pp = a_ab
        for j in range(n_sq):
            xx = xx + _bmm(pp, xx)
            if j + 1 < n_sq:
                pp = _bmm(pp, pp)
        rbx = _bmm(r_b, xx)
        q = rh + rbx[:, :, HEAD_PAD:]
        y0 = rbx[:, :, :HEAD_PAD] + _bmm(r_kk, vc)
        btx = _bmm(jnp.swapaxes(bc * e_last, 1, 2), xx)
        m = jnp.where(eye_k, jnp.exp(last), 0.0) + btx[:, :, HEAD_PAD:]
        n0 = btx[:, :, :HEAD_PAD] + _bmm(jnp.swapaxes(kc * e_last, 1, 2), vc)
        qm = jnp.concatenate([q, m], axis=1).astype(BF16)

        hs = hstate_s[hd]
        ys = []
        for c in range(nc):
            res = jnp.dot(qm[c], hs.astype(BF16), preferred_element_type=F32)
            ys.append(res[:CHUNK] + y0[c])
            hs = res[CHUNK:] + n0[c]
        hstate_s[hd] = hs
        y = jnp.concatenate(ys, axis=0)

        mean = jnp.sum(y, axis=-1, keepdims=True) / RWKV_HEAD_DIM
        dlt = jnp.where(real, y - mean, 0.0)
        var = jnp.sum(dlt * dlt, axis=-1, keepdims=True) / RWKV_HEAD_DIM
        yn = dlt * lax.rsqrt(var + RWKV_GN_EPS) * gn_g[:, cs] + gn_b[:, cs]
        oa_ref[:, cs] = ((yn + bonus) * gate[:, cs]).astype(BF16)


def _rwkv(pa, mu, vecs, wd, wa, wg, *, tb):
    b, lp, _ = pa.shape
    tile = lambda bi, i: (bi, i, 0)
    const = lambda bi, i: (0, 0)
    halo = lambda bi, i: (bi, jnp.maximum(i * (tb // SUBLANES) - 1, 0), 0)
    return pl.pallas_call(
        _rwkv_kernel,
        grid=(b, lp // tb),
        in_specs=[
            pl.BlockSpec((None, tb, PA_COLS), tile),
            pl.BlockSpec((None, SUBLANES, PA_COLS), halo),
            pl.BlockSpec((1, PA_COLS), const),
            pl.BlockSpec((SUBLANES, RWKV_PAD), const),
            pl.BlockSpec((LORA_COLS, RWKV_PAD), const),
            pl.BlockSpec((LORA_COLS, RWKV_PAD), const),
            pl.BlockSpec((LORA_COLS, RWKV_PAD), const),
        ],
        out_specs=pl.BlockSpec((None, tb, RWKV_PAD), tile),
        out_shape=jax.ShapeDtypeStruct((b, lp, RWKV_PAD), BF16),
        scratch_shapes=[pltpu.VMEM((RWKV_HEADS, HEAD_PAD, HEAD_PAD), F32)],
        compiler_params=_params(("arbitrary", "arbitrary")),
        name="rwkv7",
    )(pa, pa, mu, vecs, wd, wa, wg)


def _attn_kernel(q_ref, k_ref, vt_ref, lamv_ref, sg_ref, o_ref, acc_s, *, lam_init):
    tq = o_ref.shape[0]
    tk = vt_ref.shape[2]
    qi = pl.program_id(2)
    q = q_ref[...].reshape(2 * tq, LANES)
    acc_s[...] = jnp.zeros_like(acc_s)

    def tile(ki, carry, masked):
        m_prev, l_prev = carry
        k = k_ref[pl.ds(pl.multiple_of(ki * tk, tk), tk), :]
        st = lax.dot_general(k, q, (((1,), (1,)), ((), ())), preferred_element_type=F32)
        if masked:
            kr = lax.broadcasted_iota(jnp.int32, (tk, 2 * tq), 0)
            qc = lax.broadcasted_iota(jnp.int32, (tk, 2 * tq), 1)
            qc = jnp.where(qc >= tq, qc - tq, qc)
            st = jnp.where(kr <= qc, st, NEG_INF)
        m_new = jnp.maximum(m_prev, jnp.max(st, axis=0, keepdims=True))
        alpha = jnp.exp(m_prev - m_new)
        p = jnp.exp(st - m_new)
        l_new = alpha * l_prev + jnp.sum(p, axis=0, keepdims=True)
        acc_s[...] = alpha * acc_s[...] + jnp.dot(vt_ref[ki], p.astype(BF16),
                                                  preferred_element_type=F32)
        return m_new, l_new

    init = (jnp.full((1, 2 * tq), NEG_INF, F32), jnp.zeros((1, 2 * tq), F32))
    carry = lax.fori_loop(0, qi, lambda ki, c: tile(ki, c, False), init)
    _, l_fin = tile(qi, carry, True)

    ot = acc_s[...] / l_fin
    lv = lamv_ref[...]
    lam = (jnp.exp(jnp.sum(lv[0:1] * lv[1:2], axis=-1, keepdims=True))
           - jnp.exp(jnp.sum(lv[2:3] * lv[3:4], axis=-1, keepdims=True)) + lam_init)
    od = (ot[:, :tq] - lam * ot[:, tq:]).T
    ms = jnp.mean(od * od, axis=-1, keepdims=True)
    o_ref[...] = (od * lax.rsqrt(ms + SUBLN_EPS) * sg_ref[...] * (1.0 - lam_init)).astype(BF16)


def _attention(qs, kh, vt, lamv, sg, *, tq, lam_init):
    _, b, lp, _ = qs.shape
    nq = lp // tq
    const = lambda bi, hd, qi: (0, 0)
    return pl.pallas_call(
        functools.partial(_attn_kernel, lam_init=lam_init),
        grid=(b, DIFF_HEADS, nq),
        in_specs=[
            pl.BlockSpec((2, None, tq, LANES), lambda bi, hd, qi: (0, bi, qi, hd)),
            pl.BlockSpec((None, lp, LANES), lambda bi, hd, qi: (bi, 0, hd)),
            pl.BlockSpec((None, nq, None, LANES, tq), lambda bi, hd, qi: (bi, 0, hd, 0, 0)),
            pl.BlockSpec((SUBLANES, LANES), const),
            pl.BlockSpec((1, LANES), const),
        ],
        out_specs=pl.BlockSpec((None, tq, LANES), lambda bi, hd, qi: (bi, qi, hd)),
        out_shape=jax.ShapeDtypeStruct((b, lp, DIFF_WIDTH), BF16),
        scratch_shapes=[pltpu.VMEM((LANES, 2 * tq), F32)],
        compiler_params=_params(("arbitrary", "arbitrary", "arbitrary")),
        name="diffattn",
    )(qs, kh, vt, lamv, sg)


def _outproj_kernel(h_ref, oa_ref, ob_ref, pc_ref, pch_ref, cw_ref, wa_ref, wb_ref, wc_ref,
                    o_ref, *, tiles_per_batch):
    tm = h_ref.shape[0]
    first = (pl.program_id(0) % tiles_per_batch) == 0
    pc = pc_ref[...]
    bg = pc[:, :CONV_WIDTH]
    z = pc[:, CONV_WIDTH:2 * CONV_WIDTH] * pc[:, 2 * CONV_WIDTH:]
    ph = pch_ref[...]
    zh = jnp.where(first, 0.0, ph[:, CONV_WIDTH:2 * CONV_WIDTH] * ph[:, 2 * CONV_WIDTH:])
    row = lax.broadcasted_iota(jnp.int32, (tm, 1), 0)
    z1 = jnp.where(row == 0, zh[SUBLANES - 1:SUBLANES], pltpu.roll(z, 1, 0))
    z2 = jnp.where(row == 0, zh[SUBLANES - 2:SUBLANES - 1],
                   jnp.where(row == 1, zh[SUBLANES - 1:SUBLANES], pltpu.roll(z, 2, 0)))
    cw = cw_ref[...]
    y = cw[0:1] * z2 + cw[1:2] * z1 + cw[2:3] * z
    oc = (bg * y).astype(BF16)
    o_ref[...] = (h_ref[...]
                  + jnp.dot(oa_ref[...], wa_ref[...], preferred_element_type=F32)
                  + jnp.dot(ob_ref[...], wb_ref[...], preferred_element_type=F32)
                  + jnp.dot(oc, wc_ref[...], preferred_element_type=F32))


def _outproj(h, oa, ob, pc, cw, wa, wb, wc, *, tm, lp):
    r, d = h.shape
    row = lambda i: (i, 0)
    const = lambda i: (0, 0)
    halo = lambda i: (jnp.maximum(i * (tm // SUBLANES) - 1, 0), 0)
    return pl.pallas_call(
        functools.partial(_outproj_kernel, tiles_per_batch=lp // tm),
        grid=(r // tm,),
        in_specs=[
            pl.BlockSpec((tm, d), row),
            pl.BlockSpec((tm, RWKV_PAD), row),
            pl.BlockSpec((tm, DIFF_WIDTH), row),
            pl.BlockSpec((tm, PC_COLS), row),
            pl.BlockSpec((SUBLANES, PC_COLS), halo),
            pl.BlockSpec((SUBLANES, CONV_WIDTH), const),
            pl.BlockSpec(wa.shape, const),
            pl.BlockSpec(wb.shape, const),
            pl.BlockSpec(wc.shape, const),
        ],
        out_specs=pl.BlockSpec((tm, d), row),
        out_shape=jax.ShapeDtypeStruct((r, d), F32),
        compiler_params=_params(("arbitrary",)),
        name="outproj",
    )(h, oa, ob, pc, pc, cw, wa, wb, wc)


def _swiglu_block(xb, wg_ref, wu_ref, wd_ref):
    d_ff = wg_ref.shape[1]
    acc = jnp.zeros((xb.shape[0], wd_ref.shape[1]), F32)
    for c in range(d_ff // MXU_DIM):
        cs = slice(c * MXU_DIM, (c + 1) * MXU_DIM)
        gt = jnp.dot(xb, wg_ref[:, cs], preferred_element_type=F32)
        up = jnp.dot(xb, wu_ref[:, cs], preferred_element_type=F32)
        act = (jax.nn.silu(gt) * up).astype(BF16)
        acc = acc + jnp.dot(act, wd_ref[cs, :], preferred_element_type=F32)
    return acc


def _rms(x, g):
    ms = jnp.mean(x * x, axis=-1, keepdims=True)
    return x * lax.rsqrt(ms + NORM_EPS) * g


def _ffn_kernel(h_ref, g_ref, wg_ref, wu_ref, wd_ref, o_ref):
    x = h_ref[...]
    xn = _rms(x, g_ref[...]).astype(BF16)
    o_ref[...] = x + _swiglu_block(xn, wg_ref, wu_ref, wd_ref)


def _dense_ffn(h, g, wg, wu, wd, *, tm):
    r, d = h.shape
    row = lambda i: (i, 0)
    const = lambda i: (0, 0)
    return pl.pallas_call(
        _ffn_kernel,
        grid=(r // tm,),
        in_specs=[
            pl.BlockSpec((tm, d), row),
            pl.BlockSpec((1, d), const),
            pl.BlockSpec(wg.shape, const),
            pl.BlockSpec(wu.shape, const),
            pl.BlockSpec(wd.shape, const),
        ],
        out_specs=pl.BlockSpec((tm, d), row),
        out_shape=jax.ShapeDtypeStruct((r, d), F32),
        compiler_params=_params(("arbitrary",)),
        name="dense_ffn",
    )(h, g, wg, wu, wd)


def _router_kernel(h_ref, g_ref, rw_ref, hn_ref, route_ref):
    tm = h_ref.shape[0]
    xn = _rms(h_ref[...], g_ref[...])
    hn_ref[...] = xn
    logits = _dot_f32(xn, rw_ref[...])
    lane = lax.broadcasted_iota(jnp.int32, (tm, LANES), 1)
    logits = jnp.where(lane < N_EXPERTS, logits, -jnp.inf)
    m1 = jnp.max(logits, axis=-1, keepdims=True)
    i1 = jnp.min(jnp.where(logits == m1, lane, LANES), axis=-1, keepdims=True)
    rest = jnp.where(lane == i1, -jnp.inf, logits)
    m2 = jnp.max(rest, axis=-1, keepdims=True)
    i2 = jnp.min(jnp.where(rest == m2, lane, LANES), axis=-1, keepdims=True)
    e2 = jnp.exp(m2 - m1)
    den = 1.0 + e2
    route_ref[...] = jnp.where(lane == 0, i1.astype(F32),
                     jnp.where(lane == 1, i2.astype(F32),
                     jnp.where(lane == 2, 1.0 / den,
                     jnp.where(lane == 3, e2 / den, 0.0))))


def _router(h, g, rw, *, tm):
    r, d = h.shape
    row = lambda i: (i, 0)
    const = lambda i: (0, 0)
    return pl.pallas_call(
        _router_kernel,
        grid=(r // tm,),
        in_specs=[pl.BlockSpec((tm, d), row), pl.BlockSpec((1, d), const),
                  pl.BlockSpec((d, LANES), const)],
        out_specs=[pl.BlockSpec((tm, d), row), pl.BlockSpec((tm, LANES), row)],
        out_shape=[jax.ShapeDtypeStruct((r, d), F32), jax.ShapeDtypeStruct((r, LANES), F32)],
        compiler_params=_params(("arbitrary",)),
        name="router",
    )(h, g, rw)


def _row_copy(src, dst, sem, src_row, dst_row):
    return pltpu.make_async_copy(src.at[pl.ds(src_row, 1), :], dst.at[pl.ds(dst_row, 1), :], sem)


DMA_UNROLL = 8


def _moe_kernel(bexp_ref, nvalid_ref, rsrc_ref, rdst_ref, hn_hbm, wg_ref, wu_ref, wd_ref, y_hbm,
                xbuf, ybuf, gsem, ssem):
    del bexp_ref
    blk = xbuf.shape[1]
    i = pl.program_id(0)
    nb = pl.num_programs(0)
    slot = i % 2

    def for_rows(n, fn):
        def body(j, c):
            for u in range(DMA_UNROLL):
                fn(j * DMA_UNROLL + u)
            return c
        full = n // DMA_UNROLL
        lax.fori_loop(0, full, body, 0)
        if not isinstance(n, int):
            def tail(j, c):
                fn(j)
                return c
            lax.fori_loop(full * DMA_UNROLL, n, tail, 0)

    def gather_start(block, s):
        for_rows(blk, lambda j: _row_copy(hn_hbm, xbuf.at[s], gsem.at[s],
                                          rsrc_ref[block * blk + j], j).start())

    def gather_wait(s):
        for_rows(blk, lambda j: _row_copy(hn_hbm, xbuf.at[s], gsem.at[s], 0, j).wait())

    def scatter_start(block, s):
        for_rows(nvalid_ref[block], lambda j: _row_copy(ybuf.at[s], y_hbm, ssem.at[s], j,
                                                        rdst_ref[block * blk + j]).start())

    def scatter_wait(block, s):
        for_rows(nvalid_ref[block], lambda j: _row_copy(ybuf.at[s], y_hbm, ssem.at[s], j, 0).wait())

    @pl.when(i == 0)
    def _():
        gather_start(0, 0)

    @pl.when(i + 1 < nb)
    def _():
        gather_start(i + 1, 1 - slot)

    gather_wait(slot)

    @pl.when(i >= 2)
    def _():
        scatter_wait(i - 2, slot)

    ybuf[slot] = _swiglu_block(xbuf[slot].astype(BF16), wg_ref, wu_ref, wd_ref)
    scatter_start(i, slot)

    @pl.when(i == nb - 1)
    def _():
        scatter_wait(i, slot)

        @pl.when(nb > 1)
        def _():
            scatter_wait(i - 1, 1 - slot)


def _moe_experts(block_expert, n_valid, row_src, row_dst, hn, wg, wu, wd):
    nb = block_expert.shape[0]
    n_out_rows = TOP_K * hn.shape[0]
    d = hn.shape[1]
    d_ff = wg.shape[2]
    wmap = lambda i, be, nv, rs, rd: (be[i], 0, 0)
    grid_spec = pltpu.PrefetchScalarGridSpec(
        num_scalar_prefetch=4,
        grid=(nb,),
        in_specs=[
            pl.BlockSpec(memory_space=pl.ANY),
            pl.BlockSpec((None, d, d_ff), wmap),
            pl.BlockSpec((None, d, d_ff), wmap),
            pl.BlockSpec((None, d_ff, d), wmap),
        ],
        out_specs=pl.BlockSpec(memory_space=pl.ANY),
        scratch_shapes=[pltpu.VMEM((2, MOE_BLOCK, d), F32), pltpu.VMEM((2, MOE_BLOCK, d), F32),
                        pltpu.SemaphoreType.DMA((2,)), pltpu.SemaphoreType.DMA((2,))],
    )
    return pl.pallas_call(
        _moe_kernel,
        grid_spec=grid_spec,
        out_shape=jax.ShapeDtypeStruct((n_out_rows, d), F32),
        compiler_params=_params(("arbitrary",)),
        name="moe_experts",
    )(block_expert, n_valid, row_src, row_dst, hn, wg, wu, wd)


def _combine_kernel(h_ref, route_ref, y0_ref, y1_ref, o_ref):
    route = route_ref[...]
    o_ref[...] = h_ref[...] + route[:, 2:3] * y0_ref[...] + route[:, 3:4] * y1_ref[...]


def _moe_combine(h, route, y, *, tm):
    r, d = h.shape
    nt = r // tm
    row = lambda i: (i, 0)
    return pl.pallas_call(
        _combine_kernel,
        grid=(nt,),
        in_specs=[
            pl.BlockSpec((tm, d), row),
            pl.BlockSpec((tm, LANES), row),
            pl.BlockSpec((tm, d), row),
            pl.BlockSpec((tm, d), lambda i: (i + nt, 0)),
        ],
        out_specs=pl.BlockSpec((tm, d), row),
        out_shape=jax.ShapeDtypeStruct((r, d), F32),
        compiler_params=_params(("arbitrary",)),
        name="moe_combine",
    )(h, route, y, y)


def _routing_tables(route):
    r = route.shape[0]
    tk = r * TOP_K
    flat_e = route[:, :TOP_K].reshape(tk).astype(jnp.int32)
    onehot = (flat_e[:, None] == jnp.arange(N_EXPERTS, dtype=jnp.int32)[None, :]).astype(jnp.int32)
    csum = jnp.cumsum(onehot, axis=0)
    rank = jnp.sum(csum * onehot, axis=1) - 1
    counts = csum[-1]
    padded = (counts + MOE_BLOCK - 1) // MOE_BLOCK * MOE_BLOCK
    pend = jnp.cumsum(padded)
    pstart = pend - padded
    dest = pstart[flat_e] + rank
    nb = -(-tk // MOE_BLOCK) + N_EXPERTS
    n_slots = nb * MOE_BLOCK
    pair = jnp.full((n_slots,), -1, jnp.int32).at[dest].set(
        jnp.arange(tk, dtype=jnp.int32), unique_indices=True, mode="promise_in_bounds")
    used = pair >= 0
    row_src = jnp.where(used, pair // TOP_K, 0)
    row_dst = jnp.where(used, (pair % TOP_K) * r + pair // TOP_K, 0)
    block_start = jnp.arange(nb, dtype=jnp.int32) * MOE_BLOCK
    block_expert = jnp.minimum(jnp.searchsorted(pend, block_start, side='right'),
                               N_EXPERTS - 1).astype(jnp.int32)
    n_valid = jnp.clip(counts[block_expert] - (block_start - pstart[block_expert]), 0, MOE_BLOCK)
    return block_expert, n_valid.astype(jnp.int32), row_src, row_dst


def _pad_heads(t, axis):
    axis = axis % t.ndim
    shp = t.shape
    t = t.reshape(shp[:axis] + (RWKV_HEADS, RWKV_HEAD_DIM) + shp[axis + 1:])
    pad = [(0, 0)] * t.ndim
    pad[axis + 1] = (0, HEAD_PAD - RWKV_HEAD_DIM)
    t = jnp.pad(t, pad)
    return t.reshape(shp[:axis] + (RWKV_PAD,) + shp[axis + 1:])


def _rope_tables(lp):
    half = ROPE_DIM // 2
    inv_freq = ROPE_THETA ** (-jnp.arange(0, ROPE_DIM, 2, dtype=F32) / ROPE_DIM)
    ang = jnp.arange(lp, dtype=F32)[:, None] * inv_freq[None, :]
    cos, sin = jnp.cos(ang), jnp.sin(ang)
    ones = jnp.ones((lp, DIFF_QK_DIM - ROPE_DIM), F32)
    zeros = jnp.zeros((lp, DIFF_QK_DIM - ROPE_DIM), F32)
    zh = jnp.zeros((lp, half), F32)
    rc = jnp.concatenate([cos, cos, ones], axis=1)
    rs1 = jnp.concatenate([-sin, zh, zeros], axis=1)
    rs2 = jnp.concatenate([zh, sin, zeros], axis=1)
    tile2 = lambda a: jnp.concatenate([a, a], axis=1)
    return tile2(rc), tile2(rs1), tile2(rs2)


def kernel(x, meta_tokens, mix_norm_g, w_in, tm_mu, tm_w0, tm_w_decay_up, tm_a0, tm_w_a_up, tm_w_g_up, tm_k_k, tm_k_a, tm_r_k, tm_gn_g, tm_gn_b, da_q_norm_g, da_k_norm_g, da_lambda_q1, da_lambda_k1, da_lambda_q2, da_lambda_k2, da_subln_g, sc_conv_w, w_out, ffn_norm_g, ffn_w_gate, ffn_w_up, ffn_w_down, router_w, moe_w_gate, moe_w_up, moe_w_down):
    b, seq, d = x.shape
    depth = w_in.shape[0]
    l = N_META + seq
    lp = -(-l // LANES) * LANES
    tm = _row_tile(lp)
    r = b * lp
    s1 = RWKV_WIDTH

    meta = jnp.broadcast_to(meta_tokens.astype(x.dtype)[None], (b, N_META, d))
    h = jnp.concatenate([meta, x, jnp.zeros((b, lp - l, d), x.dtype)], axis=1).reshape(r, d)
    rc, rs1, rs2 = _rope_tables(lp)

    for i in range(depth):
        lam_init = 0.8 - 0.6 * math.exp(-0.3 * i)
        wi = w_in[i]
        w_cat = jnp.concatenate(
            [_pad_heads(wi[:, 0:s1], 1), _pad_heads(wi[:, s1:2 * s1], 1),
             _pad_heads(wi[:, 2 * s1:3 * s1], 1), wi[:, 3 * s1:]], axis=1).astype(BF16)
        qg = jnp.tile(da_q_norm_g[i], 2)[None]
        kg = jnp.tile(da_k_norm_g[i], 2)[None]
        pa, qs, kh, vt, pc = _inproj(h, mix_norm_g[i][None], w_cat, rc, rs1, rs2, qg, kg,
                                     tm=tm, lp=lp)

        mu = tm_mu[i]
        mu_p = jnp.concatenate([_pad_heads(mu[0:s1], 0), _pad_heads(mu[s1:2 * s1], 0),
                                _pad_heads(mu[2 * s1:3 * s1], 0), mu[3 * s1:]])[None]
        vecs = jnp.stack([_pad_heads(tm_w0[i], 0), _pad_heads(tm_a0[i], 0),
                          _pad_heads(tm_k_k[i], 0), _pad_heads(tm_k_a[i], 0),
                          _pad_heads(tm_r_k[i].reshape(s1), 0), _pad_heads(tm_gn_g[i], 0),
                          _pad_heads(tm_gn_b[i], 0), jnp.zeros((RWKV_PAD,), F32)])
        zl = lambda n: jnp.zeros((n, RWKV_PAD), F32)
        wd_p = jnp.concatenate([_pad_heads(tm_w_decay_up[i], 1), zl(AAA_LORA + GATE_LORA)], axis=0)
        wa_p = jnp.concatenate([zl(DECAY_LORA), _pad_heads(tm_w_a_up[i], 1), zl(GATE_LORA)], axis=0)
        wg_p = jnp.concatenate([zl(DECAY_LORA + AAA_LORA), _pad_heads(tm_w_g_up[i], 1)], axis=0)
        oa = _rwkv(pa.reshape(b, lp, PA_COLS), mu_p, vecs, wd_p, wa_p, wg_p, tb=tm)

        lamv = jnp.zeros((SUBLANES, LANES), F32).at[0:4, :DIFF_QK_DIM].set(
            jnp.stack([da_lambda_q1[i], da_lambda_k1[i], da_lambda_q2[i], da_lambda_k2[i]]))
        ob = _attention(qs.reshape(2, b, lp, DIFF_WIDTH), kh.reshape(b, lp, DIFF_WIDTH),
                        vt.reshape(b, lp // tm, DIFF_HEADS, LANES, tm), lamv, da_subln_g[i][None],
                        tq=tm, lam_init=lam_init)

        wo = w_out[i]
        cw = jnp.zeros((SUBLANES, CONV_WIDTH), F32).at[:CONV_K].set(sc_conv_w[i])
        h = _outproj(h, oa.reshape(r, RWKV_PAD), ob.reshape(r, DIFF_WIDTH), pc, cw,
                     _pad_heads(wo[:s1], 0).astype(BF16), wo[s1:s1 + DIFF_WIDTH].astype(BF16),
                     wo[s1 + DIFF_WIDTH:].astype(BF16), tm=tm, lp=lp)

        j = i // 2
        if i % 2 == 0:
            h = _dense_ffn(h, ffn_norm_g[i][None], ffn_w_gate[j].astype(BF16),
                           ffn_w_up[j].astype(BF16), ffn_w_down[j].astype(BF16), tm=tm)
        else:
            rw = jnp.zeros((d, LANES), F32).at[:, :N_EXPERTS].set(router_w[j])
            hn, route = _router(h, ffn_norm_g[i][None], rw, tm=tm)
            block_expert, n_valid, row_src, row_dst = _routing_tables(route)
            y = _moe_experts(block_expert, n_valid, row_src, row_dst, hn,
                             moe_w_gate[j].astype(BF16), moe_w_up[j].astype(BF16),
                             moe_w_down[j].astype(BF16))
            h = _moe_combine(h, route, y, tm=tm)

    return h.reshape(b, lp, d)[:, N_META:l]
```

```python
import functools
import math

import jax
import jax.numpy as jnp
from jax import lax
from jax.experimental import pallas as pl
from jax.experimental.pallas import tpu as pltpu

N_META = 16
RWKV_HEAD_DIM = 64
RWKV_HEADS = 4
RWKV_WIDTH = RWKV_HEADS * RWKV_HEAD_DIM
DECAY_LORA = 32
AAA_LORA = 32
GATE_LORA = 64
LORA_COLS = DECAY_LORA + AAA_LORA + GATE_LORA
DIFF_QK_DIM = 64
DIFF_V_DIM = 128
DIFF_HEADS = 4
DIFF_WIDTH = DIFF_HEADS * DIFF_V_DIM
CONV_WIDTH = 256
CONV_K = 3
ROPE_THETA = 500000.0
ROPE_DIM = DIFF_QK_DIM // 4
N_EXPERTS = 8
TOP_K = 2
MOE_BLOCK = 256
NORM_EPS = 1e-6
RWKV_GN_EPS = 64e-5
SUBLN_EPS = 1e-5
NEG_INF = -1e30

LANES = 128
SUBLANES = 8
MXU_DIM = 256
VMEM_LIMIT_BYTES = 56 * 1024 * 1024

HEAD_PAD = LANES
RWKV_PAD = RWKV_HEADS * HEAD_PAD
PA_COLS = 3 * RWKV_PAD + LORA_COLS
PB_COLS = 3 * DIFF_WIDTH
PC_COLS = 3 * CONV_WIDTH
CHUNK = 64

F32 = jnp.float32
BF16 = jnp.bfloat16
HIGHEST = lax.Precision.HIGHEST


def _dot(a, b):
    return jnp.dot(a.astype(BF16), b.astype(BF16), preferred_element_type=F32)


def _dot_nt(a, b):
    return lax.dot_general(a.astype(BF16), b.astype(BF16), (((1,), (1,)), ((), ())),
                           preferred_element_type=F32)


def _dot_f32(a, b):
    return jnp.dot(a, b, preferred_element_type=F32, precision=HIGHEST)


def _row_tile(lp):
    for t in (384, 256, 128):
        if lp % t == 0:
            return t
    raise ValueError(f"padded length {lp} is not a multiple of {LANES}")


def _params(sem):
    return pltpu.CompilerParams(dimension_semantics=sem, vmem_limit_bytes=VMEM_LIMIT_BYTES)


def _inproj_kernel(h_ref, g_ref, w_ref, rc_ref, rs1_ref, rs2_ref, qg_ref, kg_ref,
                   pa_ref, qs_ref, kh_ref, vt_ref, pc_ref):
    tm = h_ref.shape[0]
    x = h_ref[...]
    ms = jnp.mean(x * x, axis=-1, keepdims=True)
    xn = (x * lax.rsqrt(ms + NORM_EPS) * g_ref[...]).astype(BF16)
    pa_ref[...] = jnp.dot(xn, w_ref[:, :PA_COLS], preferred_element_type=F32)
    pc_ref[...] = jnp.dot(xn, w_ref[:, PA_COLS + PB_COLS:], preferred_element_type=F32)
    pb = jnp.dot(xn, w_ref[:, PA_COLS:PA_COLS + PB_COLS], preferred_element_type=F32)

    lane = lax.broadcasted_iota(jnp.int32, (tm, LANES), 1)
    lo = lane < DIFF_QK_DIM
    rc, rs1, rs2 = rc_ref[...], rs1_ref[...], rs2_ref[...]

    def norm_rope(t, g):
        t2 = t * t
        ss_lo = jnp.sum(jnp.where(lo, t2, 0.0), axis=-1, keepdims=True)
        ss_hi = jnp.sum(jnp.where(lo, 0.0, t2), axis=-1, keepdims=True)
        inv = jnp.where(lo, lax.rsqrt(ss_lo / DIFF_QK_DIM + NORM_EPS),
                        lax.rsqrt(ss_hi / DIFF_QK_DIM + NORM_EPS))
        tn = t * inv * g
        half = ROPE_DIM // 2
        return tn * rc + pltpu.roll(tn, LANES - half, 1) * rs1 + pltpu.roll(tn, half, 1) * rs2

    for hd in range(DIFF_HEADS):
        cs = slice(hd * LANES, (hd + 1) * LANES)
        q = norm_rope(pb[:, cs], qg_ref[...]) * (DIFF_QK_DIM ** -0.5)
        qs_ref[0, :, cs] = jnp.where(lo, q, 0.0).astype(BF16)
        qs_ref[1, :, cs] = jnp.where(lo, 0.0, q).astype(BF16)
        kcs = slice(DIFF_WIDTH + hd * LANES, DIFF_WIDTH + (hd + 1) * LANES)
        kh_ref[:, cs] = norm_rope(pb[:, kcs], kg_ref[...]).astype(BF16)
        vcs = slice(2 * DIFF_WIDTH + hd * LANES, 2 * DIFF_WIDTH + (hd + 1) * LANES)
        vt_ref[0, hd] = pb[:, vcs].T.astype(BF16)


def _inproj(h, g, w, rc, rs1, rs2, qg, kg, *, tm, lp):
    r, d = h.shape
    nt_b = lp // tm
    row = lambda i: (i, 0)
    const = lambda i: (0, 0)
    rope = lambda i: (i % nt_b, 0)
    return pl.pallas_call(
        _inproj_kernel,
        grid=(r // tm,),
        in_specs=[
            pl.BlockSpec((tm, d), row),
            pl.BlockSpec((1, d), const),
            pl.BlockSpec(w.shape, const),
            pl.BlockSpec((tm, LANES), rope),
            pl.BlockSpec((tm, LANES), rope),
            pl.BlockSpec((tm, LANES), rope),
            pl.BlockSpec((1, LANES), const),
            pl.BlockSpec((1, LANES), const),
        ],
        out_specs=[
            pl.BlockSpec((tm, PA_COLS), row),
            pl.BlockSpec((2, tm, DIFF_WIDTH), lambda i: (0, i, 0)),
            pl.BlockSpec((tm, DIFF_WIDTH), row),
            pl.BlockSpec((1, DIFF_HEADS, LANES, tm), lambda i: (i, 0, 0, 0)),
            pl.BlockSpec((tm, PC_COLS), row),
        ],
        out_shape=[
            jax.ShapeDtypeStruct((r, PA_COLS), F32),
            jax.ShapeDtypeStruct((2, r, DIFF_WIDTH), BF16),
            jax.ShapeDtypeStruct((r, DIFF_WIDTH), BF16),
            jax.ShapeDtypeStruct((r // tm, DIFF_HEADS, LANES, tm), BF16),
            jax.ShapeDtypeStruct((r, PC_COLS), F32),
        ],
        compiler_params=_params(("arbitrary",)),
        name="inproj",
    )(h, g, w, rc, rs1, rs2, qg, kg)


def _split3(x):
    hi = x.astype(BF16)
    r1 = x - hi.astype(F32)
    mid = r1.astype(BF16)
    lo = (r1 - mid.astype(F32)).astype(BF16)
    return hi, mid, lo


def _dot_hi(a, b):
    ah = a.astype(BF16)
    al = (a - ah.astype(F32)).astype(BF16)
    bh = b.astype(BF16)
    bl = (b - bh.astype(F32)).astype(BF16)
    d = lambda p, q: jnp.dot(p, q, preferred_element_type=F32)
    return d(ah, bh) + d(ah, bl) + d(al, bh)


def _bmm(a, b):
    return jnp.einsum('cik,ckj->cij', a.astype(BF16), b.astype(BF16), preferred_element_type=F32)


def _bmm_nt(a, b):
    return jnp.einsum('cik,cjk->cij', a.astype(BF16), b.astype(BF16), preferred_element_type=F32)


def _rwkv_kernel(pa_ref, halo_ref, mu_ref, vec_ref, wd_ref, wa_ref, wg_ref, oa_ref, hstate_s):
    tb = pa_ref.shape[0]
    nc = tb // CHUNK
    i = pl.program_id(1)

    @pl.when(i == 0)
    def _():
        hstate_s[...] = jnp.zeros_like(hstate_s)

    x = pa_ref[...]
    prev_row = jnp.where(i == 0, 0.0, halo_ref[SUBLANES - 1:SUBLANES, :])
    row = lax.broadcasted_iota(jnp.int32, (tb, 1), 0)
    prev = jnp.where(row == 0, prev_row, pltpu.roll(x, 1, 0))
    hx = x + (prev - x) * mu_ref[...]
    r = hx[:, 0:RWKV_PAD]
    k = hx[:, RWKV_PAD:2 * RWKV_PAD]
    v = hx[:, 2 * RWKV_PAD:3 * RWKV_PAD]
    lora = hx[:, 3 * RWKV_PAD:]
    w0, a0, k_k, k_a = vec_ref[0:1, :], vec_ref[1:2, :], vec_ref[2:3, :], vec_ref[3:4, :]
    r_k, gn_g, gn_b = vec_ref[4:5, :], vec_ref[5:6, :], vec_ref[6:7, :]

    w = -jax.nn.softplus(-(w0 + _dot_hi(jnp.tanh(lora), wd_ref[...]))) - 0.5
    wl = -jnp.exp(w)
    alr = jax.nn.sigmoid(a0 + _dot(lora, wa_ref[...]))
    gate = _dot(jax.nn.sigmoid(lora), wg_ref[...])
    kk = k * k_k
    k2 = k * (1.0 + (alr - 1.0) * k_a)
    rkr = r * k2 * r_k

    ri = lax.broadcasted_iota(jnp.int32, (CHUNK, CHUNK), 0)
    ci = lax.broadcasted_iota(jnp.int32, (CHUNK, CHUNK), 1)
    low_incl = (ci <= ri)[None]
    low_strict = (ci < ri)[None]
    tril_b = jnp.where(ci <= ri, 1.0, 0.0).astype(BF16)
    eye_k = (lax.broadcasted_iota(jnp.int32, (HEAD_PAD, HEAD_PAD), 0)
             == lax.broadcasted_iota(jnp.int32, (HEAD_PAD, HEAD_PAD), 1))[None]
    n_sq = int(math.log2(CHUNK))

    parts = _split3(wl)
    cums = []
    for c in range(nc):
        rows = slice(c * CHUNK, (c + 1) * CHUNK)
        cums.append(sum(jnp.dot(tril_b, p[rows], preferred_element_type=F32) for p in parts))
    cum_all = jnp.concatenate(cums, axis=0)

    lane = lax.broadcasted_iota(jnp.int32, (tb, HEAD_PAD), 1)
    real = lane < RWKV_HEAD_DIM
    to3 = lambda t: t.reshape(nc, CHUNK, HEAD_PAD)
    for hd in range(RWKV_HEADS):
        cs = slice(hd * HEAD_PAD, (hd + 1) * HEAD_PAD)
        kk_h = kk[:, cs]
        nrm = jnp.sqrt(jnp.sum(kk_h * kk_h, axis=-1, keepdims=True))
        kk_h = kk_h / jnp.maximum(nrm, 1e-12)
        v_h = v[:, cs]
        bonus = jnp.sum(rkr[:, cs], axis=-1, keepdims=True) * v_h

        wl3, cum = to3(wl[:, cs]), to3(cum_all[:, cs])
        rr, kc, vc = to3(r[:, cs]), to3(k2[:, cs]), to3(v_h)
        ac, bc = to3(-kk_h), to3(kk_h * alr[:, cs])
        last = cum[:, CHUNK - 1:CHUNK, :]
        e_neg = jnp.exp(-cum)
        e_last = jnp.exp(last - cum)
        ah = ac * jnp.exp(cum - wl3)
        rh = rr * jnp.exp(cum)
        ar = jnp.concatenate([ah, rh], axis=1)
        gb = _bmm_nt(ar, bc * e_neg)
        gk = _bmm_nt(ar, kc * e_neg)
        a_ab = jnp.where(low_strict, gb[:, :CHUNK], 0.0)
        r_b = jnp.where(low_incl, gb[:, CHUNK:], 0.0)
        a_ak = jnp.where(low_strict, gk[:, :CHUNK], 0.0)
        r_kk = jnp.where(low_incl, gk[:, CHUNK:], 0.0)
        xx = jnp.concatenate([_bmm(a_ak, vc), ah], axis=2)
        pp = a_ab
        for j in range(n_sq):
            xx = xx + _bmm(pp, xx)
            if j + 1 < n_sq:
                pp = _bmm(pp, pp)
        rbx = _bmm(r_b, xx)
        q = rh + rbx[:, :, HEAD_PAD:]
        y0 = rbx[:, :, :HEAD_PAD] + _bmm(r_kk, vc)
        btx = _bmm(jnp.swapaxes(bc * e_last, 1, 2), xx)
        m = jnp.where(eye_k, jnp.exp(last), 0.0) + btx[:, :, HEAD_PAD:]
        n0 = btx[:, :, :HEAD_PAD] + _bmm(jnp.swapaxes(kc * e_last, 1, 2), vc)
        qm = jnp.concatenate([q, m], axis=1).astype(BF16)

        hs = hstate_s[hd]
        ys = []
        for c in range(nc):
            res = jnp.dot(qm[c], hs.astype(BF16), preferred_element_type=F32)
            ys.append(res[:CHUNK] + y0[c])
            hs = res[CHUNK:] + n0[c]
        hstate_s[hd] = hs
        y = jnp.concatenate(ys, axis=0)

        mean = jnp.sum(y, axis=-1, keepdims=True) / RWKV_HEAD_DIM
        dlt = jnp.where(real, y - mean, 0.0)
        var = jnp.sum(dlt * dlt, axis=-1, keepdims=True) / RWKV_HEAD_DIM
        yn = dlt * lax.rsqrt(var + RWKV_GN_EPS) * gn_g[:, cs] + gn_b[:, cs]
        oa_ref[:, cs] = ((yn + bonus) * gate[:, cs]).astype(BF16)


def _rwkv(pa, mu, vecs, wd, wa, wg, *, tb):
    b, lp, _ = pa.shape
    tile = lambda bi, i: (bi, i, 0)
    const = lambda bi, i: (0, 0)
    halo = lambda bi, i: (bi, jnp.maximum(i * (tb // SUBLANES) - 1, 0), 0)
    return pl.pallas_call(
        _rwkv_kernel,
        grid=(b, lp // tb),
        in_specs=[
            pl.BlockSpec((None, tb, PA_COLS), tile),
            pl.BlockSpec((None, SUBLANES, PA_COLS), halo),
            pl.BlockSpec((1, PA_COLS), const),
            pl.BlockSpec((SUBLANES, RWKV_PAD), const),
            pl.BlockSpec((LORA_COLS, RWKV_PAD), const),
            pl.BlockSpec((LORA_COLS, RWKV_PAD), const),
            pl.BlockSpec((LORA_COLS, RWKV_PAD), const),
        ],
        out_specs=pl.BlockSpec((None, tb, RWKV_PAD), tile),
        out_shape=jax.ShapeDtypeStruct((b, lp, RWKV_PAD), BF16),
        scratch_shapes=[pltpu.VMEM((RWKV_HEADS, HEAD_PAD, HEAD_PAD), F32)],
        compiler_params=_params(("arbitrary", "arbitrary")),
        name="rwkv7",
    )(pa, pa, mu, vecs, wd, wa, wg)


def _attn_kernel(q_ref, k_ref, vt_ref, lamv_ref, sg_ref, o_ref, acc_s, *, lam_init):
    tq = o_ref.shape[0]
    tk = vt_ref.shape[2]
    qi = pl.program_id(2)
    q = q_ref[...].reshape(2 * tq, LANES)
    acc_s[...] = jnp.zeros_like(acc_s)

    def tile(ki, carry, masked):
        m_prev, l_prev = carry
        k = k_ref[pl.ds(pl.multiple_of(ki * tk, tk), tk), :]
        st = lax.dot_general(k, q, (((1,), (1,)), ((), ())), preferred_element_type=F32)
        if masked:
            kr = lax.broadcasted_iota(jnp.int32, (tk, 2 * tq), 0)
            qc = lax.broadcasted_iota(jnp.int32, (tk, 2 * tq), 1)
            qc = jnp.where(qc >= tq, qc - tq, qc)
            st = jnp.where(kr <= qc, st, NEG_INF)
        m_new = jnp.maximum(m_prev, jnp.max(st, axis=0, keepdims=True))
        alpha = jnp.exp(m_prev - m_new)
        p = jnp.exp(st - m_new)
        l_new = alpha * l_prev + jnp.sum(p, axis=0, keepdims=True)
        acc_s[...] = alpha * acc_s[...] + jnp.dot(vt_ref[ki], p.astype(BF16),
                                                  preferred_element_type=F32)
        return m_new, l_new

    init = (jnp.full((1, 2 * tq), NEG_INF, F32), jnp.zeros((1, 2 * tq), F32))
    carry = lax.fori_loop(0, qi, lambda ki, c: tile(ki, c, False), init)
    _, l_fin = tile(qi, carry, True)

    ot = acc_s[...] / l_fin
    lv = lamv_ref[...]
    lam = (jnp.exp(jnp.sum(lv[0:1] * lv[1:2], axis=-1, keepdims=True))
           - jnp.exp(jnp.sum(lv[2:3] * lv[3:4], axis=-1, keepdims=True)) + lam_init)
    od = (ot[:, :tq] - lam * ot[:, tq:]).T
    ms = jnp.mean(od * od, axis=-1, keepdims=True)
    o_ref[...] = (od * lax.rsqrt(ms + SUBLN_EPS) * sg_ref[...] * (1.0 - lam_init)).astype(BF16)


def _attention(qs, kh, vt, lamv, sg, *, tq, lam_init):
    _, b, lp, _ = qs.shape
    nq = lp // tq
    const = lambda bi, hd, qi: (0, 0)
    return pl.pallas_call(
        functools.partial(_attn_kernel, lam_init=lam_init),
        grid=(b, DIFF_HEADS, nq),
        in_specs=[
            pl.BlockSpec((2, None, tq, LANES), lambda bi, hd, qi: (0, bi, qi, hd)),
            pl.BlockSpec((None, lp, LANES), lambda bi, hd, qi: (bi, 0, hd)),
            pl.BlockSpec((None, nq, None, LANES, tq), lambda bi, hd, qi: (bi, 0, hd, 0, 0)),
            pl.BlockSpec((SUBLANES, LANES), const),
            pl.BlockSpec((1, LANES), const),
        ],
        out_specs=pl.BlockSpec((None, tq, LANES), lambda bi, hd, qi: (bi, qi, hd)),
        out_shape=jax.ShapeDtypeStruct((b, lp, DIFF_WIDTH), BF16),
        scratch_shapes=[pltpu.VMEM((LANES, 2 * tq), F32)],
        compiler_params=_params(("arbitrary", "arbitrary", "arbitrary")),
        name="diffattn",
    )(qs, kh, vt, lamv, sg)


def _outproj_kernel(h_ref, oa_ref, ob_ref, pc_ref, pch_ref, cw_ref, wa_ref, wb_ref, wc_ref,
                    o_ref, *, tiles_per_batch):
    tm = h_ref.shape[0]
    first = (pl.program_id(0) % tiles_per_batch) == 0
    pc = pc_ref[...]
    bg = pc[:, :CONV_WIDTH]
    z = pc[:, CONV_WIDTH:2 * CONV_WIDTH] * pc[:, 2 * CONV_WIDTH:]
    ph = pch_ref[...]
    zh = jnp.where(first, 0.0, ph[:, CONV_WIDTH:2 * CONV_WIDTH] * ph[:, 2 * CONV_WIDTH:])
    row = lax.broadcasted_iota(jnp.int32, (tm, 1), 0)
    z1 = jnp.where(row == 0, zh[SUBLANES - 1:SUBLANES], pltpu.roll(z, 1, 0))
    z2 = jnp.where(row == 0, zh[SUBLANES - 2:SUBLANES - 1],
                   jnp.where(row == 1, zh[SUBLANES - 1:SUBLANES], pltpu.roll(z, 2, 0)))
    cw = cw_ref[...]
    y = cw[0:1] * z2 + cw[1:2] * z1 + cw[2:3] * z
    oc = (bg * y).astype(BF16)
    o_ref[...] = (h_ref[...]
                  + jnp.dot(oa_ref[...], wa_ref[...], preferred_element_type=F32)
                  + jnp.dot(ob_ref[...], wb_ref[...], preferred_element_type=F32)
                  + jnp.dot(oc, wc_ref[...], preferred_element_type=F32))


def _outproj(h, oa, ob, pc, cw, wa, wb, wc, *, tm, lp):
    r, d = h.shape
    row = lambda i: (i, 0)
    const = lambda i: (0, 0)
    halo = lambda i: (jnp.maximum(i * (tm // SUBLANES) - 1, 0), 0)
    return pl.pallas_call(
        functools.partial(_outproj_kernel, tiles_per_batch=lp // tm),
        grid=(r // tm,),
        in_specs=[
            pl.BlockSpec((tm, d), row),
            pl.BlockSpec((tm, RWKV_PAD), row),
            pl.BlockSpec((tm, DIFF_WIDTH), row),
            pl.BlockSpec((tm, PC_COLS), row),
            pl.BlockSpec((SUBLANES, PC_COLS), halo),
            pl.BlockSpec((SUBLANES, CONV_WIDTH), const),
            pl.BlockSpec(wa.shape, const),
            pl.BlockSpec(wb.shape, const),
            pl.BlockSpec(wc.shape, const),
        ],
        out_specs=pl.BlockSpec((tm, d), row),
        out_shape=jax.ShapeDtypeStruct((r, d), F32),
        compiler_params=_params(("arbitrary",)),
        name="outproj",
    )(h, oa, ob, pc, pc, cw, wa, wb, wc)


def _swiglu_block(xb, wg_ref, wu_ref, wd_ref, side_work=None):
    d_ff = wg_ref.shape[1]
    acc = jnp.zeros((xb.shape[0], wd_ref.shape[1]), F32)
    for c in range(d_ff // MXU_DIM):
        cs = slice(c * MXU_DIM, (c + 1) * MXU_DIM)
        gt = jnp.dot(xb, wg_ref[:, cs], preferred_element_type=F32)
        up = jnp.dot(xb, wu_ref[:, cs], preferred_element_type=F32)
        act = (jax.nn.silu(gt) * up).astype(BF16)
        acc = acc + jnp.dot(act, wd_ref[cs, :], preferred_element_type=F32)
        if side_work is not None:
            side_work(c)
    return acc


def _rms(x, g):
    ms = jnp.mean(x * x, axis=-1, keepdims=True)
    return x * lax.rsqrt(ms + NORM_EPS) * g


def _ffn_kernel(h_ref, g_ref, wg_ref, wu_ref, wd_ref, o_ref):
    x = h_ref[...]
    xn = _rms(x, g_ref[...]).astype(BF16)
    o_ref[...] = x + _swiglu_block(xn, wg_ref, wu_ref, wd_ref)


def _dense_ffn(h, g, wg, wu, wd, *, tm):
    r, d = h.shape
    row = lambda i: (i, 0)
    const = lambda i: (0, 0)
    return pl.pallas_call(
        _ffn_kernel,
        grid=(r // tm,),
        in_specs=[
            pl.BlockSpec((tm, d), row),
            pl.BlockSpec((1, d), const),
            pl.BlockSpec(wg.shape, const),
            pl.BlockSpec(wu.shape, const),
            pl.BlockSpec(wd.shape, const),
        ],
        out_specs=pl.BlockSpec((tm, d), row),
        out_shape=jax.ShapeDtypeStruct((r, d), F32),
        compiler_params=_params(("arbitrary",)),
        name="dense_ffn",
    )(h, g, wg, wu, wd)


def _router_kernel(h_ref, g_ref, rw_ref, hn_ref, route_ref):
    tm = h_ref.shape[0]
    xn = _rms(h_ref[...], g_ref[...])
    hn_ref[...] = xn
    logits = _dot_f32(xn, rw_ref[...])
    lane = lax.broadcasted_iota(jnp.int32, (tm, LANES), 1)
    logits = jnp.where(lane < N_EXPERTS, logits, -jnp.inf)
    m1 = jnp.max(logits, axis=-1, keepdims=True)
    i1 = jnp.min(jnp.where(logits == m1, lane, LANES), axis=-1, keepdims=True)
    rest = jnp.where(lane == i1, -jnp.inf, logits)
    m2 = jnp.max(rest, axis=-1, keepdims=True)
    i2 = jnp.min(jnp.where(rest == m2, lane, LANES), axis=-1, keepdims=True)
    e2 = jnp.exp(m2 - m1)
    den = 1.0 + e2
    route_ref[...] = jnp.where(lane == 0, i1.astype(F32),
                     jnp.where(lane == 1, i2.astype(F32),
                     jnp.where(lane == 2, 1.0 / den,
                     jnp.where(lane == 3, e2 / den, 0.0))))


def _router(h, g, rw, *, tm):
    r, d = h.shape
    row = lambda i: (i, 0)
    const = lambda i: (0, 0)
    return pl.pallas_call(
        _router_kernel,
        grid=(r // tm,),
        in_specs=[pl.BlockSpec((tm, d), row), pl.BlockSpec((1, d), const),
                  pl.BlockSpec((d, LANES), const)],
        out_specs=[pl.BlockSpec((tm, d), row), pl.BlockSpec((tm, LANES), row)],
        out_shape=[jax.ShapeDtypeStruct((r, d), F32), jax.ShapeDtypeStruct((r, LANES), F32)],
        compiler_params=_params(("arbitrary",)),
        name="router",
    )(h, g, rw)


def _row_copy(src, dst, sem, src_row, dst_row):
    return pltpu.make_async_copy(src.at[pl.ds(src_row, 1), :], dst.at[pl.ds(dst_row, 1), :], sem)


DMA_UNROLL = 8
DMA_ISSUE_SLICES = 8


def _moe_kernel(bexp_ref, rsrc_ref, rdst_ref, hn_hbm, wg_ref, wu_ref, wd_ref, y_hbm,
                xbuf, ybuf, gsem, ssem):
    del bexp_ref
    blk = xbuf.shape[1]
    i = pl.program_id(0)
    nb = pl.num_programs(0)
    slot = i % 2
    other = 1 - slot
    dump0 = y_hbm.shape[0] - blk

    def for_rows(fn):
        def body(j, c):
            for u in range(DMA_UNROLL):
                fn(j * DMA_UNROLL + u)
            return c
        lax.fori_loop(0, blk // DMA_UNROLL, body, 0)

    def gather(block, s, j):
        return _row_copy(hn_hbm, xbuf.at[s], gsem.at[s], rsrc_ref[block * blk + j], j)

    def scatter(dst_row, s, j):
        return _row_copy(ybuf.at[s], y_hbm, ssem.at[s], j, dst_row)

    @pl.when(i == 0)
    def _():
        for_rows(lambda j: gather(0, 0, j).start())
        ybuf[1] = jnp.zeros(ybuf.shape[1:], ybuf.dtype)

    for_rows(lambda j: gather(0, slot, j).wait())

    @pl.when(i >= 1)
    def _():
        for_rows(lambda j: scatter(0, slot, j).wait())

    nxt = jnp.minimum(i + 1, nb - 1)
    prev = jnp.maximum(i - 1, 0)
    per = -(-blk // DMA_ISSUE_SLICES)

    def side_work(c):
        for j in range(c * per, min((c + 1) * per, blk)):
            gather(nxt, other, j).start()
            dst = jnp.where(i == 0, dump0 + j, rdst_ref[prev * blk + j])
            scatter(dst, other, j).start()

    ybuf[slot] = _swiglu_block(xbuf[slot].astype(BF16), wg_ref, wu_ref, wd_ref, side_work)

    @pl.when(i == nb - 1)
    def _():
        for_rows(lambda j: gather(0, other, j).wait())
        for_rows(lambda j: scatter(0, other, j).wait())
        for_rows(lambda j: scatter(rdst_ref[i * blk + j], slot, j).start())
        for_rows(lambda j: scatter(0, slot, j).wait())


def _moe_experts(block_expert, row_src, row_dst, hn, wg, wu, wd):
    nb = block_expert.shape[0]
    r, d = hn.shape
    d_ff = wg.shape[2]
    wmap = lambda i, be, rs, rd: (be[i], 0, 0)
    grid_spec = pltpu.PrefetchScalarGridSpec(
        num_scalar_prefetch=3,
        grid=(nb,),
        in_specs=[
            pl.BlockSpec(memory_space=pl.ANY),
            pl.BlockSpec((None, d, d_ff), wmap),
            pl.BlockSpec((None, d, d_ff), wmap),
            pl.BlockSpec((None, d_ff, d), wmap),
        ],
        out_specs=pl.BlockSpec(memory_space=pl.ANY),
        scratch_shapes=[pltpu.VMEM((2, MOE_BLOCK, d), F32), pltpu.VMEM((2, MOE_BLOCK, d), F32),
                        pltpu.SemaphoreType.DMA((2,)), pltpu.SemaphoreType.DMA((2,))],
    )
    return pl.pallas_call(
        _moe_kernel,
        grid_spec=grid_spec,
        out_shape=jax.ShapeDtypeStruct((TOP_K * r + MOE_BLOCK, d), F32),
        compiler_params=_params(("arbitrary",)),
        name="moe_experts",
    )(block_expert, row_src, row_dst, hn, wg, wu, wd)


def _combine_kernel(h_ref, route_ref, y0_ref, y1_ref, o_ref):
    route = route_ref[...]
    o_ref[...] = h_ref[...] + route[:, 2:3] * y0_ref[...] + route[:, 3:4] * y1_ref[...]


def _moe_combine(h, route, y, *, tm):
    r, d = h.shape
    nt = r // tm
    row = lambda i: (i, 0)
    return pl.pallas_call(
        _combine_kernel,
        grid=(nt,),
        in_specs=[
            pl.BlockSpec((tm, d), row),
            pl.BlockSpec((tm, LANES), row),
            pl.BlockSpec((tm, d), row),
            pl.BlockSpec((tm, d), lambda i: (i + nt, 0)),
        ],
        out_specs=pl.BlockSpec((tm, d), row),
        out_shape=jax.ShapeDtypeStruct((r, d), F32),
        compiler_params=_params(("arbitrary",)),
        name="moe_combine",
    )(h, route, y, y)


def _routing_tables(route):
    r = route.shape[0]
    tk = r * TOP_K
    flat_e = route[:, :TOP_K].reshape(tk).astype(jnp.int32)
    onehot = (flat_e[:, None] == jnp.arange(N_EXPERTS, dtype=jnp.int32)[None, :]).astype(jnp.int32)
    csum = jnp.cumsum(onehot, axis=0)
    rank = jnp.sum(csum * onehot, axis=1) - 1
    counts = csum[-1]
    padded = (counts + MOE_BLOCK - 1) // MOE_BLOCK * MOE_BLOCK
    pend = jnp.cumsum(padded)
    pstart = pend - padded
    dest = pstart[flat_e] + rank
    nb = -(-tk // MOE_BLOCK) + N_EXPERTS
    n_slots = nb * MOE_BLOCK
    pair = jnp.full((n_slots,), -1, jnp.int32).at[dest].set(
        jnp.arange(tk, dtype=jnp.int32), unique_indices=True, mode="promise_in_bounds")
    used = pair >= 0
    row_src = jnp.where(used, pair // TOP_K, 0)
    row_dst = jnp.where(used, (pair % TOP_K) * r + pair // TOP_K,
                        tk + jnp.arange(n_slots, dtype=jnp.int32) % MOE_BLOCK)
    block_start = jnp.arange(nb, dtype=jnp.int32) * MOE_BLOCK
    block_expert = jnp.minimum(jnp.searchsorted(pend, block_start, side='right'),
                               N_EXPERTS - 1).astype(jnp.int32)
    return block_expert, row_src, row_dst


def _pad_heads(t, axis):
    axis = axis % t.ndim
    shp = t.shape
    t = t.reshape(shp[:axis] + (RWKV_HEADS, RWKV_HEAD_DIM) + shp[axis + 1:])
    pad = [(0, 0)] * t.ndim
    pad[axis + 1] = (0, HEAD_PAD - RWKV_HEAD_DIM)
    t = jnp.pad(t, pad)
    return t.reshape(shp[:axis] + (RWKV_PAD,) + shp[axis + 1:])


def _rope_tables(lp):
    half = ROPE_DIM // 2
    inv_freq = ROPE_THETA ** (-jnp.arange(0, ROPE_DIM, 2, dtype=F32) / ROPE_DIM)
    ang = jnp.arange(lp, dtype=F32)[:, None] * inv_freq[None, :]
    cos, sin = jnp.cos(ang), jnp.sin(ang)
    ones = jnp.ones((lp, DIFF_QK_DIM - ROPE_DIM), F32)
    zeros = jnp.zeros((lp, DIFF_QK_DIM - ROPE_DIM), F32)
    zh = jnp.zeros((lp, half), F32)
    rc = jnp.concatenate([cos, cos, ones], axis=1)
    rs1 = jnp.concatenate([-sin, zh, zeros], axis=1)
    rs2 = jnp.concatenate([zh, sin, zeros], axis=1)
    tile2 = lambda a: jnp.concatenate([a, a], axis=1)
    return tile2(rc), tile2(rs1), tile2(rs2)


def kernel(x, meta_tokens, mix_norm_g, w_in, tm_mu, tm_w0, tm_w_decay_up, tm_a0, tm_w_a_up, tm_w_g_up, tm_k_k, tm_k_a, tm_r_k, tm_gn_g, tm_gn_b, da_q_norm_g, da_k_norm_g, da_lambda_q1, da_lambda_k1, da_lambda_q2, da_lambda_k2, da_subln_g, sc_conv_w, w_out, ffn_norm_g, ffn_w_gate, ffn_w_up, ffn_w_down, router_w, moe_w_gate, moe_w_up, moe_w_down):
    b, seq, d = x.shape
    depth = w_in.shape[0]
    l = N_META + seq
    lp = -(-l // LANES) * LANES
    tm = _row_tile(lp)
    r = b * lp
    s1 = RWKV_WIDTH

    meta = jnp.broadcast_to(meta_tokens.astype(x.dtype)[None], (b, N_META, d))
    h = jnp.concatenate([meta, x, jnp.zeros((b, lp - l, d), x.dtype)], axis=1).reshape(r, d)
    rc, rs1, rs2 = _rope_tables(lp)

    for i in range(depth):
        lam_init = 0.8 - 0.6 * math.exp(-0.3 * i)
        wi = w_in[i]
        w_cat = jnp.concatenate(
            [_pad_heads(wi[:, 0:s1], 1), _pad_heads(wi[:, s1:2 * s1], 1),
             _pad_heads(wi[:, 2 * s1:3 * s1], 1), wi[:, 3 * s1:]], axis=1).astype(BF16)
        qg = jnp.tile(da_q_norm_g[i], 2)[None]
        kg = jnp.tile(da_k_norm_g[i], 2)[None]
        pa, qs, kh, vt, pc = _inproj(h, mix_norm_g[i][None], w_cat, rc, rs1, rs2, qg, kg,
                                     tm=tm, lp=lp)

        mu = tm_mu[i]
        mu_p = jnp.concatenate([_pad_heads(mu[0:s1], 0), _pad_heads(mu[s1:2 * s1], 0),
                                _pad_heads(mu[2 * s1:3 * s1], 0), mu[3 * s1:]])[None]
        vecs = jnp.stack([_pad_heads(tm_w0[i], 0), _pad_heads(tm_a0[i], 0),
                          _pad_heads(tm_k_k[i], 0), _pad_heads(tm_k_a[i], 0),
                          _pad_heads(tm_r_k[i].reshape(s1), 0), _pad_heads(tm_gn_g[i], 0),
                          _pad_heads(tm_gn_b[i], 0), jnp.zeros((RWKV_PAD,), F32)])
        zl = lambda n: jnp.zeros((n, RWKV_PAD), F32)
        wd_p = jnp.concatenate([_pad_heads(tm_w_decay_up[i], 1), zl(AAA_LORA + GATE_LORA)], axis=0)
        wa_p = jnp.concatenate([zl(DECAY_LORA), _pad_heads(tm_w_a_up[i], 1), zl(GATE_LORA)], axis=0)
        wg_p = jnp.concatenate([zl(DECAY_LORA + AAA_LORA), _pad_heads(tm_w_g_up[i], 1)], axis=0)
        oa = _rwkv(pa.reshape(b, lp, PA_COLS), mu_p, vecs, wd_p, wa_p, wg_p, tb=tm)

        lamv = jnp.zeros((SUBLANES, LANES), F32).at[0:4, :DIFF_QK_DIM].set(
            jnp.stack([da_lambda_q1[i], da_lambda_k1[i], da_lambda_q2[i], da_lambda_k2[i]]))
        ob = _attention(qs.reshape(2, b, lp, DIFF_WIDTH), kh.reshape(b, lp, DIFF_WIDTH),
                        vt.reshape(b, lp // tm, DIFF_HEADS, LANES, tm), lamv, da_subln_g[i][None],
                        tq=tm, lam_init=lam_init)

        wo = w_out[i]
        cw = jnp.zeros((SUBLANES, CONV_WIDTH), F32).at[:CONV_K].set(sc_conv_w[i])
        h = _outproj(h, oa.reshape(r, RWKV_PAD), ob.reshape(r, DIFF_WIDTH), pc, cw,
                     _pad_heads(wo[:s1], 0).astype(BF16), wo[s1:s1 + DIFF_WIDTH].astype(BF16),
                     wo[s1 + DIFF_WIDTH:].astype(BF16), tm=tm, lp=lp)

        j = i // 2
        if i % 2 == 0:
            h = _dense_ffn(h, ffn_norm_g[i][None], ffn_w_gate[j].astype(BF16),
                           ffn_w_up[j].astype(BF16), ffn_w_down[j].astype(BF16), tm=tm)
        else:
            rw = jnp.zeros((d, LANES), F32).at[:, :N_EXPERTS].set(router_w[j])
            hn, route = _router(h, ffn_norm_g[i][None], rw, tm=tm)
            block_expert, row_src, row_dst = _routing_tables(route)
            y = _moe_experts(block_expert, row_src, row_dst, hn, moe_w_gate[j].astype(BF16),
                             moe_w_up[j].astype(BF16), moe_w_down[j].astype(BF16))
            h = _moe_combine(h, route, y, tm=tm)

    return h.reshape(b, lp, d)[:, N_META:l]
```

```python
import functools
import math

import jax
import jax.numpy as jnp
from jax import lax
from jax.experimental import pallas as pl
from jax.experimental.pallas import tpu as pltpu

N_META = 16
RWKV_HEAD_DIM = 64
RWKV_HEADS = 4
RWKV_WIDTH = RWKV_HEADS * RWKV_HEAD_DIM
DECAY_LORA = 32
AAA_LORA = 32
GATE_LORA = 64
LORA_COLS = DECAY_LORA + AAA_LORA + GATE_LORA
DIFF_QK_DIM = 64
DIFF_V_DIM = 128
DIFF_HEADS = 4
DIFF_WIDTH = DIFF_HEADS * DIFF_V_DIM
CONV_WIDTH = 256
CONV_K = 3
ROPE_THETA = 500000.0
ROPE_DIM = DIFF_QK_DIM // 4
N_EXPERTS = 8
TOP_K = 2
MOE_BLOCK = 256
NORM_EPS = 1e-6
RWKV_GN_EPS = 64e-5
SUBLN_EPS = 1e-5
NEG_INF = -1e30

LANES = 128
SUBLANES = 8
MXU_DIM = 256
VMEM_LIMIT_BYTES = 56 * 1024 * 1024

HEAD_PAD = LANES
RWKV_PAD = RWKV_HEADS * HEAD_PAD
PA_COLS = 3 * RWKV_PAD + LORA_COLS
PB_COLS = 3 * DIFF_WIDTH
PC_COLS = 3 * CONV_WIDTH
CHUNK = 64

F32 = jnp.float32
BF16 = jnp.bfloat16
HIGHEST = lax.Precision.HIGHEST


def _dot(a, b):
    return jnp.dot(a.astype(BF16), b.astype(BF16), preferred_element_type=F32)


def _dot_nt(a, b):
    return lax.dot_general(a.astype(BF16), b.astype(BF16), (((1,), (1,)), ((), ())),
                           preferred_element_type=F32)


def _dot_f32(a, b):
    return jnp.dot(a, b, preferred_element_type=F32, precision=HIGHEST)


def _row_tile(lp):
    for t in (384, 256, 128):
        if lp % t == 0:
            return t
    raise ValueError(f"padded length {lp} is not a multiple of {LANES}")


def _params(sem):
    return pltpu.CompilerParams(dimension_semantics=sem, vmem_limit_bytes=VMEM_LIMIT_BYTES)


def _inproj_kernel(h_ref, g_ref, w_ref, rc_ref, rs1_ref, rs2_ref, qg_ref, kg_ref,
                   pa_ref, qs_ref, kh_ref, vt_ref, pc_ref):
    tm = h_ref.shape[0]
    x = h_ref[...]
    ms = jnp.mean(x * x, axis=-1, keepdims=True)
    xn = (x * lax.rsqrt(ms + NORM_EPS) * g_ref[...]).astype(BF16)
    pa_ref[...] = jnp.dot(xn, w_ref[:, :PA_COLS], preferred_element_type=F32)
    pc_ref[...] = jnp.dot(xn, w_ref[:, PA_COLS + PB_COLS:], preferred_element_type=F32)
    pb = jnp.dot(xn, w_ref[:, PA_COLS:PA_COLS + PB_COLS], preferred_element_type=F32)

    lane = lax.broadcasted_iota(jnp.int32, (tm, LANES), 1)
    lo = lane < DIFF_QK_DIM
    rc, rs1, rs2 = rc_ref[...], rs1_ref[...], rs2_ref[...]

    def norm_rope(t, g):
        t2 = t * t
        ss_lo = jnp.sum(jnp.where(lo, t2, 0.0), axis=-1, keepdims=True)
        ss_hi = jnp.sum(jnp.where(lo, 0.0, t2), axis=-1, keepdims=True)
        inv = jnp.where(lo, lax.rsqrt(ss_lo / DIFF_QK_DIM + NORM_EPS),
                        lax.rsqrt(ss_hi / DIFF_QK_DIM + NORM_EPS))
        tn = t * inv * g
        half = ROPE_DIM // 2
        return tn * rc + pltpu.roll(tn, LANES - half, 1) * rs1 + pltpu.roll(tn, half, 1) * rs2

    for hd in range(DIFF_HEADS):
        cs = slice(hd * LANES, (hd + 1) * LANES)
        q = norm_rope(pb[:, cs], qg_ref[...]) * (DIFF_QK_DIM ** -0.5)
        qs_ref[0, :, cs] = jnp.where(lo, q, 0.0).astype(BF16)
        qs_ref[1, :, cs] = jnp.where(lo, 0.0, q).astype(BF16)
        kcs = slice(DIFF_WIDTH + hd * LANES, DIFF_WIDTH + (hd + 1) * LANES)
        kh_ref[:, cs] = norm_rope(pb[:, kcs], kg_ref[...]).astype(BF16)
        vcs = slice(2 * DIFF_WIDTH + hd * LANES, 2 * DIFF_WIDTH + (hd + 1) * LANES)
        vt_ref[0, hd] = pb[:, vcs].T.astype(BF16)


def _inproj(h, g, w, rc, rs1, rs2, qg, kg, *, tm, lp):
    r, d = h.shape
    nt_b = lp // tm
    row = lambda i: (i, 0)
    const = lambda i: (0, 0)
    rope = lambda i: (i % nt_b, 0)
    return pl.pallas_call(
        _inproj_kernel,
        grid=(r // tm,),
        in_specs=[
            pl.BlockSpec((tm, d), row),
            pl.BlockSpec((1, d), const),
            pl.BlockSpec(w.shape, const),
            pl.BlockSpec((tm, LANES), rope),
            pl.BlockSpec((tm, LANES), rope),
            pl.BlockSpec((tm, LANES), rope),
            pl.BlockSpec((1, LANES), const),
            pl.BlockSpec((1, LANES), const),
        ],
        out_specs=[
            pl.BlockSpec((tm, PA_COLS), row),
            pl.BlockSpec((2, tm, DIFF_WIDTH), lambda i: (0, i, 0)),
            pl.BlockSpec((tm, DIFF_WIDTH), row),
            pl.BlockSpec((1, DIFF_HEADS, LANES, tm), lambda i: (i, 0, 0, 0)),
            pl.BlockSpec((tm, PC_COLS), row),
        ],
        out_shape=[
            jax.ShapeDtypeStruct((r, PA_COLS), F32),
            jax.ShapeDtypeStruct((2, r, DIFF_WIDTH), BF16),
            jax.ShapeDtypeStruct((r, DIFF_WIDTH), BF16),
            jax.ShapeDtypeStruct((r // tm, DIFF_HEADS, LANES, tm), BF16),
            jax.ShapeDtypeStruct((r, PC_COLS), F32),
        ],
        compiler_params=_params(("arbitrary",)),
        name="inproj",
    )(h, g, w, rc, rs1, rs2, qg, kg)


def _split3(x):
    hi = x.astype(BF16)
    r1 = x - hi.astype(F32)
    mid = r1.astype(BF16)
    lo = (r1 - mid.astype(F32)).astype(BF16)
    return hi, mid, lo


def _dot_hi(a, b):
    ah = a.astype(BF16)
    al = (a - ah.astype(F32)).astype(BF16)
    bh = b.astype(BF16)
    bl = (b - bh.astype(F32)).astype(BF16)
    d = lambda p, q: jnp.dot(p, q, preferred_element_type=F32)
    return d(ah, bh) + d(ah, bl) + d(al, bh)


def _bmm(a, b):
    return jnp.einsum('cik,ckj->cij', a.astype(BF16), b.astype(BF16), preferred_element_type=F32)


def _bmm_nt(a, b):
    return jnp.einsum('cik,cjk->cij', a.astype(BF16), b.astype(BF16), preferred_element_type=F32)


def _rwkv_kernel(pa_ref, halo_ref, mu_ref, vec_ref, wd_ref, wa_ref, wg_ref, oa_ref, hstate_s):
    tb = pa_ref.shape[0]
    nc = tb // CHUNK
    i = pl.program_id(1)

    @pl.when(i == 0)
    def _():
        hstate_s[...] = jnp.zeros_like(hstate_s)

    x = pa_ref[...]
    prev_row = jnp.where(i == 0, 0.0, halo_ref[SUBLANES - 1:SUBLANES, :])
    row = lax.broadcasted_iota(jnp.int32, (tb, 1), 0)
    prev = jnp.where(row == 0, prev_row, pltpu.roll(x, 1, 0))
    hx = x + (prev - x) * mu_ref[...]
    r = hx[:, 0:RWKV_PAD]
    k = hx[:, RWKV_PAD:2 * RWKV_PAD]
    v = hx[:, 2 * RWKV_PAD:3 * RWKV_PAD]
    lora = hx[:, 3 * RWKV_PAD:]
    w0, a0, k_k, k_a = vec_ref[0:1, :], vec_ref[1:2, :], vec_ref[2:3, :], vec_ref[3:4, :]
    r_k, gn_g, gn_b = vec_ref[4:5, :], vec_ref[5:6, :], vec_ref[6:7, :]

    w = -jax.nn.softplus(-(w0 + _dot_hi(jnp.tanh(lora), wd_ref[...]))) - 0.5
    wl = -jnp.exp(w)
    alr = jax.nn.sigmoid(a0 + _dot(lora, wa_ref[...]))
    gate = _dot(jax.nn.sigmoid(lora), wg_ref[...])
    kk = k * k_k
    k2 = k * (1.0 + (alr - 1.0) * k_a)
    rkr = r * k2 * r_k

    ri = lax.broadcasted_iota(jnp.int32, (CHUNK, CHUNK), 0)
    ci = lax.broadcasted_iota(jnp.int32, (CHUNK, CHUNK), 1)
    low_incl = (ci <= ri)[None]
    low_strict = (ci < ri)[None]
    tril_b = jnp.where(ci <= ri, 1.0, 0.0).astype(BF16)
    eye_k = (lax.broadcasted_iota(jnp.int32, (HEAD_PAD, HEAD_PAD), 0)
             == lax.broadcasted_iota(jnp.int32, (HEAD_PAD, HEAD_PAD), 1))[None]
    n_sq = int(math.log2(CHUNK))

    parts = _split3(wl)
    cums = []
    for c in range(nc):
        rows = slice(c * CHUNK, (c + 1) * CHUNK)
        cums.append(sum(jnp.dot(tril_b, p[rows], preferred_element_type=F32) for p in parts))
    cum_all = jnp.concatenate(cums, axis=0)

    lane = lax.broadcasted_iota(jnp.int32, (tb, HEAD_PAD), 1)
    real = lane < RWKV_HEAD_DIM
    to3 = lambda t: t.reshape(nc, CHUNK, HEAD_PAD)
    for hd in range(RWKV_HEADS):
        cs = slice(hd * HEAD_PAD, (hd + 1) * HEAD_PAD)
        kk_h = kk[:, cs]
        nrm = jnp.sqrt(jnp.sum(kk_h * kk_h, axis=-1, keepdims=True))
        kk_h = kk_h / jnp.maximum(nrm, 1e-12)
        v_h = v[:, cs]
        bonus = jnp.sum(rkr[:, cs], axis=-1, keepdims=True) * v_h

        wl3, cum = to3(wl[:, cs]), to3(cum_all[:, cs])
        rr, kc, vc = to3(r[:, cs]), to3(k2[:, cs]), to3(v_h)
        ac, bc = to3(-kk_h), to3(kk_h * alr[:, cs])
        last = cum[:, CHUNK - 1:CHUNK, :]
        e_neg = jnp.exp(-cum)
        e_last = jnp.exp(last - cum)
        ah = ac * jnp.exp(cum - wl3)
        rh = rr * jnp.exp(cum)
        ar = jnp.concatenate([ah, rh], axis=1)
        gb = _bmm_nt(ar, bc * e_neg)
        gk = _bmm_nt(ar, kc * e_neg)
        a_ab = jnp.where(low_strict, gb[:, :CHUNK], 0.0)
        r_b = jnp.where(low_incl, gb[:, CHUNK:], 0.0)
        a_ak = jnp.where(low_strict, gk[:, :CHUNK], 0.0)
        r_kk = jnp.where(low_incl, gk[:, CHUNK:], 0.0)
        xx = jnp.concatenate([_bmm(a_ak, vc), ah], axis=2)
        pp = a_ab
        for j in range(n_sq):
            xx = xx + _bmm(pp, xx)
            if j + 1 < n_sq:
                pp = _bmm(pp, pp)
        rbx = _bmm(r_b, xx)
        q = rh + rbx[:, :, HEAD_PAD:]
        y0 = rbx[:, :, :HEAD_PAD] + _bmm(r_kk, vc)
        btx = _bmm(jnp.swapaxes(bc * e_last, 1, 2), xx)
        m = jnp.where(eye_k, jnp.exp(last), 0.0) + btx[:, :, HEAD_PAD:]
        n0 = btx[:, :, :HEAD_PAD] + _bmm(jnp.swapaxes(kc * e_last, 1, 2), vc)
        qm = jnp.concatenate([q, m], axis=1).astype(BF16)

        hs = hstate_s[hd]
        ys = []
        for c in range(nc):
            res = jnp.dot(qm[c], hs.astype(BF16), preferred_element_type=F32)
            ys.append(res[:CHUNK] + y0[c])
            hs = res[CHUNK:] + n0[c]
        hstate_s[hd] = hs
        y = jnp.concatenate(ys, axis=0)

        mean = jnp.sum(y, axis=-1, keepdims=True) / RWKV_HEAD_DIM
        dlt = jnp.where(real, y - mean, 0.0)
        var = jnp.sum(dlt * dlt, axis=-1, keepdims=True) / RWKV_HEAD_DIM
        yn = dlt * lax.rsqrt(var + RWKV_GN_EPS) * gn_g[:, cs] + gn_b[:, cs]
        oa_ref[:, cs] = ((yn + bonus) * gate[:, cs]).astype(BF16)


def _rwkv(pa, mu, vecs, wd, wa, wg, *, tb):
    b, lp, _ = pa.shape
    tile = lambda bi, i: (bi, i, 0)
    const = lambda bi, i: (0, 0)
    halo = lambda bi, i: (bi, jnp.maximum(i * (tb // SUBLANES) - 1, 0), 0)
    return pl.pallas_call(
        _rwkv_kernel,
        grid=(b, lp // tb),
        in_specs=[
            pl.BlockSpec((None, tb, PA_COLS), tile),
            pl.BlockSpec((None, SUBLANES, PA_COLS), halo),
            pl.BlockSpec((1, PA_COLS), const),
            pl.BlockSpec((SUBLANES, RWKV_PAD), const),
            pl.BlockSpec((LORA_COLS, RWKV_PAD), const),
            pl.BlockSpec((LORA_COLS, RWKV_PAD), const),
            pl.BlockSpec((LORA_COLS, RWKV_PAD), const),
        ],
        out_specs=pl.BlockSpec((None, tb, RWKV_PAD), tile),
        out_shape=jax.ShapeDtypeStruct((b, lp, RWKV_PAD), BF16),
        scratch_shapes=[pltpu.VMEM((RWKV_HEADS, HEAD_PAD, HEAD_PAD), F32)],
        compiler_params=_params(("arbitrary", "arbitrary")),
        name="rwkv7",
    )(pa, pa, mu, vecs, wd, wa, wg)


def _attn_kernel(q_ref, k_ref, vt_ref, lamv_ref, sg_ref, o_ref, acc_s, st_s, p_s, *, lam_init):
    tq = o_ref.shape[0]
    tk = vt_ref.shape[2]
    qi = pl.program_id(2)

    def scores(ki):
        k = k_ref[pl.ds(pl.multiple_of(ki * tk, tk), tk), :]
        return lax.dot_general(k, q_ref[...].reshape(2 * tq, LANES), (((1,), (1,)), ((), ())),
                               preferred_element_type=F32)

    def values(ki, p, alpha):
        acc_s[...] = alpha * acc_s[...] + jnp.dot(vt_ref[ki], p, preferred_element_type=F32)

    st = scores(qi)
    kr = lax.broadcasted_iota(jnp.int32, (tk, 2 * tq), 0)
    qc = lax.broadcasted_iota(jnp.int32, (tk, 2 * tq), 1)
    qc = jnp.where(qc >= tq, qc - tq, qc)
    st = jnp.where(kr <= qc, st, NEG_INF)
    m0 = jnp.max(st, axis=0, keepdims=True)
    p0 = jnp.exp(st - m0)
    l0 = jnp.sum(p0, axis=0, keepdims=True)
    acc_s[...] = jnp.dot(vt_ref[qi], p0.astype(BF16), preferred_element_type=F32)

    @pl.when(qi > 0)
    def _():
        st_s[0] = scores(0)
        p_s[1] = jnp.zeros(p_s.shape[1:], p_s.dtype)

    def trip(k, carry, slot):
        m_prev, l_prev, alpha_prev = carry
        st_s[1 - slot] = scores(jnp.minimum(k + 1, qi - 1))
        s_k = st_s[slot]
        m_new = jnp.maximum(m_prev, jnp.max(s_k, axis=0, keepdims=True))
        alpha = jnp.exp(m_prev - m_new)
        p = jnp.exp(s_k - m_new)
        l_new = alpha * l_prev + jnp.sum(p, axis=0, keepdims=True)
        values(jnp.maximum(k - 1, 0), p_s[1 - slot], alpha_prev)
        p_s[slot] = p.astype(BF16)
        return m_new, l_new, alpha

    carry = lax.fori_loop(0, qi // 2, lambda j, c: trip(2 * j + 1, trip(2 * j, c, 0), 1),
                          (m0, l0, jnp.ones_like(l0)))
    odd = qi % 2 == 1
    _, l_fin, alpha_last = lax.cond(odd, lambda c: trip(qi - 1, c, 0), lambda c: c, carry)

    @pl.when(odd)
    def _():
        values(qi - 1, p_s[0], alpha_last)

    @pl.when(jnp.logical_and(qi > 0, jnp.logical_not(odd)))
    def _():
        values(qi - 1, p_s[1], alpha_last)

    ot = acc_s[...] / l_fin
    lv = lamv_ref[...]
    lam = (jnp.exp(jnp.sum(lv[0:1] * lv[1:2], axis=-1, keepdims=True))
           - jnp.exp(jnp.sum(lv[2:3] * lv[3:4], axis=-1, keepdims=True)) + lam_init)
    od = (ot[:, :tq] - lam * ot[:, tq:]).T
    ms = jnp.mean(od * od, axis=-1, keepdims=True)
    o_ref[...] = (od * lax.rsqrt(ms + SUBLN_EPS) * sg_ref[...] * (1.0 - lam_init)).astype(BF16)


def _attention(qs, kh, vt, lamv, sg, *, tq, lam_init):
    _, b, lp, _ = qs.shape
    nq = lp // tq
    const = lambda bi, hd, qi: (0, 0)
    return pl.pallas_call(
        functools.partial(_attn_kernel, lam_init=lam_init),
        grid=(b, DIFF_HEADS, nq),
        in_specs=[
            pl.BlockSpec((2, None, tq, LANES), lambda bi, hd, qi: (0, bi, qi, hd)),
            pl.BlockSpec((None, lp, LANES), lambda bi, hd, qi: (bi, 0, hd)),
            pl.BlockSpec((None, nq, None, LANES, tq), lambda bi, hd, qi: (bi, 0, hd, 0, 0)),
            pl.BlockSpec((SUBLANES, LANES), const),
            pl.BlockSpec((1, LANES), const),
        ],
        out_specs=pl.BlockSpec((None, tq, LANES), lambda bi, hd, qi: (bi, qi, hd)),
        out_shape=jax.ShapeDtypeStruct((b, lp, DIFF_WIDTH), BF16),
        scratch_shapes=[pltpu.VMEM((LANES, 2 * tq), F32), pltpu.VMEM((2, tq, 2 * tq), F32),
                        pltpu.VMEM((2, tq, 2 * tq), BF16)],
        compiler_params=_params(("arbitrary", "arbitrary", "arbitrary")),
        name="diffattn",
    )(qs, kh, vt, lamv, sg)


def _outproj_kernel(h_ref, oa_ref, ob_ref, pc_ref, pch_ref, cw_ref, wa_ref, wb_ref, wc_ref,
                    o_ref, *, tiles_per_batch):
    tm = h_ref.shape[0]
    first = (pl.program_id(0) % tiles_per_batch) == 0
    pc = pc_ref[...]
    bg = pc[:, :CONV_WIDTH]
    z = pc[:, CONV_WIDTH:2 * CONV_WIDTH] * pc[:, 2 * CONV_WIDTH:]
    ph = pch_ref[...]
    zh = jnp.where(first, 0.0, ph[:, CONV_WIDTH:2 * CONV_WIDTH] * ph[:, 2 * CONV_WIDTH:])
    row = lax.broadcasted_iota(jnp.int32, (tm, 1), 0)
    z1 = jnp.where(row == 0, zh[SUBLANES - 1:SUBLANES], pltpu.roll(z, 1, 0))
    z2 = jnp.where(row == 0, zh[SUBLANES - 2:SUBLANES - 1],
                   jnp.where(row == 1, zh[SUBLANES - 1:SUBLANES], pltpu.roll(z, 2, 0)))
    cw = cw_ref[...]
    y = cw[0:1] * z2 + cw[1:2] * z1 + cw[2:3] * z
    oc = (bg * y).astype(BF16)
    o_ref[...] = (h_ref[...]
                  + jnp.dot(oa_ref[...], wa_ref[...], preferred_element_type=F32)
                  + jnp.dot(ob_ref[...], wb_ref[...], preferred_element_type=F32)
                  + jnp.dot(oc, wc_ref[...], preferred_element_type=F32))


def _outproj(h, oa, ob, pc, cw, wa, wb, wc, *, tm, lp):
    r, d = h.shape
    row = lambda i: (i, 0)
    const = lambda i: (0, 0)
    halo = lambda i: (jnp.maximum(i * (tm // SUBLANES) - 1, 0), 0)
    return pl.pallas_call(
        functools.partial(_outproj_kernel, tiles_per_batch=lp // tm),
        grid=(r // tm,),
        in_specs=[
            pl.BlockSpec((tm, d), row),
            pl.BlockSpec((tm, RWKV_PAD), row),
            pl.BlockSpec((tm, DIFF_WIDTH), row),
            pl.BlockSpec((tm, PC_COLS), row),
            pl.BlockSpec((SUBLANES, PC_COLS), halo),
            pl.BlockSpec((SUBLANES, CONV_WIDTH), const),
            pl.BlockSpec(wa.shape, const),
            pl.BlockSpec(wb.shape, const),
            pl.BlockSpec(wc.shape, const),
        ],
        out_specs=pl.BlockSpec((tm, d), row),
        out_shape=jax.ShapeDtypeStruct((r, d), F32),
        compiler_params=_params(("arbitrary",)),
        name="outproj",
    )(h, oa, ob, pc, pc, cw, wa, wb, wc)


def _swiglu_block(xb, wg_ref, wu_ref, wd_ref, side_work=None):
    d_ff = wg_ref.shape[1]
    acc = jnp.zeros((xb.shape[0], wd_ref.shape[1]), F32)
    for c in range(d_ff // MXU_DIM):
        cs = slice(c * MXU_DIM, (c + 1) * MXU_DIM)
        gt = jnp.dot(xb, wg_ref[:, cs], preferred_element_type=F32)
        up = jnp.dot(xb, wu_ref[:, cs], preferred_element_type=F32)
        act = (jax.nn.silu(gt) * up).astype(BF16)
        acc = acc + jnp.dot(act, wd_ref[cs, :], preferred_element_type=F32)
        if side_work is not None:
            side_work(c)
    return acc


def _rms(x, g):
    ms = jnp.mean(x * x, axis=-1, keepdims=True)
    return x * lax.rsqrt(ms + NORM_EPS) * g


def _ffn_kernel(h_ref, g_ref, wg_ref, wu_ref, wd_ref, o_ref):
    x = h_ref[...]
    xn = _rms(x, g_ref[...]).astype(BF16)
    o_ref[...] = x + _swiglu_block(xn, wg_ref, wu_ref, wd_ref)


def _dense_ffn(h, g, wg, wu, wd, *, tm):
    r, d = h.shape
    row = lambda i: (i, 0)
    const = lambda i: (0, 0)
    return pl.pallas_call(
        _ffn_kernel,
        grid=(r // tm,),
        in_specs=[
            pl.BlockSpec((tm, d), row),
            pl.BlockSpec((1, d), const),
            pl.BlockSpec(wg.shape, const),
            pl.BlockSpec(wu.shape, const),
            pl.BlockSpec(wd.shape, const),
        ],
        out_specs=pl.BlockSpec((tm, d), row),
        out_shape=jax.ShapeDtypeStruct((r, d), F32),
        compiler_params=_params(("arbitrary",)),
        name="dense_ffn",
    )(h, g, wg, wu, wd)


def _router_kernel(h_ref, g_ref, rw_ref, hn_ref, route_ref):
    tm = h_ref.shape[0]
    xn = _rms(h_ref[...], g_ref[...])
    hn_ref[...] = xn
    logits = _dot_f32(xn, rw_ref[...])
    lane = lax.broadcasted_iota(jnp.int32, (tm, LANES), 1)
    logits = jnp.where(lane < N_EXPERTS, logits, -jnp.inf)
    m1 = jnp.max(logits, axis=-1, keepdims=True)
    i1 = jnp.min(jnp.where(logits == m1, lane, LANES), axis=-1, keepdims=True)
    rest = jnp.where(lane == i1, -jnp.inf, logits)
    m2 = jnp.max(rest, axis=-1, keepdims=True)
    i2 = jnp.min(jnp.where(rest == m2, lane, LANES), axis=-1, keepdims=True)
    e2 = jnp.exp(m2 - m1)
    den = 1.0 + e2
    route_ref[...] = jnp.where(lane == 0, i1.astype(F32),
                     jnp.where(lane == 1, i2.astype(F32),
                     jnp.where(lane == 2, 1.0 / den,
                     jnp.where(lane == 3, e2 / den, 0.0))))


def _router(h, g, rw, *, tm):
    r, d = h.shape
    row = lambda i: (i, 0)
    const = lambda i: (0, 0)
    return pl.pallas_call(
        _router_kernel,
        grid=(r // tm,),
        in_specs=[pl.BlockSpec((tm, d), row), pl.BlockSpec((1, d), const),
                  pl.BlockSpec((d, LANES), const)],
        out_specs=[pl.BlockSpec((tm, d), row), pl.BlockSpec((tm, LANES), row)],
        out_shape=[jax.ShapeDtypeStruct((r, d), F32), jax.ShapeDtypeStruct((r, LANES), F32)],
        compiler_params=_params(("arbitrary",)),
        name="router",
    )(h, g, rw)


def _row_copy(src, dst, sem, src_row, dst_row):
    return pltpu.make_async_copy(src.at[pl.ds(src_row, 1), :], dst.at[pl.ds(dst_row, 1), :], sem)


DMA_UNROLL = 8
DMA_ISSUE_SLICES = 8


def _moe_kernel(bexp_ref, rsrc_ref, rdst_ref, hn_hbm, wg_ref, wu_ref, wd_ref, y_hbm,
                xbuf, ybuf, gsem, ssem):
    del bexp_ref
    blk = xbuf.shape[1]
    i = pl.program_id(0)
    nb = pl.num_programs(0)
    slot = i % 2
    other = 1 - slot
    dump0 = y_hbm.shape[0] - blk

    def for_rows(fn):
        def body(j, c):
            for u in range(DMA_UNROLL):
                fn(j * DMA_UNROLL + u)
            return c
        lax.fori_loop(0, blk // DMA_UNROLL, body, 0)

    def gather(block, s, j):
        return _row_copy(hn_hbm, xbuf.at[s], gsem.at[s], rsrc_ref[block * blk + j], j)

    def scatter(dst_row, s, j):
        return _row_copy(ybuf.at[s], y_hbm, ssem.at[s], j, dst_row)

    @pl.when(i == 0)
    def _():
        for_rows(lambda j: gather(0, 0, j).start())
        ybuf[1] = jnp.zeros(ybuf.shape[1:], ybuf.dtype)

    for_rows(lambda j: gather(0, slot, j).wait())

    @pl.when(i >= 1)
    def _():
        for_rows(lambda j: scatter(0, slot, j).wait())

    nxt = jnp.minimum(i + 1, nb - 1)
    prev = jnp.maximum(i - 1, 0)
    per = -(-blk // DMA_ISSUE_SLICES)

    def side_work(c):
        for j in range(c * per, min((c + 1) * per, blk)):
            gather(nxt, other, j).start(priority=j % 2)
            dst = jnp.where(i == 0, dump0 + j, rdst_ref[prev * blk + j])
            scatter(dst, other, j).start(priority=(j + 1) % 2)

    ybuf[slot] = _swiglu_block(xbuf[slot].astype(BF16), wg_ref, wu_ref, wd_ref, side_work)

    @pl.when(i == nb - 1)
    def _():
        for_rows(lambda j: gather(0, other, j).wait())
        for_rows(lambda j: scatter(0, other, j).wait())
        for_rows(lambda j: scatter(rdst_ref[i * blk + j], slot, j).start())
        for_rows(lambda j: scatter(0, slot, j).wait())


def _moe_experts(block_expert, row_src, row_dst, hn, wg, wu, wd):
    nb = block_expert.shape[0]
    r, d = hn.shape
    d_ff = wg.shape[2]
    wmap = lambda i, be, rs, rd: (be[i], 0, 0)
    grid_spec = pltpu.PrefetchScalarGridSpec(
        num_scalar_prefetch=3,
        grid=(nb,),
        in_specs=[
            pl.BlockSpec(memory_space=pl.ANY),
            pl.BlockSpec((None, d, d_ff), wmap),
            pl.BlockSpec((None, d, d_ff), wmap),
            pl.BlockSpec((None, d_ff, d), wmap),
        ],
        out_specs=pl.BlockSpec(memory_space=pl.ANY),
        scratch_shapes=[pltpu.VMEM((2, MOE_BLOCK, d), F32), pltpu.VMEM((2, MOE_BLOCK, d), F32),
                        pltpu.SemaphoreType.DMA((2,)), pltpu.SemaphoreType.DMA((2,))],
    )
    return pl.pallas_call(
        _moe_kernel,
        grid_spec=grid_spec,
        out_shape=jax.ShapeDtypeStruct((TOP_K * r + MOE_BLOCK, d), F32),
        compiler_params=_params(("arbitrary",)),
        name="moe_experts",
    )(block_expert, row_src, row_dst, hn, wg, wu, wd)


def _combine_kernel(h_ref, route_ref, y0_ref, y1_ref, o_ref):
    route = route_ref[...]
    o_ref[...] = h_ref[...] + route[:, 2:3] * y0_ref[...] + route[:, 3:4] * y1_ref[...]


def _moe_combine(h, route, y, *, tm):
    r, d = h.shape
    nt = r // tm
    row = lambda i: (i, 0)
    return pl.pallas_call(
        _combine_kernel,
        grid=(nt,),
        in_specs=[
            pl.BlockSpec((tm, d), row),
            pl.BlockSpec((tm, LANES), row),
            pl.BlockSpec((tm, d), row),
            pl.BlockSpec((tm, d), lambda i: (i + nt, 0)),
        ],
        out_specs=pl.BlockSpec((tm, d), row),
        out_shape=jax.ShapeDtypeStruct((r, d), F32),
        compiler_params=_params(("arbitrary",)),
        name="moe_combine",
    )(h, route, y, y)


def _routing_tables(route):
    r = route.shape[0]
    tk = r * TOP_K
    flat_e = route[:, :TOP_K].reshape(tk).astype(jnp.int32)
    onehot = (flat_e[:, None] == jnp.arange(N_EXPERTS, dtype=jnp.int32)[None, :]).astype(jnp.int32)
    csum = jnp.cumsum(onehot, axis=0)
    rank = jnp.sum(csum * onehot, axis=1) - 1
    counts = csum[-1]
    padded = (counts + MOE_BLOCK - 1) // MOE_BLOCK * MOE_BLOCK
    pend = jnp.cumsum(padded)
    pstart = pend - padded
    dest = pstart[flat_e] + rank
    nb = -(-tk // MOE_BLOCK) + N_EXPERTS
    n_slots = nb * MOE_BLOCK
    pair = jnp.full((n_slots,), -1, jnp.int32).at[dest].set(
        jnp.arange(tk, dtype=jnp.int32), unique_indices=True, mode="promise_in_bounds")
    used = pair >= 0
    row_src = jnp.where(used, pair // TOP_K, 0)
    row_dst = jnp.where(used, (pair % TOP_K) * r + pair // TOP_K,
                        tk + jnp.arange(n_slots, dtype=jnp.int32) % MOE_BLOCK)
    block_start = jnp.arange(nb, dtype=jnp.int32) * MOE_BLOCK
    block_expert = jnp.minimum(jnp.searchsorted(pend, block_start, side='right'),
                               N_EXPERTS - 1).astype(jnp.int32)
    return block_expert, row_src, row_dst


def _pad_heads(t, axis):
    axis = axis % t.ndim
    shp = t.shape
    t = t.reshape(shp[:axis] + (RWKV_HEADS, RWKV_HEAD_DIM) + shp[axis + 1:])
    pad = [(0, 0)] * t.ndim
    pad[axis + 1] = (0, HEAD_PAD - RWKV_HEAD_DIM)
    t = jnp.pad(t, pad)
    return t.reshape(shp[:axis] + (RWKV_PAD,) + shp[axis + 1:])


def _rope_tables(lp):
    half = ROPE_DIM // 2
    inv_freq = ROPE_THETA ** (-jnp.arange(0, ROPE_DIM, 2, dtype=F32) / ROPE_DIM)
    ang = jnp.arange(lp, dtype=F32)[:, None] * inv_freq[None, :]
    cos, sin = jnp.cos(ang), jnp.sin(ang)
    ones = jnp.ones((lp, DIFF_QK_DIM - ROPE_DIM), F32)
    zeros = jnp.zeros((lp, DIFF_QK_DIM - ROPE_DIM), F32)
    zh = jnp.zeros((lp, half), F32)
    rc = jnp.concatenate([cos, cos, ones], axis=1)
    rs1 = jnp.concatenate([-sin, zh, zeros], axis=1)
    rs2 = jnp.concatenate([zh, sin, zeros], axis=1)
    tile2 = lambda a: jnp.concatenate([a, a], axis=1)
    return tile2(rc), tile2(rs1), tile2(rs2)


def kernel(x, meta_tokens, mix_norm_g, w_in, tm_mu, tm_w0, tm_w_decay_up, tm_a0, tm_w_a_up, tm_w_g_up, tm_k_k, tm_k_a, tm_r_k, tm_gn_g, tm_gn_b, da_q_norm_g, da_k_norm_g, da_lambda_q1, da_lambda_k1, da_lambda_q2, da_lambda_k2, da_subln_g, sc_conv_w, w_out, ffn_norm_g, ffn_w_gate, ffn_w_up, ffn_w_down, router_w, moe_w_gate, moe_w_up, moe_w_down):
    b, seq, d = x.shape
    depth = w_in.shape[0]
    l = N_META + seq
    lp = -(-l // LANES) * LANES
    tm = _row_tile(lp)
    r = b * lp
    s1 = RWKV_WIDTH

    meta = jnp.broadcast_to(meta_tokens.astype(x.dtype)[None], (b, N_META, d))
    h = jnp.concatenate([meta, x, jnp.zeros((b, lp - l, d), x.dtype)], axis=1).reshape(r, d)
    rc, rs1, rs2 = _rope_tables(lp)

    for i in range(depth):
        lam_init = 0.8 - 0.6 * math.exp(-0.3 * i)
        wi = w_in[i]
        w_cat = jnp.concatenate(
            [_pad_heads(wi[:, 0:s1], 1), _pad_heads(wi[:, s1:2 * s1], 1),
             _pad_heads(wi[:, 2 * s1:3 * s1], 1), wi[:, 3 * s1:]], axis=1).astype(BF16)
        qg = jnp.tile(da_q_norm_g[i], 2)[None]
        kg = jnp.tile(da_k_norm_g[i], 2)[None]
        pa, qs, kh, vt, pc = _inproj(h, mix_norm_g[i][None], w_cat, rc, rs1, rs2, qg, kg,
                                     tm=tm, lp=lp)

        mu = tm_mu[i]
        mu_p = jnp.concatenate([_pad_heads(mu[0:s1], 0), _pad_heads(mu[s1:2 * s1], 0),
                                _pad_heads(mu[2 * s1:3 * s1], 0), mu[3 * s1:]])[None]
        vecs = jnp.stack([_pad_heads(tm_w0[i], 0), _pad_heads(tm_a0[i], 0),
                          _pad_heads(tm_k_k[i], 0), _pad_heads(tm_k_a[i], 0),
                          _pad_heads(tm_r_k[i].reshape(s1), 0), _pad_heads(tm_gn_g[i], 0),
                          _pad_heads(tm_gn_b[i], 0), jnp.zeros((RWKV_PAD,), F32)])
        zl = lambda n: jnp.zeros((n, RWKV_PAD), F32)
        wd_p = jnp.concatenate([_pad_heads(tm_w_decay_up[i], 1), zl(AAA_LORA + GATE_LORA)], axis=0)
        wa_p = jnp.concatenate([zl(DECAY_LORA), _pad_heads(tm_w_a_up[i], 1), zl(GATE_LORA)], axis=0)
        wg_p = jnp.concatenate([zl(DECAY_LORA + AAA_LORA), _pad_heads(tm_w_g_up[i], 1)], axis=0)
        oa = _rwkv(pa.reshape(b, lp, PA_COLS), mu_p, vecs, wd_p, wa_p, wg_p, tb=tm)

        lamv = jnp.zeros((SUBLANES, LANES), F32).at[0:4, :DIFF_QK_DIM].set(
            jnp.stack([da_lambda_q1[i], da_lambda_k1[i], da_lambda_q2[i], da_lambda_k2[i]]))
        ob = _attention(qs.reshape(2, b, lp, DIFF_WIDTH), kh.reshape(b, lp, DIFF_WIDTH),
                        vt.reshape(b, lp // tm, DIFF_HEADS, LANES, tm), lamv, da_subln_g[i][None],
                        tq=tm, lam_init=lam_init)

        wo = w_out[i]
        cw = jnp.zeros((SUBLANES, CONV_WIDTH), F32).at[:CONV_K].set(sc_conv_w[i])
        h = _outproj(h, oa.reshape(r, RWKV_PAD), ob.reshape(r, DIFF_WIDTH), pc, cw,
                     _pad_heads(wo[:s1], 0).astype(BF16), wo[s1:s1 + DIFF_WIDTH].astype(BF16),
                     wo[s1 + DIFF_WIDTH:].astype(BF16), tm=tm, lp=lp)

        j = i // 2
        if i % 2 == 0:
            h = _dense_ffn(h, ffn_norm_g[i][None], ffn_w_gate[j].astype(BF16),
                           ffn_w_up[j].astype(BF16), ffn_w_down[j].astype(BF16), tm=tm)
        else:
            rw = jnp.zeros((d, LANES), F32).at[:, :N_EXPERTS].set(router_w[j])
            hn, route = _router(h, ffn_norm_g[i][None], rw, tm=tm)
            block_expert, row_src, row_dst = _routing_tables(route)
            y = _moe_experts(block_expert, row_src, row_dst, hn, moe_w_gate[j].astype(BF16),
                             moe_w_up[j].astype(BF16), moe_w_down[j].astype(BF16))
            h = _moe_combine(h, route, y, tm=tm)

    return h.reshape(b, lp, d)[:, N_META:l]
```

```python
import functools
import math

import jax
import jax.numpy as jnp
from jax import lax
from jax.experimental import pallas as pl
from jax.experimental.pallas import tpu as pltpu

N_META = 16
RWKV_HEAD_DIM = 64
RWKV_HEADS = 4
RWKV_WIDTH = RWKV_HEADS * RWKV_HEAD_DIM
DECAY_LORA = 32
AAA_LORA = 32
GATE_LORA = 64
LORA_COLS = DECAY_LORA + AAA_LORA + GATE_LORA
DIFF_QK_DIM = 64
DIFF_V_DIM = 128
DIFF_HEADS = 4
DIFF_WIDTH = DIFF_HEADS * DIFF_V_DIM
CONV_WIDTH = 256
CONV_K = 3
ROPE_THETA = 500000.0
ROPE_DIM = DIFF_QK_DIM // 4
N_EXPERTS = 8
TOP_K = 2
MOE_BLOCK = 256
NORM_EPS = 1e-6
RWKV_GN_EPS = 64e-5
SUBLN_EPS = 1e-5
NEG_INF = -1e30

LANES = 128
SUBLANES = 8
MXU_DIM = 256
VMEM_LIMIT_BYTES = 56 * 1024 * 1024

HEAD_PAD = LANES
RWKV_PAD = RWKV_HEADS * HEAD_PAD
PA_COLS = 3 * RWKV_PAD + LORA_COLS
PB_COLS = 3 * DIFF_WIDTH
PC_COLS = 3 * CONV_WIDTH
VT_ROWS = DIFF_V_DIM + 16
LOG2_E = math.log2(math.e)
CHUNK = 64

F32 = jnp.float32
BF16 = jnp.bfloat16
HIGHEST = lax.Precision.HIGHEST


def _dot(a, b):
    return jnp.dot(a.astype(BF16), b.astype(BF16), preferred_element_type=F32)


def _dot_nt(a, b):
    return lax.dot_general(a.astype(BF16), b.astype(BF16), (((1,), (1,)), ((), ())),
                           preferred_element_type=F32)


def _dot_f32(a, b):
    return jnp.dot(a, b, preferred_element_type=F32, precision=HIGHEST)


def _row_tile(lp):
    for t in (384, 256, 128):
        if lp % t == 0:
            return t
    raise ValueError(f"padded length {lp} is not a multiple of {LANES}")


def _params(sem):
    return pltpu.CompilerParams(dimension_semantics=sem, vmem_limit_bytes=VMEM_LIMIT_BYTES)


def _inproj_kernel(h_ref, g_ref, w_ref, rc_ref, rs1_ref, rs2_ref, qg_ref, kg_ref,
                   pa_ref, qs_ref, kh_ref, vt_ref, pc_ref):
    tm = h_ref.shape[0]
    x = h_ref[...]
    ms = jnp.mean(x * x, axis=-1, keepdims=True)
    xn = (x * lax.rsqrt(ms + NORM_EPS) * g_ref[...]).astype(BF16)
    pa_ref[...] = jnp.dot(xn, w_ref[:, :PA_COLS], preferred_element_type=F32)
    pc_ref[...] = jnp.dot(xn, w_ref[:, PA_COLS + PB_COLS:], preferred_element_type=F32)
    pb = jnp.dot(xn, w_ref[:, PA_COLS:PA_COLS + PB_COLS], preferred_element_type=F32)

    lane = lax.broadcasted_iota(jnp.int32, (tm, LANES), 1)
    lo = lane < DIFF_QK_DIM
    rc, rs1, rs2 = rc_ref[...], rs1_ref[...], rs2_ref[...]

    def norm_rope(t, g):
        t2 = t * t
        ss_lo = jnp.sum(jnp.where(lo, t2, 0.0), axis=-1, keepdims=True)
        ss_hi = jnp.sum(jnp.where(lo, 0.0, t2), axis=-1, keepdims=True)
        inv = jnp.where(lo, lax.rsqrt(ss_lo / DIFF_QK_DIM + NORM_EPS),
                        lax.rsqrt(ss_hi / DIFF_QK_DIM + NORM_EPS))
        tn = t * inv * g
        half = ROPE_DIM // 2
        return tn * rc + pltpu.roll(tn, LANES - half, 1) * rs1 + pltpu.roll(tn, half, 1) * rs2

    for hd in range(DIFF_HEADS):
        cs = slice(hd * LANES, (hd + 1) * LANES)
        q = norm_rope(pb[:, cs], qg_ref[...]) * (DIFF_QK_DIM ** -0.5 * LOG2_E)
        qs_ref[0, :, cs] = jnp.where(lo, q, 0.0).astype(BF16)
        qs_ref[1, :, cs] = jnp.where(lo, 0.0, q).astype(BF16)
        kcs = slice(DIFF_WIDTH + hd * LANES, DIFF_WIDTH + (hd + 1) * LANES)
        kh_ref[:, cs] = norm_rope(pb[:, kcs], kg_ref[...]).astype(BF16)
        vcs = slice(2 * DIFF_WIDTH + hd * LANES, 2 * DIFF_WIDTH + (hd + 1) * LANES)
        vt_ref[0, hd, :DIFF_V_DIM] = pb[:, vcs].T.astype(BF16)
        vt_ref[0, hd, DIFF_V_DIM:] = jnp.ones((VT_ROWS - DIFF_V_DIM, tm), BF16)


def _inproj(h, g, w, rc, rs1, rs2, qg, kg, *, tm, lp):
    r, d = h.shape
    nt_b = lp // tm
    row = lambda i: (i, 0)
    const = lambda i: (0, 0)
    rope = lambda i: (i % nt_b, 0)
    return pl.pallas_call(
        _inproj_kernel,
        grid=(r // tm,),
        in_specs=[
            pl.BlockSpec((tm, d), row),
            pl.BlockSpec((1, d), const),
            pl.BlockSpec(w.shape, const),
            pl.BlockSpec((tm, LANES), rope),
            pl.BlockSpec((tm, LANES), rope),
            pl.BlockSpec((tm, LANES), rope),
            pl.BlockSpec((1, LANES), const),
            pl.BlockSpec((1, LANES), const),
        ],
        out_specs=[
            pl.BlockSpec((tm, PA_COLS), row),
            pl.BlockSpec((2, tm, DIFF_WIDTH), lambda i: (0, i, 0)),
            pl.BlockSpec((tm, DIFF_WIDTH), row),
            pl.BlockSpec((1, DIFF_HEADS, VT_ROWS, tm), lambda i: (i, 0, 0, 0)),
            pl.BlockSpec((tm, PC_COLS), row),
        ],
        out_shape=[
            jax.ShapeDtypeStruct((r, PA_COLS), F32),
            jax.ShapeDtypeStruct((2, r, DIFF_WIDTH), BF16),
            jax.ShapeDtypeStruct((r, DIFF_WIDTH), BF16),
            jax.ShapeDtypeStruct((r // tm, DIFF_HEADS, VT_ROWS, tm), BF16),
            jax.ShapeDtypeStruct((r, PC_COLS), F32),
        ],
        compiler_params=_params(("arbitrary",)),
        name="inproj",
    )(h, g, w, rc, rs1, rs2, qg, kg)


def _split3(x):
    hi = x.astype(BF16)
    r1 = x - hi.astype(F32)
    mid = r1.astype(BF16)
    lo = (r1 - mid.astype(F32)).astype(BF16)
    return hi, mid, lo


def _dot_hi(a, b):
    ah = a.astype(BF16)
    al = (a - ah.astype(F32)).astype(BF16)
    bh = b.astype(BF16)
    bl = (b - bh.astype(F32)).astype(BF16)
    d = lambda p, q: jnp.dot(p, q, preferred_element_type=F32)
    return d(ah, bh) + d(ah, bl) + d(al, bh)


def _bmm(a, b):
    return jnp.einsum('cik,ckj->cij', a.astype(BF16), b.astype(BF16), preferred_element_type=F32)


def _bmm_nt(a, b):
    return jnp.einsum('cik,cjk->cij', a.astype(BF16), b.astype(BF16), preferred_element_type=F32)


def _rwkv_kernel(pa_ref, halo_ref, mu_ref, vec_ref, wd_ref, wa_ref, wg_ref, oa_ref, hstate_s):
    tb = pa_ref.shape[0]
    nc = tb // CHUNK
    i = pl.program_id(1)

    @pl.when(i == 0)
    def _():
        hstate_s[...] = jnp.zeros_like(hstate_s)

    x = pa_ref[...]
    prev_row = jnp.where(i == 0, 0.0, halo_ref[SUBLANES - 1:SUBLANES, :])
    row = lax.broadcasted_iota(jnp.int32, (tb, 1), 0)
    prev = jnp.where(row == 0, prev_row, pltpu.roll(x, 1, 0))
    hx = x + (prev - x) * mu_ref[...]
    r = hx[:, 0:RWKV_PAD]
    k = hx[:, RWKV_PAD:2 * RWKV_PAD]
    v = hx[:, 2 * RWKV_PAD:3 * RWKV_PAD]
    lora = hx[:, 3 * RWKV_PAD:]
    w0, a0, k_k, k_a = vec_ref[0:1, :], vec_ref[1:2, :], vec_ref[2:3, :], vec_ref[3:4, :]
    r_k, gn_g, gn_b = vec_ref[4:5, :], vec_ref[5:6, :], vec_ref[6:7, :]

    w = -jax.nn.softplus(-(w0 + _dot_hi(jnp.tanh(lora), wd_ref[...]))) - 0.5
    wl = -jnp.exp(w)
    alr = jax.nn.sigmoid(a0 + _dot(lora, wa_ref[...]))
    gate = _dot(jax.nn.sigmoid(lora), wg_ref[...])
    kk = k * k_k
    k2 = k * (1.0 + (alr - 1.0) * k_a)
    rkr = r * k2 * r_k

    ri = lax.broadcasted_iota(jnp.int32, (CHUNK, CHUNK), 0)
    ci = lax.broadcasted_iota(jnp.int32, (CHUNK, CHUNK), 1)
    low_incl = (ci <= ri)[None]
    low_strict = (ci < ri)[None]
    tril_b = jnp.where(ci <= ri, 1.0, 0.0).astype(BF16)
    eye_k = (lax.broadcasted_iota(jnp.int32, (HEAD_PAD, HEAD_PAD), 0)
             == lax.broadcasted_iota(jnp.int32, (HEAD_PAD, HEAD_PAD), 1))[None]
    n_sq = int(math.log2(CHUNK))

    parts = _split3(wl)
    cums = []
    for c in range(nc):
        rows = slice(c * CHUNK, (c + 1) * CHUNK)
        cums.append(sum(jnp.dot(tril_b, p[rows], preferred_element_type=F32) for p in parts))
    cum_all = jnp.concatenate(cums, axis=0)

    lane = lax.broadcasted_iota(jnp.int32, (tb, HEAD_PAD), 1)
    real = lane < RWKV_HEAD_DIM
    to3 = lambda t: t.reshape(nc, CHUNK, HEAD_PAD)
    for hd in range(RWKV_HEADS):
        cs = slice(hd * HEAD_PAD, (hd + 1) * HEAD_PAD)
        kk_h = kk[:, cs]
        nrm = jnp.sqrt(jnp.sum(kk_h * kk_h, axis=-1, keepdims=True))
        kk_h = kk_h / jnp.maximum(nrm, 1e-12)
        v_h = v[:, cs]
        bonus = jnp.sum(rkr[:, cs], axis=-1, keepdims=True) * v_h

        wl3, cum = to3(wl[:, cs]), to3(cum_all[:, cs])
        rr, kc, vc = to3(r[:, cs]), to3(k2[:, cs]), to3(v_h)
        ac, bc = to3(-kk_h), to3(kk_h * alr[:, cs])
        last = cum[:, CHUNK - 1:CHUNK, :]
        e_neg = jnp.exp(-cum)
        e_last = jnp.exp(last - cum)
        ah = ac * jnp.exp(cum - wl3)
        rh = rr * jnp.exp(cum)
        ar = jnp.concatenate([ah, rh], axis=1)
        gb = _bmm_nt(ar, bc * e_neg)
        gk = _bmm_nt(ar, kc * e_neg)
        a_ab = jnp.where(low_strict, gb[:, :CHUNK], 0.0)
        r_b = jnp.where(low_incl, gb[:, CHUNK:], 0.0)
        a_ak = jnp.where(low_strict, gk[:, :CHUNK], 0.0)
        r_kk = jnp.where(low_incl, gk[:, CHUNK:], 0.0)
        xx = jnp.concatenate([_bmm(a_ak, vc), ah], axis=2)
        pp = a_ab
        for j in range(n_sq):
            xx = xx + _bmm(pp, xx)
            if j + 1 < n_sq:
                pp = _bmm(pp, pp)
        rbx = _bmm(r_b, xx)
        q = rh + rbx[:, :, HEAD_PAD:]
        y0 = rbx[:, :, :HEAD_PAD] + _bmm(r_kk, vc)
        btx = _bmm(jnp.swapaxes(bc * e_last, 1, 2), xx)
        m = jnp.where(eye_k, jnp.exp(last), 0.0) + btx[:, :, HEAD_PAD:]
        n0 = btx[:, :, :HEAD_PAD] + _bmm(jnp.swapaxes(kc * e_last, 1, 2), vc)
        qm = jnp.concatenate([q, m], axis=1).astype(BF16)

        hs = hstate_s[hd]
        ys = []
        for c in range(nc):
            res = jnp.dot(qm[c], hs.astype(BF16), preferred_element_type=F32)
            ys.append(res[:CHUNK] + y0[c])
            hs = res[CHUNK:] + n0[c]
        hstate_s[hd] = hs
        y = jnp.concatenate(ys, axis=0)

        mean = jnp.sum(y, axis=-1, keepdims=True) / RWKV_HEAD_DIM
        dlt = jnp.where(real, y - mean, 0.0)
        var = jnp.sum(dlt * dlt, axis=-1, keepdims=True) / RWKV_HEAD_DIM
        yn = dlt * lax.rsqrt(var + RWKV_GN_EPS) * gn_g[:, cs] + gn_b[:, cs]
        oa_ref[:, cs] = ((yn + bonus) * gate[:, cs]).astype(BF16)


def _rwkv(pa, mu, vecs, wd, wa, wg, *, tb):
    b, lp, _ = pa.shape
    tile = lambda bi, i: (bi, i, 0)
    const = lambda bi, i: (0, 0)
    halo = lambda bi, i: (bi, jnp.maximum(i * (tb // SUBLANES) - 1, 0), 0)
    return pl.pallas_call(
        _rwkv_kernel,
        grid=(b, lp // tb),
        in_specs=[
            pl.BlockSpec((None, tb, PA_COLS), tile),
            pl.BlockSpec((None, SUBLANES, PA_COLS), halo),
            pl.BlockSpec((1, PA_COLS), const),
            pl.BlockSpec((SUBLANES, RWKV_PAD), const),
            pl.BlockSpec((LORA_COLS, RWKV_PAD), const),
            pl.BlockSpec((LORA_COLS, RWKV_PAD), const),
            pl.BlockSpec((LORA_COLS, RWKV_PAD), const),
        ],
        out_specs=pl.BlockSpec((None, tb, RWKV_PAD), tile),
        out_shape=jax.ShapeDtypeStruct((b, lp, RWKV_PAD), BF16),
        scratch_shapes=[pltpu.VMEM((RWKV_HEADS, HEAD_PAD, HEAD_PAD), F32)],
        compiler_params=_params(("arbitrary", "arbitrary")),
        name="rwkv7",
    )(pa, pa, mu, vecs, wd, wa, wg)


def _attn_kernel(q_ref, k_ref, vt_ref, lamv_ref, sg_ref, o_ref, acc_s, st_s, p_s, *, lam_init):
    tq = o_ref.shape[0]
    tk = vt_ref.shape[2]
    qi = pl.program_id(2)

    def scores(ki):
        k = k_ref[pl.ds(pl.multiple_of(ki * tk, tk), tk), :]
        return lax.dot_general(k, q_ref[...].reshape(2 * tq, LANES), (((1,), (1,)), ((), ())),
                               preferred_element_type=F32)

    def values(ki, p, alpha):
        acc_s[...] = alpha * acc_s[...] + jnp.dot(vt_ref[ki], p, preferred_element_type=F32)

    st = scores(qi)

    @pl.when(qi > 0)
    def _():
        st_s[0] = scores(0)

    kr = lax.broadcasted_iota(jnp.int32, (tk, 2 * tq), 0)
    qc = lax.broadcasted_iota(jnp.int32, (tk, 2 * tq), 1)
    qc = jnp.where(qc >= tq, qc - tq, qc)
    st = jnp.where(kr <= qc, st, NEG_INF)
    m0 = jnp.max(st, axis=0, keepdims=True)
    p_s[1] = jnp.exp2(st - m0).astype(BF16)
    acc_s[...] = jnp.zeros_like(acc_s)

    def trip(k, carry, slot):
        m_prev, alpha_prev = carry
        st_s[1 - slot] = scores(jnp.minimum(k + 1, qi - 1))
        s_k = st_s[slot]
        m_new = jnp.maximum(m_prev, jnp.max(s_k, axis=0, keepdims=True))
        alpha = jnp.exp2(m_prev - m_new)
        values(jnp.where(k == 0, qi, k - 1), p_s[1 - slot], alpha_prev)
        p_s[slot] = jnp.exp2(s_k - m_new).astype(BF16)
        return m_new, alpha

    carry = lax.fori_loop(0, qi // 2, lambda j, c: trip(2 * j + 1, trip(2 * j, c, 0), 1),
                          (m0, jnp.ones_like(m0)))
    odd = qi % 2 == 1
    _, alpha_last = lax.cond(odd, lambda c: trip(qi - 1, c, 0), lambda c: c, carry)

    @pl.when(odd)
    def _():
        values(qi - 1, p_s[0], alpha_last)

    @pl.when(jnp.logical_not(odd))
    def _():
        values(jnp.where(qi == 0, 0, qi - 1), p_s[1], alpha_last)

    acc = acc_s[...]
    ot = acc[:DIFF_V_DIM] / acc[DIFF_V_DIM:DIFF_V_DIM + 1]
    lv = lamv_ref[...]
    lam = (jnp.exp(jnp.sum(lv[0:1] * lv[1:2], axis=-1, keepdims=True))
           - jnp.exp(jnp.sum(lv[2:3] * lv[3:4], axis=-1, keepdims=True)) + lam_init)
    od = (ot[:, :tq] - lam * ot[:, tq:]).T
    ms = jnp.mean(od * od, axis=-1, keepdims=True)
    o_ref[...] = (od * lax.rsqrt(ms + SUBLN_EPS) * sg_ref[...] * (1.0 - lam_init)).astype(BF16)


def _attention(qs, kh, vt, lamv, sg, *, tq, lam_init):
    _, b, lp, _ = qs.shape
    nq = lp // tq
    const = lambda bi, hd, qi: (0, 0)
    return pl.pallas_call(
        functools.partial(_attn_kernel, lam_init=lam_init),
        grid=(b, DIFF_HEADS, nq),
        in_specs=[
            pl.BlockSpec((2, None, tq, LANES), lambda bi, hd, qi: (0, bi, qi, hd)),
            pl.BlockSpec((None, lp, LANES), lambda bi, hd, qi: (bi, 0, hd)),
            pl.BlockSpec((None, nq, None, VT_ROWS, tq), lambda bi, hd, qi: (bi, 0, hd, 0, 0)),
            pl.BlockSpec((SUBLANES, LANES), const),
            pl.BlockSpec((1, LANES), const),
        ],
        out_specs=pl.BlockSpec((None, tq, LANES), lambda bi, hd, qi: (bi, qi, hd)),
        out_shape=jax.ShapeDtypeStruct((b, lp, DIFF_WIDTH), BF16),
        scratch_shapes=[pltpu.VMEM((VT_ROWS, 2 * tq), F32), pltpu.VMEM((2, tq, 2 * tq), F32),
                        pltpu.VMEM((2, tq, 2 * tq), BF16)],
        compiler_params=_params(("arbitrary", "arbitrary", "arbitrary")),
        name="diffattn",
    )(qs, kh, vt, lamv, sg)


def _outproj_kernel(h_ref, oa_ref, ob_ref, pc_ref, pch_ref, cw_ref, wa_ref, wb_ref, wc_ref,
                    o_ref, *, tiles_per_batch):
    tm = h_ref.shape[0]
    first = (pl.program_id(0) % tiles_per_batch) == 0
    pc = pc_ref[...]
    bg = pc[:, :CONV_WIDTH]
    z = pc[:, CONV_WIDTH:2 * CONV_WIDTH] * pc[:, 2 * CONV_WIDTH:]
    ph = pch_ref[...]
    zh = jnp.where(first, 0.0, ph[:, CONV_WIDTH:2 * CONV_WIDTH] * ph[:, 2 * CONV_WIDTH:])
    row = lax.broadcasted_iota(jnp.int32, (tm, 1), 0)
    z1 = jnp.where(row == 0, zh[SUBLANES - 1:SUBLANES], pltpu.roll(z, 1, 0))
    z2 = jnp.where(row == 0, zh[SUBLANES - 2:SUBLANES - 1],
                   jnp.where(row == 1, zh[SUBLANES - 1:SUBLANES], pltpu.roll(z, 2, 0)))
    cw = cw_ref[...]
    y = cw[0:1] * z2 + cw[1:2] * z1 + cw[2:3] * z
    oc = (bg * y).astype(BF16)
    o_ref[...] = (h_ref[...]
                  + jnp.dot(oa_ref[...], wa_ref[...], preferred_element_type=F32)
                  + jnp.dot(ob_ref[...], wb_ref[...], preferred_element_type=F32)
                  + jnp.dot(oc, wc_ref[...], preferred_element_type=F32))


def _outproj(h, oa, ob, pc, cw, wa, wb, wc, *, tm, lp):
    r, d = h.shape
    row = lambda i: (i, 0)
    const = lambda i: (0, 0)
    halo = lambda i: (jnp.maximum(i * (tm // SUBLANES) - 1, 0), 0)
    return pl.pallas_call(
        functools.partial(_outproj_kernel, tiles_per_batch=lp // tm),
        grid=(r // tm,),
        in_specs=[
            pl.BlockSpec((tm, d), row),
            pl.BlockSpec((tm, RWKV_PAD), row),
            pl.BlockSpec((tm, DIFF_WIDTH), row),
            pl.BlockSpec((tm, PC_COLS), row),
            pl.BlockSpec((SUBLANES, PC_COLS), halo),
            pl.BlockSpec((SUBLANES, CONV_WIDTH), const),
            pl.BlockSpec(wa.shape, const),
            pl.BlockSpec(wb.shape, const),
            pl.BlockSpec(wc.shape, const),
        ],
        out_specs=pl.BlockSpec((tm, d), row),
        out_shape=jax.ShapeDtypeStruct((r, d), F32),
        compiler_params=_params(("arbitrary",)),
        name="outproj",
    )(h, oa, ob, pc, pc, cw, wa, wb, wc)


def _swiglu_block(xb, wg_ref, wu_ref, wd_ref, side_work=None):
    d_ff = wg_ref.shape[1]
    acc = jnp.zeros((xb.shape[0], wd_ref.shape[1]), F32)
    for c in range(d_ff // MXU_DIM):
        cs = slice(c * MXU_DIM, (c + 1) * MXU_DIM)
        gt = jnp.dot(xb, wg_ref[:, cs], preferred_element_type=F32)
        up = jnp.dot(xb, wu_ref[:, cs], preferred_element_type=F32)
        act = (jax.nn.silu(gt) * up).astype(BF16)
        acc = acc + jnp.dot(act, wd_ref[cs, :], preferred_element_type=F32)
        if side_work is not None:
            side_work(c)
    return acc


def _rms(x, g):
    ms = jnp.mean(x * x, axis=-1, keepdims=True)
    return x * lax.rsqrt(ms + NORM_EPS) * g


def _ffn_kernel(h_ref, g_ref, wg_ref, wu_ref, wd_ref, o_ref):
    x = h_ref[...]
    xn = _rms(x, g_ref[...]).astype(BF16)
    o_ref[...] = x + _swiglu_block(xn, wg_ref, wu_ref, wd_ref)


def _dense_ffn(h, g, wg, wu, wd, *, tm):
    r, d = h.shape
    row = lambda i: (i, 0)
    const = lambda i: (0, 0)
    return pl.pallas_call(
        _ffn_kernel,
        grid=(r // tm,),
        in_specs=[
            pl.BlockSpec((tm, d), row),
            pl.BlockSpec((1, d), const),
            pl.BlockSpec(wg.shape, const),
            pl.BlockSpec(wu.shape, const),
            pl.BlockSpec(wd.shape, const),
        ],
        out_specs=pl.BlockSpec((tm, d), row),
        out_shape=jax.ShapeDtypeStruct((r, d), F32),
        compiler_params=_params(("arbitrary",)),
        name="dense_ffn",
    )(h, g, wg, wu, wd)


def _router_kernel(h_ref, g_ref, rw_ref, hn_ref, route_ref):
    tm = h_ref.shape[0]
    xn = _rms(h_ref[...], g_ref[...])
    hn_ref[...] = xn
    logits = _dot_f32(xn, rw_ref[...])
    lane = lax.broadcasted_iota(jnp.int32, (tm, LANES), 1)
    logits = jnp.where(lane < N_EXPERTS, logits, -jnp.inf)
    m1 = jnp.max(logits, axis=-1, keepdims=True)
    i1 = jnp.min(jnp.where(logits == m1, lane, LANES), axis=-1, keepdims=True)
    rest = jnp.where(lane == i1, -jnp.inf, logits)
    m2 = jnp.max(rest, axis=-1, keepdims=True)
    i2 = jnp.min(jnp.where(rest == m2, lane, LANES), axis=-1, keepdims=True)
    e2 = jnp.exp(m2 - m1)
    den = 1.0 + e2
    route_ref[...] = jnp.where(lane == 0, i1.astype(F32),
                     jnp.where(lane == 1, i2.astype(F32),
                     jnp.where(lane == 2, 1.0 / den,
                     jnp.where(lane == 3, e2 / den, 0.0))))


def _router(h, g, rw, *, tm):
    r, d = h.shape
    row = lambda i: (i, 0)
    const = lambda i: (0, 0)
    return pl.pallas_call(
        _router_kernel,
        grid=(r // tm,),
        in_specs=[pl.BlockSpec((tm, d), row), pl.BlockSpec((1, d), const),
                  pl.BlockSpec((d, LANES), const)],
        out_specs=[pl.BlockSpec((tm, d), row), pl.BlockSpec((tm, LANES), row)],
        out_shape=[jax.ShapeDtypeStruct((r, d), F32), jax.ShapeDtypeStruct((r, LANES), F32)],
        compiler_params=_params(("arbitrary",)),
        name="router",
    )(h, g, rw)


def _row_copy(src, dst, sem, src_row, dst_row):
    return pltpu.make_async_copy(src.at[pl.ds(src_row, 1), :], dst.at[pl.ds(dst_row, 1), :], sem)


DMA_UNROLL = 8
DMA_ISSUE_SLICES = 8


def _moe_kernel(bexp_ref, rsrc_ref, rdst_ref, hn_hbm, wg_ref, wu_ref, wd_ref, y_hbm,
                xbuf, ybuf, gsem, ssem):
    del bexp_ref
    blk = xbuf.shape[1]
    i = pl.program_id(0)
    nb = pl.num_programs(0)
    slot = i % 2
    other = 1 - slot
    dump0 = y_hbm.shape[0] - blk

    def for_rows(fn):
        def body(j, c):
            for u in range(DMA_UNROLL):
                fn(j * DMA_UNROLL + u)
            return c
        lax.fori_loop(0, blk // DMA_UNROLL, body, 0)

    def gather(block, s, j):
        return _row_copy(hn_hbm, xbuf.at[s], gsem.at[s], rsrc_ref[block * blk + j], j)

    def scatter(dst_row, s, j):
        return _row_copy(ybuf.at[s], y_hbm, ssem.at[s], j, dst_row)

    @pl.when(i == 0)
    def _():
        for_rows(lambda j: gather(0, 0, j).start())
        ybuf[1] = jnp.zeros(ybuf.shape[1:], ybuf.dtype)

    for_rows(lambda j: gather(0, slot, j).wait())

    @pl.when(i >= 1)
    def _():
        for_rows(lambda j: scatter(0, slot, j).wait())

    nxt = jnp.minimum(i + 1, nb - 1)
    prev = jnp.maximum(i - 1, 0)
    per = -(-blk // DMA_ISSUE_SLICES)

    def side_work(c):
        for j in range(c * per, min((c + 1) * per, blk)):
            gather(nxt, other, j).start(priority=j % 2)
            dst = jnp.where(i == 0, dump0 + j, rdst_ref[prev * blk + j])
            scatter(dst, other, j).start(priority=(j + 1) % 2)

    ybuf[slot] = _swiglu_block(xbuf[slot].astype(BF16), wg_ref, wu_ref, wd_ref, side_work)

    @pl.when(i == nb - 1)
    def _():
        for_rows(lambda j: gather(0, other, j).wait())
        for_rows(lambda j: scatter(0, other, j).wait())
        for_rows(lambda j: scatter(rdst_ref[i * blk + j], slot, j).start())
        for_rows(lambda j: scatter(0, slot, j).wait())


def _moe_experts(block_expert, row_src, row_dst, hn, wg, wu, wd):
    nb = block_expert.shape[0]
    r, d = hn.shape
    d_ff = wg.shape[2]
    wmap = lambda i, be, rs, rd: (be[i], 0, 0)
    grid_spec = pltpu.PrefetchScalarGridSpec(
        num_scalar_prefetch=3,
        grid=(nb,),
        in_specs=[
            pl.BlockSpec(memory_space=pl.ANY),
            pl.BlockSpec((None, d, d_ff), wmap),
            pl.BlockSpec((None, d, d_ff), wmap),
            pl.BlockSpec((None, d_ff, d), wmap),
        ],
        out_specs=pl.BlockSpec(memory_space=pl.ANY),
        scratch_shapes=[pltpu.VMEM((2, MOE_BLOCK, d), F32), pltpu.VMEM((2, MOE_BLOCK, d), F32),
                        pltpu.SemaphoreType.DMA((2,)), pltpu.SemaphoreType.DMA((2,))],
    )
    return pl.pallas_call(
        _moe_kernel,
        grid_spec=grid_spec,
        out_shape=jax.ShapeDtypeStruct((TOP_K * r + MOE_BLOCK, d), F32),
        compiler_params=_params(("arbitrary",)),
        name="moe_experts",
    )(block_expert, row_src, row_dst, hn, wg, wu, wd)


def _combine_kernel(h_ref, route_ref, y0_ref, y1_ref, o_ref):
    route = route_ref[...]
    o_ref[...] = h_ref[...] + route[:, 2:3] * y0_ref[...] + route[:, 3:4] * y1_ref[...]


def _moe_combine(h, route, y, *, tm):
    r, d = h.shape
    nt = r // tm
    row = lambda i: (i, 0)
    return pl.pallas_call(
        _combine_kernel,
        grid=(nt,),
        in_specs=[
            pl.BlockSpec((tm, d), row),
            pl.BlockSpec((tm, LANES), row),
            pl.BlockSpec((tm, d), row),
            pl.BlockSpec((tm, d), lambda i: (i + nt, 0)),
        ],
        out_specs=pl.BlockSpec((tm, d), row),
        out_shape=jax.ShapeDtypeStruct((r, d), F32),
        compiler_params=_params(("arbitrary",)),
        name="moe_combine",
    )(h, route, y, y)


def _slot_owner_kernel(dest_ref, owner_ref):
    n_pairs = dest_ref.shape[0]
    n_slots = owner_ref.shape[0]

    def clear(i, c):
        for u in range(DMA_UNROLL):
            owner_ref[i * DMA_UNROLL + u] = -1
        return c

    lax.fori_loop(0, n_slots // DMA_UNROLL, clear, 0)

    def place(i, c):
        for u in range(DMA_UNROLL):
            p = i * DMA_UNROLL + u
            owner_ref[dest_ref[p]] = p
        return c

    lax.fori_loop(0, n_pairs // DMA_UNROLL, place, 0)


def _slot_owner(dest, n_slots):
    assert dest.shape[0] % DMA_UNROLL == 0 and n_slots % DMA_UNROLL == 0
    smem = pl.BlockSpec(memory_space=pltpu.SMEM)
    return pl.pallas_call(
        _slot_owner_kernel,
        in_specs=[smem],
        out_specs=smem,
        out_shape=jax.ShapeDtypeStruct((n_slots,), jnp.int32),
        name="slot_owner",
    )(dest)


def _routing_tables(route):
    r = route.shape[0]
    tk = r * TOP_K
    flat_e = route[:, :TOP_K].reshape(tk).astype(jnp.int32)
    onehot = (flat_e[:, None] == jnp.arange(N_EXPERTS, dtype=jnp.int32)[None, :]).astype(jnp.int32)
    csum = jnp.cumsum(onehot, axis=0)
    rank = jnp.sum(csum * onehot, axis=1) - 1
    counts = csum[-1]
    padded = (counts + MOE_BLOCK - 1) // MOE_BLOCK * MOE_BLOCK
    pend = jnp.cumsum(padded)
    pstart = pend - padded
    dest = pstart[flat_e] + rank
    nb = -(-tk // MOE_BLOCK) + N_EXPERTS
    n_slots = nb * MOE_BLOCK
    pair = _slot_owner(dest, n_slots)
    used = pair >= 0
    row_src = jnp.where(used, pair // TOP_K, 0)
    row_dst = jnp.where(used, (pair % TOP_K) * r + pair // TOP_K,
                        tk + jnp.arange(n_slots, dtype=jnp.int32) % MOE_BLOCK)
    block_start = jnp.arange(nb, dtype=jnp.int32) * MOE_BLOCK
    block_expert = jnp.minimum(jnp.searchsorted(pend, block_start, side='right'),
                               N_EXPERTS - 1).astype(jnp.int32)
    return block_expert, row_src, row_dst


def _pad_heads(t, axis):
    axis = axis % t.ndim
    shp = t.shape
    t = t.reshape(shp[:axis] + (RWKV_HEADS, RWKV_HEAD_DIM) + shp[axis + 1:])
    pad = [(0, 0)] * t.ndim
    pad[axis + 1] = (0, HEAD_PAD - RWKV_HEAD_DIM)
    t = jnp.pad(t, pad)
    return t.reshape(shp[:axis] + (RWKV_PAD,) + shp[axis + 1:])


def _rope_tables(lp):
    half = ROPE_DIM // 2
    inv_freq = ROPE_THETA ** (-jnp.arange(0, ROPE_DIM, 2, dtype=F32) / ROPE_DIM)
    ang = jnp.arange(lp, dtype=F32)[:, None] * inv_freq[None, :]
    cos, sin = jnp.cos(ang), jnp.sin(ang)
    ones = jnp.ones((lp, DIFF_QK_DIM - ROPE_DIM), F32)
    zeros = jnp.zeros((lp, DIFF_QK_DIM - ROPE_DIM), F32)
    zh = jnp.zeros((lp, half), F32)
    rc = jnp.concatenate([cos, cos, ones], axis=1)
    rs1 = jnp.concatenate([-sin, zh, zeros], axis=1)
    rs2 = jnp.concatenate([zh, sin, zeros], axis=1)
    tile2 = lambda a: jnp.concatenate([a, a], axis=1)
    return tile2(rc), tile2(rs1), tile2(rs2)


def kernel(x, meta_tokens, mix_norm_g, w_in, tm_mu, tm_w0, tm_w_decay_up, tm_a0, tm_w_a_up, tm_w_g_up, tm_k_k, tm_k_a, tm_r_k, tm_gn_g, tm_gn_b, da_q_norm_g, da_k_norm_g, da_lambda_q1, da_lambda_k1, da_lambda_q2, da_lambda_k2, da_subln_g, sc_conv_w, w_out, ffn_norm_g, ffn_w_gate, ffn_w_up, ffn_w_down, router_w, moe_w_gate, moe_w_up, moe_w_down):
    b, seq, d = x.shape
    depth = w_in.shape[0]
    l = N_META + seq
    lp = -(-l // LANES) * LANES
    tm = _row_tile(lp)
    r = b * lp
    s1 = RWKV_WIDTH

    meta = jnp.broadcast_to(meta_tokens.astype(x.dtype)[None], (b, N_META, d))
    h = jnp.concatenate([meta, x, jnp.zeros((b, lp - l, d), x.dtype)], axis=1).reshape(r, d)
    rc, rs1, rs2 = _rope_tables(lp)

    for i in range(depth):
        lam_init = 0.8 - 0.6 * math.exp(-0.3 * i)
        wi = w_in[i]
        w_cat = jnp.concatenate(
            [_pad_heads(wi[:, 0:s1], 1), _pad_heads(wi[:, s1:2 * s1], 1),
             _pad_heads(wi[:, 2 * s1:3 * s1], 1), wi[:, 3 * s1:]], axis=1).astype(BF16)
        qg = jnp.tile(da_q_norm_g[i], 2)[None]
        kg = jnp.tile(da_k_norm_g[i], 2)[None]
        pa, qs, kh, vt, pc = _inproj(h, mix_norm_g[i][None], w_cat, rc, rs1, rs2, qg, kg,
                                     tm=tm, lp=lp)

        mu = tm_mu[i]
        mu_p = jnp.concatenate([_pad_heads(mu[0:s1], 0), _pad_heads(mu[s1:2 * s1], 0),
                                _pad_heads(mu[2 * s1:3 * s1], 0), mu[3 * s1:]])[None]
        vecs = jnp.stack([_pad_heads(tm_w0[i], 0), _pad_heads(tm_a0[i], 0),
                          _pad_heads(tm_k_k[i], 0), _pad_heads(tm_k_a[i], 0),
                          _pad_heads(tm_r_k[i].reshape(s1), 0), _pad_heads(tm_gn_g[i], 0),
                          _pad_heads(tm_gn_b[i], 0), jnp.zeros((RWKV_PAD,), F32)])
        zl = lambda n: jnp.zeros((n, RWKV_PAD), F32)
        wd_p = jnp.concatenate([_pad_heads(tm_w_decay_up[i], 1), zl(AAA_LORA + GATE_LORA)], axis=0)
        wa_p = jnp.concatenate([zl(DECAY_LORA), _pad_heads(tm_w_a_up[i], 1), zl(GATE_LORA)], axis=0)
        wg_p = jnp.concatenate([zl(DECAY_LORA + AAA_LORA), _pad_heads(tm_w_g_up[i], 1)], axis=0)
        oa = _rwkv(pa.reshape(b, lp, PA_COLS), mu_p, vecs, wd_p, wa_p, wg_p, tb=tm)

        lamv = jnp.zeros((SUBLANES, LANES), F32).at[0:4, :DIFF_QK_DIM].set(
            jnp.stack([da_lambda_q1[i], da_lambda_k1[i], da_lambda_q2[i], da_lambda_k2[i]]))
        ob = _attention(qs.reshape(2, b, lp, DIFF_WIDTH), kh.reshape(b, lp, DIFF_WIDTH),
                        vt.reshape(b, lp // tm, DIFF_HEADS, VT_ROWS, tm), lamv, da_subln_g[i][None],
                        tq=tm, lam_init=lam_init)

        wo = w_out[i]
        cw = jnp.zeros((SUBLANES, CONV_WIDTH), F32).at[:CONV_K].set(sc_conv_w[i])
        h = _outproj(h, oa.reshape(r, RWKV_PAD), ob.reshape(r, DIFF_WIDTH), pc, cw,
                     _pad_heads(wo[:s1], 0).astype(BF16), wo[s1:s1 + DIFF_WIDTH].astype(BF16),
                     wo[s1 + DIFF_WIDTH:].astype(BF16), tm=tm, lp=lp)

        j = i // 2
        if i % 2 == 0:
            h = _dense_ffn(h, ffn_norm_g[i][None], ffn_w_gate[j].astype(BF16),
                           ffn_w_up[j].astype(BF16), ffn_w_down[j].astype(BF16), tm=tm)
        else:
            rw = jnp.zeros((d, LANES), F32).at[:, :N_EXPERTS].set(router_w[j])
            hn, route = _router(h, ffn_norm_g[i][None], rw, tm=tm)
            block_expert, row_src, row_dst = _routing_tables(route)
            y = _moe_experts(block_expert, row_src, row_dst, hn, moe_w_gate[j].astype(BF16),
                             moe_w_up[j].astype(BF16), moe_w_down[j].astype(BF16))
            h = _moe_combine(h, route, y, tm=tm)

    return h.reshape(b, lp, d)[:, N_META:l]
```

```python
import functools
import math

import jax
import jax.numpy as jnp
from jax import lax
from jax.experimental import pallas as pl
from jax.experimental.pallas import tpu as pltpu

N_META = 16
RWKV_HEAD_DIM = 64
RWKV_HEADS = 4
RWKV_WIDTH = RWKV_HEADS * RWKV_HEAD_DIM
DECAY_LORA = 32
AAA_LORA = 32
GATE_LORA = 64
LORA_COLS = DECAY_LORA + AAA_LORA + GATE_LORA
DIFF_QK_DIM = 64
DIFF_V_DIM = 128
DIFF_HEADS = 4
DIFF_WIDTH = DIFF_HEADS * DIFF_V_DIM
CONV_WIDTH = 256
CONV_K = 3
ROPE_THETA = 500000.0
ROPE_DIM = DIFF_QK_DIM // 4
N_EXPERTS = 8
TOP_K = 2
MOE_BLOCK = 256
NORM_EPS = 1e-6
RWKV_GN_EPS = 64e-5
SUBLN_EPS = 1e-5
NEG_INF = -1e30

LANES = 128
SUBLANES = 8
MXU_DIM = 256
VMEM_LIMIT_BYTES = 56 * 1024 * 1024

HEAD_PAD = LANES
RWKV_PAD = RWKV_HEADS * HEAD_PAD
PA_COLS = 3 * RWKV_PAD + LORA_COLS
PB_COLS = 3 * DIFF_WIDTH
PC_COLS = 3 * CONV_WIDTH
VT_ROWS = DIFF_V_DIM + 16
LOG2_E = math.log2(math.e)
CHUNK = 64

F32 = jnp.float32
BF16 = jnp.bfloat16
HIGHEST = lax.Precision.HIGHEST


def _dot(a, b):
    return jnp.dot(a.astype(BF16), b.astype(BF16), preferred_element_type=F32)


def _dot_nt(a, b):
    return lax.dot_general(a.astype(BF16), b.astype(BF16), (((1,), (1,)), ((), ())),
                           preferred_element_type=F32)


def _dot_f32(a, b):
    return jnp.dot(a, b, preferred_element_type=F32, precision=HIGHEST)


def _row_tile(lp):
    for t in (384, 256, 128):
        if lp % t == 0:
            return t
    raise ValueError(f"padded length {lp} is not a multiple of {LANES}")


def _params(sem):
    return pltpu.CompilerParams(dimension_semantics=sem, vmem_limit_bytes=VMEM_LIMIT_BYTES)


def _inproj_kernel(h_ref, g_ref, w_ref, rc_ref, rs1_ref, rs2_ref, qg_ref, kg_ref,
                   pa_ref, qs_ref, kh_ref, vt_ref, pc_ref):
    tm = h_ref.shape[0]
    x = h_ref[...]
    ms = jnp.mean(x * x, axis=-1, keepdims=True)
    xn = (x * lax.rsqrt(ms + NORM_EPS) * g_ref[...]).astype(BF16)
    pa_ref[...] = jnp.dot(xn, w_ref[:, :PA_COLS], preferred_element_type=F32)
    pc_ref[...] = jnp.dot(xn, w_ref[:, PA_COLS + PB_COLS:], preferred_element_type=F32)
    pb = jnp.dot(xn, w_ref[:, PA_COLS:PA_COLS + PB_COLS], preferred_element_type=F32)

    lane = lax.broadcasted_iota(jnp.int32, (tm, LANES), 1)
    lo = lane < DIFF_QK_DIM
    rc, rs1, rs2 = rc_ref[...], rs1_ref[...], rs2_ref[...]

    def norm_rope(t, g):
        t2 = t * t
        ss_lo = jnp.sum(jnp.where(lo, t2, 0.0), axis=-1, keepdims=True)
        ss_hi = jnp.sum(jnp.where(lo, 0.0, t2), axis=-1, keepdims=True)
        inv = jnp.where(lo, lax.rsqrt(ss_lo / DIFF_QK_DIM + NORM_EPS),
                        lax.rsqrt(ss_hi / DIFF_QK_DIM + NORM_EPS))
        tn = t * inv * g
        half = ROPE_DIM // 2
        return tn * rc + pltpu.roll(tn, LANES - half, 1) * rs1 + pltpu.roll(tn, half, 1) * rs2

    for hd in range(DIFF_HEADS):
        cs = slice(hd * LANES, (hd + 1) * LANES)
        q = norm_rope(pb[:, cs], qg_ref[...]) * (DIFF_QK_DIM ** -0.5 * LOG2_E)
        qs_ref[0, :, cs] = jnp.where(lo, q, 0.0).astype(BF16)
        qs_ref[1, :, cs] = jnp.where(lo, 0.0, q).astype(BF16)
        kcs = slice(DIFF_WIDTH + hd * LANES, DIFF_WIDTH + (hd + 1) * LANES)
        kh_ref[:, cs] = norm_rope(pb[:, kcs], kg_ref[...]).astype(BF16)
        vcs = slice(2 * DIFF_WIDTH + hd * LANES, 2 * DIFF_WIDTH + (hd + 1) * LANES)
        vt_ref[0, hd, :DIFF_V_DIM] = pb[:, vcs].T.astype(BF16)
        vt_ref[0, hd, DIFF_V_DIM:] = jnp.ones((VT_ROWS - DIFF_V_DIM, tm), BF16)


def _inproj(h, g, w, rc, rs1, rs2, qg, kg, *, tm, lp):
    r, d = h.shape
    nt_b = lp // tm
    row = lambda i: (i, 0)
    const = lambda i: (0, 0)
    rope = lambda i: (i % nt_b, 0)
    return pl.pallas_call(
        _inproj_kernel,
        grid=(r // tm,),
        in_specs=[
            pl.BlockSpec((tm, d), row),
            pl.BlockSpec((1, d), const),
            pl.BlockSpec(w.shape, const),
            pl.BlockSpec((tm, LANES), rope),
            pl.BlockSpec((tm, LANES), rope),
            pl.BlockSpec((tm, LANES), rope),
            pl.BlockSpec((1, LANES), const),
            pl.BlockSpec((1, LANES), const),
        ],
        out_specs=[
            pl.BlockSpec((tm, PA_COLS), row),
            pl.BlockSpec((2, tm, DIFF_WIDTH), lambda i: (0, i, 0)),
            pl.BlockSpec((tm, DIFF_WIDTH), row),
            pl.BlockSpec((1, DIFF_HEADS, VT_ROWS, tm), lambda i: (i, 0, 0, 0)),
            pl.BlockSpec((tm, PC_COLS), row),
        ],
        out_shape=[
            jax.ShapeDtypeStruct((r, PA_COLS), F32),
            jax.ShapeDtypeStruct((2, r, DIFF_WIDTH), BF16),
            jax.ShapeDtypeStruct((r, DIFF_WIDTH), BF16),
            jax.ShapeDtypeStruct((r // tm, DIFF_HEADS, VT_ROWS, tm), BF16),
            jax.ShapeDtypeStruct((r, PC_COLS), F32),
        ],
        compiler_params=_params(("arbitrary",)),
        name="inproj",
    )(h, g, w, rc, rs1, rs2, qg, kg)


def _split3(x):
    hi = x.astype(BF16)
    r1 = x - hi.astype(F32)
    mid = r1.astype(BF16)
    lo = (r1 - mid.astype(F32)).astype(BF16)
    return hi, mid, lo


def _dot_hi(a, b):
    ah = a.astype(BF16)
    al = (a - ah.astype(F32)).astype(BF16)
    bh = b.astype(BF16)
    bl = (b - bh.astype(F32)).astype(BF16)
    d = lambda p, q: jnp.dot(p, q, preferred_element_type=F32)
    return d(ah, bh) + d(ah, bl) + d(al, bh)


def _bmm(a, b):
    return jnp.einsum('cik,ckj->cij', a.astype(BF16), b.astype(BF16), preferred_element_type=F32)


def _bmm_nt(a, b):
    return jnp.einsum('cik,cjk->cij', a.astype(BF16), b.astype(BF16), preferred_element_type=F32)


def _rwkv_kernel(pa_ref, halo_ref, mu_ref, vec_ref, wd_ref, wa_ref, wg_ref, oa_ref, hstate_s):
    tb = pa_ref.shape[0]
    nc = tb // CHUNK
    i = pl.program_id(1)

    @pl.when(i == 0)
    def _():
        hstate_s[...] = jnp.zeros_like(hstate_s)

    x = pa_ref[...]
    prev_row = jnp.where(i == 0, 0.0, halo_ref[SUBLANES - 1:SUBLANES, :])
    row = lax.broadcasted_iota(jnp.int32, (tb, 1), 0)
    prev = jnp.where(row == 0, prev_row, pltpu.roll(x, 1, 0))
    hx = x + (prev - x) * mu_ref[...]
    r = hx[:, 0:RWKV_PAD]
    k = hx[:, RWKV_PAD:2 * RWKV_PAD]
    v = hx[:, 2 * RWKV_PAD:3 * RWKV_PAD]
    lora = hx[:, 3 * RWKV_PAD:]
    w0, a0, k_k, k_a = vec_ref[0:1, :], vec_ref[1:2, :], vec_ref[2:3, :], vec_ref[3:4, :]
    r_k, gn_g, gn_b = vec_ref[4:5, :], vec_ref[5:6, :], vec_ref[6:7, :]

    w = -jax.nn.softplus(-(w0 + _dot_hi(jnp.tanh(lora), wd_ref[...]))) - 0.5
    wl = -jnp.exp(w)
    alr = jax.nn.sigmoid(a0 + _dot(lora, wa_ref[...]))
    gate = _dot(jax.nn.sigmoid(lora), wg_ref[...])
    kk = k * k_k
    k2 = k * (1.0 + (alr - 1.0) * k_a)
    rkr = r * k2 * r_k

    ri = lax.broadcasted_iota(jnp.int32, (CHUNK, CHUNK), 0)
    ci = lax.broadcasted_iota(jnp.int32, (CHUNK, CHUNK), 1)
    lane_c = lax.broadcasted_iota(jnp.int32, (CHUNK, HEAD_PAD), 1)
    row_c = lax.broadcasted_iota(jnp.int32, (CHUNK, HEAD_PAD), 0)
    col_c = jnp.where(lane_c >= CHUNK, lane_c - CHUNK, lane_c)
    low_incl2 = (col_c <= row_c)[None]
    low_strict2 = (col_c < row_c)[None]
    tril_b = jnp.where(ci <= ri, 1.0, 0.0).astype(BF16)
    eye_k = (lax.broadcasted_iota(jnp.int32, (HEAD_PAD, HEAD_PAD), 0)
             == lax.broadcasted_iota(jnp.int32, (HEAD_PAD, HEAD_PAD), 1))[None]
    n_sq = int(math.log2(CHUNK))

    parts = _split3(wl)
    cums = []
    for c in range(nc):
        rows = slice(c * CHUNK, (c + 1) * CHUNK)
        cums.append(sum(jnp.dot(tril_b, p[rows], preferred_element_type=F32) for p in parts))
    cum_all = jnp.concatenate(cums, axis=0)

    lane = lax.broadcasted_iota(jnp.int32, (tb, HEAD_PAD), 1)
    real = lane < RWKV_HEAD_DIM
    to3 = lambda t: t.reshape(nc, CHUNK, HEAD_PAD)
    heads = range(RWKV_HEADS)
    hsl = [slice(hd * HEAD_PAD, (hd + 1) * HEAD_PAD) for hd in heads]
    kkn, bonus = [], []
    for cs in hsl:
        kk_h = kk[:, cs]
        nrm = jnp.sqrt(jnp.sum(kk_h * kk_h, axis=-1, keepdims=True))
        kkn.append(kk_h / jnp.maximum(nrm, 1e-12))
        bonus.append(jnp.sum(rkr[:, cs], axis=-1, keepdims=True) * v[:, cs])

    vc = [to3(v[:, cs]) for cs in hsl]
    zero_v = jnp.zeros_like(vc[0])
    ah, rh, last, g, bk_last = [], [], [], [], []
    for hd, cs in enumerate(hsl):
        wl3, cum = to3(wl[:, cs]), to3(cum_all[:, cs])
        rr, kc = to3(r[:, cs]), to3(k2[:, cs])
        ac, bc = to3(-kkn[hd]), to3(kkn[hd] * alr[:, cs])
        last.append(cum[:, CHUNK - 1:CHUNK, :])
        e_neg = jnp.exp(-cum)
        e_last = jnp.exp(last[hd] - cum)
        ah.append(ac * jnp.exp(cum - wl3))
        rh.append(rr * jnp.exp(cum))
        bk_last.append(jnp.concatenate([bc * e_last, kc * e_last], axis=1))
        g.append(_bmm_nt(jnp.concatenate([ah[hd], rh[hd]], axis=1),
                         jnp.concatenate([bc * e_neg, kc * e_neg], axis=1)))
    a_abk = [jnp.where(low_strict2, g[hd][:, :CHUNK], 0.0) for hd in heads]
    r_bk = [jnp.where(low_incl2, g[hd][:, CHUNK:], 0.0) for hd in heads]
    xx = [jnp.concatenate([_bmm(a_abk[hd], jnp.concatenate([zero_v, vc[hd]], axis=1)), ah[hd]],
                          axis=2) for hd in heads]
    ppad = [jnp.where(lane_c < CHUNK, a_abk[hd], 0.0) for hd in heads]
    for j in range(n_sq):
        for hd in heads:
            pj = ppad[hd][:, :, :CHUNK]
            if j + 1 < n_sq:
                res = _bmm(pj, jnp.concatenate([xx[hd], ppad[hd]], axis=2))
                xx[hd] = xx[hd] + res[:, :, :2 * HEAD_PAD]
                ppad[hd] = res[:, :, 2 * HEAD_PAD:]
            else:
                xx[hd] = xx[hd] + _bmm(pj, xx[hd])
    qm, y0, n0 = [], [], []
    for hd in heads:
        lhs = jnp.concatenate([r_bk[hd], jnp.swapaxes(bk_last[hd], 1, 2)], axis=1)
        rhs = jnp.concatenate([xx[hd], jnp.concatenate([vc[hd], zero_v], axis=2)], axis=1)
        out = _bmm(lhs, rhs)
        y0.append(out[:, :CHUNK, :HEAD_PAD])
        n0.append(out[:, CHUNK:, :HEAD_PAD])
        q = rh[hd] + out[:, :CHUNK, HEAD_PAD:]
        m = jnp.where(eye_k, jnp.exp(last[hd]), 0.0) + out[:, CHUNK:, HEAD_PAD:]
        qm.append(jnp.concatenate([q, m], axis=1).astype(BF16))

    hs = [hstate_s[hd] for hd in heads]
    ys = [[] for _ in heads]
    for c in range(nc):
        for hd in heads:
            res = jnp.dot(qm[hd][c], hs[hd].astype(BF16), preferred_element_type=F32)
            ys[hd].append(res[:CHUNK] + y0[hd][c])
            hs[hd] = res[CHUNK:] + n0[hd][c]

    for hd, cs in enumerate(hsl):
        hstate_s[hd] = hs[hd]
        y = jnp.concatenate(ys[hd], axis=0)
        mean = jnp.sum(y, axis=-1, keepdims=True) / RWKV_HEAD_DIM
        dlt = jnp.where(real, y - mean, 0.0)
        var = jnp.sum(dlt * dlt, axis=-1, keepdims=True) / RWKV_HEAD_DIM
        yn = dlt * lax.rsqrt(var + RWKV_GN_EPS) * gn_g[:, cs] + gn_b[:, cs]
        oa_ref[:, cs] = ((yn + bonus[hd]) * gate[:, cs]).astype(BF16)


def _rwkv(pa, mu, vecs, wd, wa, wg, *, tb):
    b, lp, _ = pa.shape
    tile = lambda bi, i: (bi, i, 0)
    const = lambda bi, i: (0, 0)
    halo = lambda bi, i: (bi, jnp.maximum(i * (tb // SUBLANES) - 1, 0), 0)
    return pl.pallas_call(
        _rwkv_kernel,
        grid=(b, lp // tb),
        in_specs=[
            pl.BlockSpec((None, tb, PA_COLS), tile),
            pl.BlockSpec((None, SUBLANES, PA_COLS), halo),
            pl.BlockSpec((1, PA_COLS), const),
            pl.BlockSpec((SUBLANES, RWKV_PAD), const),
            pl.BlockSpec((LORA_COLS, RWKV_PAD), const),
            pl.BlockSpec((LORA_COLS, RWKV_PAD), const),
            pl.BlockSpec((LORA_COLS, RWKV_PAD), const),
        ],
        out_specs=pl.BlockSpec((None, tb, RWKV_PAD), tile),
        out_shape=jax.ShapeDtypeStruct((b, lp, RWKV_PAD), BF16),
        scratch_shapes=[pltpu.VMEM((RWKV_HEADS, HEAD_PAD, HEAD_PAD), F32)],
        compiler_params=_params(("arbitrary", "arbitrary")),
        name="rwkv7",
    )(pa, pa, mu, vecs, wd, wa, wg)


def _attn_kernel(q_ref, k_ref, vt_ref, lamv_ref, sg_ref, o_ref, acc_s, st_s, p_s, *, lam_init):
    tq = o_ref.shape[0]
    tk = vt_ref.shape[2]
    qi = pl.program_id(2)

    def scores(ki):
        k = k_ref[pl.ds(pl.multiple_of(ki * tk, tk), tk), :]
        return lax.dot_general(k, q_ref[...].reshape(2 * tq, LANES), (((1,), (1,)), ((), ())),
                               preferred_element_type=F32)

    def values(ki, p, alpha):
        acc_s[...] = alpha * acc_s[...] + jnp.dot(vt_ref[ki], p, preferred_element_type=F32)

    st = scores(qi)

    @pl.when(qi > 0)
    def _():
        st_s[0] = scores(0)

    kr = lax.broadcasted_iota(jnp.int32, (tk, 2 * tq), 0)
    qc = lax.broadcasted_iota(jnp.int32, (tk, 2 * tq), 1)
    qc = jnp.where(qc >= tq, qc - tq, qc)
    st = jnp.where(kr <= qc, st, NEG_INF)
    m0 = jnp.max(st, axis=0, keepdims=True)
    p_s[1] = jnp.exp2(st - m0).astype(BF16)
    acc_s[...] = jnp.zeros_like(acc_s)

    def trip(k, carry, slot):
        m_prev, alpha_prev = carry
        st_s[1 - slot] = scores(jnp.minimum(k + 1, qi - 1))
        s_k = st_s[slot]
        m_new = jnp.maximum(m_prev, jnp.max(s_k, axis=0, keepdims=True))
        alpha = jnp.exp2(m_prev - m_new)
        values(jnp.where(k == 0, qi, k - 1), p_s[1 - slot], alpha_prev)
        p_s[slot] = jnp.exp2(s_k - m_new).astype(BF16)
        return m_new, alpha

    carry = lax.fori_loop(0, qi // 2, lambda j, c: trip(2 * j + 1, trip(2 * j, c, 0), 1),
                          (m0, jnp.ones_like(m0)))
    odd = qi % 2 == 1
    _, alpha_last = lax.cond(odd, lambda c: trip(qi - 1, c, 0), lambda c: c, carry)

    @pl.when(odd)
    def _():
        values(qi - 1, p_s[0], alpha_last)

    @pl.when(jnp.logical_not(odd))
    def _():
        values(jnp.where(qi == 0, 0, qi - 1), p_s[1], alpha_last)

    acc = acc_s[...]
    ot = acc[:DIFF_V_DIM] / acc[DIFF_V_DIM:DIFF_V_DIM + 1]
    lv = lamv_ref[...]
    lam = (jnp.exp(jnp.sum(lv[0:1] * lv[1:2], axis=-1, keepdims=True))
           - jnp.exp(jnp.sum(lv[2:3] * lv[3:4], axis=-1, keepdims=True)) + lam_init)
    od = (ot[:, :tq] - lam * ot[:, tq:]).T
    ms = jnp.mean(od * od, axis=-1, keepdims=True)
    o_ref[...] = (od * lax.rsqrt(ms + SUBLN_EPS) * sg_ref[...] * (1.0 - lam_init)).astype(BF16)


def _attention(qs, kh, vt, lamv, sg, *, tq, lam_init):
    _, b, lp, _ = qs.shape
    nq = lp // tq
    const = lambda bi, hd, qi: (0, 0)
    return pl.pallas_call(
        functools.partial(_attn_kernel, lam_init=lam_init),
        grid=(b, DIFF_HEADS, nq),
        in_specs=[
            pl.BlockSpec((2, None, tq, LANES), lambda bi, hd, qi: (0, bi, qi, hd)),
            pl.BlockSpec((None, lp, LANES), lambda bi, hd, qi: (bi, 0, hd)),
            pl.BlockSpec((None, nq, None, VT_ROWS, tq), lambda bi, hd, qi: (bi, 0, hd, 0, 0)),
            pl.BlockSpec((SUBLANES, LANES), const),
            pl.BlockSpec((1, LANES), const),
        ],
        out_specs=pl.BlockSpec((None, tq, LANES), lambda bi, hd, qi: (bi, qi, hd)),
        out_shape=jax.ShapeDtypeStruct((b, lp, DIFF_WIDTH), BF16),
        scratch_shapes=[pltpu.VMEM((VT_ROWS, 2 * tq), F32), pltpu.VMEM((2, tq, 2 * tq), F32),
                        pltpu.VMEM((2, tq, 2 * tq), BF16)],
        compiler_params=_params(("arbitrary", "arbitrary", "arbitrary")),
        name="diffattn",
    )(qs, kh, vt, lamv, sg)


def _outproj_kernel(h_ref, oa_ref, ob_ref, pc_ref, pch_ref, cw_ref, wa_ref, wb_ref, wc_ref,
                    o_ref, *, tiles_per_batch):
    tm = h_ref.shape[0]
    first = (pl.program_id(0) % tiles_per_batch) == 0
    pc = pc_ref[...]
    bg = pc[:, :CONV_WIDTH]
    z = pc[:, CONV_WIDTH:2 * CONV_WIDTH] * pc[:, 2 * CONV_WIDTH:]
    ph = pch_ref[...]
    zh = jnp.where(first, 0.0, ph[:, CONV_WIDTH:2 * CONV_WIDTH] * ph[:, 2 * CONV_WIDTH:])
    row = lax.broadcasted_iota(jnp.int32, (tm, 1), 0)
    z1 = jnp.where(row == 0, zh[SUBLANES - 1:SUBLANES], pltpu.roll(z, 1, 0))
    z2 = jnp.where(row == 0, zh[SUBLANES - 2:SUBLANES - 1],
                   jnp.where(row == 1, zh[SUBLANES - 1:SUBLANES], pltpu.roll(z, 2, 0)))
    cw = cw_ref[...]
    y = cw[0:1] * z2 + cw[1:2] * z1 + cw[2:3] * z
    oc = (bg * y).astype(BF16)
    o_ref[...] = (h_ref[...]
                  + jnp.dot(oa_ref[...], wa_ref[...], preferred_element_type=F32)
                  + jnp.dot(ob_ref[...], wb_ref[...], preferred_element_type=F32)
                  + jnp.dot(oc, wc_ref[...], preferred_element_type=F32))


def _outproj(h, oa, ob, pc, cw, wa, wb, wc, *, tm, lp):
    r, d = h.shape
    row = lambda i: (i, 0)
    const = lambda i: (0, 0)
    halo = lambda i: (jnp.maximum(i * (tm // SUBLANES) - 1, 0), 0)
    return pl.pallas_call(
        functools.partial(_outproj_kernel, tiles_per_batch=lp // tm),
        grid=(r // tm,),
        in_specs=[
            pl.BlockSpec((tm, d), row),
            pl.BlockSpec((tm, RWKV_PAD), row),
            pl.BlockSpec((tm, DIFF_WIDTH), row),
            pl.BlockSpec((tm, PC_COLS), row),
            pl.BlockSpec((SUBLANES, PC_COLS), halo),
            pl.BlockSpec((SUBLANES, CONV_WIDTH), const),
            pl.BlockSpec(wa.shape, const),
            pl.BlockSpec(wb.shape, const),
            pl.BlockSpec(wc.shape, const),
        ],
        out_specs=pl.BlockSpec((tm, d), row),
        out_shape=jax.ShapeDtypeStruct((r, d), F32),
        compiler_params=_params(("arbitrary",)),
        name="outproj",
    )(h, oa, ob, pc, pc, cw, wa, wb, wc)


def _swiglu_block(xb, wg_ref, wu_ref, wd_ref, side_work=None):
    d_ff = wg_ref.shape[1]
    acc = jnp.zeros((xb.shape[0], wd_ref.shape[1]), F32)
    for c in range(d_ff // MXU_DIM):
        cs = slice(c * MXU_DIM, (c + 1) * MXU_DIM)
        gt = jnp.dot(xb, wg_ref[:, cs], preferred_element_type=F32)
        up = jnp.dot(xb, wu_ref[:, cs], preferred_element_type=F32)
        act = (jax.nn.silu(gt) * up).astype(BF16)
        acc = acc + jnp.dot(act, wd_ref[cs, :], preferred_element_type=F32)
        if side_work is not None:
            side_work(c)
    return acc


def _rms(x, g):
    ms = jnp.mean(x * x, axis=-1, keepdims=True)
    return x * lax.rsqrt(ms + NORM_EPS) * g


def _ffn_kernel(h_ref, g_ref, wg_ref, wu_ref, wd_ref, o_ref):
    x = h_ref[...]
    xn = _rms(x, g_ref[...]).astype(BF16)
    o_ref[...] = x + _swiglu_block(xn, wg_ref, wu_ref, wd_ref)


def _dense_ffn(h, g, wg, wu, wd, *, tm):
    r, d = h.shape
    row = lambda i: (i, 0)
    const = lambda i: (0, 0)
    return pl.pallas_call(
        _ffn_kernel,
        grid=(r // tm,),
        in_specs=[
            pl.BlockSpec((tm, d), row),
            pl.BlockSpec((1, d), const),
            pl.BlockSpec(wg.shape, const),
            pl.BlockSpec(wu.shape, const),
            pl.BlockSpec(wd.shape, const),
        ],
        out_specs=pl.BlockSpec((tm, d), row),
        out_shape=jax.ShapeDtypeStruct((r, d), F32),
        compiler_params=_params(("arbitrary",)),
        name="dense_ffn",
    )(h, g, wg, wu, wd)


def _router_kernel(h_ref, g_ref, rw_ref, hn_ref, route_ref):
    tm = h_ref.shape[0]
    xn = _rms(h_ref[...], g_ref[...])
    hn_ref[...] = xn
    logits = _dot_f32(xn, rw_ref[...])
    lane = lax.broadcasted_iota(jnp.int32, (tm, LANES), 1)
    logits = jnp.where(lane < N_EXPERTS, logits, -jnp.inf)
    m1 = jnp.max(logits, axis=-1, keepdims=True)
    i1 = jnp.min(jnp.where(logits == m1, lane, LANES), axis=-1, keepdims=True)
    rest = jnp.where(lane == i1, -jnp.inf, logits)
    m2 = jnp.max(rest, axis=-1, keepdims=True)
    i2 = jnp.min(jnp.where(rest == m2, lane, LANES), axis=-1, keepdims=True)
    e2 = jnp.exp(m2 - m1)
    den = 1.0 + e2
    route_ref[...] = jnp.where(lane == 0, i1.astype(F32),
                     jnp.where(lane == 1, i2.astype(F32),
                     jnp.where(lane == 2, 1.0 / den,
                     jnp.where(lane == 3, e2 / den, 0.0))))


def _router(h, g, rw, *, tm):
    r, d = h.shape
    row = lambda i: (i, 0)
    const = lambda i: (0, 0)
    return pl.pallas_call(
        _router_kernel,
        grid=(r // tm,),
        in_specs=[pl.BlockSpec((tm, d), row), pl.BlockSpec((1, d), const),
                  pl.BlockSpec((d, LANES), const)],
        out_specs=[pl.BlockSpec((tm, d), row), pl.BlockSpec((tm, LANES), row)],
        out_shape=[jax.ShapeDtypeStruct((r, d), F32), jax.ShapeDtypeStruct((r, LANES), F32)],
        compiler_params=_params(("arbitrary",)),
        name="router",
    )(h, g, rw)


def _row_copy(src, dst, sem, src_row, dst_row):
    return pltpu.make_async_copy(src.at[pl.ds(src_row, 1), :], dst.at[pl.ds(dst_row, 1), :], sem)


DMA_UNROLL = 8
DMA_ISSUE_SLICES = 8


def _moe_kernel(bexp_ref, rsrc_ref, rdst_ref, hn_hbm, wg_ref, wu_ref, wd_ref, y_hbm,
                xbuf, ybuf, gsem, ssem):
    del bexp_ref
    blk = xbuf.shape[1]
    i = pl.program_id(0)
    nb = pl.num_programs(0)
    slot = i % 2
    other = 1 - slot
    dump0 = y_hbm.shape[0] - blk

    def for_rows(fn):
        def body(j, c):
            for u in range(DMA_UNROLL):
                fn(j * DMA_UNROLL + u)
            return c
        lax.fori_loop(0, blk // DMA_UNROLL, body, 0)

    def gather(block, s, j):
        return _row_copy(hn_hbm, xbuf.at[s], gsem.at[s], rsrc_ref[block * blk + j], j)

    def scatter(dst_row, s, j):
        return _row_copy(ybuf.at[s], y_hbm, ssem.at[s], j, dst_row)

    @pl.when(i == 0)
    def _():
        for_rows(lambda j: gather(0, 0, j).start())
        ybuf[1] = jnp.zeros(ybuf.shape[1:], ybuf.dtype)

    for_rows(lambda j: gather(0, slot, j).wait())

    @pl.when(i >= 1)
    def _():
        for_rows(lambda j: scatter(0, slot, j).wait())

    nxt = jnp.minimum(i + 1, nb - 1)
    prev = jnp.maximum(i - 1, 0)
    per = -(-blk // DMA_ISSUE_SLICES)

    def side_work(c):
        for j in range(c * per, min((c + 1) * per, blk)):
            gather(nxt, other, j).start(priority=j % 2)
            dst = jnp.where(i == 0, dump0 + j, rdst_ref[prev * blk + j])
            scatter(dst, other, j).start(priority=(j + 1) % 2)

    ybuf[slot] = _swiglu_block(xbuf[slot].astype(BF16), wg_ref, wu_ref, wd_ref, side_work)

    @pl.when(i == nb - 1)
    def _():
        for_rows(lambda j: gather(0, other, j).wait())
        for_rows(lambda j: scatter(0, other, j).wait())
        for_rows(lambda j: scatter(rdst_ref[i * blk + j], slot, j).start())
        for_rows(lambda j: scatter(0, slot, j).wait())


def _moe_experts(block_expert, row_src, row_dst, hn, wg, wu, wd):
    nb = block_expert.shape[0]
    r, d = hn.shape
    d_ff = wg.shape[2]
    wmap = lambda i, be, rs, rd: (be[i], 0, 0)
    grid_spec = pltpu.PrefetchScalarGridSpec(
        num_scalar_prefetch=3,
        grid=(nb,),
        in_specs=[
            pl.BlockSpec(memory_space=pl.ANY),
            pl.BlockSpec((None, d, d_ff), wmap),
            pl.BlockSpec((None, d, d_ff), wmap),
            pl.BlockSpec((None, d_ff, d), wmap),
        ],
        out_specs=pl.BlockSpec(memory_space=pl.ANY),
        scratch_shapes=[pltpu.VMEM((2, MOE_BLOCK, d), F32), pltpu.VMEM((2, MOE_BLOCK, d), F32),
                        pltpu.SemaphoreType.DMA((2,)), pltpu.SemaphoreType.DMA((2,))],
    )
    return pl.pallas_call(
        _moe_kernel,
        grid_spec=grid_spec,
        out_shape=jax.ShapeDtypeStruct((TOP_K * r + MOE_BLOCK, d), F32),
        compiler_params=_params(("arbitrary",)),
        name="moe_experts",
    )(block_expert, row_src, row_dst, hn, wg, wu, wd)


def _combine_kernel(h_ref, route_ref, y0_ref, y1_ref, o_ref):
    route = route_ref[...]
    o_ref[...] = h_ref[...] + route[:, 2:3] * y0_ref[...] + route[:, 3:4] * y1_ref[...]


def _moe_combine(h, route, y, *, tm):
    r, d = h.shape
    nt = r // tm
    row = lambda i: (i, 0)
    return pl.pallas_call(
        _combine_kernel,
        grid=(nt,),
        in_specs=[
            pl.BlockSpec((tm, d), row),
            pl.BlockSpec((tm, LANES), row),
            pl.BlockSpec((tm, d), row),
            pl.BlockSpec((tm, d), lambda i: (i + nt, 0)),
        ],
        out_specs=pl.BlockSpec((tm, d), row),
        out_shape=jax.ShapeDtypeStruct((r, d), F32),
        compiler_params=_params(("arbitrary",)),
        name="moe_combine",
    )(h, route, y, y)


def _slot_owner_kernel(dest_ref, owner_ref):
    n_pairs = dest_ref.shape[0]
    n_slots = owner_ref.shape[0]

    def clear(i, c):
        for u in range(DMA_UNROLL):
            owner_ref[i * DMA_UNROLL + u] = -1
        return c

    lax.fori_loop(0, n_slots // DMA_UNROLL, clear, 0)

    def place(i, c):
        for u in range(DMA_UNROLL):
            p = i * DMA_UNROLL + u
            owner_ref[dest_ref[p]] = p
        return c

    lax.fori_loop(0, n_pairs // DMA_UNROLL, place, 0)


def _slot_owner(dest, n_slots):
    assert dest.shape[0] % DMA_UNROLL == 0 and n_slots % DMA_UNROLL == 0
    smem = pl.BlockSpec(memory_space=pltpu.SMEM)
    return pl.pallas_call(
        _slot_owner_kernel,
        in_specs=[smem],
        out_specs=smem,
        out_shape=jax.ShapeDtypeStruct((n_slots,), jnp.int32),
        name="slot_owner",
    )(dest)


def _routing_tables(route):
    r = route.shape[0]
    tk = r * TOP_K
    flat_e = route[:, :TOP_K].reshape(tk).astype(jnp.int32)
    onehot = (flat_e[:, None] == jnp.arange(N_EXPERTS, dtype=jnp.int32)[None, :]).astype(jnp.int32)
    csum = jnp.cumsum(onehot, axis=0)
    rank = jnp.sum(csum * onehot, axis=1) - 1
    counts = csum[-1]
    padded = (counts + MOE_BLOCK - 1) // MOE_BLOCK * MOE_BLOCK
    pend = jnp.cumsum(padded)
    pstart = pend - padded
    dest = pstart[flat_e] + rank
    nb = -(-tk // MOE_BLOCK) + N_EXPERTS
    n_slots = nb * MOE_BLOCK
    pair = _slot_owner(dest, n_slots)
    used = pair >= 0
    row_src = jnp.where(used, pair // TOP_K, 0)
    row_dst = jnp.where(used, (pair % TOP_K) * r + pair // TOP_K,
                        tk + jnp.arange(n_slots, dtype=jnp.int32) % MOE_BLOCK)
    block_start = jnp.arange(nb, dtype=jnp.int32) * MOE_BLOCK
    block_expert = jnp.minimum(jnp.searchsorted(pend, block_start, side='right'),
                               N_EXPERTS - 1).astype(jnp.int32)
    return block_expert, row_src, row_dst


def _pad_heads(t, axis):
    axis = axis % t.ndim
    shp = t.shape
    t = t.reshape(shp[:axis] + (RWKV_HEADS, RWKV_HEAD_DIM) + shp[axis + 1:])
    pad = [(0, 0)] * t.ndim
    pad[axis + 1] = (0, HEAD_PAD - RWKV_HEAD_DIM)
    t = jnp.pad(t, pad)
    return t.reshape(shp[:axis] + (RWKV_PAD,) + shp[axis + 1:])


def _rope_tables(lp):
    half = ROPE_DIM // 2
    inv_freq = ROPE_THETA ** (-jnp.arange(0, ROPE_DIM, 2, dtype=F32) / ROPE_DIM)
    ang = jnp.arange(lp, dtype=F32)[:, None] * inv_freq[None, :]
    cos, sin = jnp.cos(ang), jnp.sin(ang)
    ones = jnp.ones((lp, DIFF_QK_DIM - ROPE_DIM), F32)
    zeros = jnp.zeros((lp, DIFF_QK_DIM - ROPE_DIM), F32)
    zh = jnp.zeros((lp, half), F32)
    rc = jnp.concatenate([cos, cos, ones], axis=1)
    rs1 = jnp.concatenate([-sin, zh, zeros], axis=1)
    rs2 = jnp.concatenate([zh, sin, zeros], axis=1)
    tile2 = lambda a: jnp.concatenate([a, a], axis=1)
    return tile2(rc), tile2(rs1), tile2(rs2)


def kernel(x, meta_tokens, mix_norm_g, w_in, tm_mu, tm_w0, tm_w_decay_up, tm_a0, tm_w_a_up, tm_w_g_up, tm_k_k, tm_k_a, tm_r_k, tm_gn_g, tm_gn_b, da_q_norm_g, da_k_norm_g, da_lambda_q1, da_lambda_k1, da_lambda_q2, da_lambda_k2, da_subln_g, sc_conv_w, w_out, ffn_norm_g, ffn_w_gate, ffn_w_up, ffn_w_down, router_w, moe_w_gate, moe_w_up, moe_w_down):
    b, seq, d = x.shape
    depth = w_in.shape[0]
    l = N_META + seq
    lp = -(-l // LANES) * LANES
    tm = _row_tile(lp)
    r = b * lp
    s1 = RWKV_WIDTH

    meta = jnp.broadcast_to(meta_tokens.astype(x.dtype)[None], (b, N_META, d))
    h = jnp.concatenate([meta, x, jnp.zeros((b, lp - l, d), x.dtype)], axis=1).reshape(r, d)
    rc, rs1, rs2 = _rope_tables(lp)

    for i in range(depth):
        lam_init = 0.8 - 0.6 * math.exp(-0.3 * i)
        wi = w_in[i]
        w_cat = jnp.concatenate(
            [_pad_heads(wi[:, 0:s1], 1), _pad_heads(wi[:, s1:2 * s1], 1),
             _pad_heads(wi[:, 2 * s1:3 * s1], 1), wi[:, 3 * s1:]], axis=1).astype(BF16)
        qg = jnp.tile(da_q_norm_g[i], 2)[None]
        kg = jnp.tile(da_k_norm_g[i], 2)[None]
        pa, qs, kh, vt, pc = _inproj(h, mix_norm_g[i][None], w_cat, rc, rs1, rs2, qg, kg,
                                     tm=tm, lp=lp)

        mu = tm_mu[i]
        mu_p = jnp.concatenate([_pad_heads(mu[0:s1], 0), _pad_heads(mu[s1:2 * s1], 0),
                                _pad_heads(mu[2 * s1:3 * s1], 0), mu[3 * s1:]])[None]
        vecs = jnp.stack([_pad_heads(tm_w0[i], 0), _pad_heads(tm_a0[i], 0),
                          _pad_heads(tm_k_k[i], 0), _pad_heads(tm_k_a[i], 0),
                          _pad_heads(tm_r_k[i].reshape(s1), 0), _pad_heads(tm_gn_g[i], 0),
                          _pad_heads(tm_gn_b[i], 0), jnp.zeros((RWKV_PAD,), F32)])
        zl = lambda n: jnp.zeros((n, RWKV_PAD), F32)
        wd_p = jnp.concatenate([_pad_heads(tm_w_decay_up[i], 1), zl(AAA_LORA + GATE_LORA)], axis=0)
        wa_p = jnp.concatenate([zl(DECAY_LORA), _pad_heads(tm_w_a_up[i], 1), zl(GATE_LORA)], axis=0)
        wg_p = jnp.concatenate([zl(DECAY_LORA + AAA_LORA), _pad_heads(tm_w_g_up[i], 1)], axis=0)
        oa = _rwkv(pa.reshape(b, lp, PA_COLS), mu_p, vecs, wd_p, wa_p, wg_p, tb=tm)

        lamv = jnp.zeros((SUBLANES, LANES), F32).at[0:4, :DIFF_QK_DIM].set(
            jnp.stack([da_lambda_q1[i], da_lambda_k1[i], da_lambda_q2[i], da_lambda_k2[i]]))
        ob = _attention(qs.reshape(2, b, lp, DIFF_WIDTH), kh.reshape(b, lp, DIFF_WIDTH),
                        vt.reshape(b, lp // tm, DIFF_HEADS, VT_ROWS, tm), lamv, da_subln_g[i][None],
                        tq=tm, lam_init=lam_init)

        wo = w_out[i]
        cw = jnp.zeros((SUBLANES, CONV_WIDTH), F32).at[:CONV_K].set(sc_conv_w[i])
        h = _outproj(h, oa.reshape(r, RWKV_PAD), ob.reshape(r, DIFF_WIDTH), pc, cw,
                     _pad_heads(wo[:s1], 0).astype(BF16), wo[s1:s1 + DIFF_WIDTH].astype(BF16),
                     wo[s1 + DIFF_WIDTH:].astype(BF16), tm=tm, lp=lp)

        j = i // 2
        if i % 2 == 0:
            h = _dense_ffn(h, ffn_norm_g[i][None], ffn_w_gate[j].astype(BF16),
                           ffn_w_up[j].astype(BF16), ffn_w_down[j].astype(BF16), tm=tm)
        else:
            rw = jnp.zeros((d, LANES), F32).at[:, :N_EXPERTS].set(router_w[j])
            hn, route = _router(h, ffn_norm_g[i][None], rw, tm=tm)
            block_expert, row_src, row_dst = _routing_tables(route)
            y = _moe_experts(block_expert, row_src, row_dst, hn, moe_w_gate[j].astype(BF16),
                             moe_w_up[j].astype(BF16), moe_w_down[j].astype(BF16))
            h = _moe_combine(h, route, y, tm=tm)

    return h.reshape(b, lp, d)[:, N_META:l]
```

```python
import functools
import math

import jax
import jax.numpy as jnp
from jax import lax
from jax.experimental import pallas as pl
from jax.experimental.pallas import tpu as pltpu

N_META = 16
RWKV_HEAD_DIM = 64
RWKV_HEADS = 4
RWKV_WIDTH = RWKV_HEADS * RWKV_HEAD_DIM
DECAY_LORA = 32
AAA_LORA = 32
GATE_LORA = 64
LORA_COLS = DECAY_LORA + AAA_LORA + GATE_LORA
DIFF_QK_DIM = 64
DIFF_V_DIM = 128
DIFF_HEADS = 4
DIFF_WIDTH = DIFF_HEADS * DIFF_V_DIM
CONV_WIDTH = 256
CONV_K = 3
ROPE_THETA = 500000.0
ROPE_DIM = DIFF_QK_DIM // 4
N_EXPERTS = 8
TOP_K = 2
MOE_BLOCK = 256
NORM_EPS = 1e-6
RWKV_GN_EPS = 64e-5
SUBLN_EPS = 1e-5
NEG_INF = -1e30

LANES = 128
SUBLANES = 8
MXU_DIM = 256
VMEM_LIMIT_BYTES = 56 * 1024 * 1024

HEAD_PAD = LANES
RWKV_PAD = RWKV_HEADS * HEAD_PAD
PA_COLS = 3 * RWKV_PAD + LORA_COLS
PB_COLS = 3 * DIFF_WIDTH
PC_COLS = 3 * CONV_WIDTH
VT_ROWS = DIFF_V_DIM + 16
LOG2_E = math.log2(math.e)
CHUNK = 64

F32 = jnp.float32
BF16 = jnp.bfloat16
HIGHEST = lax.Precision.HIGHEST


def _dot(a, b):
    return jnp.dot(a.astype(BF16), b.astype(BF16), preferred_element_type=F32)


def _dot_nt(a, b):
    return lax.dot_general(a.astype(BF16), b.astype(BF16), (((1,), (1,)), ((), ())),
                           preferred_element_type=F32)


def _dot_f32(a, b):
    return jnp.dot(a, b, preferred_element_type=F32, precision=HIGHEST)


def _row_tile(lp):
    for t in (384, 256, 128):
        if lp % t == 0:
            return t
    raise ValueError(f"padded length {lp} is not a multiple of {LANES}")


def _params(sem):
    return pltpu.CompilerParams(dimension_semantics=sem, vmem_limit_bytes=VMEM_LIMIT_BYTES)


def _inproj_kernel(h_ref, g_ref, w_ref, rc_ref, rs1_ref, rs2_ref, qg_ref, kg_ref,
                   pa_ref, qs_ref, kh_ref, vt_ref, pc_ref):
    tm = h_ref.shape[0]
    x = h_ref[...]
    ms = jnp.mean(x * x, axis=-1, keepdims=True)
    xn = (x * lax.rsqrt(ms + NORM_EPS) * g_ref[...]).astype(BF16)
    pa_ref[...] = jnp.dot(xn, w_ref[:, :PA_COLS], preferred_element_type=F32)
    pc_ref[...] = jnp.dot(xn, w_ref[:, PA_COLS + PB_COLS:], preferred_element_type=F32)
    pb = jnp.dot(xn, w_ref[:, PA_COLS:PA_COLS + PB_COLS], preferred_element_type=F32)

    lane = lax.broadcasted_iota(jnp.int32, (tm, LANES), 1)
    lo = lane < DIFF_QK_DIM
    rc, rs1, rs2 = rc_ref[...], rs1_ref[...], rs2_ref[...]

    def norm_rope(t, g):
        t2 = t * t
        ss_lo = jnp.sum(jnp.where(lo, t2, 0.0), axis=-1, keepdims=True)
        ss_hi = jnp.sum(jnp.where(lo, 0.0, t2), axis=-1, keepdims=True)
        inv = jnp.where(lo, lax.rsqrt(ss_lo / DIFF_QK_DIM + NORM_EPS),
                        lax.rsqrt(ss_hi / DIFF_QK_DIM + NORM_EPS))
        tn = t * inv * g
        half = ROPE_DIM // 2
        return tn * rc + pltpu.roll(tn, LANES - half, 1) * rs1 + pltpu.roll(tn, half, 1) * rs2

    for hd in range(DIFF_HEADS):
        cs = slice(hd * LANES, (hd + 1) * LANES)
        q = norm_rope(pb[:, cs], qg_ref[...]) * (DIFF_QK_DIM ** -0.5 * LOG2_E)
        qs_ref[0, :, cs] = jnp.where(lo, q, 0.0).astype(BF16)
        qs_ref[1, :, cs] = jnp.where(lo, 0.0, q).astype(BF16)
        kcs = slice(DIFF_WIDTH + hd * LANES, DIFF_WIDTH + (hd + 1) * LANES)
        kh_ref[:, cs] = norm_rope(pb[:, kcs], kg_ref[...]).astype(BF16)
        vcs = slice(2 * DIFF_WIDTH + hd * LANES, 2 * DIFF_WIDTH + (hd + 1) * LANES)
        vt_ref[0, hd, :DIFF_V_DIM] = pb[:, vcs].T.astype(BF16)
        vt_ref[0, hd, DIFF_V_DIM:] = jnp.ones((VT_ROWS - DIFF_V_DIM, tm), BF16)


def _inproj(h, g, w, rc, rs1, rs2, qg, kg, *, tm, lp):
    r, d = h.shape
    nt_b = lp // tm
    row = lambda i: (i, 0)
    const = lambda i: (0, 0)
    rope = lambda i: (i % nt_b, 0)
    return pl.pallas_call(
        _inproj_kernel,
        grid=(r // tm,),
        in_specs=[
            pl.BlockSpec((tm, d), row),
            pl.BlockSpec((1, d), const),
            pl.BlockSpec(w.shape, const),
            pl.BlockSpec((tm, LANES), rope),
            pl.BlockSpec((tm, LANES), rope),
            pl.BlockSpec((tm, LANES), rope),
            pl.BlockSpec((1, LANES), const),
            pl.BlockSpec((1, LANES), const),
        ],
        out_specs=[
            pl.BlockSpec((tm, PA_COLS), row),
            pl.BlockSpec((2, tm, DIFF_WIDTH), lambda i: (0, i, 0)),
            pl.BlockSpec((tm, DIFF_WIDTH), row),
            pl.BlockSpec((1, DIFF_HEADS, VT_ROWS, tm), lambda i: (i, 0, 0, 0)),
            pl.BlockSpec((tm, PC_COLS), row),
        ],
        out_shape=[
            jax.ShapeDtypeStruct((r, PA_COLS), F32),
            jax.ShapeDtypeStruct((2, r, DIFF_WIDTH), BF16),
            jax.ShapeDtypeStruct((r, DIFF_WIDTH), BF16),
            jax.ShapeDtypeStruct((r // tm, DIFF_HEADS, VT_ROWS, tm), BF16),
            jax.ShapeDtypeStruct((r, PC_COLS), F32),
        ],
        compiler_params=_params(("arbitrary",)),
        name="inproj",
    )(h, g, w, rc, rs1, rs2, qg, kg)


def _split3(x):
    hi = x.astype(BF16)
    r1 = x - hi.astype(F32)
    mid = r1.astype(BF16)
    lo = (r1 - mid.astype(F32)).astype(BF16)
    return hi, mid, lo


def _dot_hi(a, b):
    ah = a.astype(BF16)
    al = (a - ah.astype(F32)).astype(BF16)
    bh = b.astype(BF16)
    bl = (b - bh.astype(F32)).astype(BF16)
    d = lambda p, q: jnp.dot(p, q, preferred_element_type=F32)
    return d(ah, bh) + d(ah, bl) + d(al, bh)


def _bmm(a, b):
    return jnp.einsum('cik,ckj->cij', a.astype(BF16), b.astype(BF16), preferred_element_type=F32)


def _bmm_nt(a, b):
    return jnp.einsum('cik,cjk->cij', a.astype(BF16), b.astype(BF16), preferred_element_type=F32)


def _rwkv_kernel(pa_ref, halo_ref, mu_ref, vec_ref, wd_ref, wa_ref, wg_ref, oa_ref, hstate_s):
    tb = pa_ref.shape[0]
    nc = tb // CHUNK
    i = pl.program_id(1)

    @pl.when(i == 0)
    def _():
        hstate_s[...] = jnp.zeros_like(hstate_s)

    x = pa_ref[...]
    prev_row = jnp.where(i == 0, 0.0, halo_ref[SUBLANES - 1:SUBLANES, :])
    row = lax.broadcasted_iota(jnp.int32, (tb, 1), 0)
    prev = jnp.where(row == 0, prev_row, pltpu.roll(x, 1, 0))
    hx = x + (prev - x) * mu_ref[...]
    r = hx[:, 0:RWKV_PAD]
    k = hx[:, RWKV_PAD:2 * RWKV_PAD]
    v = hx[:, 2 * RWKV_PAD:3 * RWKV_PAD]
    lora = hx[:, 3 * RWKV_PAD:]
    w0, a0, k_k, k_a = vec_ref[0:1, :], vec_ref[1:2, :], vec_ref[2:3, :], vec_ref[3:4, :]
    r_k, gn_g, gn_b = vec_ref[4:5, :], vec_ref[5:6, :], vec_ref[6:7, :]

    w = -jax.nn.softplus(-(w0 + _dot_hi(jnp.tanh(lora), wd_ref[...]))) - 0.5
    wl = -jnp.exp(w)
    alr = jax.nn.sigmoid(a0 + _dot(lora, wa_ref[...]))
    gate = _dot(jax.nn.sigmoid(lora), wg_ref[...])
    kk = k * k_k
    k2 = k * (1.0 + (alr - 1.0) * k_a)
    rkr = r * k2 * r_k

    ri = lax.broadcasted_iota(jnp.int32, (CHUNK, CHUNK), 0)
    ci = lax.broadcasted_iota(jnp.int32, (CHUNK, CHUNK), 1)
    lane_c = lax.broadcasted_iota(jnp.int32, (CHUNK, HEAD_PAD), 1)
    row_c = lax.broadcasted_iota(jnp.int32, (CHUNK, HEAD_PAD), 0)
    col_c = jnp.where(lane_c >= CHUNK, lane_c - CHUNK, lane_c)
    low_incl2 = (col_c <= row_c)[None]
    low_strict2 = (col_c < row_c)[None]
    tril_b = jnp.where(ci <= ri, 1.0, 0.0).astype(BF16)
    eye_k = (lax.broadcasted_iota(jnp.int32, (HEAD_PAD, HEAD_PAD), 0)
             == lax.broadcasted_iota(jnp.int32, (HEAD_PAD, HEAD_PAD), 1))[None]
    n_sq = int(math.log2(CHUNK))

    parts = _split3(wl)
    cums = []
    for c in range(nc):
        rows = slice(c * CHUNK, (c + 1) * CHUNK)
        cums.append(sum(jnp.dot(tril_b, p[rows], preferred_element_type=F32) for p in parts))
    cum_all = jnp.concatenate(cums, axis=0)

    lane = lax.broadcasted_iota(jnp.int32, (tb, HEAD_PAD), 1)
    real = lane < RWKV_HEAD_DIM
    to3 = lambda t: t.reshape(nc, CHUNK, HEAD_PAD)
    heads = range(RWKV_HEADS)
    hsl = [slice(hd * HEAD_PAD, (hd + 1) * HEAD_PAD) for hd in heads]
    kkn, bonus = [], []
    for cs in hsl:
        kk_h = kk[:, cs]
        nrm = jnp.sqrt(jnp.sum(kk_h * kk_h, axis=-1, keepdims=True))
        kkn.append(kk_h / jnp.maximum(nrm, 1e-12))
        bonus.append(jnp.sum(rkr[:, cs], axis=-1, keepdims=True) * v[:, cs])

    vc = [to3(v[:, cs]) for cs in hsl]
    zero_v = jnp.zeros_like(vc[0])
    ah, rh, last, g, bk_last = [], [], [], [], []
    for hd, cs in enumerate(hsl):
        wl3, cum = to3(wl[:, cs]), to3(cum_all[:, cs])
        rr, kc = to3(r[:, cs]), to3(k2[:, cs])
        ac, bc = to3(-kkn[hd]), to3(kkn[hd] * alr[:, cs])
        last.append(cum[:, CHUNK - 1:CHUNK, :])
        e_neg = jnp.exp(-cum)
        e_last = jnp.exp(last[hd] - cum)
        ah.append(ac * jnp.exp(cum - wl3))
        rh.append(rr * jnp.exp(cum))
        bk_last.append(jnp.concatenate([bc * e_last, kc * e_last], axis=1))
        g.append(_bmm_nt(jnp.concatenate([ah[hd], rh[hd]], axis=1),
                         jnp.concatenate([bc * e_neg, kc * e_neg], axis=1)))
    a_abk = [jnp.where(low_strict2, g[hd][:, :CHUNK], 0.0) for hd in heads]
    r_bk = [jnp.where(low_incl2, g[hd][:, CHUNK:], 0.0) for hd in heads]
    xx = [jnp.concatenate([_bmm(a_abk[hd], jnp.concatenate([zero_v, vc[hd]], axis=1)), ah[hd]],
                          axis=2) for hd in heads]
    ppad = [jnp.where(lane_c < CHUNK, a_abk[hd], 0.0) for hd in heads]
    for j in range(n_sq):
        for hd in heads:
            pj = ppad[hd][:, :, :CHUNK]
            if j + 1 < n_sq:
                res = _bmm(pj, jnp.concatenate([xx[hd], ppad[hd]], axis=2))
                xx[hd] = xx[hd] + res[:, :, :2 * HEAD_PAD]
                ppad[hd] = res[:, :, 2 * HEAD_PAD:]
            else:
                xx[hd] = xx[hd] + _bmm(pj, xx[hd])
    qm, y0, n0 = [], [], []
    for hd in heads:
        lhs = jnp.concatenate([r_bk[hd], jnp.swapaxes(bk_last[hd], 1, 2)], axis=1)
        rhs = jnp.concatenate([xx[hd], jnp.concatenate([vc[hd], zero_v], axis=2)], axis=1)
        out = _bmm(lhs, rhs)
        y0.append(out[:, :CHUNK, :HEAD_PAD])
        n0.append(out[:, CHUNK:, :HEAD_PAD])
        q = rh[hd] + out[:, :CHUNK, HEAD_PAD:]
        m = jnp.where(eye_k, jnp.exp(last[hd]), 0.0) + out[:, CHUNK:, HEAD_PAD:]
        qm.append(jnp.concatenate([q, m], axis=1).astype(BF16))

    hs = [hstate_s[hd] for hd in heads]
    ys = [[] for _ in heads]
    for c in range(nc):
        for hd in heads:
            res = jnp.dot(qm[hd][c], hs[hd].astype(BF16), preferred_element_type=F32)
            ys[hd].append(res[:CHUNK] + y0[hd][c])
            hs[hd] = res[CHUNK:] + n0[hd][c]

    for hd, cs in enumerate(hsl):
        hstate_s[hd] = hs[hd]
        y = jnp.concatenate(ys[hd], axis=0)
        mean = jnp.sum(y, axis=-1, keepdims=True) / RWKV_HEAD_DIM
        dlt = jnp.where(real, y - mean, 0.0)
        var = jnp.sum(dlt * dlt, axis=-1, keepdims=True) / RWKV_HEAD_DIM
        yn = dlt * lax.rsqrt(var + RWKV_GN_EPS) * gn_g[:, cs] + gn_b[:, cs]
        oa_ref[:, cs] = ((yn + bonus[hd]) * gate[:, cs]).astype(BF16)


def _rwkv(pa, mu, vecs, wd, wa, wg, *, tb):
    b, lp, _ = pa.shape
    tile = lambda bi, i: (bi, i, 0)
    const = lambda bi, i: (0, 0)
    halo = lambda bi, i: (bi, jnp.maximum(i * (tb // SUBLANES) - 1, 0), 0)
    return pl.pallas_call(
        _rwkv_kernel,
        grid=(b, lp // tb),
        in_specs=[
            pl.BlockSpec((None, tb, PA_COLS), tile),
            pl.BlockSpec((None, SUBLANES, PA_COLS), halo),
            pl.BlockSpec((1, PA_COLS), const),
            pl.BlockSpec((SUBLANES, RWKV_PAD), const),
            pl.BlockSpec((LORA_COLS, RWKV_PAD), const),
            pl.BlockSpec((LORA_COLS, RWKV_PAD), const),
            pl.BlockSpec((LORA_COLS, RWKV_PAD), const),
        ],
        out_specs=pl.BlockSpec((None, tb, RWKV_PAD), tile),
        out_shape=jax.ShapeDtypeStruct((b, lp, RWKV_PAD), BF16),
        scratch_shapes=[pltpu.VMEM((RWKV_HEADS, HEAD_PAD, HEAD_PAD), F32)],
        compiler_params=_params(("arbitrary", "arbitrary")),
        name="rwkv7",
    )(pa, pa, mu, vecs, wd, wa, wg)


ATTN_HEADS_PER_STEP = 4


def _attn_kernel(q_ref, k_ref, vt_ref, lamv_ref, sg_ref, o_ref, acc_s, st_s, p_s, *, lam_init):
    tq = o_ref.shape[0]
    tk = vt_ref.shape[3]
    qi = pl.program_id(2)
    heads = range(ATTN_HEADS_PER_STEP)
    hl = [slice(h * LANES, (h + 1) * LANES) for h in heads]

    def scores(h, ki):
        k = k_ref[pl.ds(pl.multiple_of(ki * tk, tk), tk), hl[h]]
        return lax.dot_general(k, q_ref[:, :, hl[h]].reshape(2 * tq, LANES),
                               (((1,), (1,)), ((), ())), preferred_element_type=F32)

    def values(h, ki, p, alpha):
        acc_s[h] = alpha * acc_s[h] + jnp.dot(vt_ref[ki, h], p, preferred_element_type=F32)

    st = [scores(h, qi) for h in heads]

    @pl.when(qi > 0)
    def _():
        for h in heads:
            st_s[h, 0] = scores(h, 0)

    kr = lax.broadcasted_iota(jnp.int32, (tk, 2 * tq), 0)
    qc = lax.broadcasted_iota(jnp.int32, (tk, 2 * tq), 1)
    qc = jnp.where(qc >= tq, qc - tq, qc)
    m0 = []
    for h in heads:
        s_h = jnp.where(kr <= qc, st[h], NEG_INF)
        m0.append(jnp.max(s_h, axis=0, keepdims=True))
        p_s[h, 1] = jnp.exp2(s_h - m0[h]).astype(BF16)
        acc_s[h] = jnp.zeros(acc_s.shape[1:], acc_s.dtype)

    def trip(k, carry, slot):
        out = []
        nxt = jnp.minimum(k + 1, qi - 1)
        prv = jnp.where(k == 0, qi, k - 1)
        for h in heads:
            st_s[h, 1 - slot] = scores(h, nxt)
        for h in heads:
            m_prev, alpha_prev = carry[h]
            s_k = st_s[h, slot]
            m_new = jnp.maximum(m_prev, jnp.max(s_k, axis=0, keepdims=True))
            alpha = jnp.exp2(m_prev - m_new)
            values(h, prv, p_s[h, 1 - slot], alpha_prev)
            p_s[h, slot] = jnp.exp2(s_k - m_new).astype(BF16)
            out.append((m_new, alpha))
        return tuple(out)

    def two_trips(j, c):
        c = trip(2 * j, c, 0)
        return lax.cond(2 * j + 1 < qi, lambda cc: trip(2 * j + 1, cc, 1), lambda cc: cc, c)

    init = tuple((m0[h], jnp.ones_like(m0[h])) for h in heads)
    carry = lax.fori_loop(0, (qi + 1) // 2, two_trips, init)
    odd = qi % 2 == 1

    @pl.when(odd)
    def _():
        for h in heads:
            values(h, qi - 1, p_s[h, 0], carry[h][1])

    @pl.when(jnp.logical_not(odd))
    def _():
        for h in heads:
            values(h, jnp.where(qi == 0, 0, qi - 1), p_s[h, 1], carry[h][1])

    lv = lamv_ref[...]
    lam = (jnp.exp(jnp.sum(lv[0:1] * lv[1:2], axis=-1, keepdims=True))
           - jnp.exp(jnp.sum(lv[2:3] * lv[3:4], axis=-1, keepdims=True)) + lam_init)
    for h in heads:
        acc = acc_s[h]
        ot = acc[:DIFF_V_DIM] / acc[DIFF_V_DIM:DIFF_V_DIM + 1]
        od = (ot[:, :tq] - lam * ot[:, tq:]).T
        ms = jnp.mean(od * od, axis=-1, keepdims=True)
        o_ref[:, hl[h]] = (od * lax.rsqrt(ms + SUBLN_EPS) * sg_ref[...]
                           * (1.0 - lam_init)).astype(BF16)


def _attention(qs, kh, vt, lamv, sg, *, tq, lam_init):
    _, b, lp, _ = qs.shape
    nq = lp // tq
    hp = ATTN_HEADS_PER_STEP
    const = lambda bi, hg, qi: (0, 0)
    return pl.pallas_call(
        functools.partial(_attn_kernel, lam_init=lam_init),
        grid=(b, DIFF_HEADS // hp, nq),
        in_specs=[
            pl.BlockSpec((2, None, tq, hp * LANES), lambda bi, hg, qi: (0, bi, qi, hg)),
            pl.BlockSpec((None, lp, hp * LANES), lambda bi, hg, qi: (bi, 0, hg)),
            pl.BlockSpec((None, nq, hp, VT_ROWS, tq), lambda bi, hg, qi: (bi, 0, hg, 0, 0)),
            pl.BlockSpec((SUBLANES, LANES), const),
            pl.BlockSpec((1, LANES), const),
        ],
        out_specs=pl.BlockSpec((None, tq, hp * LANES), lambda bi, hg, qi: (bi, qi, hg)),
        out_shape=jax.ShapeDtypeStruct((b, lp, DIFF_WIDTH), BF16),
        scratch_shapes=[pltpu.VMEM((hp, VT_ROWS, 2 * tq), F32),
                        pltpu.VMEM((hp, 2, tq, 2 * tq), F32),
                        pltpu.VMEM((hp, 2, tq, 2 * tq), BF16)],
        compiler_params=_params(("arbitrary", "arbitrary", "arbitrary")),
        name="diffattn",
    )(qs, kh, vt, lamv, sg)


def _outproj_kernel(h_ref, oa_ref, ob_ref, pc_ref, pch_ref, cw_ref, wa_ref, wb_ref, wc_ref,
                    o_ref, *, tiles_per_batch):
    tm = h_ref.shape[0]
    first = (pl.program_id(0) % tiles_per_batch) == 0
    pc = pc_ref[...]
    bg = pc[:, :CONV_WIDTH]
    z = pc[:, CONV_WIDTH:2 * CONV_WIDTH] * pc[:, 2 * CONV_WIDTH:]
    ph = pch_ref[...]
    zh = jnp.where(first, 0.0, ph[:, CONV_WIDTH:2 * CONV_WIDTH] * ph[:, 2 * CONV_WIDTH:])
    row = lax.broadcasted_iota(jnp.int32, (tm, 1), 0)
    z1 = jnp.where(row == 0, zh[SUBLANES - 1:SUBLANES], pltpu.roll(z, 1, 0))
    z2 = jnp.where(row == 0, zh[SUBLANES - 2:SUBLANES - 1],
                   jnp.where(row == 1, zh[SUBLANES - 1:SUBLANES], pltpu.roll(z, 2, 0)))
    cw = cw_ref[...]
    y = cw[0:1] * z2 + cw[1:2] * z1 + cw[2:3] * z
    oc = (bg * y).astype(BF16)
    o_ref[...] = (h_ref[...]
                  + jnp.dot(oa_ref[...], wa_ref[...], preferred_element_type=F32)
                  + jnp.dot(ob_ref[...], wb_ref[...], preferred_element_type=F32)
                  + jnp.dot(oc, wc_ref[...], preferred_element_type=F32))


def _outproj(h, oa, ob, pc, cw, wa, wb, wc, *, tm, lp):
    r, d = h.shape
    row = lambda i: (i, 0)
    const = lambda i: (0, 0)
    halo = lambda i: (jnp.maximum(i * (tm // SUBLANES) - 1, 0), 0)
    return pl.pallas_call(
        functools.partial(_outproj_kernel, tiles_per_batch=lp // tm),
        grid=(r // tm,),
        in_specs=[
            pl.BlockSpec((tm, d), row),
            pl.BlockSpec((tm, RWKV_PAD), row),
            pl.BlockSpec((tm, DIFF_WIDTH), row),
            pl.BlockSpec((tm, PC_COLS), row),
            pl.BlockSpec((SUBLANES, PC_COLS), halo),
            pl.BlockSpec((SUBLANES, CONV_WIDTH), const),
            pl.BlockSpec(wa.shape, const),
            pl.BlockSpec(wb.shape, const),
            pl.BlockSpec(wc.shape, const),
        ],
        out_specs=pl.BlockSpec((tm, d), row),
        out_shape=jax.ShapeDtypeStruct((r, d), F32),
        compiler_params=_params(("arbitrary",)),
        name="outproj",
    )(h, oa, ob, pc, pc, cw, wa, wb, wc)


def _swiglu_block(xb, wg_ref, wu_ref, wd_ref, side_work=None):
    d_ff = wg_ref.shape[1]
    acc = jnp.zeros((xb.shape[0], wd_ref.shape[1]), F32)
    for c in range(d_ff // MXU_DIM):
        cs = slice(c * MXU_DIM, (c + 1) * MXU_DIM)
        gt = jnp.dot(xb, wg_ref[:, cs], preferred_element_type=F32)
        up = jnp.dot(xb, wu_ref[:, cs], preferred_element_type=F32)
        act = (jax.nn.silu(gt) * up).astype(BF16)
        acc = acc + jnp.dot(act, wd_ref[cs, :], preferred_element_type=F32)
        if side_work is not None:
            side_work(c)
    return acc


def _rms(x, g):
    ms = jnp.mean(x * x, axis=-1, keepdims=True)
    return x * lax.rsqrt(ms + NORM_EPS) * g


def _ffn_kernel(h_ref, g_ref, wg_ref, wu_ref, wd_ref, o_ref):
    x = h_ref[...]
    xn = _rms(x, g_ref[...]).astype(BF16)
    o_ref[...] = x + _swiglu_block(xn, wg_ref, wu_ref, wd_ref)


def _dense_ffn(h, g, wg, wu, wd, *, tm):
    r, d = h.shape
    row = lambda i: (i, 0)
    const = lambda i: (0, 0)
    return pl.pallas_call(
        _ffn_kernel,
        grid=(r // tm,),
        in_specs=[
            pl.BlockSpec((tm, d), row),
            pl.BlockSpec((1, d), const),
            pl.BlockSpec(wg.shape, const),
            pl.BlockSpec(wu.shape, const),
            pl.BlockSpec(wd.shape, const),
        ],
        out_specs=pl.BlockSpec((tm, d), row),
        out_shape=jax.ShapeDtypeStruct((r, d), F32),
        compiler_params=_params(("arbitrary",)),
        name="dense_ffn",
    )(h, g, wg, wu, wd)


def _router_kernel(h_ref, g_ref, rw_ref, hn_ref, route_ref):
    tm = h_ref.shape[0]
    xn = _rms(h_ref[...], g_ref[...])
    hn_ref[...] = xn
    logits = _dot_f32(xn, rw_ref[...])
    lane = lax.broadcasted_iota(jnp.int32, (tm, LANES), 1)
    logits = jnp.where(lane < N_EXPERTS, logits, -jnp.inf)
    m1 = jnp.max(logits, axis=-1, keepdims=True)
    i1 = jnp.min(jnp.where(logits == m1, lane, LANES), axis=-1, keepdims=True)
    rest = jnp.where(lane == i1, -jnp.inf, logits)
    m2 = jnp.max(rest, axis=-1, keepdims=True)
    i2 = jnp.min(jnp.where(rest == m2, lane, LANES), axis=-1, keepdims=True)
    e2 = jnp.exp(m2 - m1)
    den = 1.0 + e2
    route_ref[...] = jnp.where(lane == 0, i1.astype(F32),
                     jnp.where(lane == 1, i2.astype(F32),
                     jnp.where(lane == 2, 1.0 / den,
                     jnp.where(lane == 3, e2 / den, 0.0))))


def _router(h, g, rw, *, tm):
    r, d = h.shape
    row = lambda i: (i, 0)
    const = lambda i: (0, 0)
    return pl.pallas_call(
        _router_kernel,
        grid=(r // tm,),
        in_specs=[pl.BlockSpec((tm, d), row), pl.BlockSpec((1, d), const),
                  pl.BlockSpec((d, LANES), const)],
        out_specs=[pl.BlockSpec((tm, d), row), pl.BlockSpec((tm, LANES), row)],
        out_shape=[jax.ShapeDtypeStruct((r, d), F32), jax.ShapeDtypeStruct((r, LANES), F32)],
        compiler_params=_params(("arbitrary",)),
        name="router",
    )(h, g, rw)


def _row_copy(src, dst, sem, src_row, dst_row):
    return pltpu.make_async_copy(src.at[pl.ds(src_row, 1), :], dst.at[pl.ds(dst_row, 1), :], sem)


DMA_UNROLL = 8
DMA_ISSUE_SLICES = 8


def _moe_kernel(bexp_ref, rsrc_ref, rdst_ref, hn_hbm, wg_ref, wu_ref, wd_ref, y_hbm,
                xbuf, ybuf, gsem, ssem):
    del bexp_ref
    blk = xbuf.shape[1]
    i = pl.program_id(0)
    nb = pl.num_programs(0)
    slot = i % 2
    other = 1 - slot
    dump0 = y_hbm.shape[0] - blk

    def for_rows(fn):
        def body(j, c):
            for u in range(DMA_UNROLL):
                fn(j * DMA_UNROLL + u)
            return c
        lax.fori_loop(0, blk // DMA_UNROLL, body, 0)

    def gather(block, s, j):
        return _row_copy(hn_hbm, xbuf.at[s], gsem.at[s], rsrc_ref[block * blk + j], j)

    def scatter(dst_row, s, j):
        return _row_copy(ybuf.at[s], y_hbm, ssem.at[s], j, dst_row)

    @pl.when(i == 0)
    def _():
        for_rows(lambda j: gather(0, 0, j).start())
        ybuf[1] = jnp.zeros(ybuf.shape[1:], ybuf.dtype)

    for_rows(lambda j: gather(0, slot, j).wait())

    @pl.when(i >= 1)
    def _():
        for_rows(lambda j: scatter(0, slot, j).wait())

    nxt = jnp.minimum(i + 1, nb - 1)
    prev = jnp.maximum(i - 1, 0)
    per = -(-blk // DMA_ISSUE_SLICES)

    def side_work(c):
        for j in range(c * per, min((c + 1) * per, blk)):
            gather(nxt, other, j).start(priority=j % 2)
            dst = jnp.where(i == 0, dump0 + j, rdst_ref[prev * blk + j])
            scatter(dst, other, j).start(priority=(j + 1) % 2)

    ybuf[slot] = _swiglu_block(xbuf[slot].astype(BF16), wg_ref, wu_ref, wd_ref, side_work)

    @pl.when(i == nb - 1)
    def _():
        for_rows(lambda j: gather(0, other, j).wait())
        for_rows(lambda j: scatter(0, other, j).wait())
        for_rows(lambda j: scatter(rdst_ref[i * blk + j], slot, j).start())
        for_rows(lambda j: scatter(0, slot, j).wait())


def _moe_experts(block_expert, row_src, row_dst, hn, wg, wu, wd):
    nb = block_expert.shape[0]
    r, d = hn.shape
    d_ff = wg.shape[2]
    wmap = lambda i, be, rs, rd: (be[i], 0, 0)
    grid_spec = pltpu.PrefetchScalarGridSpec(
        num_scalar_prefetch=3,
        grid=(nb,),
        in_specs=[
            pl.BlockSpec(memory_space=pl.ANY),
            pl.BlockSpec((None, d, d_ff), wmap),
            pl.BlockSpec((None, d, d_ff), wmap),
            pl.BlockSpec((None, d_ff, d), wmap),
        ],
        out_specs=pl.BlockSpec(memory_space=pl.ANY),
        scratch_shapes=[pltpu.VMEM((2, MOE_BLOCK, d), F32), pltpu.VMEM((2, MOE_BLOCK, d), F32),
                        pltpu.SemaphoreType.DMA((2,)), pltpu.SemaphoreType.DMA((2,))],
    )
    return pl.pallas_call(
        _moe_kernel,
        grid_spec=grid_spec,
        out_shape=jax.ShapeDtypeStruct((TOP_K * r + MOE_BLOCK, d), F32),
        compiler_params=_params(("arbitrary",)),
        name="moe_experts",
    )(block_expert, row_src, row_dst, hn, wg, wu, wd)


def _combine_kernel(h_ref, route_ref, y0_ref, y1_ref, o_ref):
    route = route_ref[...]
    o_ref[...] = h_ref[...] + route[:, 2:3] * y0_ref[...] + route[:, 3:4] * y1_ref[...]


def _moe_combine(h, route, y, *, tm):
    r, d = h.shape
    nt = r // tm
    row = lambda i: (i, 0)
    return pl.pallas_call(
        _combine_kernel,
        grid=(nt,),
        in_specs=[
            pl.BlockSpec((tm, d), row),
            pl.BlockSpec((tm, LANES), row),
            pl.BlockSpec((tm, d), row),
            pl.BlockSpec((tm, d), lambda i: (i + nt, 0)),
        ],
        out_specs=pl.BlockSpec((tm, d), row),
        out_shape=jax.ShapeDtypeStruct((r, d), F32),
        compiler_params=_params(("arbitrary",)),
        name="moe_combine",
    )(h, route, y, y)


def _slot_owner_kernel(dest_ref, owner_ref):
    n_pairs = dest_ref.shape[0]
    n_slots = owner_ref.shape[0]

    def clear(i, c):
        for u in range(DMA_UNROLL):
            owner_ref[i * DMA_UNROLL + u] = -1
        return c

    lax.fori_loop(0, n_slots // DMA_UNROLL, clear, 0)

    def place(i, c):
        for u in range(DMA_UNROLL):
            p = i * DMA_UNROLL + u
            owner_ref[dest_ref[p]] = p
        return c

    lax.fori_loop(0, n_pairs // DMA_UNROLL, place, 0)


def _slot_owner(dest, n_slots):
    assert dest.shape[0] % DMA_UNROLL == 0 and n_slots % DMA_UNROLL == 0
    smem = pl.BlockSpec(memory_space=pltpu.SMEM)
    return pl.pallas_call(
        _slot_owner_kernel,
        in_specs=[smem],
        out_specs=smem,
        out_shape=jax.ShapeDtypeStruct((n_slots,), jnp.int32),
        name="slot_owner",
    )(dest)


def _routing_tables(route):
    r = route.shape[0]
    tk = r * TOP_K
    flat_e = route[:, :TOP_K].reshape(tk).astype(jnp.int32)
    onehot = (flat_e[:, None] == jnp.arange(N_EXPERTS, dtype=jnp.int32)[None, :]).astype(jnp.int32)
    csum = jnp.cumsum(onehot, axis=0)
    rank = jnp.sum(csum * onehot, axis=1) - 1
    counts = csum[-1]
    padded = (counts + MOE_BLOCK - 1) // MOE_BLOCK * MOE_BLOCK
    pend = jnp.cumsum(padded)
    pstart = pend - padded
    dest = pstart[flat_e] + rank
    nb = -(-tk // MOE_BLOCK) + N_EXPERTS
    n_slots = nb * MOE_BLOCK
    pair = _slot_owner(dest, n_slots)
    used = pair >= 0
    row_src = jnp.where(used, pair // TOP_K, 0)
    row_dst = jnp.where(used, (pair % TOP_K) * r + pair // TOP_K,
                        tk + jnp.arange(n_slots, dtype=jnp.int32) % MOE_BLOCK)
    block_start = jnp.arange(nb, dtype=jnp.int32) * MOE_BLOCK
    block_expert = jnp.minimum(jnp.searchsorted(pend, block_start, side='right'),
                               N_EXPERTS - 1).astype(jnp.int32)
    return block_expert, row_src, row_dst


def _pad_heads(t, axis):
    axis = axis % t.ndim
    shp = t.shape
    t = t.reshape(shp[:axis] + (RWKV_HEADS, RWKV_HEAD_DIM) + shp[axis + 1:])
    pad = [(0, 0)] * t.ndim
    pad[axis + 1] = (0, HEAD_PAD - RWKV_HEAD_DIM)
    t = jnp.pad(t, pad)
    return t.reshape(shp[:axis] + (RWKV_PAD,) + shp[axis + 1:])


def _rope_tables(lp):
    half = ROPE_DIM // 2
    inv_freq = ROPE_THETA ** (-jnp.arange(0, ROPE_DIM, 2, dtype=F32) / ROPE_DIM)
    ang = jnp.arange(lp, dtype=F32)[:, None] * inv_freq[None, :]
    cos, sin = jnp.cos(ang), jnp.sin(ang)
    ones = jnp.ones((lp, DIFF_QK_DIM - ROPE_DIM), F32)
    zeros = jnp.zeros((lp, DIFF_QK_DIM - ROPE_DIM), F32)
    zh = jnp.zeros((lp, half), F32)
    rc = jnp.concatenate([cos, cos, ones], axis=1)
    rs1 = jnp.concatenate([-sin, zh, zeros], axis=1)
    rs2 = jnp.concatenate([zh, sin, zeros], axis=1)
    tile2 = lambda a: jnp.concatenate([a, a], axis=1)
    return tile2(rc), tile2(rs1), tile2(rs2)


def kernel(x, meta_tokens, mix_norm_g, w_in, tm_mu, tm_w0, tm_w_decay_up, tm_a0, tm_w_a_up, tm_w_g_up, tm_k_k, tm_k_a, tm_r_k, tm_gn_g, tm_gn_b, da_q_norm_g, da_k_norm_g, da_lambda_q1, da_lambda_k1, da_lambda_q2, da_lambda_k2, da_subln_g, sc_conv_w, w_out, ffn_norm_g, ffn_w_gate, ffn_w_up, ffn_w_down, router_w, moe_w_gate, moe_w_up, moe_w_down):
    b, seq, d = x.shape
    depth = w_in.shape[0]
    l = N_META + seq
    lp = -(-l // LANES) * LANES
    tm = _row_tile(lp)
    r = b * lp
    s1 = RWKV_WIDTH

    meta = jnp.broadcast_to(meta_tokens.astype(x.dtype)[None], (b, N_META, d))
    h = jnp.concatenate([meta, x, jnp.zeros((b, lp - l, d), x.dtype)], axis=1).reshape(r, d)
    rc, rs1, rs2 = _rope_tables(lp)

    for i in range(depth):
        lam_init = 0.8 - 0.6 * math.exp(-0.3 * i)
        wi = w_in[i]
        w_cat = jnp.concatenate(
            [_pad_heads(wi[:, 0:s1], 1), _pad_heads(wi[:, s1:2 * s1], 1),
             _pad_heads(wi[:, 2 * s1:3 * s1], 1), wi[:, 3 * s1:]], axis=1).astype(BF16)
        qg = jnp.tile(da_q_norm_g[i], 2)[None]
        kg = jnp.tile(da_k_norm_g[i], 2)[None]
        pa, qs, kh, vt, pc = _inproj(h, mix_norm_g[i][None], w_cat, rc, rs1, rs2, qg, kg,
                                     tm=tm, lp=lp)

        mu = tm_mu[i]
        mu_p = jnp.concatenate([_pad_heads(mu[0:s1], 0), _pad_heads(mu[s1:2 * s1], 0),
                                _pad_heads(mu[2 * s1:3 * s1], 0), mu[3 * s1:]])[None]
        vecs = jnp.stack([_pad_heads(tm_w0[i], 0), _pad_heads(tm_a0[i], 0),
                          _pad_heads(tm_k_k[i], 0), _pad_heads(tm_k_a[i], 0),
                          _pad_heads(tm_r_k[i].reshape(s1), 0), _pad_heads(tm_gn_g[i], 0),
                          _pad_heads(tm_gn_b[i], 0), jnp.zeros((RWKV_PAD,), F32)])
        zl = lambda n: jnp.zeros((n, RWKV_PAD), F32)
        wd_p = jnp.concatenate([_pad_heads(tm_w_decay_up[i], 1), zl(AAA_LORA + GATE_LORA)], axis=0)
        wa_p = jnp.concatenate([zl(DECAY_LORA), _pad_heads(tm_w_a_up[i], 1), zl(GATE_LORA)], axis=0)
        wg_p = jnp.concatenate([zl(DECAY_LORA + AAA_LORA), _pad_heads(tm_w_g_up[i], 1)], axis=0)
        oa = _rwkv(pa.reshape(b, lp, PA_COLS), mu_p, vecs, wd_p, wa_p, wg_p, tb=tm)

        lamv = jnp.zeros((SUBLANES, LANES), F32).at[0:4, :DIFF_QK_DIM].set(
            jnp.stack([da_lambda_q1[i], da_lambda_k1[i], da_lambda_q2[i], da_lambda_k2[i]]))
        ob = _attention(qs.reshape(2, b, lp, DIFF_WIDTH), kh.reshape(b, lp, DIFF_WIDTH),
                        vt.reshape(b, lp // tm, DIFF_HEADS, VT_ROWS, tm), lamv, da_subln_g[i][None],
                        tq=tm, lam_init=lam_init)

        wo = w_out[i]
        cw = jnp.zeros((SUBLANES, CONV_WIDTH), F32).at[:CONV_K].set(sc_conv_w[i])
        h = _outproj(h, oa.reshape(r, RWKV_PAD), ob.reshape(r, DIFF_WIDTH), pc, cw,
                     _pad_heads(wo[:s1], 0).astype(BF16), wo[s1:s1 + DIFF_WIDTH].astype(BF16),
                     wo[s1 + DIFF_WIDTH:].astype(BF16), tm=tm, lp=lp)

        j = i // 2
        if i % 2 == 0:
            h = _dense_ffn(h, ffn_norm_g[i][None], ffn_w_gate[j].astype(BF16),
                           ffn_w_up[j].astype(BF16), ffn_w_down[j].astype(BF16), tm=tm)
        else:
            rw = jnp.zeros((d, LANES), F32).at[:, :N_EXPERTS].set(router_w[j])
            hn, route = _router(h, ffn_norm_g[i][None], rw, tm=tm)
            block_expert, row_src, row_dst = _routing_tables(route)
            y = _moe_experts(block_expert, row_src, row_dst, hn, moe_w_gate[j].astype(BF16),
                             moe_w_up[j].astype(BF16), moe_w_down[j].astype(BF16))
            h = _moe_combine(h, route, y, tm=tm)

    return h.reshape(b, lp, d)[:, N_META:l]
```

```python
import functools
import math

import jax
import jax.numpy as jnp
from jax import lax
from jax.experimental import pallas as pl
from jax.experimental.pallas import tpu as pltpu

N_META = 16
RWKV_HEAD_DIM = 64
RWKV_HEADS = 4
RWKV_WIDTH = RWKV_HEADS * RWKV_HEAD_DIM
DECAY_LORA = 32
AAA_LORA = 32
GATE_LORA = 64
LORA_COLS = DECAY_LORA + AAA_LORA + GATE_LORA
DIFF_QK_DIM = 64
DIFF_V_DIM = 128
DIFF_HEADS = 4
DIFF_WIDTH = DIFF_HEADS * DIFF_V_DIM
CONV_WIDTH = 256
CONV_K = 3
ROPE_THETA = 500000.0
ROPE_DIM = DIFF_QK_DIM // 4
N_EXPERTS = 8
TOP_K = 2
MOE_BLOCK = 256
NORM_EPS = 1e-6
RWKV_GN_EPS = 64e-5
SUBLN_EPS = 1e-5
NEG_INF = -1e30

LANES = 128
SUBLANES = 8
MXU_DIM = 256
VMEM_LIMIT_BYTES = 56 * 1024 * 1024

HEAD_PAD = LANES
RWKV_PAD = RWKV_HEADS * HEAD_PAD
PA_COLS = 3 * RWKV_PAD + LORA_COLS
PB_COLS = 3 * DIFF_WIDTH
PC_COLS = 3 * CONV_WIDTH
VT_ROWS = DIFF_V_DIM + 16
LOG2_E = math.log2(math.e)
CHUNK = 64

F32 = jnp.float32
BF16 = jnp.bfloat16
HIGHEST = lax.Precision.HIGHEST


def _dot(a, b):
    return jnp.dot(a.astype(BF16), b.astype(BF16), preferred_element_type=F32)


def _dot_nt(a, b):
    return lax.dot_general(a.astype(BF16), b.astype(BF16), (((1,), (1,)), ((), ())),
                           preferred_element_type=F32)


def _dot_f32(a, b):
    return jnp.dot(a, b, preferred_element_type=F32, precision=HIGHEST)


def _row_tile(lp):
    for t in (384, 256, 128):
        if lp % t == 0:
            return t
    raise ValueError(f"padded length {lp} is not a multiple of {LANES}")


def _params(sem):
    return pltpu.CompilerParams(dimension_semantics=sem, vmem_limit_bytes=VMEM_LIMIT_BYTES)


def _inproj_kernel(h_ref, g_ref, w_ref, rc_ref, rs1_ref, rs2_ref, qg_ref, kg_ref,
                   pa_ref, qs_ref, kh_ref, vt_ref, pc_ref):
    tm = h_ref.shape[0]
    x = h_ref[...]
    ms = jnp.mean(x * x, axis=-1, keepdims=True)
    xn = (x * lax.rsqrt(ms + NORM_EPS) * g_ref[...]).astype(BF16)
    pa_ref[...] = jnp.dot(xn, w_ref[:, :PA_COLS], preferred_element_type=F32)
    pc_ref[...] = jnp.dot(xn, w_ref[:, PA_COLS + PB_COLS:], preferred_element_type=F32)
    pb = jnp.dot(xn, w_ref[:, PA_COLS:PA_COLS + PB_COLS], preferred_element_type=F32)

    lane = lax.broadcasted_iota(jnp.int32, (tm, LANES), 1)
    lo = lane < DIFF_QK_DIM
    rc, rs1, rs2 = rc_ref[...], rs1_ref[...], rs2_ref[...]

    def norm_rope(t, g):
        t2 = t * t
        ss_lo = jnp.sum(jnp.where(lo, t2, 0.0), axis=-1, keepdims=True)
        ss_hi = jnp.sum(jnp.where(lo, 0.0, t2), axis=-1, keepdims=True)
        inv = jnp.where(lo, lax.rsqrt(ss_lo / DIFF_QK_DIM + NORM_EPS),
                        lax.rsqrt(ss_hi / DIFF_QK_DIM + NORM_EPS))
        tn = t * inv * g
        half = ROPE_DIM // 2
        return tn * rc + pltpu.roll(tn, LANES - half, 1) * rs1 + pltpu.roll(tn, half, 1) * rs2

    for hd in range(DIFF_HEADS):
        cs = slice(hd * LANES, (hd + 1) * LANES)
        q = norm_rope(pb[:, cs], qg_ref[...]) * (DIFF_QK_DIM ** -0.5 * LOG2_E)
        qs_ref[0, :, cs] = jnp.where(lo, q, 0.0).astype(BF16)
        qs_ref[1, :, cs] = jnp.where(lo, 0.0, q).astype(BF16)
        kcs = slice(DIFF_WIDTH + hd * LANES, DIFF_WIDTH + (hd + 1) * LANES)
        kh_ref[:, cs] = norm_rope(pb[:, kcs], kg_ref[...]).astype(BF16)
        vcs = slice(2 * DIFF_WIDTH + hd * LANES, 2 * DIFF_WIDTH + (hd + 1) * LANES)
        vt_ref[0, hd, :DIFF_V_DIM] = pb[:, vcs].T.astype(BF16)
        vt_ref[0, hd, DIFF_V_DIM:] = jnp.ones((VT_ROWS - DIFF_V_DIM, tm), BF16)


def _inproj(h, g, w, rc, rs1, rs2, qg, kg, *, tm, lp):
    r, d = h.shape
    nt_b = lp // tm
    row = lambda i: (i, 0)
    const = lambda i: (0, 0)
    rope = lambda i: (i % nt_b, 0)
    return pl.pallas_call(
        _inproj_kernel,
        grid=(r // tm,),
        in_specs=[
            pl.BlockSpec((tm, d), row),
            pl.BlockSpec((1, d), const),
            pl.BlockSpec(w.shape, const),
            pl.BlockSpec((tm, LANES), rope),
            pl.BlockSpec((tm, LANES), rope),
            pl.BlockSpec((tm, LANES), rope),
            pl.BlockSpec((1, LANES), const),
            pl.BlockSpec((1, LANES), const),
        ],
        out_specs=[
            pl.BlockSpec((tm, PA_COLS), row),
            pl.BlockSpec((2, tm, DIFF_WIDTH), lambda i: (0, i, 0)),
            pl.BlockSpec((tm, DIFF_WIDTH), row),
            pl.BlockSpec((1, DIFF_HEADS, VT_ROWS, tm), lambda i: (i, 0, 0, 0)),
            pl.BlockSpec((tm, PC_COLS), row),
        ],
        out_shape=[
            jax.ShapeDtypeStruct((r, PA_COLS), F32),
            jax.ShapeDtypeStruct((2, r, DIFF_WIDTH), BF16),
            jax.ShapeDtypeStruct((r, DIFF_WIDTH), BF16),
            jax.ShapeDtypeStruct((r // tm, DIFF_HEADS, VT_ROWS, tm), BF16),
            jax.ShapeDtypeStruct((r, PC_COLS), F32),
        ],
        compiler_params=_params(("arbitrary",)),
        name="inproj",
    )(h, g, w, rc, rs1, rs2, qg, kg)


def _split3(x):
    hi = x.astype(BF16)
    r1 = x - hi.astype(F32)
    mid = r1.astype(BF16)
    lo = (r1 - mid.astype(F32)).astype(BF16)
    return hi, mid, lo


def _dot_hi(a, b):
    ah = a.astype(BF16)
    al = (a - ah.astype(F32)).astype(BF16)
    bh = b.astype(BF16)
    bl = (b - bh.astype(F32)).astype(BF16)
    d = lambda p, q: jnp.dot(p, q, preferred_element_type=F32)
    return d(ah, bh) + d(ah, bl) + d(al, bh)


def _bmm(a, b):
    return jnp.einsum('cik,ckj->cij', a.astype(BF16), b.astype(BF16), preferred_element_type=F32)


def _bmm_nt(a, b):
    return jnp.einsum('cik,cjk->cij', a.astype(BF16), b.astype(BF16), preferred_element_type=F32)


def _rwkv_kernel(pa_ref, halo_ref, mu_ref, vec_ref, wd_ref, wa_ref, wg_ref, oa_ref, hstate_s):
    tb = pa_ref.shape[0]
    nc = tb // CHUNK
    i = pl.program_id(1)

    @pl.when(i == 0)
    def _():
        hstate_s[...] = jnp.zeros_like(hstate_s)

    x = pa_ref[...]
    prev_row = jnp.where(i == 0, 0.0, halo_ref[SUBLANES - 1:SUBLANES, :])
    row = lax.broadcasted_iota(jnp.int32, (tb, 1), 0)
    prev = jnp.where(row == 0, prev_row, pltpu.roll(x, 1, 0))
    hx = x + (prev - x) * mu_ref[...]
    r = hx[:, 0:RWKV_PAD]
    k = hx[:, RWKV_PAD:2 * RWKV_PAD]
    v = hx[:, 2 * RWKV_PAD:3 * RWKV_PAD]
    lora = hx[:, 3 * RWKV_PAD:]
    w0, a0, k_k, k_a = vec_ref[0:1, :], vec_ref[1:2, :], vec_ref[2:3, :], vec_ref[3:4, :]
    r_k, gn_g, gn_b = vec_ref[4:5, :], vec_ref[5:6, :], vec_ref[6:7, :]

    w = -jax.nn.softplus(-(w0 + _dot_hi(jnp.tanh(lora), wd_ref[...]))) - 0.5
    wl = -jnp.exp(w)
    alr = jax.nn.sigmoid(a0 + _dot(lora, wa_ref[...]))
    gate = _dot(jax.nn.sigmoid(lora), wg_ref[...])
    kk = k * k_k
    k2 = k * (1.0 + (alr - 1.0) * k_a)
    rkr = r * k2 * r_k

    ri = lax.broadcasted_iota(jnp.int32, (CHUNK, CHUNK), 0)
    ci = lax.broadcasted_iota(jnp.int32, (CHUNK, CHUNK), 1)
    lane_c = lax.broadcasted_iota(jnp.int32, (CHUNK, HEAD_PAD), 1)
    row_c = lax.broadcasted_iota(jnp.int32, (CHUNK, HEAD_PAD), 0)
    col_c = jnp.where(lane_c >= CHUNK, lane_c - CHUNK, lane_c)
    low_incl2 = (col_c <= row_c)[None]
    low_strict2 = (col_c < row_c)[None]
    tril_b = jnp.where(ci <= ri, 1.0, 0.0).astype(BF16)
    eye_k = (lax.broadcasted_iota(jnp.int32, (HEAD_PAD, HEAD_PAD), 0)
             == lax.broadcasted_iota(jnp.int32, (HEAD_PAD, HEAD_PAD), 1))[None]
    n_sq = int(math.log2(CHUNK))

    parts = _split3(wl)
    cums = []
    for c in range(nc):
        rows = slice(c * CHUNK, (c + 1) * CHUNK)
        cums.append(sum(jnp.dot(tril_b, p[rows], preferred_element_type=F32) for p in parts))
    cum_all = jnp.concatenate(cums, axis=0)

    lane = lax.broadcasted_iota(jnp.int32, (tb, HEAD_PAD), 1)
    real = lane < RWKV_HEAD_DIM
    to3 = lambda t: t.reshape(nc, CHUNK, HEAD_PAD)
    heads = range(RWKV_HEADS)
    hsl = [slice(hd * HEAD_PAD, (hd + 1) * HEAD_PAD) for hd in heads]
    kkn, bonus = [], []
    for cs in hsl:
        kk_h = kk[:, cs]
        nrm = jnp.sqrt(jnp.sum(kk_h * kk_h, axis=-1, keepdims=True))
        kkn.append(kk_h / jnp.maximum(nrm, 1e-12))
        bonus.append(jnp.sum(rkr[:, cs], axis=-1, keepdims=True) * v[:, cs])

    vc = [to3(v[:, cs]) for cs in hsl]
    zero_v = jnp.zeros_like(vc[0])
    ah, rh, last, g, bk_last = [], [], [], [], []
    for hd, cs in enumerate(hsl):
        wl3, cum = to3(wl[:, cs]), to3(cum_all[:, cs])
        rr, kc = to3(r[:, cs]), to3(k2[:, cs])
        ac, bc = to3(-kkn[hd]), to3(kkn[hd] * alr[:, cs])
        last.append(cum[:, CHUNK - 1:CHUNK, :])
        e_neg = jnp.exp(-cum)
        e_last = jnp.exp(last[hd] - cum)
        ah.append(ac * jnp.exp(cum - wl3))
        rh.append(rr * jnp.exp(cum))
        bk_last.append(jnp.concatenate([bc * e_last, kc * e_last], axis=1))
        g.append(_bmm_nt(jnp.concatenate([ah[hd], rh[hd]], axis=1),
                         jnp.concatenate([bc * e_neg, kc * e_neg], axis=1)))
    a_abk = [jnp.where(low_strict2, g[hd][:, :CHUNK], 0.0) for hd in heads]
    r_bk = [jnp.where(low_incl2, g[hd][:, CHUNK:], 0.0) for hd in heads]
    xx = [jnp.concatenate([_bmm(a_abk[hd], jnp.concatenate([zero_v, vc[hd]], axis=1)), ah[hd]],
                          axis=2) for hd in heads]
    ppad = [jnp.where(lane_c < CHUNK, a_abk[hd], 0.0) for hd in heads]
    for j in range(n_sq):
        for hd in heads:
            pj = ppad[hd][:, :, :CHUNK]
            if j + 1 < n_sq:
                res = _bmm(pj, jnp.concatenate([xx[hd], ppad[hd]], axis=2))
                xx[hd] = xx[hd] + res[:, :, :2 * HEAD_PAD]
                ppad[hd] = res[:, :, 2 * HEAD_PAD:]
            else:
                xx[hd] = xx[hd] + _bmm(pj, xx[hd])
    qm, y0, n0 = [], [], []
    for hd in heads:
        lhs = jnp.concatenate([r_bk[hd], jnp.swapaxes(bk_last[hd], 1, 2)], axis=1)
        rhs = jnp.concatenate([xx[hd], jnp.concatenate([vc[hd], zero_v], axis=2)], axis=1)
        out = _bmm(lhs, rhs)
        y0.append(out[:, :CHUNK, :HEAD_PAD])
        n0.append(out[:, CHUNK:, :HEAD_PAD])
        q = rh[hd] + out[:, :CHUNK, HEAD_PAD:]
        m = jnp.where(eye_k, jnp.exp(last[hd]), 0.0) + out[:, CHUNK:, HEAD_PAD:]
        qm.append(jnp.concatenate([q, m], axis=1).astype(BF16))

    hs = [hstate_s[hd] for hd in heads]
    ys = [[] for _ in heads]
    for c in range(nc):
        for hd in heads:
            res = jnp.dot(qm[hd][c], hs[hd].astype(BF16), preferred_element_type=F32)
            ys[hd].append(res[:CHUNK] + y0[hd][c])
            hs[hd] = res[CHUNK:] + n0[hd][c]

    for hd, cs in enumerate(hsl):
        hstate_s[hd] = hs[hd]
        y = jnp.concatenate(ys[hd], axis=0)
        mean = jnp.sum(y, axis=-1, keepdims=True) / RWKV_HEAD_DIM
        dlt = jnp.where(real, y - mean, 0.0)
        var = jnp.sum(dlt * dlt, axis=-1, keepdims=True) / RWKV_HEAD_DIM
        yn = dlt * lax.rsqrt(var + RWKV_GN_EPS) * gn_g[:, cs] + gn_b[:, cs]
        oa_ref[:, cs] = ((yn + bonus[hd]) * gate[:, cs]).astype(BF16)


def _rwkv(pa, mu, vecs, wd, wa, wg, *, tb):
    b, lp, _ = pa.shape
    tile = lambda bi, i: (bi, i, 0)
    const = lambda bi, i: (0, 0)
    halo = lambda bi, i: (bi, jnp.maximum(i * (tb // SUBLANES) - 1, 0), 0)
    return pl.pallas_call(
        _rwkv_kernel,
        grid=(b, lp // tb),
        in_specs=[
            pl.BlockSpec((None, tb, PA_COLS), tile),
            pl.BlockSpec((None, SUBLANES, PA_COLS), halo),
            pl.BlockSpec((1, PA_COLS), const),
            pl.BlockSpec((SUBLANES, RWKV_PAD), const),
            pl.BlockSpec((LORA_COLS, RWKV_PAD), const),
            pl.BlockSpec((LORA_COLS, RWKV_PAD), const),
            pl.BlockSpec((LORA_COLS, RWKV_PAD), const),
        ],
        out_specs=pl.BlockSpec((None, tb, RWKV_PAD), tile),
        out_shape=jax.ShapeDtypeStruct((b, lp, RWKV_PAD), BF16),
        scratch_shapes=[pltpu.VMEM((RWKV_HEADS, HEAD_PAD, HEAD_PAD), F32)],
        compiler_params=_params(("arbitrary", "arbitrary")),
        name="rwkv7",
    )(pa, pa, mu, vecs, wd, wa, wg)


ATTN_HEADS_PER_STEP = 4


def _attn_kernel(q_ref, k_ref, vt_ref, lamv_ref, sg_ref, o_ref, acc_s, st_s, p_s, *, lam_init):
    tq = o_ref.shape[0]
    tk = vt_ref.shape[3]
    qi = pl.program_id(2)
    heads = range(ATTN_HEADS_PER_STEP)
    hl = [slice(h * LANES, (h + 1) * LANES) for h in heads]

    def scores(h, ki):
        k = k_ref[pl.ds(pl.multiple_of(ki * tk, tk), tk), hl[h]]
        return lax.dot_general(k, q_ref[:, :, hl[h]].reshape(2 * tq, LANES),
                               (((1,), (1,)), ((), ())), preferred_element_type=F32)

    def values(h, ki, p, alpha):
        acc_s[h] = alpha * acc_s[h] + jnp.dot(vt_ref[ki, h], p, preferred_element_type=F32)

    st = [scores(h, qi) for h in heads]

    @pl.when(qi > 0)
    def _():
        for h in heads:
            st_s[h, 0] = scores(h, 0)

    kr = lax.broadcasted_iota(jnp.int32, (tk, 2 * tq), 0)
    qc = lax.broadcasted_iota(jnp.int32, (tk, 2 * tq), 1)
    qc = jnp.where(qc >= tq, qc - tq, qc)
    m0 = []
    for h in heads:
        s_h = jnp.where(kr <= qc, st[h], NEG_INF)
        m0.append(jnp.max(s_h, axis=0, keepdims=True))
        p_s[h, 1] = jnp.exp2(s_h - m0[h]).astype(BF16)
        acc_s[h] = jnp.zeros(acc_s.shape[1:], acc_s.dtype)

    def trip(k, carry, slot):
        out = []
        nxt = jnp.minimum(k + 1, qi - 1)
        prv = jnp.where(k == 0, qi, k - 1)
        for h in heads:
            st_s[h, 1 - slot] = scores(h, nxt)
        for h in heads:
            m_prev, alpha_prev = carry[h]
            s_k = st_s[h, slot]
            m_new = jnp.maximum(m_prev, jnp.max(s_k, axis=0, keepdims=True))
            alpha = jnp.exp2(m_prev - m_new)
            values(h, prv, p_s[h, 1 - slot], alpha_prev)
            p_s[h, slot] = jnp.exp2(s_k - m_new).astype(BF16)
            out.append((m_new, alpha))
        return tuple(out)

    def two_trips(j, c):
        c = trip(2 * j, c, 0)
        return lax.cond(2 * j + 1 < qi, lambda cc: trip(2 * j + 1, cc, 1), lambda cc: cc, c)

    init = tuple((m0[h], jnp.ones_like(m0[h])) for h in heads)
    carry = lax.fori_loop(0, (qi + 1) // 2, two_trips, init)
    odd = qi % 2 == 1

    @pl.when(odd)
    def _():
        for h in heads:
            values(h, qi - 1, p_s[h, 0], carry[h][1])

    @pl.when(jnp.logical_not(odd))
    def _():
        for h in heads:
            values(h, jnp.where(qi == 0, 0, qi - 1), p_s[h, 1], carry[h][1])

    lv = lamv_ref[...]
    lam = (jnp.exp(jnp.sum(lv[0:1] * lv[1:2], axis=-1, keepdims=True))
           - jnp.exp(jnp.sum(lv[2:3] * lv[3:4], axis=-1, keepdims=True)) + lam_init)
    for h in heads:
        acc = acc_s[h]
        ot = acc[:DIFF_V_DIM] / acc[DIFF_V_DIM:DIFF_V_DIM + 1]
        od = (ot[:, :tq] - lam * ot[:, tq:]).T
        ms = jnp.mean(od * od, axis=-1, keepdims=True)
        o_ref[:, hl[h]] = (od * lax.rsqrt(ms + SUBLN_EPS) * sg_ref[...]
                           * (1.0 - lam_init)).astype(BF16)


def _attention(qs, kh, vt, lamv, sg, *, tq, lam_init):
    _, b, lp, _ = qs.shape
    nq = lp // tq
    hp = ATTN_HEADS_PER_STEP
    const = lambda bi, hg, qi: (0, 0)
    return pl.pallas_call(
        functools.partial(_attn_kernel, lam_init=lam_init),
        grid=(b, DIFF_HEADS // hp, nq),
        in_specs=[
            pl.BlockSpec((2, None, tq, hp * LANES), lambda bi, hg, qi: (0, bi, qi, hg)),
            pl.BlockSpec((None, lp, hp * LANES), lambda bi, hg, qi: (bi, 0, hg)),
            pl.BlockSpec((None, nq, hp, VT_ROWS, tq), lambda bi, hg, qi: (bi, 0, hg, 0, 0)),
            pl.BlockSpec((SUBLANES, LANES), const),
            pl.BlockSpec((1, LANES), const),
        ],
        out_specs=pl.BlockSpec((None, tq, hp * LANES), lambda bi, hg, qi: (bi, qi, hg)),
        out_shape=jax.ShapeDtypeStruct((b, lp, DIFF_WIDTH), BF16),
        scratch_shapes=[pltpu.VMEM((hp, VT_ROWS, 2 * tq), F32),
                        pltpu.VMEM((hp, 2, tq, 2 * tq), F32),
                        pltpu.VMEM((hp, 2, tq, 2 * tq), BF16)],
        compiler_params=_params(("arbitrary", "arbitrary", "arbitrary")),
        name="diffattn",
    )(qs, kh, vt, lamv, sg)


def _outproj_kernel(h_ref, oa_ref, ob_ref, pc_ref, pch_ref, cw_ref, wa_ref, wb_ref, wc_ref,
                    o_ref, *, tiles_per_batch):
    tm = h_ref.shape[0]
    first = (pl.program_id(0) % tiles_per_batch) == 0
    pc = pc_ref[...]
    bg = pc[:, :CONV_WIDTH]
    z = pc[:, CONV_WIDTH:2 * CONV_WIDTH] * pc[:, 2 * CONV_WIDTH:]
    ph = pch_ref[...]
    zh = jnp.where(first, 0.0, ph[:, CONV_WIDTH:2 * CONV_WIDTH] * ph[:, 2 * CONV_WIDTH:])
    row = lax.broadcasted_iota(jnp.int32, (tm, 1), 0)
    z1 = jnp.where(row == 0, zh[SUBLANES - 1:SUBLANES], pltpu.roll(z, 1, 0))
    z2 = jnp.where(row == 0, zh[SUBLANES - 2:SUBLANES - 1],
                   jnp.where(row == 1, zh[SUBLANES - 1:SUBLANES], pltpu.roll(z, 2, 0)))
    cw = cw_ref[...]
    y = cw[0:1] * z2 + cw[1:2] * z1 + cw[2:3] * z
    oc = (bg * y).astype(BF16)
    o_ref[...] = (h_ref[...]
                  + jnp.dot(oa_ref[...], wa_ref[...], preferred_element_type=F32)
                  + jnp.dot(ob_ref[...], wb_ref[...], preferred_element_type=F32)
                  + jnp.dot(oc, wc_ref[...], preferred_element_type=F32))


def _outproj(h, oa, ob, pc, cw, wa, wb, wc, *, tm, lp):
    r, d = h.shape
    row = lambda i: (i, 0)
    const = lambda i: (0, 0)
    halo = lambda i: (jnp.maximum(i * (tm // SUBLANES) - 1, 0), 0)
    return pl.pallas_call(
        functools.partial(_outproj_kernel, tiles_per_batch=lp // tm),
        grid=(r // tm,),
        in_specs=[
            pl.BlockSpec((tm, d), row),
            pl.BlockSpec((tm, RWKV_PAD), row),
            pl.BlockSpec((tm, DIFF_WIDTH), row),
            pl.BlockSpec((tm, PC_COLS), row),
            pl.BlockSpec((SUBLANES, PC_COLS), halo),
            pl.BlockSpec((SUBLANES, CONV_WIDTH), const),
            pl.BlockSpec(wa.shape, const),
            pl.BlockSpec(wb.shape, const),
            pl.BlockSpec(wc.shape, const),
        ],
        out_specs=pl.BlockSpec((tm, d), row),
        out_shape=jax.ShapeDtypeStruct((r, d), F32),
        compiler_params=_params(("arbitrary",)),
        name="outproj",
    )(h, oa, ob, pc, pc, cw, wa, wb, wc)


def _swiglu_block(xb, wg_ref, wu_ref, wd_ref, side_work=None):
    d_ff = wg_ref.shape[1]
    acc = jnp.zeros((xb.shape[0], wd_ref.shape[1]), F32)
    for c in range(d_ff // MXU_DIM):
        cs = slice(c * MXU_DIM, (c + 1) * MXU_DIM)
        gt = jnp.dot(xb, wg_ref[:, cs], preferred_element_type=F32)
        up = jnp.dot(xb, wu_ref[:, cs], preferred_element_type=F32)
        act = (jax.nn.silu(gt) * up).astype(BF16)
        acc = acc + jnp.dot(act, wd_ref[cs, :], preferred_element_type=F32)
        if side_work is not None:
            side_work(c)
    return acc


def _rms(x, g):
    ms = jnp.mean(x * x, axis=-1, keepdims=True)
    return x * lax.rsqrt(ms + NORM_EPS) * g


def _ffn_kernel(h_ref, g_ref, wg_ref, wu_ref, wd_ref, o_ref):
    x = h_ref[...]
    xn = _rms(x, g_ref[...]).astype(BF16)
    o_ref[...] = x + _swiglu_block(xn, wg_ref, wu_ref, wd_ref)


def _dense_ffn(h, g, wg, wu, wd, *, tm):
    r, d = h.shape
    row = lambda i: (i, 0)
    const = lambda i: (0, 0)
    return pl.pallas_call(
        _ffn_kernel,
        grid=(r // tm,),
        in_specs=[
            pl.BlockSpec((tm, d), row),
            pl.BlockSpec((1, d), const),
            pl.BlockSpec(wg.shape, const),
            pl.BlockSpec(wu.shape, const),
            pl.BlockSpec(wd.shape, const),
        ],
        out_specs=pl.BlockSpec((tm, d), row),
        out_shape=jax.ShapeDtypeStruct((r, d), F32),
        compiler_params=_params(("arbitrary",)),
        name="dense_ffn",
    )(h, g, wg, wu, wd)


def _router_kernel(h_ref, g_ref, rw_ref, hn_ref, route_ref):
    tm = h_ref.shape[0]
    xn = _rms(h_ref[...], g_ref[...])
    hn_ref[...] = xn
    logits = _dot_f32(xn, rw_ref[...])
    lane = lax.broadcasted_iota(jnp.int32, (tm, LANES), 1)
    logits = jnp.where(lane < N_EXPERTS, logits, -jnp.inf)
    m1 = jnp.max(logits, axis=-1, keepdims=True)
    i1 = jnp.min(jnp.where(logits == m1, lane, LANES), axis=-1, keepdims=True)
    rest = jnp.where(lane == i1, -jnp.inf, logits)
    m2 = jnp.max(rest, axis=-1, keepdims=True)
    i2 = jnp.min(jnp.where(rest == m2, lane, LANES), axis=-1, keepdims=True)
    e2 = jnp.exp(m2 - m1)
    den = 1.0 + e2
    route_ref[...] = jnp.where(lane == 0, i1.astype(F32),
                     jnp.where(lane == 1, i2.astype(F32),
                     jnp.where(lane == 2, 1.0 / den,
                     jnp.where(lane == 3, e2 / den, 0.0))))


def _router(h, g, rw, *, tm):
    r, d = h.shape
    row = lambda i: (i, 0)
    const = lambda i: (0, 0)
    return pl.pallas_call(
        _router_kernel,
        grid=(r // tm,),
        in_specs=[pl.BlockSpec((tm, d), row), pl.BlockSpec((1, d), const),
                  pl.BlockSpec((d, LANES), const)],
        out_specs=[pl.BlockSpec((tm, d), row), pl.BlockSpec((tm, LANES), row)],
        out_shape=[jax.ShapeDtypeStruct((r, d), F32), jax.ShapeDtypeStruct((r, LANES), F32)],
        compiler_params=_params(("arbitrary",)),
        name="router",
    )(h, g, rw)


def _row_copy(src, dst, sem, src_row, dst_row):
    return pltpu.make_async_copy(src.at[pl.ds(src_row, 1), :], dst.at[pl.ds(dst_row, 1), :], sem)


DMA_UNROLL = 8
DMA_ISSUE_SLICES = 8


def _moe_kernel(bexp_ref, rsrc_ref, rdst_ref, hn_hbm, wg_ref, wu_ref, wd_ref, y_hbm,
                xbuf, ybuf, gsem, ssem):
    del bexp_ref
    blk = ybuf.shape[1]
    i = pl.program_id(0)
    nb = pl.num_programs(0)
    slot = i % 2
    other = 1 - slot
    dump0 = y_hbm.shape[0] - blk

    def for_rows(fn):
        def body(j, c):
            for u in range(DMA_UNROLL):
                fn(j * DMA_UNROLL + u)
            return c
        lax.fori_loop(0, blk // DMA_UNROLL, body, 0)

    def gather(block, s, j):
        return pltpu.make_async_copy(hn_hbm.at[rsrc_ref[block * blk + j]], xbuf.at[s, :, j, :],
                                     gsem.at[s])

    def scatter(dst_row, s, j):
        return _row_copy(ybuf.at[s], y_hbm, ssem.at[s], j, dst_row)

    @pl.when(i == 0)
    def _():
        for_rows(lambda j: gather(0, 0, j).start())
        ybuf[1] = jnp.zeros(ybuf.shape[1:], ybuf.dtype)

    for_rows(lambda j: gather(0, slot, j).wait())

    @pl.when(i >= 1)
    def _():
        for_rows(lambda j: scatter(0, slot, j).wait())

    nxt = jnp.minimum(i + 1, nb - 1)
    prev = jnp.maximum(i - 1, 0)
    per = -(-blk // DMA_ISSUE_SLICES)

    def side_work(c):
        for j in range(c * per, min((c + 1) * per, blk)):
            gather(nxt, other, j).start(priority=j % 2)
            dst = jnp.where(i == 0, dump0 + j, rdst_ref[prev * blk + j])
            scatter(dst, other, j).start(priority=(j + 1) % 2)

    xb = jnp.concatenate([xbuf[slot, c] for c in range(SUBLANES)], axis=1).astype(BF16)
    ybuf[slot] = _swiglu_block(xb, wg_ref, wu_ref, wd_ref, side_work)

    @pl.when(i == nb - 1)
    def _():
        for_rows(lambda j: gather(0, other, j).wait())
        for_rows(lambda j: scatter(0, other, j).wait())
        for_rows(lambda j: scatter(rdst_ref[i * blk + j], slot, j).start())
        for_rows(lambda j: scatter(0, slot, j).wait())


def _moe_experts(block_expert, row_src, row_dst, hn, wg, wu, wd):
    nb = block_expert.shape[0]
    r, d = hn.shape
    d_ff = wg.shape[2]
    hn = hn.reshape(r, SUBLANES, d // SUBLANES)
    wmap = lambda i, be, rs, rd: (be[i], 0, 0)
    grid_spec = pltpu.PrefetchScalarGridSpec(
        num_scalar_prefetch=3,
        grid=(nb,),
        in_specs=[
            pl.BlockSpec(memory_space=pl.ANY),
            pl.BlockSpec((None, d, d_ff), wmap),
            pl.BlockSpec((None, d, d_ff), wmap),
            pl.BlockSpec((None, d_ff, d), wmap),
        ],
        out_specs=pl.BlockSpec(memory_space=pl.ANY),
        scratch_shapes=[pltpu.VMEM((2, SUBLANES, MOE_BLOCK, d // SUBLANES), F32),
                        pltpu.VMEM((2, MOE_BLOCK, d), F32),
                        pltpu.SemaphoreType.DMA((2,)), pltpu.SemaphoreType.DMA((2,))],
    )
    return pl.pallas_call(
        _moe_kernel,
        grid_spec=grid_spec,
        out_shape=jax.ShapeDtypeStruct((TOP_K * r + MOE_BLOCK, d), F32),
        compiler_params=_params(("arbitrary",)),
        name="moe_experts",
    )(block_expert, row_src, row_dst, hn, wg, wu, wd)


def _combine_kernel(h_ref, route_ref, y0_ref, y1_ref, o_ref):
    route = route_ref[...]
    o_ref[...] = h_ref[...] + route[:, 2:3] * y0_ref[...] + route[:, 3:4] * y1_ref[...]


def _moe_combine(h, route, y, *, tm):
    r, d = h.shape
    nt = r // tm
    row = lambda i: (i, 0)
    return pl.pallas_call(
        _combine_kernel,
        grid=(nt,),
        in_specs=[
            pl.BlockSpec((tm, d), row),
            pl.BlockSpec((tm, LANES), row),
            pl.BlockSpec((tm, d), row),
            pl.BlockSpec((tm, d), lambda i: (i + nt, 0)),
        ],
        out_specs=pl.BlockSpec((tm, d), row),
        out_shape=jax.ShapeDtypeStruct((r, d), F32),
        compiler_params=_params(("arbitrary",)),
        name="moe_combine",
    )(h, route, y, y)


def _slot_owner_kernel(dest_ref, owner_ref):
    n_pairs = dest_ref.shape[0]
    n_slots = owner_ref.shape[0]

    def clear(i, c):
        for u in range(DMA_UNROLL):
            owner_ref[i * DMA_UNROLL + u] = -1
        return c

    lax.fori_loop(0, n_slots // DMA_UNROLL, clear, 0)

    def place(i, c):
        for u in range(DMA_UNROLL):
            p = i * DMA_UNROLL + u
            owner_ref[dest_ref[p]] = p
        return c

    lax.fori_loop(0, n_pairs // DMA_UNROLL, place, 0)


def _slot_owner(dest, n_slots):
    assert dest.shape[0] % DMA_UNROLL == 0 and n_slots % DMA_UNROLL == 0
    smem = pl.BlockSpec(memory_space=pltpu.SMEM)
    return pl.pallas_call(
        _slot_owner_kernel,
        in_specs=[smem],
        out_specs=smem,
        out_shape=jax.ShapeDtypeStruct((n_slots,), jnp.int32),
        name="slot_owner",
    )(dest)


def _routing_tables(route):
    r = route.shape[0]
    tk = r * TOP_K
    flat_e = route[:, :TOP_K].reshape(tk).astype(jnp.int32)
    onehot = (flat_e[:, None] == jnp.arange(N_EXPERTS, dtype=jnp.int32)[None, :]).astype(jnp.int32)
    csum = jnp.cumsum(onehot, axis=0)
    rank = jnp.sum(csum * onehot, axis=1) - 1
    counts = csum[-1]
    padded = (counts + MOE_BLOCK - 1) // MOE_BLOCK * MOE_BLOCK
    pend = jnp.cumsum(padded)
    pstart = pend - padded
    dest = pstart[flat_e] + rank
    nb = -(-tk // MOE_BLOCK) + N_EXPERTS
    n_slots = nb * MOE_BLOCK
    pair = _slot_owner(dest, n_slots)
    used = pair >= 0
    row_src = jnp.where(used, pair // TOP_K, 0)
    row_dst = jnp.where(used, (pair % TOP_K) * r + pair // TOP_K,
                        tk + jnp.arange(n_slots, dtype=jnp.int32) % MOE_BLOCK)
    block_start = jnp.arange(nb, dtype=jnp.int32) * MOE_BLOCK
    block_expert = jnp.minimum(jnp.searchsorted(pend, block_start, side='right'),
                               N_EXPERTS - 1).astype(jnp.int32)
    return block_expert, row_src, row_dst


def _pad_heads(t, axis):
    axis = axis % t.ndim
    shp = t.shape
    t = t.reshape(shp[:axis] + (RWKV_HEADS, RWKV_HEAD_DIM) + shp[axis + 1:])
    pad = [(0, 0)] * t.ndim
    pad[axis + 1] = (0, HEAD_PAD - RWKV_HEAD_DIM)
    t = jnp.pad(t, pad)
    return t.reshape(shp[:axis] + (RWKV_PAD,) + shp[axis + 1:])


def _rope_tables(lp):
    half = ROPE_DIM // 2
    inv_freq = ROPE_THETA ** (-jnp.arange(0, ROPE_DIM, 2, dtype=F32) / ROPE_DIM)
    ang = jnp.arange(lp, dtype=F32)[:, None] * inv_freq[None, :]
    cos, sin = jnp.cos(ang), jnp.sin(ang)
    ones = jnp.ones((lp, DIFF_QK_DIM - ROPE_DIM), F32)
    zeros = jnp.zeros((lp, DIFF_QK_DIM - ROPE_DIM), F32)
    zh = jnp.zeros((lp, half), F32)
    rc = jnp.concatenate([cos, cos, ones], axis=1)
    rs1 = jnp.concatenate([-sin, zh, zeros], axis=1)
    rs2 = jnp.concatenate([zh, sin, zeros], axis=1)
    tile2 = lambda a: jnp.concatenate([a, a], axis=1)
    return tile2(rc), tile2(rs1), tile2(rs2)


def kernel(x, meta_tokens, mix_norm_g, w_in, tm_mu, tm_w0, tm_w_decay_up, tm_a0, tm_w_a_up, tm_w_g_up, tm_k_k, tm_k_a, tm_r_k, tm_gn_g, tm_gn_b, da_q_norm_g, da_k_norm_g, da_lambda_q1, da_lambda_k1, da_lambda_q2, da_lambda_k2, da_subln_g, sc_conv_w, w_out, ffn_norm_g, ffn_w_gate, ffn_w_up, ffn_w_down, router_w, moe_w_gate, moe_w_up, moe_w_down):
    b, seq, d = x.shape
    depth = w_in.shape[0]
    l = N_META + seq
    lp = -(-l // LANES) * LANES
    tm = _row_tile(lp)
    r = b * lp
    s1 = RWKV_WIDTH

    meta = jnp.broadcast_to(meta_tokens.astype(x.dtype)[None], (b, N_META, d))
    h = jnp.concatenate([meta, x, jnp.zeros((b, lp - l, d), x.dtype)], axis=1).reshape(r, d)
    rc, rs1, rs2 = _rope_tables(lp)

    for i in range(depth):
        lam_init = 0.8 - 0.6 * math.exp(-0.3 * i)
        wi = w_in[i]
        w_cat = jnp.concatenate(
            [_pad_heads(wi[:, 0:s1], 1), _pad_heads(wi[:, s1:2 * s1], 1),
             _pad_heads(wi[:, 2 * s1:3 * s1], 1), wi[:, 3 * s1:]], axis=1).astype(BF16)
        qg = jnp.tile(da_q_norm_g[i], 2)[None]
        kg = jnp.tile(da_k_norm_g[i], 2)[None]
        pa, qs, kh, vt, pc = _inproj(h, mix_norm_g[i][None], w_cat, rc, rs1, rs2, qg, kg,
                                     tm=tm, lp=lp)

        mu = tm_mu[i]
        mu_p = jnp.concatenate([_pad_heads(mu[0:s1], 0), _pad_heads(mu[s1:2 * s1], 0),
                                _pad_heads(mu[2 * s1:3 * s1], 0), mu[3 * s1:]])[None]
        vecs = jnp.stack([_pad_heads(tm_w0[i], 0), _pad_heads(tm_a0[i], 0),
                          _pad_heads(tm_k_k[i], 0), _pad_heads(tm_k_a[i], 0),
                          _pad_heads(tm_r_k[i].reshape(s1), 0), _pad_heads(tm_gn_g[i], 0),
                          _pad_heads(tm_gn_b[i], 0), jnp.zeros((RWKV_PAD,), F32)])
        zl = lambda n: jnp.zeros((n, RWKV_PAD), F32)
        wd_p = jnp.concatenate([_pad_heads(tm_w_decay_up[i], 1), zl(AAA_LORA + GATE_LORA)], axis=0)
        wa_p = jnp.concatenate([zl(DECAY_LORA), _pad_heads(tm_w_a_up[i], 1), zl(GATE_LORA)], axis=0)
        wg_p = jnp.concatenate([zl(DECAY_LORA + AAA_LORA), _pad_heads(tm_w_g_up[i], 1)], axis=0)
        oa = _rwkv(pa.reshape(b, lp, PA_COLS), mu_p, vecs, wd_p, wa_p, wg_p, tb=tm)

        lamv = jnp.zeros((SUBLANES, LANES), F32).at[0:4, :DIFF_QK_DIM].set(
            jnp.stack([da_lambda_q1[i], da_lambda_k1[i], da_lambda_q2[i], da_lambda_k2[i]]))
        ob = _attention(qs.reshape(2, b, lp, DIFF_WIDTH), kh.reshape(b, lp, DIFF_WIDTH),
                        vt.reshape(b, lp // tm, DIFF_HEADS, VT_ROWS, tm), lamv, da_subln_g[i][None],
                        tq=tm, lam_init=lam_init)

        wo = w_out[i]
        cw = jnp.zeros((SUBLANES, CONV_WIDTH), F32).at[:CONV_K].set(sc_conv_w[i])
        h = _outproj(h, oa.reshape(r, RWKV_PAD), ob.reshape(r, DIFF_WIDTH), pc, cw,
                     _pad_heads(wo[:s1], 0).astype(BF16), wo[s1:s1 + DIFF_WIDTH].astype(BF16),
                     wo[s1 + DIFF_WIDTH:].astype(BF16), tm=tm, lp=lp)

        j = i // 2
        if i % 2 == 0:
            h = _dense_ffn(h, ffn_norm_g[i][None], ffn_w_gate[j].astype(BF16),
                           ffn_w_up[j].astype(BF16), ffn_w_down[j].astype(BF16), tm=tm)
        else:
            rw = jnp.zeros((d, LANES), F32).at[:, :N_EXPERTS].set(router_w[j])
            hn, route = _router(h, ffn_norm_g[i][None], rw, tm=tm)
            block_expert, row_src, row_dst = _routing_tables(route)
            y = _moe_experts(block_expert, row_src, row_dst, hn, moe_w_gate[j].astype(BF16),
                             moe_w_up[j].astype(BF16), moe_w_down[j].astype(BF16))
            h = _moe_combine(h, route, y, tm=tm)

    return h.reshape(b, lp, d)[:, N_META:l]
```

```python
import functools
import math

import jax
import jax.numpy as jnp
from jax import lax
from jax.experimental import pallas as pl
from jax.experimental.pallas import tpu as pltpu

N_META = 16
RWKV_HEAD_DIM = 64
RWKV_HEADS = 4
RWKV_WIDTH = RWKV_HEADS * RWKV_HEAD_DIM
DECAY_LORA = 32
AAA_LORA = 32
GATE_LORA = 64
LORA_COLS = DECAY_LORA + AAA_LORA + GATE_LORA
DIFF_QK_DIM = 64
DIFF_V_DIM = 128
DIFF_HEADS = 4
DIFF_WIDTH = DIFF_HEADS * DIFF_V_DIM
CONV_WIDTH = 256
CONV_K = 3
ROPE_THETA = 500000.0
ROPE_DIM = DIFF_QK_DIM // 4
N_EXPERTS = 8
TOP_K = 2
MOE_BLOCK = 256
NORM_EPS = 1e-6
RWKV_GN_EPS = 64e-5
SUBLN_EPS = 1e-5
NEG_INF = -1e30

LANES = 128
SUBLANES = 8
MXU_DIM = 256
VMEM_LIMIT_BYTES = 56 * 1024 * 1024

HEAD_PAD = LANES
RWKV_PAD = RWKV_HEADS * HEAD_PAD
PA_COLS = 3 * RWKV_PAD + LORA_COLS
PB_COLS = 3 * DIFF_WIDTH
PC_COLS = 3 * CONV_WIDTH
VT_ROWS = DIFF_V_DIM + 16
LOG2_E = math.log2(math.e)
CHUNK = 64

F32 = jnp.float32
BF16 = jnp.bfloat16
HIGHEST = lax.Precision.HIGHEST


def _dot(a, b):
    return jnp.dot(a.astype(BF16), b.astype(BF16), preferred_element_type=F32)


def _dot_nt(a, b):
    return lax.dot_general(a.astype(BF16), b.astype(BF16), (((1,), (1,)), ((), ())),
                           preferred_element_type=F32)


def _dot_f32(a, b):
    return jnp.dot(a, b, preferred_element_type=F32, precision=HIGHEST)


def _row_tile(lp):
    for t in (384, 256, 128):
        if lp % t == 0:
            return t
    raise ValueError(f"padded length {lp} is not a multiple of {LANES}")


def _params(sem):
    return pltpu.CompilerParams(dimension_semantics=sem, vmem_limit_bytes=VMEM_LIMIT_BYTES)


def _inproj_kernel(h_ref, g_ref, w_ref, rc_ref, rs1_ref, rs2_ref, qg_ref, kg_ref,
                   pa_ref, qs_ref, kh_ref, vt_ref, pc_ref):
    tm = h_ref.shape[0]
    x = h_ref[...]
    ms = jnp.mean(x * x, axis=-1, keepdims=True)
    xn = (x * lax.rsqrt(ms + NORM_EPS) * g_ref[...]).astype(BF16)
    pa_ref[...] = jnp.dot(xn, w_ref[:, :PA_COLS], preferred_element_type=F32)
    pc_ref[...] = jnp.dot(xn, w_ref[:, PA_COLS + PB_COLS:], preferred_element_type=F32)
    pb = jnp.dot(xn, w_ref[:, PA_COLS:PA_COLS + PB_COLS], preferred_element_type=F32)

    lane = lax.broadcasted_iota(jnp.int32, (tm, LANES), 1)
    lo = lane < DIFF_QK_DIM
    rc, rs1, rs2 = rc_ref[...], rs1_ref[...], rs2_ref[...]

    def norm_rope(t, g):
        t2 = t * t
        ss_lo = jnp.sum(jnp.where(lo, t2, 0.0), axis=-1, keepdims=True)
        ss_hi = jnp.sum(jnp.where(lo, 0.0, t2), axis=-1, keepdims=True)
        inv = jnp.where(lo, lax.rsqrt(ss_lo / DIFF_QK_DIM + NORM_EPS),
                        lax.rsqrt(ss_hi / DIFF_QK_DIM + NORM_EPS))
        tn = t * inv * g
        half = ROPE_DIM // 2
        return tn * rc + pltpu.roll(tn, LANES - half, 1) * rs1 + pltpu.roll(tn, half, 1) * rs2

    for hd in range(DIFF_HEADS):
        cs = slice(hd * LANES, (hd + 1) * LANES)
        q = norm_rope(pb[:, cs], qg_ref[...]) * (DIFF_QK_DIM ** -0.5 * LOG2_E)
        qs_ref[0, :, cs] = jnp.where(lo, q, 0.0).astype(BF16)
        qs_ref[1, :, cs] = jnp.where(lo, 0.0, q).astype(BF16)
        kcs = slice(DIFF_WIDTH + hd * LANES, DIFF_WIDTH + (hd + 1) * LANES)
        kh_ref[:, cs] = norm_rope(pb[:, kcs], kg_ref[...]).astype(BF16)
        vcs = slice(2 * DIFF_WIDTH + hd * LANES, 2 * DIFF_WIDTH + (hd + 1) * LANES)
        vt_ref[0, hd, :DIFF_V_DIM] = pb[:, vcs].T.astype(BF16)
        vt_ref[0, hd, DIFF_V_DIM:] = jnp.ones((VT_ROWS - DIFF_V_DIM, tm), BF16)


def _inproj(h, g, w, rc, rs1, rs2, qg, kg, *, tm, lp):
    r, d = h.shape
    nt_b = lp // tm
    row = lambda i: (i, 0)
    const = lambda i: (0, 0)
    rope = lambda i: (i % nt_b, 0)
    return pl.pallas_call(
        _inproj_kernel,
        grid=(r // tm,),
        in_specs=[
            pl.BlockSpec((tm, d), row),
            pl.BlockSpec((1, d), const),
            pl.BlockSpec(w.shape, const),
            pl.BlockSpec((tm, LANES), rope),
            pl.BlockSpec((tm, LANES), rope),
            pl.BlockSpec((tm, LANES), rope),
            pl.BlockSpec((1, LANES), const),
            pl.BlockSpec((1, LANES), const),
        ],
        out_specs=[
            pl.BlockSpec((tm, PA_COLS), row),
            pl.BlockSpec((2, tm, DIFF_WIDTH), lambda i: (0, i, 0)),
            pl.BlockSpec((tm, DIFF_WIDTH), row),
            pl.BlockSpec((1, DIFF_HEADS, VT_ROWS, tm), lambda i: (i, 0, 0, 0)),
            pl.BlockSpec((tm, PC_COLS), row),
        ],
        out_shape=[
            jax.ShapeDtypeStruct((r, PA_COLS), F32),
            jax.ShapeDtypeStruct((2, r, DIFF_WIDTH), BF16),
            jax.ShapeDtypeStruct((r, DIFF_WIDTH), BF16),
            jax.ShapeDtypeStruct((r // tm, DIFF_HEADS, VT_ROWS, tm), BF16),
            jax.ShapeDtypeStruct((r, PC_COLS), F32),
        ],
        compiler_params=_params(("arbitrary",)),
        name="inproj",
    )(h, g, w, rc, rs1, rs2, qg, kg)


def _split3(x):
    hi = x.astype(BF16)
    r1 = x - hi.astype(F32)
    mid = r1.astype(BF16)
    lo = (r1 - mid.astype(F32)).astype(BF16)
    return hi, mid, lo


def _dot_hi(a, b):
    ah = a.astype(BF16)
    al = (a - ah.astype(F32)).astype(BF16)
    bh = b.astype(BF16)
    bl = (b - bh.astype(F32)).astype(BF16)
    d = lambda p, q: jnp.dot(p, q, preferred_element_type=F32)
    return d(ah, bh) + d(ah, bl) + d(al, bh)


def _bmm(a, b):
    return jnp.einsum('cik,ckj->cij', a.astype(BF16), b.astype(BF16), preferred_element_type=F32)


def _bmm_nt(a, b):
    return jnp.einsum('cik,cjk->cij', a.astype(BF16), b.astype(BF16), preferred_element_type=F32)


def _rwkv_kernel(pa_ref, halo_ref, mu_ref, vec_ref, wd_ref, wa_ref, wg_ref, oa_ref, hstate_s):
    tb = pa_ref.shape[0]
    nc = tb // CHUNK
    i = pl.program_id(1)

    @pl.when(i == 0)
    def _():
        hstate_s[...] = jnp.zeros_like(hstate_s)

    x = pa_ref[...]
    prev_row = jnp.where(i == 0, 0.0, halo_ref[SUBLANES - 1:SUBLANES, :])
    row = lax.broadcasted_iota(jnp.int32, (tb, 1), 0)
    prev = jnp.where(row == 0, prev_row, pltpu.roll(x, 1, 0))
    hx = x + (prev - x) * mu_ref[...]
    r = hx[:, 0:RWKV_PAD]
    k = hx[:, RWKV_PAD:2 * RWKV_PAD]
    v = hx[:, 2 * RWKV_PAD:3 * RWKV_PAD]
    lora = hx[:, 3 * RWKV_PAD:]
    w0, a0, k_k, k_a = vec_ref[0:1, :], vec_ref[1:2, :], vec_ref[2:3, :], vec_ref[3:4, :]
    r_k, gn_g, gn_b = vec_ref[4:5, :], vec_ref[5:6, :], vec_ref[6:7, :]

    w = -jax.nn.softplus(-(w0 + _dot_hi(jnp.tanh(lora), wd_ref[...]))) - 0.5
    wl = -jnp.exp(w)
    alr = jax.nn.sigmoid(a0 + _dot(lora, wa_ref[...]))
    gate = _dot(jax.nn.sigmoid(lora), wg_ref[...])
    kk = k * k_k
    k2 = k * (1.0 + (alr - 1.0) * k_a)
    rkr = r * k2 * r_k

    ri = lax.broadcasted_iota(jnp.int32, (CHUNK, CHUNK), 0)
    ci = lax.broadcasted_iota(jnp.int32, (CHUNK, CHUNK), 1)
    lane_c = lax.broadcasted_iota(jnp.int32, (CHUNK, HEAD_PAD), 1)
    row_c = lax.broadcasted_iota(jnp.int32, (CHUNK, HEAD_PAD), 0)
    col_c = jnp.where(lane_c >= CHUNK, lane_c - CHUNK, lane_c)
    low_incl2 = (col_c <= row_c)[None]
    low_strict2 = (col_c < row_c)[None]
    tril_b = jnp.where(ci <= ri, 1.0, 0.0).astype(BF16)
    eye_k = (lax.broadcasted_iota(jnp.int32, (HEAD_PAD, HEAD_PAD), 0)
             == lax.broadcasted_iota(jnp.int32, (HEAD_PAD, HEAD_PAD), 1))[None]
    n_sq = int(math.log2(CHUNK))

    parts = _split3(wl)
    cums = []
    for c in range(nc):
        rows = slice(c * CHUNK, (c + 1) * CHUNK)
        cums.append(sum(jnp.dot(tril_b, p[rows], preferred_element_type=F32) for p in parts))
    cum_all = jnp.concatenate(cums, axis=0)

    lane = lax.broadcasted_iota(jnp.int32, (tb, HEAD_PAD), 1)
    real = lane < RWKV_HEAD_DIM
    to3 = lambda t: t.reshape(nc, CHUNK, HEAD_PAD)
    heads = range(RWKV_HEADS)
    hsl = [slice(hd * HEAD_PAD, (hd + 1) * HEAD_PAD) for hd in heads]
    kkn, bonus = [], []
    for cs in hsl:
        kk_h = kk[:, cs]
        nrm = jnp.sqrt(jnp.sum(kk_h * kk_h, axis=-1, keepdims=True))
        kkn.append(kk_h / jnp.maximum(nrm, 1e-12))
        bonus.append(jnp.sum(rkr[:, cs], axis=-1, keepdims=True) * v[:, cs])

    vc = [to3(v[:, cs]) for cs in hsl]
    zero_v = jnp.zeros_like(vc[0])
    ah, rh, last, g, bk_last = [], [], [], [], []
    for hd, cs in enumerate(hsl):
        wl3, cum = to3(wl[:, cs]), to3(cum_all[:, cs])
        rr, kc = to3(r[:, cs]), to3(k2[:, cs])
        ac, bc = to3(-kkn[hd]), to3(kkn[hd] * alr[:, cs])
        last.append(cum[:, CHUNK - 1:CHUNK, :])
        e_neg = jnp.exp(-cum)
        e_last = jnp.exp(last[hd] - cum)
        ah.append(ac * jnp.exp(cum - wl3))
        rh.append(rr * jnp.exp(cum))
        bk_last.append(jnp.concatenate([bc * e_last, kc * e_last], axis=1))
        g.append(_bmm_nt(jnp.concatenate([ah[hd], rh[hd]], axis=1),
                         jnp.concatenate([bc * e_neg, kc * e_neg], axis=1)))
    a_abk = [jnp.where(low_strict2, g[hd][:, :CHUNK], 0.0) for hd in heads]
    r_bk = [jnp.where(low_incl2, g[hd][:, CHUNK:], 0.0) for hd in heads]
    xx = [jnp.concatenate([_bmm(a_abk[hd], jnp.concatenate([zero_v, vc[hd]], axis=1)), ah[hd]],
                          axis=2) for hd in heads]
    ppad = [jnp.where(lane_c < CHUNK, a_abk[hd], 0.0) for hd in heads]
    for j in range(n_sq):
        for hd in heads:
            pj = ppad[hd][:, :, :CHUNK]
            if j + 1 < n_sq:
                res = _bmm(pj, jnp.concatenate([xx[hd], ppad[hd]], axis=2))
                xx[hd] = xx[hd] + res[:, :, :2 * HEAD_PAD]
                ppad[hd] = res[:, :, 2 * HEAD_PAD:]
            else:
                xx[hd] = xx[hd] + _bmm(pj, xx[hd])
    qm, y0, n0 = [], [], []
    for hd in heads:
        lhs = jnp.concatenate([r_bk[hd], jnp.swapaxes(bk_last[hd], 1, 2)], axis=1)
        rhs = jnp.concatenate([xx[hd], jnp.concatenate([vc[hd], zero_v], axis=2)], axis=1)
        out = _bmm(lhs, rhs)
        y0.append(out[:, :CHUNK, :HEAD_PAD])
        n0.append(out[:, CHUNK:, :HEAD_PAD])
        q = rh[hd] + out[:, :CHUNK, HEAD_PAD:]
        m = jnp.where(eye_k, jnp.exp(last[hd]), 0.0) + out[:, CHUNK:, HEAD_PAD:]
        qm.append(jnp.concatenate([q, m], axis=1).astype(BF16))

    hs = [hstate_s[hd] for hd in heads]
    ys = [[] for _ in heads]
    for c in range(nc):
        for hd in heads:
            res = jnp.dot(qm[hd][c], hs[hd].astype(BF16), preferred_element_type=F32)
            ys[hd].append(res[:CHUNK] + y0[hd][c])
            hs[hd] = res[CHUNK:] + n0[hd][c]

    for hd, cs in enumerate(hsl):
        hstate_s[hd] = hs[hd]
        y = jnp.concatenate(ys[hd], axis=0)
        mean = jnp.sum(y, axis=-1, keepdims=True) / RWKV_HEAD_DIM
        dlt = jnp.where(real, y - mean, 0.0)
        var = jnp.sum(dlt * dlt, axis=-1, keepdims=True) / RWKV_HEAD_DIM
        yn = dlt * lax.rsqrt(var + RWKV_GN_EPS) * gn_g[:, cs] + gn_b[:, cs]
        oa_ref[:, cs] = ((yn + bonus[hd]) * gate[:, cs]).astype(BF16)


def _rwkv(pa, mu, vecs, wd, wa, wg, *, tb):
    b, lp, _ = pa.shape
    tile = lambda bi, i: (bi, i, 0)
    const = lambda bi, i: (0, 0)
    halo = lambda bi, i: (bi, jnp.maximum(i * (tb // SUBLANES) - 1, 0), 0)
    return pl.pallas_call(
        _rwkv_kernel,
        grid=(b, lp // tb),
        in_specs=[
            pl.BlockSpec((None, tb, PA_COLS), tile),
            pl.BlockSpec((None, SUBLANES, PA_COLS), halo),
            pl.BlockSpec((1, PA_COLS), const),
            pl.BlockSpec((SUBLANES, RWKV_PAD), const),
            pl.BlockSpec((LORA_COLS, RWKV_PAD), const),
            pl.BlockSpec((LORA_COLS, RWKV_PAD), const),
            pl.BlockSpec((LORA_COLS, RWKV_PAD), const),
        ],
        out_specs=pl.BlockSpec((None, tb, RWKV_PAD), tile),
        out_shape=jax.ShapeDtypeStruct((b, lp, RWKV_PAD), BF16),
        scratch_shapes=[pltpu.VMEM((RWKV_HEADS, HEAD_PAD, HEAD_PAD), F32)],
        compiler_params=_params(("arbitrary", "arbitrary")),
        name="rwkv7",
    )(pa, pa, mu, vecs, wd, wa, wg)


ATTN_HEADS_PER_STEP = 4


def _attn_kernel(q_ref, k_ref, vt_ref, lamv_ref, sg_ref, o_ref, acc_s, st_s, p_s, *, lam_init):
    tq = o_ref.shape[0]
    tk = vt_ref.shape[3]
    qi = pl.program_id(2)
    heads = range(ATTN_HEADS_PER_STEP)
    hl = [slice(h * LANES, (h + 1) * LANES) for h in heads]

    def scores(h, ki):
        k = k_ref[pl.ds(pl.multiple_of(ki * tk, tk), tk), hl[h]]
        return lax.dot_general(k, q_ref[:, :, hl[h]].reshape(2 * tq, LANES),
                               (((1,), (1,)), ((), ())), preferred_element_type=F32)

    def values(h, ki, p, alpha):
        acc_s[h] = alpha * acc_s[h] + jnp.dot(vt_ref[ki, h], p, preferred_element_type=F32)

    st = [scores(h, qi) for h in heads]

    @pl.when(qi > 0)
    def _():
        for h in heads:
            st_s[h, 0] = scores(h, 0)

    kr = lax.broadcasted_iota(jnp.int32, (tk, 2 * tq), 0)
    qc = lax.broadcasted_iota(jnp.int32, (tk, 2 * tq), 1)
    qc = jnp.where(qc >= tq, qc - tq, qc)
    m0 = []
    for h in heads:
        s_h = jnp.where(kr <= qc, st[h], NEG_INF)
        m0.append(jnp.max(s_h, axis=0, keepdims=True))
        p_s[h, 1] = jnp.exp2(s_h - m0[h]).astype(BF16)
        acc_s[h] = jnp.zeros(acc_s.shape[1:], acc_s.dtype)

    def trip(k, carry, slot):
        out = []
        nxt = jnp.minimum(k + 1, qi - 1)
        prv = jnp.where(k == 0, qi, k - 1)
        for h in heads:
            st_s[h, 1 - slot] = scores(h, nxt)
        for h in heads:
            m_prev, alpha_prev = carry[h]
            s_k = st_s[h, slot]
            m_new = jnp.maximum(m_prev, jnp.max(s_k, axis=0, keepdims=True))
            alpha = jnp.exp2(m_prev - m_new)
            values(h, prv, p_s[h, 1 - slot], alpha_prev)
            p_s[h, slot] = jnp.exp2(s_k - m_new).astype(BF16)
            out.append((m_new, alpha))
        return tuple(out)

    def two_trips(j, c):
        c = trip(2 * j, c, 0)
        return lax.cond(2 * j + 1 < qi, lambda cc: trip(2 * j + 1, cc, 1), lambda cc: cc, c)

    init = tuple((m0[h], jnp.ones_like(m0[h])) for h in heads)
    carry = lax.fori_loop(0, (qi + 1) // 2, two_trips, init)
    odd = qi % 2 == 1

    @pl.when(odd)
    def _():
        for h in heads:
            values(h, qi - 1, p_s[h, 0], carry[h][1])

    @pl.when(jnp.logical_not(odd))
    def _():
        for h in heads:
            values(h, jnp.where(qi == 0, 0, qi - 1), p_s[h, 1], carry[h][1])

    lv = lamv_ref[...]
    lam = (jnp.exp(jnp.sum(lv[0:1] * lv[1:2], axis=-1, keepdims=True))
           - jnp.exp(jnp.sum(lv[2:3] * lv[3:4], axis=-1, keepdims=True)) + lam_init)
    for h in heads:
        acc = acc_s[h]
        ot = acc[:DIFF_V_DIM] / acc[DIFF_V_DIM:DIFF_V_DIM + 1]
        od = (ot[:, :tq] - lam * ot[:, tq:]).T
        ms = jnp.mean(od * od, axis=-1, keepdims=True)
        o_ref[:, hl[h]] = (od * lax.rsqrt(ms + SUBLN_EPS) * sg_ref[...]
                           * (1.0 - lam_init)).astype(BF16)


def _attention(qs, kh, vt, lamv, sg, *, tq, lam_init):
    _, b, lp, _ = qs.shape
    nq = lp // tq
    hp = ATTN_HEADS_PER_STEP
    const = lambda bi, hg, qi: (0, 0)
    return pl.pallas_call(
        functools.partial(_attn_kernel, lam_init=lam_init),
        grid=(b, DIFF_HEADS // hp, nq),
        in_specs=[
            pl.BlockSpec((2, None, tq, hp * LANES), lambda bi, hg, qi: (0, bi, qi, hg)),
            pl.BlockSpec((None, lp, hp * LANES), lambda bi, hg, qi: (bi, 0, hg)),
            pl.BlockSpec((None, nq, hp, VT_ROWS, tq), lambda bi, hg, qi: (bi, 0, hg, 0, 0)),
            pl.BlockSpec((SUBLANES, LANES), const),
            pl.BlockSpec((1, LANES), const),
        ],
        out_specs=pl.BlockSpec((None, tq, hp * LANES), lambda bi, hg, qi: (bi, qi, hg)),
        out_shape=jax.ShapeDtypeStruct((b, lp, DIFF_WIDTH), BF16),
        scratch_shapes=[pltpu.VMEM((hp, VT_ROWS, 2 * tq), F32),
                        pltpu.VMEM((hp, 2, tq, 2 * tq), F32),
                        pltpu.VMEM((hp, 2, tq, 2 * tq), BF16)],
        compiler_params=_params(("arbitrary", "arbitrary", "arbitrary")),
        name="diffattn",
    )(qs, kh, vt, lamv, sg)


def _outproj_kernel(h_ref, oa_ref, ob_ref, pc_ref, pch_ref, cw_ref, wa_ref, wb_ref, wc_ref,
                    o_ref, *, tiles_per_batch):
    tm = h_ref.shape[0]
    first = (pl.program_id(0) % tiles_per_batch) == 0
    pc = pc_ref[...]
    bg = pc[:, :CONV_WIDTH]
    z = pc[:, CONV_WIDTH:2 * CONV_WIDTH] * pc[:, 2 * CONV_WIDTH:]
    ph = pch_ref[...]
    zh = jnp.where(first, 0.0, ph[:, CONV_WIDTH:2 * CONV_WIDTH] * ph[:, 2 * CONV_WIDTH:])
    row = lax.broadcasted_iota(jnp.int32, (tm, 1), 0)
    z1 = jnp.where(row == 0, zh[SUBLANES - 1:SUBLANES], pltpu.roll(z, 1, 0))
    z2 = jnp.where(row == 0, zh[SUBLANES - 2:SUBLANES - 1],
                   jnp.where(row == 1, zh[SUBLANES - 1:SUBLANES], pltpu.roll(z, 2, 0)))
    cw = cw_ref[...]
    y = cw[0:1] * z2 + cw[1:2] * z1 + cw[2:3] * z
    oc = (bg * y).astype(BF16)
    o_ref[...] = (h_ref[...]
                  + jnp.dot(oa_ref[...], wa_ref[...], preferred_element_type=F32)
                  + jnp.dot(ob_ref[...], wb_ref[...], preferred_element_type=F32)
                  + jnp.dot(oc, wc_ref[...], preferred_element_type=F32))


def _outproj(h, oa, ob, pc, cw, wa, wb, wc, *, tm, lp):
    r, d = h.shape
    row = lambda i: (i, 0)
    const = lambda i: (0, 0)
    halo = lambda i: (jnp.maximum(i * (tm // SUBLANES) - 1, 0), 0)
    return pl.pallas_call(
        functools.partial(_outproj_kernel, tiles_per_batch=lp // tm),
        grid=(r // tm,),
        in_specs=[
            pl.BlockSpec((tm, d), row),
            pl.BlockSpec((tm, RWKV_PAD), row),
            pl.BlockSpec((tm, DIFF_WIDTH), row),
            pl.BlockSpec((tm, PC_COLS), row),
            pl.BlockSpec((SUBLANES, PC_COLS), halo),
            pl.BlockSpec((SUBLANES, CONV_WIDTH), const),
            pl.BlockSpec(wa.shape, const),
            pl.BlockSpec(wb.shape, const),
            pl.BlockSpec(wc.shape, const),
        ],
        out_specs=pl.BlockSpec((tm, d), row),
        out_shape=jax.ShapeDtypeStruct((r, d), F32),
        compiler_params=_params(("arbitrary",)),
        name="outproj",
    )(h, oa, ob, pc, pc, cw, wa, wb, wc)


def _swiglu_block(xb, wg_ref, wu_ref, wd_ref, side_work=None, split_pred=None):
    d_ff = wg_ref.shape[1]
    n = d_ff // MXU_DIM

    def one_slice(c, acc):
        cs = slice(c * MXU_DIM, (c + 1) * MXU_DIM)
        gt = jnp.dot(xb, wg_ref[:, cs], preferred_element_type=F32)
        up = jnp.dot(xb, wu_ref[:, cs], preferred_element_type=F32)
        act = (jax.nn.silu(gt) * up).astype(BF16)
        acc = acc + jnp.dot(act, wd_ref[cs, :], preferred_element_type=F32)
        if side_work is not None:
            side_work(c)
        return acc

    acc = jnp.zeros((xb.shape[0], wd_ref.shape[1]), F32)
    if split_pred is None:
        for c in range(n):
            acc = one_slice(c, acc)
        return acc
    for c in range(n // 2):
        acc = one_slice(c, acc)

    def second_half(a):
        for c in range(n // 2, n):
            a = one_slice(c, a)
        return a

    return lax.cond(split_pred, second_half, lambda a: a, acc)


def _rms(x, g):
    ms = jnp.mean(x * x, axis=-1, keepdims=True)
    return x * lax.rsqrt(ms + NORM_EPS) * g


def _ffn_kernel(h_ref, g_ref, wg_ref, wu_ref, wd_ref, o_ref):
    x = h_ref[...]
    xn = _rms(x, g_ref[...]).astype(BF16)
    o_ref[...] = x + _swiglu_block(xn, wg_ref, wu_ref, wd_ref)


def _dense_ffn(h, g, wg, wu, wd, *, tm):
    r, d = h.shape
    row = lambda i: (i, 0)
    const = lambda i: (0, 0)
    return pl.pallas_call(
        _ffn_kernel,
        grid=(r // tm,),
        in_specs=[
            pl.BlockSpec((tm, d), row),
            pl.BlockSpec((1, d), const),
            pl.BlockSpec(wg.shape, const),
            pl.BlockSpec(wu.shape, const),
            pl.BlockSpec(wd.shape, const),
        ],
        out_specs=pl.BlockSpec((tm, d), row),
        out_shape=jax.ShapeDtypeStruct((r, d), F32),
        compiler_params=_params(("arbitrary",)),
        name="dense_ffn",
    )(h, g, wg, wu, wd)


def _router_kernel(h_ref, g_ref, rw_ref, hn_ref, route_ref):
    tm = h_ref.shape[0]
    xn = _rms(h_ref[...], g_ref[...])
    hn_ref[...] = xn
    logits = _dot_f32(xn, rw_ref[...])
    lane = lax.broadcasted_iota(jnp.int32, (tm, LANES), 1)
    logits = jnp.where(lane < N_EXPERTS, logits, -jnp.inf)
    m1 = jnp.max(logits, axis=-1, keepdims=True)
    i1 = jnp.min(jnp.where(logits == m1, lane, LANES), axis=-1, keepdims=True)
    rest = jnp.where(lane == i1, -jnp.inf, logits)
    m2 = jnp.max(rest, axis=-1, keepdims=True)
    i2 = jnp.min(jnp.where(rest == m2, lane, LANES), axis=-1, keepdims=True)
    e2 = jnp.exp(m2 - m1)
    den = 1.0 + e2
    route_ref[...] = jnp.where(lane == 0, i1.astype(F32),
                     jnp.where(lane == 1, i2.astype(F32),
                     jnp.where(lane == 2, 1.0 / den,
                     jnp.where(lane == 3, e2 / den, 0.0))))


def _router(h, g, rw, *, tm):
    r, d = h.shape
    row = lambda i: (i, 0)
    const = lambda i: (0, 0)
    return pl.pallas_call(
        _router_kernel,
        grid=(r // tm,),
        in_specs=[pl.BlockSpec((tm, d), row), pl.BlockSpec((1, d), const),
                  pl.BlockSpec((d, LANES), const)],
        out_specs=[pl.BlockSpec((tm, d), row), pl.BlockSpec((tm, LANES), row)],
        out_shape=[jax.ShapeDtypeStruct((r, d), F32), jax.ShapeDtypeStruct((r, LANES), F32)],
        compiler_params=_params(("arbitrary",)),
        name="router",
    )(h, g, rw)


def _row_copy(src, dst, sem, src_row, dst_row):
    return pltpu.make_async_copy(src.at[pl.ds(src_row, 1), :], dst.at[pl.ds(dst_row, 1), :], sem)


DMA_UNROLL = 8
DMA_ISSUE_SLICES = 5


def _moe_kernel(bexp_ref, rsrc_ref, rdst_ref, hn_hbm, wg_ref, wu_ref, wd_ref, y_hbm,
                xbuf, ybuf, gsem, ssem):
    del bexp_ref
    blk = xbuf.shape[1]
    i = pl.program_id(0)
    nb = pl.num_programs(0)
    slot = i % 2
    other = 1 - slot
    dump0 = y_hbm.shape[0] - blk

    def for_rows(fn):
        def body(j, c):
            for u in range(DMA_UNROLL):
                fn(j * DMA_UNROLL + u)
            return c
        lax.fori_loop(0, blk // DMA_UNROLL, body, 0)

    def gather(block, s, j):
        return _row_copy(hn_hbm, xbuf.at[s], gsem.at[s], rsrc_ref[block * blk + j], j)

    def scatter(dst_row, s, j):
        return _row_copy(ybuf.at[s], y_hbm, ssem.at[s], j, dst_row)

    @pl.when(i == 0)
    def _():
        for_rows(lambda j: gather(0, 0, j).start())
        ybuf[1] = jnp.zeros(ybuf.shape[1:], ybuf.dtype)

    for_rows(lambda j: gather(0, slot, j).wait())

    @pl.when(i >= 1)
    def _():
        for_rows(lambda j: scatter(0, slot, j).wait())

    nxt = jnp.minimum(i + 1, nb - 1)
    prev = jnp.maximum(i - 1, 0)
    per = -(-blk // DMA_ISSUE_SLICES)

    def side_work(c):
        for j in range(c * per, min((c + 1) * per, blk)):
            gather(nxt, other, j).start(priority=j % 2)
            dst = jnp.where(i == 0, dump0 + j, rdst_ref[prev * blk + j])
            scatter(dst, other, j).start(priority=(j + 1) % 2)

    ybuf[slot] = _swiglu_block(xbuf[slot].astype(BF16), wg_ref, wu_ref, wd_ref, side_work,
                               split_pred=rsrc_ref[0] >= 0)

    @pl.when(i == nb - 1)
    def _():
        for_rows(lambda j: gather(0, other, j).wait())
        for_rows(lambda j: scatter(0, other, j).wait())
        for_rows(lambda j: scatter(rdst_ref[i * blk + j], slot, j).start())
        for_rows(lambda j: scatter(0, slot, j).wait())


def _moe_experts(block_expert, row_src, row_dst, hn, wg, wu, wd):
    nb = block_expert.shape[0]
    r, d = hn.shape
    d_ff = wg.shape[2]
    wmap = lambda i, be, rs, rd: (be[i], 0, 0)
    grid_spec = pltpu.PrefetchScalarGridSpec(
        num_scalar_prefetch=3,
        grid=(nb,),
        in_specs=[
            pl.BlockSpec(memory_space=pl.ANY),
            pl.BlockSpec((None, d, d_ff), wmap),
            pl.BlockSpec((None, d, d_ff), wmap),
            pl.BlockSpec((None, d_ff, d), wmap),
        ],
        out_specs=pl.BlockSpec(memory_space=pl.ANY),
        scratch_shapes=[pltpu.VMEM((2, MOE_BLOCK, d), F32), pltpu.VMEM((2, MOE_BLOCK, d), F32),
                        pltpu.SemaphoreType.DMA((2,)), pltpu.SemaphoreType.DMA((2,))],
    )
    return pl.pallas_call(
        _moe_kernel,
        grid_spec=grid_spec,
        out_shape=jax.ShapeDtypeStruct((TOP_K * r + MOE_BLOCK, d), F32),
        compiler_params=_params(("arbitrary",)),
        name="moe_experts",
    )(block_expert, row_src, row_dst, hn, wg, wu, wd)


def _combine_kernel(h_ref, route_ref, y0_ref, y1_ref, o_ref):
    route = route_ref[...]
    o_ref[...] = h_ref[...] + route[:, 2:3] * y0_ref[...] + route[:, 3:4] * y1_ref[...]


def _moe_combine(h, route, y, *, tm):
    r, d = h.shape
    nt = r // tm
    row = lambda i: (i, 0)
    return pl.pallas_call(
        _combine_kernel,
        grid=(nt,),
        in_specs=[
            pl.BlockSpec((tm, d), row),
            pl.BlockSpec((tm, LANES), row),
            pl.BlockSpec((tm, d), row),
            pl.BlockSpec((tm, d), lambda i: (i + nt, 0)),
        ],
        out_specs=pl.BlockSpec((tm, d), row),
        out_shape=jax.ShapeDtypeStruct((r, d), F32),
        compiler_params=_params(("arbitrary",)),
        name="moe_combine",
    )(h, route, y, y)


def _slot_owner_kernel(dest_ref, owner_ref):
    n_pairs = dest_ref.shape[0]
    n_slots = owner_ref.shape[0]

    def clear(i, c):
        for u in range(DMA_UNROLL):
            owner_ref[i * DMA_UNROLL + u] = -1
        return c

    lax.fori_loop(0, n_slots // DMA_UNROLL, clear, 0)

    def place(i, c):
        for u in range(DMA_UNROLL):
            p = i * DMA_UNROLL + u
            owner_ref[dest_ref[p]] = p
        return c

    lax.fori_loop(0, n_pairs // DMA_UNROLL, place, 0)


def _slot_owner(dest, n_slots):
    assert dest.shape[0] % DMA_UNROLL == 0 and n_slots % DMA_UNROLL == 0
    smem = pl.BlockSpec(memory_space=pltpu.SMEM)
    return pl.pallas_call(
        _slot_owner_kernel,
        in_specs=[smem],
        out_specs=smem,
        out_shape=jax.ShapeDtypeStruct((n_slots,), jnp.int32),
        name="slot_owner",
    )(dest)


def _routing_tables(route):
    r = route.shape[0]
    tk = r * TOP_K
    flat_e = route[:, :TOP_K].reshape(tk).astype(jnp.int32)
    onehot = (flat_e[:, None] == jnp.arange(N_EXPERTS, dtype=jnp.int32)[None, :]).astype(jnp.int32)
    csum = jnp.cumsum(onehot, axis=0)
    rank = jnp.sum(csum * onehot, axis=1) - 1
    counts = csum[-1]
    padded = (counts + MOE_BLOCK - 1) // MOE_BLOCK * MOE_BLOCK
    pend = jnp.cumsum(padded)
    pstart = pend - padded
    dest = pstart[flat_e] + rank
    nb = -(-tk // MOE_BLOCK) + N_EXPERTS
    n_slots = nb * MOE_BLOCK
    pair = _slot_owner(dest, n_slots)
    used = pair >= 0
    row_src = jnp.where(used, pair // TOP_K, 0)
    row_dst = jnp.where(used, (pair % TOP_K) * r + pair // TOP_K,
                        tk + jnp.arange(n_slots, dtype=jnp.int32) % MOE_BLOCK)
    block_start = jnp.arange(nb, dtype=jnp.int32) * MOE_BLOCK
    block_expert = jnp.minimum(jnp.searchsorted(pend, block_start, side='right'),
                               N_EXPERTS - 1).astype(jnp.int32)
    return block_expert, row_src, row_dst


def _pad_heads(t, axis):
    axis = axis % t.ndim
    shp = t.shape
    t = t.reshape(shp[:axis] + (RWKV_HEADS, RWKV_HEAD_DIM) + shp[axis + 1:])
    pad = [(0, 0)] * t.ndim
    pad[axis + 1] = (0, HEAD_PAD - RWKV_HEAD_DIM)
    t = jnp.pad(t, pad)
    return t.reshape(shp[:axis] + (RWKV_PAD,) + shp[axis + 1:])


def _rope_tables(lp):
    half = ROPE_DIM // 2
    inv_freq = ROPE_THETA ** (-jnp.arange(0, ROPE_DIM, 2, dtype=F32) / ROPE_DIM)
    ang = jnp.arange(lp, dtype=F32)[:, None] * inv_freq[None, :]
    cos, sin = jnp.cos(ang), jnp.sin(ang)
    ones = jnp.ones((lp, DIFF_QK_DIM - ROPE_DIM), F32)
    zeros = jnp.zeros((lp, DIFF_QK_DIM - ROPE_DIM), F32)
    zh = jnp.zeros((lp, half), F32)
    rc = jnp.concatenate([cos, cos, ones], axis=1)
    rs1 = jnp.concatenate([-sin, zh, zeros], axis=1)
    rs2 = jnp.concatenate([zh, sin, zeros], axis=1)
    tile2 = lambda a: jnp.concatenate([a, a], axis=1)
    return tile2(rc), tile2(rs1), tile2(rs2)


def kernel(x, meta_tokens, mix_norm_g, w_in, tm_mu, tm_w0, tm_w_decay_up, tm_a0, tm_w_a_up, tm_w_g_up, tm_k_k, tm_k_a, tm_r_k, tm_gn_g, tm_gn_b, da_q_norm_g, da_k_norm_g, da_lambda_q1, da_lambda_k1, da_lambda_q2, da_lambda_k2, da_subln_g, sc_conv_w, w_out, ffn_norm_g, ffn_w_gate, ffn_w_up, ffn_w_down, router_w, moe_w_gate, moe_w_up, moe_w_down):
    b, seq, d = x.shape
    depth = w_in.shape[0]
    l = N_META + seq
    lp = -(-l // LANES) * LANES
    tm = _row_tile(lp)
    r = b * lp
    s1 = RWKV_WIDTH

    meta = jnp.broadcast_to(meta_tokens.astype(x.dtype)[None], (b, N_META, d))
    h = jnp.concatenate([meta, x, jnp.zeros((b, lp - l, d), x.dtype)], axis=1).reshape(r, d)
    rc, rs1, rs2 = _rope_tables(lp)

    for i in range(depth):
        lam_init = 0.8 - 0.6 * math.exp(-0.3 * i)
        wi = w_in[i]
        w_cat = jnp.concatenate(
            [_pad_heads(wi[:, 0:s1], 1), _pad_heads(wi[:, s1:2 * s1], 1),
             _pad_heads(wi[:, 2 * s1:3 * s1], 1), wi[:, 3 * s1:]], axis=1).astype(BF16)
        qg = jnp.tile(da_q_norm_g[i], 2)[None]
        kg = jnp.tile(da_k_norm_g[i], 2)[None]
        pa, qs, kh, vt, pc = _inproj(h, mix_norm_g[i][None], w_cat, rc, rs1, rs2, qg, kg,
                                     tm=tm, lp=lp)

        mu = tm_mu[i]
        mu_p = jnp.concatenate([_pad_heads(mu[0:s1], 0), _pad_heads(mu[s1:2 * s1], 0),
                                _pad_heads(mu[2 * s1:3 * s1], 0), mu[3 * s1:]])[None]
        vecs = jnp.stack([_pad_heads(tm_w0[i], 0), _pad_heads(tm_a0[i], 0),
                          _pad_heads(tm_k_k[i], 0), _pad_heads(tm_k_a[i], 0),
                          _pad_heads(tm_r_k[i].reshape(s1), 0), _pad_heads(tm_gn_g[i], 0),
                          _pad_heads(tm_gn_b[i], 0), jnp.zeros((RWKV_PAD,), F32)])
        zl = lambda n: jnp.zeros((n, RWKV_PAD), F32)
        wd_p = jnp.concatenate([_pad_heads(tm_w_decay_up[i], 1), zl(AAA_LORA + GATE_LORA)], axis=0)
        wa_p = jnp.concatenate([zl(DECAY_LORA), _pad_heads(tm_w_a_up[i], 1), zl(GATE_LORA)], axis=0)
        wg_p = jnp.concatenate([zl(DECAY_LORA + AAA_LORA), _pad_heads(tm_w_g_up[i], 1)], axis=0)
        oa = _rwkv(pa.reshape(b, lp, PA_COLS), mu_p, vecs, wd_p, wa_p, wg_p, tb=tm)

        lamv = jnp.zeros((SUBLANES, LANES), F32).at[0:4, :DIFF_QK_DIM].set(
            jnp.stack([da_lambda_q1[i], da_lambda_k1[i], da_lambda_q2[i], da_lambda_k2[i]]))
        ob = _attention(qs.reshape(2, b, lp, DIFF_WIDTH), kh.reshape(b, lp, DIFF_WIDTH),
                        vt.reshape(b, lp // tm, DIFF_HEADS, VT_ROWS, tm), lamv, da_subln_g[i][None],
                        tq=tm, lam_init=lam_init)

        wo = w_out[i]
        cw = jnp.zeros((SUBLANES, CONV_WIDTH), F32).at[:CONV_K].set(sc_conv_w[i])
        h = _outproj(h, oa.reshape(r, RWKV_PAD), ob.reshape(r, DIFF_WIDTH), pc, cw,
                     _pad_heads(wo[:s1], 0).astype(BF16), wo[s1:s1 + DIFF_WIDTH].astype(BF16),
                     wo[s1 + DIFF_WIDTH:].astype(BF16), tm=tm, lp=lp)

        j = i // 2
        if i % 2 == 0:
            h = _dense_ffn(h, ffn_norm_g[i][None], ffn_w_gate[j].astype(BF16),
                           ffn_w_up[j].astype(BF16), ffn_w_down[j].astype(BF16), tm=tm)
        else:
            rw = jnp.zeros((d, LANES), F32).at[:, :N_EXPERTS].set(router_w[j])
            hn, route = _router(h, ffn_norm_g[i][None], rw, tm=tm)
            block_expert, row_src, row_dst = _routing_tables(route)
            y = _moe_experts(block_expert, row_src, row_dst, hn, moe_w_gate[j].astype(BF16),
                             moe_w_up[j].astype(BF16), moe_w_down[j].astype(BF16))
            h = _moe_combine(h, route, y, tm=tm)

    return h.reshape(b, lp, d)[:, N_META:l]
```

```python
import functools
import math

import jax
import jax.numpy as jnp
from jax import lax
from jax.experimental import pallas as pl
from jax.experimental.pallas import tpu as pltpu

N_META = 16
RWKV_HEAD_DIM = 64
RWKV_HEADS = 4
RWKV_WIDTH = RWKV_HEADS * RWKV_HEAD_DIM
DECAY_LORA = 32
AAA_LORA = 32
GATE_LORA = 64
LORA_COLS = DECAY_LORA + AAA_LORA + GATE_LORA
DIFF_QK_DIM = 64
DIFF_V_DIM = 128
DIFF_HEADS = 4
DIFF_WIDTH = DIFF_HEADS * DIFF_V_DIM
CONV_WIDTH = 256
CONV_K = 3
ROPE_THETA = 500000.0
ROPE_DIM = DIFF_QK_DIM // 4
N_EXPERTS = 8
TOP_K = 2
MOE_BLOCK = 256
NORM_EPS = 1e-6
RWKV_GN_EPS = 64e-5
SUBLN_EPS = 1e-5
NEG_INF = -1e30

LANES = 128
SUBLANES = 8
MXU_DIM = 256
VMEM_LIMIT_BYTES = 56 * 1024 * 1024

HEAD_PAD = LANES
RWKV_PAD = RWKV_HEADS * HEAD_PAD
PA_COLS = 3 * RWKV_PAD + LORA_COLS
PB_COLS = 3 * DIFF_WIDTH
PC_COLS = 3 * CONV_WIDTH
VT_ROWS = DIFF_V_DIM + 16
LOG2_E = math.log2(math.e)
CHUNK = 64

F32 = jnp.float32
BF16 = jnp.bfloat16
HIGHEST = lax.Precision.HIGHEST


def _dot(a, b):
    return jnp.dot(a.astype(BF16), b.astype(BF16), preferred_element_type=F32)


def _dot_nt(a, b):
    return lax.dot_general(a.astype(BF16), b.astype(BF16), (((1,), (1,)), ((), ())),
                           preferred_element_type=F32)


def _dot_f32(a, b):
    return jnp.dot(a, b, preferred_element_type=F32, precision=HIGHEST)


def _row_tile(lp):
    for t in (384, 256, 128):
        if lp % t == 0:
            return t
    raise ValueError(f"padded length {lp} is not a multiple of {LANES}")


def _params(sem):
    return pltpu.CompilerParams(dimension_semantics=sem, vmem_limit_bytes=VMEM_LIMIT_BYTES)


def _inproj_kernel(h_ref, g_ref, w_ref, rc_ref, rs1_ref, rs2_ref, qg_ref, kg_ref,
                   pa_ref, qs_ref, kh_ref, vt_ref, pc_ref):
    tm = h_ref.shape[0]
    x = h_ref[...]
    ms = jnp.mean(x * x, axis=-1, keepdims=True)
    xn = (x * lax.rsqrt(ms + NORM_EPS) * g_ref[...]).astype(BF16)
    pa_ref[...] = jnp.dot(xn, w_ref[:, :PA_COLS], preferred_element_type=F32)
    pc_ref[...] = jnp.dot(xn, w_ref[:, PA_COLS + PB_COLS:], preferred_element_type=F32)
    pb = jnp.dot(xn, w_ref[:, PA_COLS:PA_COLS + PB_COLS], preferred_element_type=F32)

    lane = lax.broadcasted_iota(jnp.int32, (tm, LANES), 1)
    lo = lane < DIFF_QK_DIM
    rc, rs1, rs2 = rc_ref[...], rs1_ref[...], rs2_ref[...]

    def norm_rope(t, g):
        t2 = t * t
        ss_lo = jnp.sum(jnp.where(lo, t2, 0.0), axis=-1, keepdims=True)
        ss_hi = jnp.sum(jnp.where(lo, 0.0, t2), axis=-1, keepdims=True)
        inv = jnp.where(lo, lax.rsqrt(ss_lo / DIFF_QK_DIM + NORM_EPS),
                        lax.rsqrt(ss_hi / DIFF_QK_DIM + NORM_EPS))
        tn = t * inv * g
        half = ROPE_DIM // 2
        return tn * rc + pltpu.roll(tn, LANES - half, 1) * rs1 + pltpu.roll(tn, half, 1) * rs2

    for hd in range(DIFF_HEADS):
        cs = slice(hd * LANES, (hd + 1) * LANES)
        q = norm_rope(pb[:, cs], qg_ref[...]) * (DIFF_QK_DIM ** -0.5 * LOG2_E)
        qs_ref[0, :, cs] = jnp.where(lo, q, 0.0).astype(BF16)
        qs_ref[1, :, cs] = jnp.where(lo, 0.0, q).astype(BF16)
        kcs = slice(DIFF_WIDTH + hd * LANES, DIFF_WIDTH + (hd + 1) * LANES)
        kh_ref[:, cs] = norm_rope(pb[:, kcs], kg_ref[...]).astype(BF16)
        vcs = slice(2 * DIFF_WIDTH + hd * LANES, 2 * DIFF_WIDTH + (hd + 1) * LANES)
        vt_ref[0, hd, :DIFF_V_DIM] = pb[:, vcs].T.astype(BF16)
        vt_ref[0, hd, DIFF_V_DIM:] = jnp.ones((VT_ROWS - DIFF_V_DIM, tm), BF16)


def _inproj(h, g, w, rc, rs1, rs2, qg, kg, *, tm, lp):
    r, d = h.shape
    nt_b = lp // tm
    row = lambda i: (i, 0)
    const = lambda i: (0, 0)
    rope = lambda i: (i % nt_b, 0)
    return pl.pallas_call(
        _inproj_kernel,
        grid=(r // tm,),
        in_specs=[
            pl.BlockSpec((tm, d), row),
            pl.BlockSpec((1, d), const),
            pl.BlockSpec(w.shape, const),
            pl.BlockSpec((tm, LANES), rope),
            pl.BlockSpec((tm, LANES), rope),
            pl.BlockSpec((tm, LANES), rope),
            pl.BlockSpec((1, LANES), const),
            pl.BlockSpec((1, LANES), const),
        ],
        out_specs=[
            pl.BlockSpec((tm, PA_COLS), row),
            pl.BlockSpec((2, tm, DIFF_WIDTH), lambda i: (0, i, 0)),
            pl.BlockSpec((tm, DIFF_WIDTH), row),
            pl.BlockSpec((1, DIFF_HEADS, VT_ROWS, tm), lambda i: (i, 0, 0, 0)),
            pl.BlockSpec((tm, PC_COLS), row),
        ],
        out_shape=[
            jax.ShapeDtypeStruct((r, PA_COLS), F32),
            jax.ShapeDtypeStruct((2, r, DIFF_WIDTH), BF16),
            jax.ShapeDtypeStruct((r, DIFF_WIDTH), BF16),
            jax.ShapeDtypeStruct((r // tm, DIFF_HEADS, VT_ROWS, tm), BF16),
            jax.ShapeDtypeStruct((r, PC_COLS), F32),
        ],
        compiler_params=_params(("arbitrary",)),
        name="inproj",
    )(h, g, w, rc, rs1, rs2, qg, kg)


def _split3(x):
    hi = x.astype(BF16)
    r1 = x - hi.astype(F32)
    mid = r1.astype(BF16)
    lo = (r1 - mid.astype(F32)).astype(BF16)
    return hi, mid, lo


def _dot_hi(a, b):
    ah = a.astype(BF16)
    al = (a - ah.astype(F32)).astype(BF16)
    bh = b.astype(BF16)
    bl = (b - bh.astype(F32)).astype(BF16)
    d = lambda p, q: jnp.dot(p, q, preferred_element_type=F32)
    return d(ah, bh) + d(ah, bl) + d(al, bh)


def _bmm(a, b):
    return jnp.einsum('cik,ckj->cij', a.astype(BF16), b.astype(BF16), preferred_element_type=F32)


def _bmm_nt(a, b):
    return jnp.einsum('cik,cjk->cij', a.astype(BF16), b.astype(BF16), preferred_element_type=F32)


def _rwkv_kernel(pa_ref, halo_ref, mu_ref, vec_ref, wd_ref, wa_ref, wg_ref, oa_ref, hstate_s):
    tb = pa_ref.shape[0]
    nc = tb // CHUNK
    i = pl.program_id(1)

    @pl.when(i == 0)
    def _():
        hstate_s[...] = jnp.zeros_like(hstate_s)

    x = pa_ref[...]
    prev_row = jnp.where(i == 0, 0.0, halo_ref[SUBLANES - 1:SUBLANES, :])
    row = lax.broadcasted_iota(jnp.int32, (tb, 1), 0)
    prev = jnp.where(row == 0, prev_row, pltpu.roll(x, 1, 0))
    hx = x + (prev - x) * mu_ref[...]
    r = hx[:, 0:RWKV_PAD]
    k = hx[:, RWKV_PAD:2 * RWKV_PAD]
    v = hx[:, 2 * RWKV_PAD:3 * RWKV_PAD]
    lora = hx[:, 3 * RWKV_PAD:]
    w0, a0, k_k, k_a = vec_ref[0:1, :], vec_ref[1:2, :], vec_ref[2:3, :], vec_ref[3:4, :]
    r_k, gn_g, gn_b = vec_ref[4:5, :], vec_ref[5:6, :], vec_ref[6:7, :]

    w = -jax.nn.softplus(-(w0 + _dot_hi(jnp.tanh(lora), wd_ref[...]))) - 0.5
    wl = -jnp.exp(w)
    alr = jax.nn.sigmoid(a0 + _dot(lora, wa_ref[...]))
    gate = _dot(jax.nn.sigmoid(lora), wg_ref[...])
    kk = k * k_k
    k2 = k * (1.0 + (alr - 1.0) * k_a)
    rkr = r * k2 * r_k

    ri = lax.broadcasted_iota(jnp.int32, (CHUNK, CHUNK), 0)
    ci = lax.broadcasted_iota(jnp.int32, (CHUNK, CHUNK), 1)
    lane_c = lax.broadcasted_iota(jnp.int32, (CHUNK, HEAD_PAD), 1)
    row_c = lax.broadcasted_iota(jnp.int32, (CHUNK, HEAD_PAD), 0)
    col_c = jnp.where(lane_c >= CHUNK, lane_c - CHUNK, lane_c)
    low_incl2 = (col_c <= row_c)[None]
    low_strict2 = (col_c < row_c)[None]
    tril_b = jnp.where(ci <= ri, 1.0, 0.0).astype(BF16)
    eye_k = (lax.broadcasted_iota(jnp.int32, (HEAD_PAD, HEAD_PAD), 0)
             == lax.broadcasted_iota(jnp.int32, (HEAD_PAD, HEAD_PAD), 1))[None]
    n_sq = int(math.log2(CHUNK))

    parts = _split3(wl)
    cums = []
    for c in range(nc):
        rows = slice(c * CHUNK, (c + 1) * CHUNK)
        cums.append(sum(jnp.dot(tril_b, p[rows], preferred_element_type=F32) for p in parts))
    cum_all = jnp.concatenate(cums, axis=0)

    lane = lax.broadcasted_iota(jnp.int32, (tb, HEAD_PAD), 1)
    real = lane < RWKV_HEAD_DIM
    to3 = lambda t: t.reshape(nc, CHUNK, HEAD_PAD)
    heads = range(RWKV_HEADS)
    hsl = [slice(hd * HEAD_PAD, (hd + 1) * HEAD_PAD) for hd in heads]
    kkn, bonus = [], []
    for cs in hsl:
        kk_h = kk[:, cs]
        nrm = jnp.sqrt(jnp.sum(kk_h * kk_h, axis=-1, keepdims=True))
        kkn.append(kk_h / jnp.maximum(nrm, 1e-12))
        bonus.append(jnp.sum(rkr[:, cs], axis=-1, keepdims=True) * v[:, cs])

    vc = [to3(v[:, cs]) for cs in hsl]
    zero_v = jnp.zeros_like(vc[0])
    ah, rh, last, g, bk_last = [], [], [], [], []
    for hd, cs in enumerate(hsl):
        wl3, cum = to3(wl[:, cs]), to3(cum_all[:, cs])
        rr, kc = to3(r[:, cs]), to3(k2[:, cs])
        ac, bc = to3(-kkn[hd]), to3(kkn[hd] * alr[:, cs])
        last.append(cum[:, CHUNK - 1:CHUNK, :])
        e_neg = jnp.exp(-cum)
        e_last = jnp.exp(last[hd] - cum)
        ah.append(ac * jnp.exp(cum - wl3))
        rh.append(rr * jnp.exp(cum))
        bk_last.append(jnp.concatenate([bc * e_last, kc * e_last], axis=1))
        g.append(_bmm_nt(jnp.concatenate([ah[hd], rh[hd]], axis=1),
                         jnp.concatenate([bc * e_neg, kc * e_neg], axis=1)))
    a_abk = [jnp.where(low_strict2, g[hd][:, :CHUNK], 0.0) for hd in heads]
    r_bk = [jnp.where(low_incl2, g[hd][:, CHUNK:], 0.0) for hd in heads]
    xx = [jnp.concatenate([_bmm(a_abk[hd], jnp.concatenate([zero_v, vc[hd]], axis=1)), ah[hd]],
                          axis=2) for hd in heads]
    ppad = [jnp.where(lane_c < CHUNK, a_abk[hd], 0.0) for hd in heads]
    for j in range(n_sq):
        for hd in heads:
            pj = ppad[hd][:, :, :CHUNK]
            if j + 1 < n_sq:
                res = _bmm(pj, jnp.concatenate([xx[hd], ppad[hd]], axis=2))
                xx[hd] = xx[hd] + res[:, :, :2 * HEAD_PAD]
                ppad[hd] = res[:, :, 2 * HEAD_PAD:]
            else:
                xx[hd] = xx[hd] + _bmm(pj, xx[hd])
    qm, y0, n0 = [], [], []
    for hd in heads:
        lhs = jnp.concatenate([r_bk[hd], jnp.swapaxes(bk_last[hd], 1, 2)], axis=1)
        rhs = jnp.concatenate([xx[hd], jnp.concatenate([vc[hd], zero_v], axis=2)], axis=1)
        out = _bmm(lhs, rhs)
        y0.append(out[:, :CHUNK, :HEAD_PAD])
        n0.append(out[:, CHUNK:, :HEAD_PAD])
        q = rh[hd] + out[:, :CHUNK, HEAD_PAD:]
        m = jnp.where(eye_k, jnp.exp(last[hd]), 0.0) + out[:, CHUNK:, HEAD_PAD:]
        qm.append(jnp.concatenate([q, m], axis=1).astype(BF16))

    hs = [hstate_s[hd] for hd in heads]
    ys = [[] for _ in heads]
    for c in range(nc):
        for hd in heads:
            res = jnp.dot(qm[hd][c], hs[hd].astype(BF16), preferred_element_type=F32)
            ys[hd].append(res[:CHUNK] + y0[hd][c])
            hs[hd] = res[CHUNK:] + n0[hd][c]

    for hd, cs in enumerate(hsl):
        hstate_s[hd] = hs[hd]
        y = jnp.concatenate(ys[hd], axis=0)
        mean = jnp.sum(y, axis=-1, keepdims=True) / RWKV_HEAD_DIM
        dlt = jnp.where(real, y - mean, 0.0)
        var = jnp.sum(dlt * dlt, axis=-1, keepdims=True) / RWKV_HEAD_DIM
        yn = dlt * lax.rsqrt(var + RWKV_GN_EPS) * gn_g[:, cs] + gn_b[:, cs]
        oa_ref[:, cs] = ((yn + bonus[hd]) * gate[:, cs]).astype(BF16)


def _rwkv(pa, mu, vecs, wd, wa, wg, *, tb):
    b, lp, _ = pa.shape
    tile = lambda bi, i: (bi, i, 0)
    const = lambda bi, i: (0, 0)
    halo = lambda bi, i: (bi, jnp.maximum(i * (tb // SUBLANES) - 1, 0), 0)
    return pl.pallas_call(
        _rwkv_kernel,
        grid=(b, lp // tb),
        in_specs=[
            pl.BlockSpec((None, tb, PA_COLS), tile),
            pl.BlockSpec((None, SUBLANES, PA_COLS), halo),
            pl.BlockSpec((1, PA_COLS), const),
            pl.BlockSpec((SUBLANES, RWKV_PAD), const),
            pl.BlockSpec((LORA_COLS, RWKV_PAD), const),
            pl.BlockSpec((LORA_COLS, RWKV_PAD), const),
            pl.BlockSpec((LORA_COLS, RWKV_PAD), const),
        ],
        out_specs=pl.BlockSpec((None, tb, RWKV_PAD), tile),
        out_shape=jax.ShapeDtypeStruct((b, lp, RWKV_PAD), BF16),
        scratch_shapes=[pltpu.VMEM((RWKV_HEADS, HEAD_PAD, HEAD_PAD), F32)],
        compiler_params=_params(("arbitrary", "arbitrary")),
        name="rwkv7",
    )(pa, pa, mu, vecs, wd, wa, wg)


ATTN_HEADS_PER_STEP = 4


def _attn_kernel(q_ref, k_ref, vt_ref, lamv_ref, sg_ref, o_ref, acc_s, st_s, p_s, *, lam_init):
    tq = o_ref.shape[0]
    tk = vt_ref.shape[3]
    qi = pl.program_id(2)
    heads = range(ATTN_HEADS_PER_STEP)
    hl = [slice(h * LANES, (h + 1) * LANES) for h in heads]

    def scores(h, ki):
        k = k_ref[pl.ds(pl.multiple_of(ki * tk, tk), tk), hl[h]]
        return lax.dot_general(k, q_ref[:, :, hl[h]].reshape(2 * tq, LANES),
                               (((1,), (1,)), ((), ())), preferred_element_type=F32)

    def values(h, ki, p, alpha):
        acc_s[h] = alpha * acc_s[h] + jnp.dot(vt_ref[ki, h], p, preferred_element_type=F32)

    st = [scores(h, qi) for h in heads]

    @pl.when(qi > 0)
    def _():
        for h in heads:
            st_s[h, 0] = scores(h, 0)

    kr = lax.broadcasted_iota(jnp.int32, (tk, 2 * tq), 0)
    qc = lax.broadcasted_iota(jnp.int32, (tk, 2 * tq), 1)
    qc = jnp.where(qc >= tq, qc - tq, qc)
    m0 = []
    for h in heads:
        s_h = jnp.where(kr <= qc, st[h], NEG_INF)
        m0.append(jnp.max(s_h, axis=0, keepdims=True))
        p_s[h, 1] = jnp.exp2(s_h - m0[h]).astype(BF16)
        acc_s[h] = jnp.zeros(acc_s.shape[1:], acc_s.dtype)

    def trip(k, carry, slot):
        out = []
        nxt = jnp.minimum(k + 1, qi - 1)
        prv = jnp.where(k == 0, qi, k - 1)
        for h in heads:
            st_s[h, 1 - slot] = scores(h, nxt)
        for h in heads:
            m_prev, alpha_prev = carry[h]
            s_k = st_s[h, slot]
            m_new = jnp.maximum(m_prev, jnp.max(s_k, axis=0, keepdims=True))
            alpha = jnp.exp2(m_prev - m_new)
            values(h, prv, p_s[h, 1 - slot], alpha_prev)
            p_s[h, slot] = jnp.exp2(s_k - m_new).astype(BF16)
            out.append((m_new, alpha))
        return tuple(out)

    def two_trips(j, c):
        c = trip(2 * j, c, 0)
        return lax.cond(2 * j + 1 < qi, lambda cc: trip(2 * j + 1, cc, 1), lambda cc: cc, c)

    init = tuple((m0[h], jnp.ones_like(m0[h])) for h in heads)
    carry = lax.fori_loop(0, (qi + 1) // 2, two_trips, init)
    odd = qi % 2 == 1

    @pl.when(odd)
    def _():
        for h in heads:
            values(h, qi - 1, p_s[h, 0], carry[h][1])

    @pl.when(jnp.logical_not(odd))
    def _():
        for h in heads:
            values(h, jnp.where(qi == 0, 0, qi - 1), p_s[h, 1], carry[h][1])

    lv = lamv_ref[...]
    lam = (jnp.exp(jnp.sum(lv[0:1] * lv[1:2], axis=-1, keepdims=True))
           - jnp.exp(jnp.sum(lv[2:3] * lv[3:4], axis=-1, keepdims=True)) + lam_init)
    for h in heads:
        acc = acc_s[h]
        ot = acc[:DIFF_V_DIM] / acc[DIFF_V_DIM:DIFF_V_DIM + 1]
        od = (ot[:, :tq] - lam * ot[:, tq:]).T
        ms = jnp.mean(od * od, axis=-1, keepdims=True)
        o_ref[:, hl[h]] = (od * lax.rsqrt(ms + SUBLN_EPS) * sg_ref[...]
                           * (1.0 - lam_init)).astype(BF16)


def _attention(qs, kh, vt, lamv, sg, *, tq, lam_init):
    _, b, lp, _ = qs.shape
    nq = lp // tq
    hp = ATTN_HEADS_PER_STEP
    const = lambda bi, hg, qi: (0, 0)
    return pl.pallas_call(
        functools.partial(_attn_kernel, lam_init=lam_init),
        grid=(b, DIFF_HEADS // hp, nq),
        in_specs=[
            pl.BlockSpec((2, None, tq, hp * LANES), lambda bi, hg, qi: (0, bi, qi, hg)),
            pl.BlockSpec((None, lp, hp * LANES), lambda bi, hg, qi: (bi, 0, hg)),
            pl.BlockSpec((None, nq, hp, VT_ROWS, tq), lambda bi, hg, qi: (bi, 0, hg, 0, 0)),
            pl.BlockSpec((SUBLANES, LANES), const),
            pl.BlockSpec((1, LANES), const),
        ],
        out_specs=pl.BlockSpec((None, tq, hp * LANES), lambda bi, hg, qi: (bi, qi, hg)),
        out_shape=jax.ShapeDtypeStruct((b, lp, DIFF_WIDTH), BF16),
        scratch_shapes=[pltpu.VMEM((hp, VT_ROWS, 2 * tq), F32),
                        pltpu.VMEM((hp, 2, tq, 2 * tq), F32),
                        pltpu.VMEM((hp, 2, tq, 2 * tq), BF16)],
        compiler_params=_params(("arbitrary", "arbitrary", "arbitrary")),
        name="diffattn",
    )(qs, kh, vt, lamv, sg)


def _outproj_kernel(h_ref, oa_ref, ob_ref, pc_ref, pch_ref, cw_ref, wa_ref, wb_ref, wc_ref,
                    o_ref, *, tiles_per_batch):
    tm = h_ref.shape[0]
    first = (pl.program_id(0) % tiles_per_batch) == 0
    pc = pc_ref[...]
    bg = pc[:, :CONV_WIDTH]
    z = pc[:, CONV_WIDTH:2 * CONV_WIDTH] * pc[:, 2 * CONV_WIDTH:]
    ph = pch_ref[...]
    zh = jnp.where(first, 0.0, ph[:, CONV_WIDTH:2 * CONV_WIDTH] * ph[:, 2 * CONV_WIDTH:])
    row = lax.broadcasted_iota(jnp.int32, (tm, 1), 0)
    z1 = jnp.where(row == 0, zh[SUBLANES - 1:SUBLANES], pltpu.roll(z, 1, 0))
    z2 = jnp.where(row == 0, zh[SUBLANES - 2:SUBLANES - 1],
                   jnp.where(row == 1, zh[SUBLANES - 1:SUBLANES], pltpu.roll(z, 2, 0)))
    cw = cw_ref[...]
    y = cw[0:1] * z2 + cw[1:2] * z1 + cw[2:3] * z
    oc = (bg * y).astype(BF16)
    o_ref[...] = (h_ref[...]
                  + jnp.dot(oa_ref[...], wa_ref[...], preferred_element_type=F32)
                  + jnp.dot(ob_ref[...], wb_ref[...], preferred_element_type=F32)
                  + jnp.dot(oc, wc_ref[...], preferred_element_type=F32))


def _outproj(h, oa, ob, pc, cw, wa, wb, wc, *, tm, lp):
    r, d = h.shape
    row = lambda i: (i, 0)
    const = lambda i: (0, 0)
    halo = lambda i: (jnp.maximum(i * (tm // SUBLANES) - 1, 0), 0)
    return pl.pallas_call(
        functools.partial(_outproj_kernel, tiles_per_batch=lp // tm),
        grid=(r // tm,),
        in_specs=[
            pl.BlockSpec((tm, d), row),
            pl.BlockSpec((tm, RWKV_PAD), row),
            pl.BlockSpec((tm, DIFF_WIDTH), row),
            pl.BlockSpec((tm, PC_COLS), row),
            pl.BlockSpec((SUBLANES, PC_COLS), halo),
            pl.BlockSpec((SUBLANES, CONV_WIDTH), const),
            pl.BlockSpec(wa.shape, const),
            pl.BlockSpec(wb.shape, const),
            pl.BlockSpec(wc.shape, const),
        ],
        out_specs=pl.BlockSpec((tm, d), row),
        out_shape=jax.ShapeDtypeStruct((r, d), F32),
        compiler_params=_params(("arbitrary",)),
        name="outproj",
    )(h, oa, ob, pc, pc, cw, wa, wb, wc)


def _swiglu_block(xb, wg_ref, wu_ref, wd_ref, side_work=None):
    d_ff = wg_ref.shape[1]
    acc = jnp.zeros((xb.shape[0], wd_ref.shape[1]), F32)
    for c in range(d_ff // MXU_DIM):
        cs = slice(c * MXU_DIM, (c + 1) * MXU_DIM)
        gt = jnp.dot(xb, wg_ref[:, cs], preferred_element_type=F32)
        up = jnp.dot(xb, wu_ref[:, cs], preferred_element_type=F32)
        act = (jax.nn.silu(gt) * up).astype(BF16)
        acc = acc + jnp.dot(act, wd_ref[cs, :], preferred_element_type=F32)
        if side_work is not None:
            side_work(c)
    return acc


def _rms(x, g):
    ms = jnp.mean(x * x, axis=-1, keepdims=True)
    return x * lax.rsqrt(ms + NORM_EPS) * g


def _ffn_kernel(h_ref, g_ref, wg_ref, wu_ref, wd_ref, o_ref):
    x = h_ref[...]
    xn = _rms(x, g_ref[...]).astype(BF16)
    o_ref[...] = x + _swiglu_block(xn, wg_ref, wu_ref, wd_ref)


def _dense_ffn(h, g, wg, wu, wd, *, tm):
    r, d = h.shape
    row = lambda i: (i, 0)
    const = lambda i: (0, 0)
    return pl.pallas_call(
        _ffn_kernel,
        grid=(r // tm,),
        in_specs=[
            pl.BlockSpec((tm, d), row),
            pl.BlockSpec((1, d), const),
            pl.BlockSpec(wg.shape, const),
            pl.BlockSpec(wu.shape, const),
            pl.BlockSpec(wd.shape, const),
        ],
        out_specs=pl.BlockSpec((tm, d), row),
        out_shape=jax.ShapeDtypeStruct((r, d), F32),
        compiler_params=_params(("arbitrary",)),
        name="dense_ffn",
    )(h, g, wg, wu, wd)


def _router_kernel(h_ref, g_ref, rw_ref, hn_ref, route_ref):
    tm = h_ref.shape[0]
    xn = _rms(h_ref[...], g_ref[...])
    hn_ref[...] = xn
    logits = _dot_f32(xn, rw_ref[...])
    lane = lax.broadcasted_iota(jnp.int32, (tm, LANES), 1)
    logits = jnp.where(lane < N_EXPERTS, logits, -jnp.inf)
    m1 = jnp.max(logits, axis=-1, keepdims=True)
    i1 = jnp.min(jnp.where(logits == m1, lane, LANES), axis=-1, keepdims=True)
    rest = jnp.where(lane == i1, -jnp.inf, logits)
    m2 = jnp.max(rest, axis=-1, keepdims=True)
    i2 = jnp.min(jnp.where(rest == m2, lane, LANES), axis=-1, keepdims=True)
    e2 = jnp.exp(m2 - m1)
    den = 1.0 + e2
    route_ref[...] = jnp.where(lane == 0, i1.astype(F32),
                     jnp.where(lane == 1, i2.astype(F32),
                     jnp.where(lane == 2, 1.0 / den,
                     jnp.where(lane == 3, e2 / den, 0.0))))


def _router(h, g, rw, *, tm):
    r, d = h.shape
    row = lambda i: (i, 0)
    const = lambda i: (0, 0)
    return pl.pallas_call(
        _router_kernel,
        grid=(r // tm,),
        in_specs=[pl.BlockSpec((tm, d), row), pl.BlockSpec((1, d), const),
                  pl.BlockSpec((d, LANES), const)],
        out_specs=[pl.BlockSpec((tm, d), row), pl.BlockSpec((tm, LANES), row)],
        out_shape=[jax.ShapeDtypeStruct((r, d), F32), jax.ShapeDtypeStruct((r, LANES), F32)],
        compiler_params=_params(("arbitrary",)),
        name="router",
    )(h, g, rw)


def _row_copy(src, dst, sem, src_row, dst_row):
    return pltpu.make_async_copy(src.at[pl.ds(src_row, 1), :], dst.at[pl.ds(dst_row, 1), :], sem)


DMA_UNROLL = 8
DMA_ISSUE_SLICES = 8
MOE_BUFFERS = 3


def _moe_kernel(bexp_ref, rsrc_ref, rdst_ref, hn_hbm, wg_ref, wu_ref, wd_ref, y_hbm,
                xbuf, ybuf, gsem, ssem):
    del bexp_ref
    nbuf = xbuf.shape[0]
    blk = xbuf.shape[1]
    i = pl.program_id(0)
    nb = pl.num_programs(0)
    slot = i % nbuf
    slot_prev = (i + nbuf - 1) % nbuf
    slot_next2 = (i + 2) % nbuf
    dump0 = y_hbm.shape[0] - blk

    def for_rows(fn):
        def body(j, c):
            for u in range(DMA_UNROLL):
                fn(j * DMA_UNROLL + u)
            return c
        lax.fori_loop(0, blk // DMA_UNROLL, body, 0)

    def gather(block, s, j):
        return _row_copy(hn_hbm, xbuf.at[s], gsem.at[s], rsrc_ref[block * blk + j], j)

    def scatter(dst_row, s, j):
        return _row_copy(ybuf.at[s], y_hbm, ssem.at[s], j, dst_row)

    @pl.when(i == 0)
    def _():
        for_rows(lambda j: gather(0, 0, j).start())
        for_rows(lambda j: gather(1, 1, j).start())
        ybuf[nbuf - 1] = jnp.zeros(ybuf.shape[1:], ybuf.dtype)

    for_rows(lambda j: gather(0, slot, j).wait())

    @pl.when(i >= 2)
    def _():
        for_rows(lambda j: scatter(0, slot, j).wait())

    ahead = jnp.minimum(i + 2, nb - 1)
    prev = jnp.maximum(i - 1, 0)
    per = -(-blk // DMA_ISSUE_SLICES)

    def side_work(c):
        for j in range(c * per, min((c + 1) * per, blk)):
            gather(ahead, slot_next2, j).start(priority=j % 2)
            dst = jnp.where(i == 0, dump0 + j, rdst_ref[prev * blk + j])
            scatter(dst, slot_prev, j).start(priority=(j + 1) % 2)

    ybuf[slot] = _swiglu_block(xbuf[slot].astype(BF16), wg_ref, wu_ref, wd_ref, side_work)

    @pl.when(i == nb - 1)
    def _():
        for_rows(lambda j: gather(0, (i + 1) % nbuf, j).wait())
        for_rows(lambda j: gather(0, slot_next2, j).wait())
        for_rows(lambda j: scatter(0, (i + 1) % nbuf, j).wait())
        for_rows(lambda j: scatter(0, slot_prev, j).wait())
        for_rows(lambda j: scatter(rdst_ref[i * blk + j], slot, j).start())
        for_rows(lambda j: scatter(0, slot, j).wait())


def _moe_experts(block_expert, row_src, row_dst, hn, wg, wu, wd):
    nb = block_expert.shape[0]
    r, d = hn.shape
    d_ff = wg.shape[2]
    wmap = lambda i, be, rs, rd: (be[i], 0, 0)
    grid_spec = pltpu.PrefetchScalarGridSpec(
        num_scalar_prefetch=3,
        grid=(nb,),
        in_specs=[
            pl.BlockSpec(memory_space=pl.ANY),
            pl.BlockSpec((None, d, d_ff), wmap),
            pl.BlockSpec((None, d, d_ff), wmap),
            pl.BlockSpec((None, d_ff, d), wmap),
        ],
        out_specs=pl.BlockSpec(memory_space=pl.ANY),
        scratch_shapes=[pltpu.VMEM((MOE_BUFFERS, MOE_BLOCK, d), F32),
                        pltpu.VMEM((MOE_BUFFERS, MOE_BLOCK, d), F32),
                        pltpu.SemaphoreType.DMA((MOE_BUFFERS,)),
                        pltpu.SemaphoreType.DMA((MOE_BUFFERS,))],
    )
    return pl.pallas_call(
        _moe_kernel,
        grid_spec=grid_spec,
        out_shape=jax.ShapeDtypeStruct((TOP_K * r + MOE_BLOCK, d), F32),
        compiler_params=_params(("arbitrary",)),
        name="moe_experts",
    )(block_expert, row_src, row_dst, hn, wg, wu, wd)


def _combine_kernel(h_ref, route_ref, y0_ref, y1_ref, o_ref):
    route = route_ref[...]
    o_ref[...] = h_ref[...] + route[:, 2:3] * y0_ref[...] + route[:, 3:4] * y1_ref[...]


def _moe_combine(h, route, y, *, tm):
    r, d = h.shape
    nt = r // tm
    row = lambda i: (i, 0)
    return pl.pallas_call(
        _combine_kernel,
        grid=(nt,),
        in_specs=[
            pl.BlockSpec((tm, d), row),
            pl.BlockSpec((tm, LANES), row),
            pl.BlockSpec((tm, d), row),
            pl.BlockSpec((tm, d), lambda i: (i + nt, 0)),
        ],
        out_specs=pl.BlockSpec((tm, d), row),
        out_shape=jax.ShapeDtypeStruct((r, d), F32),
        compiler_params=_params(("arbitrary",)),
        name="moe_combine",
    )(h, route, y, y)


def _slot_owner_kernel(dest_ref, owner_ref):
    n_pairs = dest_ref.shape[0]
    n_slots = owner_ref.shape[0]

    def clear(i, c):
        for u in range(DMA_UNROLL):
            owner_ref[i * DMA_UNROLL + u] = -1
        return c

    lax.fori_loop(0, n_slots // DMA_UNROLL, clear, 0)

    def place(i, c):
        for u in range(DMA_UNROLL):
            p = i * DMA_UNROLL + u
            owner_ref[dest_ref[p]] = p
        return c

    lax.fori_loop(0, n_pairs // DMA_UNROLL, place, 0)


def _slot_owner(dest, n_slots):
    assert dest.shape[0] % DMA_UNROLL == 0 and n_slots % DMA_UNROLL == 0
    smem = pl.BlockSpec(memory_space=pltpu.SMEM)
    return pl.pallas_call(
        _slot_owner_kernel,
        in_specs=[smem],
        out_specs=smem,
        out_shape=jax.ShapeDtypeStruct((n_slots,), jnp.int32),
        name="slot_owner",
    )(dest)


def _routing_tables(route):
    r = route.shape[0]
    tk = r * TOP_K
    flat_e = route[:, :TOP_K].reshape(tk).astype(jnp.int32)
    onehot = (flat_e[:, None] == jnp.arange(N_EXPERTS, dtype=jnp.int32)[None, :]).astype(jnp.int32)
    csum = jnp.cumsum(onehot, axis=0)
    rank = jnp.sum(csum * onehot, axis=1) - 1
    counts = csum[-1]
    padded = (counts + MOE_BLOCK - 1) // MOE_BLOCK * MOE_BLOCK
    pend = jnp.cumsum(padded)
    pstart = pend - padded
    dest = pstart[flat_e] + rank
    nb = -(-tk // MOE_BLOCK) + N_EXPERTS
    n_slots = nb * MOE_BLOCK
    pair = _slot_owner(dest, n_slots)
    used = pair >= 0
    row_src = jnp.where(used, pair // TOP_K, 0)
    row_dst = jnp.where(used, (pair % TOP_K) * r + pair // TOP_K,
                        tk + jnp.arange(n_slots, dtype=jnp.int32) % MOE_BLOCK)
    block_start = jnp.arange(nb, dtype=jnp.int32) * MOE_BLOCK
    block_expert = jnp.minimum(jnp.searchsorted(pend, block_start, side='right'),
                               N_EXPERTS - 1).astype(jnp.int32)
    return block_expert, row_src, row_dst


def _pad_heads(t, axis):
    axis = axis % t.ndim
    shp = t.shape
    t = t.reshape(shp[:axis] + (RWKV_HEADS, RWKV_HEAD_DIM) + shp[axis + 1:])
    pad = [(0, 0)] * t.ndim
    pad[axis + 1] = (0, HEAD_PAD - RWKV_HEAD_DIM)
    t = jnp.pad(t, pad)
    return t.reshape(shp[:axis] + (RWKV_PAD,) + shp[axis + 1:])


def _rope_tables(lp):
    half = ROPE_DIM // 2
    inv_freq = ROPE_THETA ** (-jnp.arange(0, ROPE_DIM, 2, dtype=F32) / ROPE_DIM)
    ang = jnp.arange(lp, dtype=F32)[:, None] * inv_freq[None, :]
    cos, sin = jnp.cos(ang), jnp.sin(ang)
    ones = jnp.ones((lp, DIFF_QK_DIM - ROPE_DIM), F32)
    zeros = jnp.zeros((lp, DIFF_QK_DIM - ROPE_DIM), F32)
    zh = jnp.zeros((lp, half), F32)
    rc = jnp.concatenate([cos, cos, ones], axis=1)
    rs1 = jnp.concatenate([-sin, zh, zeros], axis=1)
    rs2 = jnp.concatenate([zh, sin, zeros], axis=1)
    tile2 = lambda a: jnp.concatenate([a, a], axis=1)
    return tile2(rc), tile2(rs1), tile2(rs2)


def kernel(x, meta_tokens, mix_norm_g, w_in, tm_mu, tm_w0, tm_w_decay_up, tm_a0, tm_w_a_up, tm_w_g_up, tm_k_k, tm_k_a, tm_r_k, tm_gn_g, tm_gn_b, da_q_norm_g, da_k_norm_g, da_lambda_q1, da_lambda_k1, da_lambda_q2, da_lambda_k2, da_subln_g, sc_conv_w, w_out, ffn_norm_g, ffn_w_gate, ffn_w_up, ffn_w_down, router_w, moe_w_gate, moe_w_up, moe_w_down):
    b, seq, d = x.shape
    depth = w_in.shape[0]
    l = N_META + seq
    lp = -(-l // LANES) * LANES
    tm = _row_tile(lp)
    r = b * lp
    s1 = RWKV_WIDTH

    meta = jnp.broadcast_to(meta_tokens.astype(x.dtype)[None], (b, N_META, d))
    h = jnp.concatenate([meta, x, jnp.zeros((b, lp - l, d), x.dtype)], axis=1).reshape(r, d)
    rc, rs1, rs2 = _rope_tables(lp)

    for i in range(depth):
        lam_init = 0.8 - 0.6 * math.exp(-0.3 * i)
        wi = w_in[i]
        w_cat = jnp.concatenate(
            [_pad_heads(wi[:, 0:s1], 1), _pad_heads(wi[:, s1:2 * s1], 1),
             _pad_heads(wi[:, 2 * s1:3 * s1], 1), wi[:, 3 * s1:]], axis=1).astype(BF16)
        qg = jnp.tile(da_q_norm_g[i], 2)[None]
        kg = jnp.tile(da_k_norm_g[i], 2)[None]
        pa, qs, kh, vt, pc = _inproj(h, mix_norm_g[i][None], w_cat, rc, rs1, rs2, qg, kg,
                                     tm=tm, lp=lp)

        mu = tm_mu[i]
        mu_p = jnp.concatenate([_pad_heads(mu[0:s1], 0), _pad_heads(mu[s1:2 * s1], 0),
                                _pad_heads(mu[2 * s1:3 * s1], 0), mu[3 * s1:]])[None]
        vecs = jnp.stack([_pad_heads(tm_w0[i], 0), _pad_heads(tm_a0[i], 0),
                          _pad_heads(tm_k_k[i], 0), _pad_heads(tm_k_a[i], 0),
                          _pad_heads(tm_r_k[i].reshape(s1), 0), _pad_heads(tm_gn_g[i], 0),
                          _pad_heads(tm_gn_b[i], 0), jnp.zeros((RWKV_PAD,), F32)])
        zl = lambda n: jnp.zeros((n, RWKV_PAD), F32)
        wd_p = jnp.concatenate([_pad_heads(tm_w_decay_up[i], 1), zl(AAA_LORA + GATE_LORA)], axis=0)
        wa_p = jnp.concatenate([zl(DECAY_LORA), _pad_heads(tm_w_a_up[i], 1), zl(GATE_LORA)], axis=0)
        wg_p = jnp.concatenate([zl(DECAY_LORA + AAA_LORA), _pad_heads(tm_w_g_up[i], 1)], axis=0)
        oa = _rwkv(pa.reshape(b, lp, PA_COLS), mu_p, vecs, wd_p, wa_p, wg_p, tb=tm)

        lamv = jnp.zeros((SUBLANES, LANES), F32).at[0:4, :DIFF_QK_DIM].set(
            jnp.stack([da_lambda_q1[i], da_lambda_k1[i], da_lambda_q2[i], da_lambda_k2[i]]))
        ob = _attention(qs.reshape(2, b, lp, DIFF_WIDTH), kh.reshape(b, lp, DIFF_WIDTH),
                        vt.reshape(b, lp // tm, DIFF_HEADS, VT_ROWS, tm), lamv, da_subln_g[i][None],
                        tq=tm, lam_init=lam_init)

        wo = w_out[i]
        cw = jnp.zeros((SUBLANES, CONV_WIDTH), F32).at[:CONV_K].set(sc_conv_w[i])
        h = _outproj(h, oa.reshape(r, RWKV_PAD), ob.reshape(r, DIFF_WIDTH), pc, cw,
                     _pad_heads(wo[:s1], 0).astype(BF16), wo[s1:s1 + DIFF_WIDTH].astype(BF16),
                     wo[s1 + DIFF_WIDTH:].astype(BF16), tm=tm, lp=lp)

        j = i // 2
        if i % 2 == 0:
            h = _dense_ffn(h, ffn_norm_g[i][None], ffn_w_gate[j].astype(BF16),
                           ffn_w_up[j].astype(BF16), ffn_w_down[j].astype(BF16), tm=tm)
        else:
            rw = jnp.zeros((d, LANES), F32).at[:, :N_EXPERTS].set(router_w[j])
            hn, route = _router(h, ffn_norm_g[i][None], rw, tm=tm)
            block_expert, row_src, row_dst = _routing_tables(route)
            y = _moe_experts(block_expert, row_src, row_dst, hn, moe_w_gate[j].astype(BF16),
                             moe_w_up[j].astype(BF16), moe_w_down[j].astype(BF16))
            h = _moe_combine(h, route, y, tm=tm)

    return h.reshape(b, lp, d)[:, N_META:l]
```

```python
import functools
import math

import jax
import jax.numpy as jnp
from jax import lax
from jax.experimental import pallas as pl
from jax.experimental.pallas import tpu as pltpu

N_META = 16
RWKV_HEAD_DIM = 64
RWKV_HEADS = 4
RWKV_WIDTH = RWKV_HEADS * RWKV_HEAD_DIM
DECAY_LORA = 32
AAA_LORA = 32
GATE_LORA = 64
LORA_COLS = DECAY_LORA + AAA_LORA + GATE_LORA
DIFF_QK_DIM = 64
DIFF_V_DIM = 128
DIFF_HEADS = 4
DIFF_WIDTH = DIFF_HEADS * DIFF_V_DIM
CONV_WIDTH = 256
CONV_K = 3
ROPE_THETA = 500000.0
ROPE_DIM = DIFF_QK_DIM // 4
N_EXPERTS = 8
TOP_K = 2
MOE_BLOCK = 256
NORM_EPS = 1e-6
RWKV_GN_EPS = 64e-5
SUBLN_EPS = 1e-5
NEG_INF = -1e30

LANES = 128
SUBLANES = 8
MXU_DIM = 256
VMEM_LIMIT_BYTES = 56 * 1024 * 1024

HEAD_PAD = LANES
RWKV_PAD = RWKV_HEADS * HEAD_PAD
PA_COLS = 3 * RWKV_PAD + LORA_COLS
PB_COLS = 3 * DIFF_WIDTH
PC_COLS = 3 * CONV_WIDTH
VT_ROWS = DIFF_V_DIM + 16
LOG2_E = math.log2(math.e)
CHUNK = 64

F32 = jnp.float32
BF16 = jnp.bfloat16
HIGHEST = lax.Precision.HIGHEST


def _dot(a, b):
    return jnp.dot(a.astype(BF16), b.astype(BF16), preferred_element_type=F32)


def _dot_nt(a, b):
    return lax.dot_general(a.astype(BF16), b.astype(BF16), (((1,), (1,)), ((), ())),
                           preferred_element_type=F32)


def _dot_f32(a, b):
    return jnp.dot(a, b, preferred_element_type=F32, precision=HIGHEST)


def _row_tile(lp):
    for t in (384, 256, 128):
        if lp % t == 0:
            return t
    raise ValueError(f"padded length {lp} is not a multiple of {LANES}")


def _params(sem):
    return pltpu.CompilerParams(dimension_semantics=sem, vmem_limit_bytes=VMEM_LIMIT_BYTES)


def _inproj_kernel(h_ref, g_ref, w_ref, rc_ref, rs1_ref, rs2_ref, qg_ref, kg_ref,
                   pa_ref, qs_ref, kh_ref, vt_ref, pc_ref):
    tm = h_ref.shape[0]
    x = h_ref[...]
    ms = jnp.mean(x * x, axis=-1, keepdims=True)
    xn = (x * lax.rsqrt(ms + NORM_EPS) * g_ref[...]).astype(BF16)
    pa_ref[...] = jnp.dot(xn, w_ref[:, :PA_COLS], preferred_element_type=F32)
    pc_ref[...] = jnp.dot(xn, w_ref[:, PA_COLS + PB_COLS:], preferred_element_type=F32)
    pb = jnp.dot(xn, w_ref[:, PA_COLS:PA_COLS + PB_COLS], preferred_element_type=F32)

    lane = lax.broadcasted_iota(jnp.int32, (tm, LANES), 1)
    lo = lane < DIFF_QK_DIM
    rc, rs1, rs2 = rc_ref[...], rs1_ref[...], rs2_ref[...]

    def norm_rope(t, g):
        t2 = t * t
        ss_lo = jnp.sum(jnp.where(lo, t2, 0.0), axis=-1, keepdims=True)
        ss_hi = jnp.sum(jnp.where(lo, 0.0, t2), axis=-1, keepdims=True)
        inv = jnp.where(lo, lax.rsqrt(ss_lo / DIFF_QK_DIM + NORM_EPS),
                        lax.rsqrt(ss_hi / DIFF_QK_DIM + NORM_EPS))
        tn = t * inv * g
        half = ROPE_DIM // 2
        return tn * rc + pltpu.roll(tn, LANES - half, 1) * rs1 + pltpu.roll(tn, half, 1) * rs2

    for hd in range(DIFF_HEADS):
        cs = slice(hd * LANES, (hd + 1) * LANES)
        q = norm_rope(pb[:, cs], qg_ref[...]) * (DIFF_QK_DIM ** -0.5 * LOG2_E)
        qs_ref[0, :, cs] = jnp.where(lo, q, 0.0).astype(BF16)
        qs_ref[1, :, cs] = jnp.where(lo, 0.0, q).astype(BF16)
        kcs = slice(DIFF_WIDTH + hd * LANES, DIFF_WIDTH + (hd + 1) * LANES)
        kh_ref[:, cs] = norm_rope(pb[:, kcs], kg_ref[...]).astype(BF16)
        vcs = slice(2 * DIFF_WIDTH + hd * LANES, 2 * DIFF_WIDTH + (hd + 1) * LANES)
        vt_ref[0, hd, :DIFF_V_DIM] = pb[:, vcs].T.astype(BF16)
        vt_ref[0, hd, DIFF_V_DIM:] = jnp.ones((VT_ROWS - DIFF_V_DIM, tm), BF16)


def _inproj(h, g, w, rc, rs1, rs2, qg, kg, *, tm, lp):
    r, d = h.shape
    nt_b = lp // tm
    row = lambda i: (i, 0)
    const = lambda i: (0, 0)
    rope = lambda i: (i % nt_b, 0)
    return pl.pallas_call(
        _inproj_kernel,
        grid=(r // tm,),
        in_specs=[
            pl.BlockSpec((tm, d), row),
            pl.BlockSpec((1, d), const),
            pl.BlockSpec(w.shape, const),
            pl.BlockSpec((tm, LANES), rope),
            pl.BlockSpec((tm, LANES), rope),
            pl.BlockSpec((tm, LANES), rope),
            pl.BlockSpec((1, LANES), const),
            pl.BlockSpec((1, LANES), const),
        ],
        out_specs=[
            pl.BlockSpec((tm, PA_COLS), row),
            pl.BlockSpec((2, tm, DIFF_WIDTH), lambda i: (0, i, 0)),
            pl.BlockSpec((tm, DIFF_WIDTH), row),
            pl.BlockSpec((1, DIFF_HEADS, VT_ROWS, tm), lambda i: (i, 0, 0, 0)),
            pl.BlockSpec((tm, PC_COLS), row),
        ],
        out_shape=[
            jax.ShapeDtypeStruct((r, PA_COLS), F32),
            jax.ShapeDtypeStruct((2, r, DIFF_WIDTH), BF16),
            jax.ShapeDtypeStruct((r, DIFF_WIDTH), BF16),
            jax.ShapeDtypeStruct((r // tm, DIFF_HEADS, VT_ROWS, tm), BF16),
            jax.ShapeDtypeStruct((r, PC_COLS), F32),
        ],
        compiler_params=_params(("arbitrary",)),
        name="inproj",
    )(h, g, w, rc, rs1, rs2, qg, kg)


def _split3(x):
    hi = x.astype(BF16)
    r1 = x - hi.astype(F32)
    mid = r1.astype(BF16)
    lo = (r1 - mid.astype(F32)).astype(BF16)
    return hi, mid, lo


def _dot_hi(a, b):
    ah = a.astype(BF16)
    al = (a - ah.astype(F32)).astype(BF16)
    bh = b.astype(BF16)
    bl = (b - bh.astype(F32)).astype(BF16)
    d = lambda p, q: jnp.dot(p, q, preferred_element_type=F32)
    return d(ah, bh) + d(ah, bl) + d(al, bh)


def _bmm(a, b):
    return jnp.einsum('cik,ckj->cij', a.astype(BF16), b.astype(BF16), preferred_element_type=F32)


def _bmm_nt(a, b):
    return jnp.einsum('cik,cjk->cij', a.astype(BF16), b.astype(BF16), preferred_element_type=F32)


def _rwkv_kernel(pa_ref, halo_ref, mu_ref, vec_ref, wd_ref, wa_ref, wg_ref, oa_ref, hstate_s):
    tb = pa_ref.shape[0]
    nc = tb // CHUNK
    i = pl.program_id(1)

    @pl.when(i == 0)
    def _():
        hstate_s[...] = jnp.zeros_like(hstate_s)

    x = pa_ref[...]
    prev_row = jnp.where(i == 0, 0.0, halo_ref[SUBLANES - 1:SUBLANES, :])
    row = lax.broadcasted_iota(jnp.int32, (tb, 1), 0)
    prev = jnp.where(row == 0, prev_row, pltpu.roll(x, 1, 0))
    hx = x + (prev - x) * mu_ref[...]
    r = hx[:, 0:RWKV_PAD]
    k = hx[:, RWKV_PAD:2 * RWKV_PAD]
    v = hx[:, 2 * RWKV_PAD:3 * RWKV_PAD]
    lora = hx[:, 3 * RWKV_PAD:]
    w0, a0, k_k, k_a = vec_ref[0:1, :], vec_ref[1:2, :], vec_ref[2:3, :], vec_ref[3:4, :]
    r_k, gn_g, gn_b = vec_ref[4:5, :], vec_ref[5:6, :], vec_ref[6:7, :]

    w = -jax.nn.softplus(-(w0 + _dot_hi(jnp.tanh(lora), wd_ref[...]))) - 0.5
    wl = -jnp.exp(w)
    alr = jax.nn.sigmoid(a0 + _dot(lora, wa_ref[...]))
    gate = _dot(jax.nn.sigmoid(lora), wg_ref[...])
    kk = k * k_k
    k2 = k * (1.0 + (alr - 1.0) * k_a)
    rkr = r * k2 * r_k

    ri = lax.broadcasted_iota(jnp.int32, (CHUNK, CHUNK), 0)
    ci = lax.broadcasted_iota(jnp.int32, (CHUNK, CHUNK), 1)
    lane_c = lax.broadcasted_iota(jnp.int32, (CHUNK, HEAD_PAD), 1)
    row_c = lax.broadcasted_iota(jnp.int32, (CHUNK, HEAD_PAD), 0)
    col_c = jnp.where(lane_c >= CHUNK, lane_c - CHUNK, lane_c)
    low_incl2 = (col_c <= row_c)[None]
    low_strict2 = (col_c < row_c)[None]
    tril_b = jnp.where(ci <= ri, 1.0, 0.0).astype(BF16)
    eye_k = (lax.broadcasted_iota(jnp.int32, (HEAD_PAD, HEAD_PAD), 0)
             == lax.broadcasted_iota(jnp.int32, (HEAD_PAD, HEAD_PAD), 1))[None]
    n_sq = int(math.log2(CHUNK))

    parts = _split3(wl)
    cums = []
    for c in range(nc):
        rows = slice(c * CHUNK, (c + 1) * CHUNK)
        cums.append(sum(jnp.dot(tril_b, p[rows], preferred_element_type=F32) for p in parts))
    cum_all = jnp.concatenate(cums, axis=0)

    lane = lax.broadcasted_iota(jnp.int32, (tb, HEAD_PAD), 1)
    real = lane < RWKV_HEAD_DIM
    to3 = lambda t: t.reshape(nc, CHUNK, HEAD_PAD)
    heads = range(RWKV_HEADS)
    hsl = [slice(hd * HEAD_PAD, (hd + 1) * HEAD_PAD) for hd in heads]
    kkn, bonus = [], []
    for cs in hsl:
        kk_h = kk[:, cs]
        nrm = jnp.sqrt(jnp.sum(kk_h * kk_h, axis=-1, keepdims=True))
        kkn.append(kk_h / jnp.maximum(nrm, 1e-12))
        bonus.append(jnp.sum(rkr[:, cs], axis=-1, keepdims=True) * v[:, cs])

    vc = [to3(v[:, cs]) for cs in hsl]
    zero_v = jnp.zeros_like(vc[0])
    ah, rh, last, g, bk_last = [], [], [], [], []
    for hd, cs in enumerate(hsl):
        wl3, cum = to3(wl[:, cs]), to3(cum_all[:, cs])
        rr, kc = to3(r[:, cs]), to3(k2[:, cs])
        ac, bc = to3(-kkn[hd]), to3(kkn[hd] * alr[:, cs])
        last.append(cum[:, CHUNK - 1:CHUNK, :])
        e_neg = jnp.exp(-cum)
        e_last = jnp.exp(last[hd] - cum)
        ah.append(ac * jnp.exp(cum - wl3))
        rh.append(rr * jnp.exp(cum))
        bk_last.append(jnp.concatenate([bc * e_last, kc * e_last], axis=1))
        g.append(_bmm_nt(jnp.concatenate([ah[hd], rh[hd]], axis=1),
                         jnp.concatenate([bc * e_neg, kc * e_neg], axis=1)))
    a_abk = [jnp.where(low_strict2, g[hd][:, :CHUNK], 0.0) for hd in heads]
    r_bk = [jnp.where(low_incl2, g[hd][:, CHUNK:], 0.0) for hd in heads]
    xx = [jnp.concatenate([_bmm(a_abk[hd], jnp.concatenate([zero_v, vc[hd]], axis=1)), ah[hd]],
                          axis=2) for hd in heads]
    ppad = [jnp.where(lane_c < CHUNK, a_abk[hd], 0.0) for hd in heads]
    for j in range(n_sq):
        for hd in heads:
            pj = ppad[hd][:, :, :CHUNK]
            if j + 1 < n_sq:
                res = _bmm(pj, jnp.concatenate([xx[hd], ppad[hd]], axis=2))
                xx[hd] = xx[hd] + res[:, :, :2 * HEAD_PAD]
                ppad[hd] = res[:, :, 2 * HEAD_PAD:]
            else:
                xx[hd] = xx[hd] + _bmm(pj, xx[hd])
    qm, y0, n0 = [], [], []
    for hd in heads:
        lhs = jnp.concatenate([r_bk[hd], jnp.swapaxes(bk_last[hd], 1, 2)], axis=1)
        rhs = jnp.concatenate([xx[hd], jnp.concatenate([vc[hd], zero_v], axis=2)], axis=1)
        out = _bmm(lhs, rhs)
        y0.append(out[:, :CHUNK, :HEAD_PAD])
        n0.append(out[:, CHUNK:, :HEAD_PAD])
        q = rh[hd] + out[:, :CHUNK, HEAD_PAD:]
        m = jnp.where(eye_k, jnp.exp(last[hd]), 0.0) + out[:, CHUNK:, HEAD_PAD:]
        qm.append(jnp.concatenate([q, m], axis=1).astype(BF16))

    hs = [hstate_s[hd] for hd in heads]
    ys = [[] for _ in heads]
    for c in range(nc):
        for hd in heads:
            res = jnp.dot(qm[hd][c], hs[hd].astype(BF16), preferred_element_type=F32)
            ys[hd].append(res[:CHUNK] + y0[hd][c])
            hs[hd] = res[CHUNK:] + n0[hd][c]

    for hd, cs in enumerate(hsl):
        hstate_s[hd] = hs[hd]
        y = jnp.concatenate(ys[hd], axis=0)
        mean = jnp.sum(y, axis=-1, keepdims=True) / RWKV_HEAD_DIM
        dlt = jnp.where(real, y - mean, 0.0)
        var = jnp.sum(dlt * dlt, axis=-1, keepdims=True) / RWKV_HEAD_DIM
        yn = dlt * lax.rsqrt(var + RWKV_GN_EPS) * gn_g[:, cs] + gn_b[:, cs]
        oa_ref[:, cs] = ((yn + bonus[hd]) * gate[:, cs]).astype(BF16)


def _rwkv(pa, mu, vecs, wd, wa, wg, *, tb):
    b, lp, _ = pa.shape
    tile = lambda bi, i: (bi, i, 0)
    const = lambda bi, i: (0, 0)
    halo = lambda bi, i: (bi, jnp.maximum(i * (tb // SUBLANES) - 1, 0), 0)
    return pl.pallas_call(
        _rwkv_kernel,
        grid=(b, lp // tb),
        in_specs=[
            pl.BlockSpec((None, tb, PA_COLS), tile),
            pl.BlockSpec((None, SUBLANES, PA_COLS), halo),
            pl.BlockSpec((1, PA_COLS), const),
            pl.BlockSpec((SUBLANES, RWKV_PAD), const),
            pl.BlockSpec((LORA_COLS, RWKV_PAD), const),
            pl.BlockSpec((LORA_COLS, RWKV_PAD), const),
            pl.BlockSpec((LORA_COLS, RWKV_PAD), const),
        ],
        out_specs=pl.BlockSpec((None, tb, RWKV_PAD), tile),
        out_shape=jax.ShapeDtypeStruct((b, lp, RWKV_PAD), BF16),
        scratch_shapes=[pltpu.VMEM((RWKV_HEADS, HEAD_PAD, HEAD_PAD), F32)],
        compiler_params=_params(("arbitrary", "arbitrary")),
        name="rwkv7",
    )(pa, pa, mu, vecs, wd, wa, wg)


ATTN_HEADS_PER_STEP = 4


def _attn_kernel(q_ref, k_ref, vt_ref, lamv_ref, sg_ref, o_ref, acc_s, st_s, p_s, *, lam_init):
    tq = o_ref.shape[0]
    tk = vt_ref.shape[3]
    qi = pl.program_id(2)
    heads = range(ATTN_HEADS_PER_STEP)
    hl = [slice(h * LANES, (h + 1) * LANES) for h in heads]

    def scores(h, ki):
        k = k_ref[pl.ds(pl.multiple_of(ki * tk, tk), tk), hl[h]]
        return lax.dot_general(k, q_ref[:, :, hl[h]].reshape(2 * tq, LANES),
                               (((1,), (1,)), ((), ())), preferred_element_type=F32)

    def values(h, ki, p, alpha):
        acc_s[h] = alpha * acc_s[h] + jnp.dot(vt_ref[ki, h], p, preferred_element_type=F32)

    st = [scores(h, qi) for h in heads]

    @pl.when(qi > 0)
    def _():
        for h in heads:
            st_s[h, 0] = scores(h, 0)

    kr = lax.broadcasted_iota(jnp.int32, (tk, 2 * tq), 0)
    qc = lax.broadcasted_iota(jnp.int32, (tk, 2 * tq), 1)
    qc = jnp.where(qc >= tq, qc - tq, qc)
    m0 = []
    for h in heads:
        s_h = jnp.where(kr <= qc, st[h], NEG_INF)
        m0.append(jnp.max(s_h, axis=0, keepdims=True))
        p_s[h, 1] = jnp.exp2(s_h - m0[h]).astype(BF16)
        acc_s[h] = jnp.zeros(acc_s.shape[1:], acc_s.dtype)

    def trip(k, carry, slot):
        out = []
        nxt = jnp.minimum(k + 1, qi - 1)
        prv = jnp.where(k == 0, qi, k - 1)
        for h in heads:
            st_s[h, 1 - slot] = scores(h, nxt)
        for h in heads:
            m_prev, alpha_prev = carry[h]
            s_k = st_s[h, slot]
            m_new = jnp.maximum(m_prev, jnp.max(s_k, axis=0, keepdims=True))
            alpha = jnp.exp2(m_prev - m_new)
            values(h, prv, p_s[h, 1 - slot], alpha_prev)
            p_s[h, slot] = jnp.exp2(s_k - m_new).astype(BF16)
            out.append((m_new, alpha))
        return tuple(out)

    def two_trips(j, c):
        c = trip(2 * j, c, 0)
        return lax.cond(2 * j + 1 < qi, lambda cc: trip(2 * j + 1, cc, 1), lambda cc: cc, c)

    init = tuple((m0[h], jnp.ones_like(m0[h])) for h in heads)
    carry = lax.fori_loop(0, (qi + 1) // 2, two_trips, init)
    odd = qi % 2 == 1

    @pl.when(odd)
    def _():
        for h in heads:
            values(h, qi - 1, p_s[h, 0], carry[h][1])

    @pl.when(jnp.logical_not(odd))
    def _():
        for h in heads:
            values(h, jnp.where(qi == 0, 0, qi - 1), p_s[h, 1], carry[h][1])

    lv = lamv_ref[...]
    lam = (jnp.exp(jnp.sum(lv[0:1] * lv[1:2], axis=-1, keepdims=True))
           - jnp.exp(jnp.sum(lv[2:3] * lv[3:4], axis=-1, keepdims=True)) + lam_init)
    for h in heads:
        acc = acc_s[h]
        ot = acc[:DIFF_V_DIM] / acc[DIFF_V_DIM:DIFF_V_DIM + 1]
        od = (ot[:, :tq] - lam * ot[:, tq:]).T
        ms = jnp.mean(od * od, axis=-1, keepdims=True)
        o_ref[:, hl[h]] = (od * lax.rsqrt(ms + SUBLN_EPS) * sg_ref[...]
                           * (1.0 - lam_init)).astype(BF16)


def _attention(qs, kh, vt, lamv, sg, *, tq, lam_init):
    _, b, lp, _ = qs.shape
    nq = lp // tq
    hp = ATTN_HEADS_PER_STEP
    const = lambda bi, hg, qi: (0, 0)
    return pl.pallas_call(
        functools.partial(_attn_kernel, lam_init=lam_init),
        grid=(b, DIFF_HEADS // hp, nq),
        in_specs=[
            pl.BlockSpec((2, None, tq, hp * LANES), lambda bi, hg, qi: (0, bi, qi, hg)),
            pl.BlockSpec((None, lp, hp * LANES), lambda bi, hg, qi: (bi, 0, hg)),
            pl.BlockSpec((None, nq, hp, VT_ROWS, tq), lambda bi, hg, qi: (bi, 0, hg, 0, 0)),
            pl.BlockSpec((SUBLANES, LANES), const),
            pl.BlockSpec((1, LANES), const),
        ],
        out_specs=pl.BlockSpec((None, tq, hp * LANES), lambda bi, hg, qi: (bi, qi, hg)),
        out_shape=jax.ShapeDtypeStruct((b, lp, DIFF_WIDTH), BF16),
        scratch_shapes=[pltpu.VMEM((hp, VT_ROWS, 2 * tq), F32),
                        pltpu.VMEM((hp, 2, tq, 2 * tq), F32),
                        pltpu.VMEM((hp, 2, tq, 2 * tq), BF16)],
        compiler_params=_params(("arbitrary", "arbitrary", "arbitrary")),
        name="diffattn",
    )(qs, kh, vt, lamv, sg)


def _outproj_kernel(h_ref, oa_ref, ob_ref, pc_ref, pch_ref, cw_ref, wa_ref, wb_ref, wc_ref,
                    o_ref, *, tiles_per_batch):
    tm = h_ref.shape[0]
    first = (pl.program_id(0) % tiles_per_batch) == 0
    pc = pc_ref[...]
    bg = pc[:, :CONV_WIDTH]
    z = pc[:, CONV_WIDTH:2 * CONV_WIDTH] * pc[:, 2 * CONV_WIDTH:]
    ph = pch_ref[...]
    zh = jnp.where(first, 0.0, ph[:, CONV_WIDTH:2 * CONV_WIDTH] * ph[:, 2 * CONV_WIDTH:])
    row = lax.broadcasted_iota(jnp.int32, (tm, 1), 0)
    z1 = jnp.where(row == 0, zh[SUBLANES - 1:SUBLANES], pltpu.roll(z, 1, 0))
    z2 = jnp.where(row == 0, zh[SUBLANES - 2:SUBLANES - 1],
                   jnp.where(row == 1, zh[SUBLANES - 1:SUBLANES], pltpu.roll(z, 2, 0)))
    cw = cw_ref[...]
    y = cw[0:1] * z2 + cw[1:2] * z1 + cw[2:3] * z
    oc = (bg * y).astype(BF16)
    o_ref[...] = (h_ref[...]
                  + jnp.dot(oa_ref[...], wa_ref[...], preferred_element_type=F32)
                  + jnp.dot(ob_ref[...], wb_ref[...], preferred_element_type=F32)
                  + jnp.dot(oc, wc_ref[...], preferred_element_type=F32))


def _outproj(h, oa, ob, pc, cw, wa, wb, wc, *, tm, lp):
    r, d = h.shape
    row = lambda i: (i, 0)
    const = lambda i: (0, 0)
    halo = lambda i: (jnp.maximum(i * (tm // SUBLANES) - 1, 0), 0)
    return pl.pallas_call(
        functools.partial(_outproj_kernel, tiles_per_batch=lp // tm),
        grid=(r // tm,),
        in_specs=[
            pl.BlockSpec((tm, d), row),
            pl.BlockSpec((tm, RWKV_PAD), row),
            pl.BlockSpec((tm, DIFF_WIDTH), row),
            pl.BlockSpec((tm, PC_COLS), row),
            pl.BlockSpec((SUBLANES, PC_COLS), halo),
            pl.BlockSpec((SUBLANES, CONV_WIDTH), const),
            pl.BlockSpec(wa.shape, const),
            pl.BlockSpec(wb.shape, const),
            pl.BlockSpec(wc.shape, const),
        ],
        out_specs=pl.BlockSpec((tm, d), row),
        out_shape=jax.ShapeDtypeStruct((r, d), F32),
        compiler_params=_params(("arbitrary",)),
        name="outproj",
    )(h, oa, ob, pc, pc, cw, wa, wb, wc)


def _swiglu_block(xb, wg_ref, wu_ref, wd_ref, side_work=None):
    d_ff = wg_ref.shape[1]
    acc = jnp.zeros((xb.shape[0], wd_ref.shape[1]), F32)
    for c in range(d_ff // MXU_DIM):
        cs = slice(c * MXU_DIM, (c + 1) * MXU_DIM)
        gt = jnp.dot(xb, wg_ref[:, cs], preferred_element_type=F32)
        up = jnp.dot(xb, wu_ref[:, cs], preferred_element_type=F32)
        act = (jax.nn.silu(gt) * up).astype(BF16)
        acc = acc + jnp.dot(act, wd_ref[cs, :], preferred_element_type=F32)
        if side_work is not None:
            side_work(c)
    return acc


def _rms(x, g):
    ms = jnp.mean(x * x, axis=-1, keepdims=True)
    return x * lax.rsqrt(ms + NORM_EPS) * g


def _ffn_kernel(h_ref, g_ref, wg_ref, wu_ref, wd_ref, o_ref):
    x = h_ref[...]
    xn = _rms(x, g_ref[...]).astype(BF16)
    o_ref[...] = x + _swiglu_block(xn, wg_ref, wu_ref, wd_ref)


def _dense_ffn(h, g, wg, wu, wd, *, tm):
    r, d = h.shape
    row = lambda i: (i, 0)
    const = lambda i: (0, 0)
    return pl.pallas_call(
        _ffn_kernel,
        grid=(r // tm,),
        in_specs=[
            pl.BlockSpec((tm, d), row),
            pl.BlockSpec((1, d), const),
            pl.BlockSpec(wg.shape, const),
            pl.BlockSpec(wu.shape, const),
            pl.BlockSpec(wd.shape, const),
        ],
        out_specs=pl.BlockSpec((tm, d), row),
        out_shape=jax.ShapeDtypeStruct((r, d), F32),
        compiler_params=_params(("arbitrary",)),
        name="dense_ffn",
    )(h, g, wg, wu, wd)


def _router_kernel(h_ref, g_ref, rw_ref, hn_ref, route_ref):
    tm = h_ref.shape[0]
    xn = _rms(h_ref[...], g_ref[...])
    hn_ref[...] = xn
    logits = _dot_f32(xn, rw_ref[...])
    lane = lax.broadcasted_iota(jnp.int32, (tm, LANES), 1)
    logits = jnp.where(lane < N_EXPERTS, logits, -jnp.inf)
    m1 = jnp.max(logits, axis=-1, keepdims=True)
    i1 = jnp.min(jnp.where(logits == m1, lane, LANES), axis=-1, keepdims=True)
    rest = jnp.where(lane == i1, -jnp.inf, logits)
    m2 = jnp.max(rest, axis=-1, keepdims=True)
    i2 = jnp.min(jnp.where(rest == m2, lane, LANES), axis=-1, keepdims=True)
    e2 = jnp.exp(m2 - m1)
    den = 1.0 + e2
    route_ref[...] = jnp.where(lane == 0, i1.astype(F32),
                     jnp.where(lane == 1, i2.astype(F32),
                     jnp.where(lane == 2, 1.0 / den,
                     jnp.where(lane == 3, e2 / den, 0.0))))


def _router(h, g, rw, *, tm):
    r, d = h.shape
    row = lambda i: (i, 0)
    const = lambda i: (0, 0)
    return pl.pallas_call(
        _router_kernel,
        grid=(r // tm,),
        in_specs=[pl.BlockSpec((tm, d), row), pl.BlockSpec((1, d), const),
                  pl.BlockSpec((d, LANES), const)],
        out_specs=[pl.BlockSpec((tm, d), row), pl.BlockSpec((tm, LANES), row)],
        out_shape=[jax.ShapeDtypeStruct((r, d), F32), jax.ShapeDtypeStruct((r, LANES), F32)],
        compiler_params=_params(("arbitrary",)),
        name="router",
    )(h, g, rw)


def _row_copy(src, dst, sem, src_row, dst_row):
    return pltpu.make_async_copy(src.at[pl.ds(src_row, 1), :], dst.at[pl.ds(dst_row, 1), :], sem)


DMA_UNROLL = 8
DMA_ISSUE_SLICES = 8
MOE_BUFFERS = 3


def _moe_kernel(bexp_ref, rsrc_ref, rdst_ref, hn_hbm, wg_ref, wu_ref, wd_ref, y_hbm,
                xbuf, ybuf, gsem, ssem):
    del bexp_ref
    nbuf = xbuf.shape[0]
    blk = xbuf.shape[1]
    i = pl.program_id(0)
    nb = pl.num_programs(0)
    slot = i % nbuf
    slot_prev = (i + nbuf - 1) % nbuf
    slot_next2 = (i + 2) % nbuf
    dump0 = y_hbm.shape[0] - nbuf * blk

    def for_rows(fn):
        def body(j, c):
            for u in range(DMA_UNROLL):
                fn(j * DMA_UNROLL + u)
            return c
        lax.fori_loop(0, blk // DMA_UNROLL, body, 0)

    def gather(block, s, j):
        return _row_copy(hn_hbm, xbuf.at[s], gsem.at[s], rsrc_ref[block * blk + j], j)

    def scatter(dst_row, s, j):
        return _row_copy(ybuf.at[s], y_hbm, ssem.at[s], j, dst_row)

    @pl.when(i == 0)
    def _():
        for_rows(lambda j: gather(0, 0, j).start())
        for_rows(lambda j: gather(1, 1, j).start())
        ybuf[nbuf - 1] = jnp.zeros(ybuf.shape[1:], ybuf.dtype)
        for k in range(nbuf - 1):
            zero_fill = pltpu.make_async_copy(
                ybuf.at[nbuf - 1], y_hbm.at[pl.ds(dump0 + k * blk, blk), :], ssem.at[k])
            zero_fill.start()
            zero_fill.wait()

    for_rows(lambda j: gather(0, slot, j).wait())

    @pl.when(i >= 2)
    def _():
        for_rows(lambda j: scatter(0, slot, j).wait())

    ahead = jnp.minimum(i + 2, nb - 1)
    prev = jnp.maximum(i - 1, 0)
    per = -(-blk // DMA_ISSUE_SLICES)

    def side_work(c):
        for j in range(c * per, min((c + 1) * per, blk)):
            gather(ahead, slot_next2, j).start(priority=j % 2)
            dst = jnp.where(i == 0, dump0 + (nbuf - 1) * blk + j, rdst_ref[prev * blk + j])
            scatter(dst, slot_prev, j).start(priority=(j + 1) % 2)

    ybuf[slot] = _swiglu_block(xbuf[slot].astype(BF16), wg_ref, wu_ref, wd_ref, side_work)

    @pl.when(i == nb - 1)
    def _():
        for_rows(lambda j: gather(0, (i + 1) % nbuf, j).wait())
        for_rows(lambda j: gather(0, slot_next2, j).wait())
        for_rows(lambda j: scatter(0, (i + 1) % nbuf, j).wait())
        for_rows(lambda j: scatter(0, slot_prev, j).wait())
        for_rows(lambda j: scatter(rdst_ref[i * blk + j], slot, j).start())
        for_rows(lambda j: scatter(0, slot, j).wait())


def _moe_experts(block_expert, row_src, row_dst, hn, wg, wu, wd):
    nb = block_expert.shape[0]
    r, d = hn.shape
    d_ff = wg.shape[2]
    wmap = lambda i, be, rs, rd: (be[i], 0, 0)
    grid_spec = pltpu.PrefetchScalarGridSpec(
        num_scalar_prefetch=3,
        grid=(nb,),
        in_specs=[
            pl.BlockSpec(memory_space=pl.ANY),
            pl.BlockSpec((None, d, d_ff), wmap),
            pl.BlockSpec((None, d, d_ff), wmap),
            pl.BlockSpec((None, d_ff, d), wmap),
        ],
        out_specs=pl.BlockSpec(memory_space=pl.ANY),
        scratch_shapes=[pltpu.VMEM((MOE_BUFFERS, MOE_BLOCK, d), F32),
                        pltpu.VMEM((MOE_BUFFERS, MOE_BLOCK, d), F32),
                        pltpu.SemaphoreType.DMA((MOE_BUFFERS,)),
                        pltpu.SemaphoreType.DMA((MOE_BUFFERS,))],
    )
    return pl.pallas_call(
        _moe_kernel,
        grid_spec=grid_spec,
        out_shape=jax.ShapeDtypeStruct((TOP_K * r + MOE_BUFFERS * MOE_BLOCK, d), F32),
        compiler_params=_params(("arbitrary",)),
        name="moe_experts",
    )(block_expert, row_src, row_dst, hn, wg, wu, wd)


def _combine_kernel(h_ref, route_ref, y0_ref, y1_ref, o_ref):
    route = route_ref[...]
    o_ref[...] = h_ref[...] + route[:, 2:3] * y0_ref[...] + route[:, 3:4] * y1_ref[...]


def _moe_combine(h, route, y, *, tm):
    r, d = h.shape
    nt = r // tm
    row = lambda i: (i, 0)
    return pl.pallas_call(
        _combine_kernel,
        grid=(nt,),
        in_specs=[
            pl.BlockSpec((tm, d), row),
            pl.BlockSpec((tm, LANES), row),
            pl.BlockSpec((tm, d), row),
            pl.BlockSpec((tm, d), lambda i: (i + nt, 0)),
        ],
        out_specs=pl.BlockSpec((tm, d), row),
        out_shape=jax.ShapeDtypeStruct((r, d), F32),
        compiler_params=_params(("arbitrary",)),
        name="moe_combine",
    )(h, route, y, y)


def _slot_owner_kernel(dest_ref, owner_ref):
    n_pairs = dest_ref.shape[0]
    n_slots = owner_ref.shape[0]

    def clear(i, c):
        for u in range(DMA_UNROLL):
            owner_ref[i * DMA_UNROLL + u] = -1
        return c

    lax.fori_loop(0, n_slots // DMA_UNROLL, clear, 0)

    def place(i, c):
        for u in range(DMA_UNROLL):
            p = i * DMA_UNROLL + u
            owner_ref[dest_ref[p]] = p
        return c

    lax.fori_loop(0, n_pairs // DMA_UNROLL, place, 0)


def _slot_owner(dest, n_slots):
    assert dest.shape[0] % DMA_UNROLL == 0 and n_slots % DMA_UNROLL == 0
    smem = pl.BlockSpec(memory_space=pltpu.SMEM)
    return pl.pallas_call(
        _slot_owner_kernel,
        in_specs=[smem],
        out_specs=smem,
        out_shape=jax.ShapeDtypeStruct((n_slots,), jnp.int32),
        name="slot_owner",
    )(dest)


def _routing_tables(route):
    r = route.shape[0]
    tk = r * TOP_K
    flat_e = route[:, :TOP_K].reshape(tk).astype(jnp.int32)
    onehot = (flat_e[:, None] == jnp.arange(N_EXPERTS, dtype=jnp.int32)[None, :]).astype(jnp.int32)
    csum = jnp.cumsum(onehot, axis=0)
    rank = jnp.sum(csum * onehot, axis=1) - 1
    counts = csum[-1]
    padded = (counts + MOE_BLOCK - 1) // MOE_BLOCK * MOE_BLOCK
    pend = jnp.cumsum(padded)
    pstart = pend - padded
    dest = pstart[flat_e] + rank
    nb = -(-tk // MOE_BLOCK) + N_EXPERTS
    n_slots = nb * MOE_BLOCK
    pair = _slot_owner(dest, n_slots)
    used = pair >= 0
    row_src = jnp.where(used, pair // TOP_K, 0)
    slot_id = jnp.arange(n_slots, dtype=jnp.int32)
    dump_row = tk + (slot_id // MOE_BLOCK) % MOE_BUFFERS * MOE_BLOCK + slot_id % MOE_BLOCK
    row_dst = jnp.where(used, (pair % TOP_K) * r + pair // TOP_K, dump_row)
    block_start = jnp.arange(nb, dtype=jnp.int32) * MOE_BLOCK
    block_expert = jnp.minimum(jnp.searchsorted(pend, block_start, side='right'),
                               N_EXPERTS - 1).astype(jnp.int32)
    return block_expert, row_src, row_dst


def _pad_heads(t, axis):
    axis = axis % t.ndim
    shp = t.shape
    t = t.reshape(shp[:axis] + (RWKV_HEADS, RWKV_HEAD_DIM) + shp[axis + 1:])
    pad = [(0, 0)] * t.ndim
    pad[axis + 1] = (0, HEAD_PAD - RWKV_HEAD_DIM)
    t = jnp.pad(t, pad)
    return t.reshape(shp[:axis] + (RWKV_PAD,) + shp[axis + 1:])


def _cast_kernel(x_ref, o_ref):
    o_ref[...] = x_ref[...].astype(o_ref.dtype)


def _to_bf16(w):
    e, a, b = w.shape
    spec = pl.BlockSpec((None, a // 2, b), lambda i, j: (i, j, 0))
    return pl.pallas_call(
        _cast_kernel,
        grid=(e, 2),
        in_specs=[spec],
        out_specs=spec,
        out_shape=jax.ShapeDtypeStruct(w.shape, BF16),
        compiler_params=_params(("arbitrary", "arbitrary")),
        name="to_bf16",
    )(w)


def _rope_tables(lp):
    half = ROPE_DIM // 2
    inv_freq = ROPE_THETA ** (-jnp.arange(0, ROPE_DIM, 2, dtype=F32) / ROPE_DIM)
    ang = jnp.arange(lp, dtype=F32)[:, None] * inv_freq[None, :]
    cos, sin = jnp.cos(ang), jnp.sin(ang)
    ones = jnp.ones((lp, DIFF_QK_DIM - ROPE_DIM), F32)
    zeros = jnp.zeros((lp, DIFF_QK_DIM - ROPE_DIM), F32)
    zh = jnp.zeros((lp, half), F32)
    rc = jnp.concatenate([cos, cos, ones], axis=1)
    rs1 = jnp.concatenate([-sin, zh, zeros], axis=1)
    rs2 = jnp.concatenate([zh, sin, zeros], axis=1)
    tile2 = lambda a: jnp.concatenate([a, a], axis=1)
    return tile2(rc), tile2(rs1), tile2(rs2)


def kernel(x, meta_tokens, mix_norm_g, w_in, tm_mu, tm_w0, tm_w_decay_up, tm_a0, tm_w_a_up, tm_w_g_up, tm_k_k, tm_k_a, tm_r_k, tm_gn_g, tm_gn_b, da_q_norm_g, da_k_norm_g, da_lambda_q1, da_lambda_k1, da_lambda_q2, da_lambda_k2, da_subln_g, sc_conv_w, w_out, ffn_norm_g, ffn_w_gate, ffn_w_up, ffn_w_down, router_w, moe_w_gate, moe_w_up, moe_w_down):
    b, seq, d = x.shape
    depth = w_in.shape[0]
    l = N_META + seq
    lp = -(-l // LANES) * LANES
    tm = _row_tile(lp)
    r = b * lp
    s1 = RWKV_WIDTH

    meta = jnp.broadcast_to(meta_tokens.astype(x.dtype)[None], (b, N_META, d))
    h = jnp.concatenate([meta, x, jnp.zeros((b, lp - l, d), x.dtype)], axis=1).reshape(r, d)
    rc, rs1, rs2 = _rope_tables(lp)

    for i in range(depth):
        lam_init = 0.8 - 0.6 * math.exp(-0.3 * i)
        wi = w_in[i]
        w_cat = jnp.concatenate(
            [_pad_heads(wi[:, 0:s1], 1), _pad_heads(wi[:, s1:2 * s1], 1),
             _pad_heads(wi[:, 2 * s1:3 * s1], 1), wi[:, 3 * s1:]], axis=1).astype(BF16)
        qg = jnp.tile(da_q_norm_g[i], 2)[None]
        kg = jnp.tile(da_k_norm_g[i], 2)[None]
        pa, qs, kh, vt, pc = _inproj(h, mix_norm_g[i][None], w_cat, rc, rs1, rs2, qg, kg,
                                     tm=tm, lp=lp)

        mu = tm_mu[i]
        mu_p = jnp.concatenate([_pad_heads(mu[0:s1], 0), _pad_heads(mu[s1:2 * s1], 0),
                                _pad_heads(mu[2 * s1:3 * s1], 0), mu[3 * s1:]])[None]
        vecs = jnp.stack([_pad_heads(tm_w0[i], 0), _pad_heads(tm_a0[i], 0),
                          _pad_heads(tm_k_k[i], 0), _pad_heads(tm_k_a[i], 0),
                          _pad_heads(tm_r_k[i].reshape(s1), 0), _pad_heads(tm_gn_g[i], 0),
                          _pad_heads(tm_gn_b[i], 0), jnp.zeros((RWKV_PAD,), F32)])
        zl = lambda n: jnp.zeros((n, RWKV_PAD), F32)
        wd_p = jnp.concatenate([_pad_heads(tm_w_decay_up[i], 1), zl(AAA_LORA + GATE_LORA)], axis=0)
        wa_p = jnp.concatenate([zl(DECAY_LORA), _pad_heads(tm_w_a_up[i], 1), zl(GATE_LORA)], axis=0)
        wg_p = jnp.concatenate([zl(DECAY_LORA + AAA_LORA), _pad_heads(tm_w_g_up[i], 1)], axis=0)
        oa = _rwkv(pa.reshape(b, lp, PA_COLS), mu_p, vecs, wd_p, wa_p, wg_p, tb=tm)

        lamv = jnp.zeros((SUBLANES, LANES), F32).at[0:4, :DIFF_QK_DIM].set(
            jnp.stack([da_lambda_q1[i], da_lambda_k1[i], da_lambda_q2[i], da_lambda_k2[i]]))
        ob = _attention(qs.reshape(2, b, lp, DIFF_WIDTH), kh.reshape(b, lp, DIFF_WIDTH),
                        vt.reshape(b, lp // tm, DIFF_HEADS, VT_ROWS, tm), lamv, da_subln_g[i][None],
                        tq=tm, lam_init=lam_init)

        wo = w_out[i]
        cw = jnp.zeros((SUBLANES, CONV_WIDTH), F32).at[:CONV_K].set(sc_conv_w[i])
        h = _outproj(h, oa.reshape(r, RWKV_PAD), ob.reshape(r, DIFF_WIDTH), pc, cw,
                     _pad_heads(wo[:s1], 0).astype(BF16), wo[s1:s1 + DIFF_WIDTH].astype(BF16),
                     wo[s1 + DIFF_WIDTH:].astype(BF16), tm=tm, lp=lp)

        j = i // 2
        if i % 2 == 0:
            h = _dense_ffn(h, ffn_norm_g[i][None], ffn_w_gate[j].astype(BF16),
                           ffn_w_up[j].astype(BF16), ffn_w_down[j].astype(BF16), tm=tm)
        else:
            rw = jnp.zeros((d, LANES), F32).at[:, :N_EXPERTS].set(router_w[j])
            hn, route = _router(h, ffn_norm_g[i][None], rw, tm=tm)
            block_expert, row_src, row_dst = _routing_tables(route)
            y = _moe_experts(block_expert, row_src, row_dst, hn, _to_bf16(moe_w_gate[j]),
                             _to_bf16(moe_w_up[j]), _to_bf16(moe_w_down[j]))
            h = _moe_combine(h, route, y, tm=tm)

    return h.reshape(b, lp, d)[:, N_META:l]
```

```python
import functools
import math

import jax
import jax.numpy as jnp
from jax import lax
from jax.experimental import pallas as pl
from jax.experimental.pallas import tpu as pltpu

N_META = 16
RWKV_HEAD_DIM = 64
RWKV_HEADS = 4
RWKV_WIDTH = RWKV_HEADS * RWKV_HEAD_DIM
DECAY_LORA = 32
AAA_LORA = 32
GATE_LORA = 64
LORA_COLS = DECAY_LORA + AAA_LORA + GATE_LORA
DIFF_QK_DIM = 64
DIFF_V_DIM = 128
DIFF_HEADS = 4
DIFF_WIDTH = DIFF_HEADS * DIFF_V_DIM
CONV_WIDTH = 256
CONV_K = 3
ROPE_THETA = 500000.0
ROPE_DIM = DIFF_QK_DIM // 4
N_EXPERTS = 8
TOP_K = 2
MOE_BLOCK = 256
NORM_EPS = 1e-6
RWKV_GN_EPS = 64e-5
SUBLN_EPS = 1e-5
NEG_INF = -1e30

LANES = 128
SUBLANES = 8
MXU_DIM = 256
VMEM_LIMIT_BYTES = 56 * 1024 * 1024

HEAD_PAD = LANES
RWKV_PAD = RWKV_HEADS * HEAD_PAD
PA_COLS = 3 * RWKV_PAD + LORA_COLS
PB_COLS = 3 * DIFF_WIDTH
PC_COLS = 3 * CONV_WIDTH
VT_ROWS = DIFF_V_DIM + 16
LOG2_E = math.log2(math.e)
CHUNK = 64

F32 = jnp.float32
BF16 = jnp.bfloat16
HIGHEST = lax.Precision.HIGHEST


def _dot(a, b):
    return jnp.dot(a.astype(BF16), b.astype(BF16), preferred_element_type=F32)


def _dot_nt(a, b):
    return lax.dot_general(a.astype(BF16), b.astype(BF16), (((1,), (1,)), ((), ())),
                           preferred_element_type=F32)


def _dot_f32(a, b):
    return jnp.dot(a, b, preferred_element_type=F32, precision=HIGHEST)


def _row_tile(lp):
    for t in (384, 256, 128):
        if lp % t == 0:
            return t
    raise ValueError(f"padded length {lp} is not a multiple of {LANES}")


def _params(sem):
    return pltpu.CompilerParams(dimension_semantics=sem, vmem_limit_bytes=VMEM_LIMIT_BYTES)


def _inproj_kernel(h_ref, g_ref, w_ref, rc_ref, rs1_ref, rs2_ref, qg_ref, kg_ref,
                   pa_ref, qs_ref, kh_ref, vt_ref, pc_ref):
    tm = h_ref.shape[0]
    x = h_ref[...]
    ms = jnp.mean(x * x, axis=-1, keepdims=True)
    xn = (x * lax.rsqrt(ms + NORM_EPS) * g_ref[...]).astype(BF16)
    pa_ref[...] = jnp.dot(xn, w_ref[:, :PA_COLS], preferred_element_type=F32)
    pc_ref[...] = jnp.dot(xn, w_ref[:, PA_COLS + PB_COLS:], preferred_element_type=F32)
    pb = jnp.dot(xn, w_ref[:, PA_COLS:PA_COLS + PB_COLS], preferred_element_type=F32)

    lane = lax.broadcasted_iota(jnp.int32, (tm, LANES), 1)
    lo = lane < DIFF_QK_DIM
    rc, rs1, rs2 = rc_ref[...], rs1_ref[...], rs2_ref[...]

    def norm_rope(t, g):
        t2 = t * t
        ss_lo = jnp.sum(jnp.where(lo, t2, 0.0), axis=-1, keepdims=True)
        ss_hi = jnp.sum(jnp.where(lo, 0.0, t2), axis=-1, keepdims=True)
        inv = jnp.where(lo, lax.rsqrt(ss_lo / DIFF_QK_DIM + NORM_EPS),
                        lax.rsqrt(ss_hi / DIFF_QK_DIM + NORM_EPS))
        tn = t * inv * g
        half = ROPE_DIM // 2
        return tn * rc + pltpu.roll(tn, LANES - half, 1) * rs1 + pltpu.roll(tn, half, 1) * rs2

    for hd in range(DIFF_HEADS):
        cs = slice(hd * LANES, (hd + 1) * LANES)
        q = norm_rope(pb[:, cs], qg_ref[...]) * (DIFF_QK_DIM ** -0.5 * LOG2_E)
        qs_ref[0, :, cs] = jnp.where(lo, q, 0.0).astype(BF16)
        qs_ref[1, :, cs] = jnp.where(lo, 0.0, q).astype(BF16)
        kcs = slice(DIFF_WIDTH + hd * LANES, DIFF_WIDTH + (hd + 1) * LANES)
        kh_ref[:, cs] = norm_rope(pb[:, kcs], kg_ref[...]).astype(BF16)
        vcs = slice(2 * DIFF_WIDTH + hd * LANES, 2 * DIFF_WIDTH + (hd + 1) * LANES)
        vt_ref[0, hd, :DIFF_V_DIM] = pb[:, vcs].T.astype(BF16)
        vt_ref[0, hd, DIFF_V_DIM:] = jnp.ones((VT_ROWS - DIFF_V_DIM, tm), BF16)


def _inproj(h, g, w, rc, rs1, rs2, qg, kg, *, tm, lp):
    r, d = h.shape
    nt_b = lp // tm
    row = lambda i: (i, 0)
    const = lambda i: (0, 0)
    rope = lambda i: (i % nt_b, 0)
    return pl.pallas_call(
        _inproj_kernel,
        grid=(r // tm,),
        in_specs=[
            pl.BlockSpec((tm, d), row),
            pl.BlockSpec((1, d), const),
            pl.BlockSpec(w.shape, const),
            pl.BlockSpec((tm, LANES), rope),
            pl.BlockSpec((tm, LANES), rope),
            pl.BlockSpec((tm, LANES), rope),
            pl.BlockSpec((1, LANES), const),
            pl.BlockSpec((1, LANES), const),
        ],
        out_specs=[
            pl.BlockSpec((tm, PA_COLS), row),
            pl.BlockSpec((2, tm, DIFF_WIDTH), lambda i: (0, i, 0)),
            pl.BlockSpec((tm, DIFF_WIDTH), row),
            pl.BlockSpec((1, DIFF_HEADS, VT_ROWS, tm), lambda i: (i, 0, 0, 0)),
            pl.BlockSpec((tm, PC_COLS), row),
        ],
        out_shape=[
            jax.ShapeDtypeStruct((r, PA_COLS), F32),
            jax.ShapeDtypeStruct((2, r, DIFF_WIDTH), BF16),
            jax.ShapeDtypeStruct((r, DIFF_WIDTH), BF16),
            jax.ShapeDtypeStruct((r // tm, DIFF_HEADS, VT_ROWS, tm), BF16),
            jax.ShapeDtypeStruct((r, PC_COLS), F32),
        ],
        compiler_params=_params(("arbitrary",)),
        name="inproj",
    )(h, g, w, rc, rs1, rs2, qg, kg)


def _split3(x):
    hi = x.astype(BF16)
    r1 = x - hi.astype(F32)
    mid = r1.astype(BF16)
    lo = (r1 - mid.astype(F32)).astype(BF16)
    return hi, mid, lo


def _dot_hi(a, b):
    ah = a.astype(BF16)
    al = (a - ah.astype(F32)).astype(BF16)
    bh = b.astype(BF16)
    bl = (b - bh.astype(F32)).astype(BF16)
    d = lambda p, q: jnp.dot(p, q, preferred_element_type=F32)
    return d(ah, bh) + d(ah, bl) + d(al, bh)


def _bmm(a, b):
    return jnp.einsum('cik,ckj->cij', a.astype(BF16), b.astype(BF16), preferred_element_type=F32)


def _bmm_nt(a, b):
    return jnp.einsum('cik,cjk->cij', a.astype(BF16), b.astype(BF16), preferred_element_type=F32)


def _rwkv_kernel(pa_ref, halo_ref, mu_ref, vec_ref, wd_ref, wa_ref, wg_ref, oa_ref, hstate_s):
    tb = pa_ref.shape[0]
    nc = tb // CHUNK
    i = pl.program_id(1)

    @pl.when(i == 0)
    def _():
        hstate_s[...] = jnp.zeros_like(hstate_s)

    x = pa_ref[...]
    prev_row = jnp.where(i == 0, 0.0, halo_ref[SUBLANES - 1:SUBLANES, :])
    row = lax.broadcasted_iota(jnp.int32, (tb, 1), 0)
    prev = jnp.where(row == 0, prev_row, pltpu.roll(x, 1, 0))
    hx = x + (prev - x) * mu_ref[...]
    r = hx[:, 0:RWKV_PAD]
    k = hx[:, RWKV_PAD:2 * RWKV_PAD]
    v = hx[:, 2 * RWKV_PAD:3 * RWKV_PAD]
    lora = hx[:, 3 * RWKV_PAD:]
    w0, a0, k_k, k_a = vec_ref[0:1, :], vec_ref[1:2, :], vec_ref[2:3, :], vec_ref[3:4, :]
    r_k, gn_g, gn_b = vec_ref[4:5, :], vec_ref[5:6, :], vec_ref[6:7, :]

    w = -jax.nn.softplus(-(w0 + _dot_hi(jnp.tanh(lora), wd_ref[...]))) - 0.5
    wl = -jnp.exp(w)
    alr = jax.nn.sigmoid(a0 + _dot(lora, wa_ref[...]))
    gate = _dot(jax.nn.sigmoid(lora), wg_ref[...])
    kk = k * k_k
    k2 = k * (1.0 + (alr - 1.0) * k_a)
    rkr = r * k2 * r_k

    ri = lax.broadcasted_iota(jnp.int32, (CHUNK, CHUNK), 0)
    ci = lax.broadcasted_iota(jnp.int32, (CHUNK, CHUNK), 1)
    lane_c = lax.broadcasted_iota(jnp.int32, (CHUNK, HEAD_PAD), 1)
    row_c = lax.broadcasted_iota(jnp.int32, (CHUNK, HEAD_PAD), 0)
    col_c = jnp.where(lane_c >= CHUNK, lane_c - CHUNK, lane_c)
    low_incl2 = (col_c <= row_c)[None]
    low_strict2 = (col_c < row_c)[None]
    tril_b = jnp.where(ci <= ri, 1.0, 0.0).astype(BF16)
    eye_k = (lax.broadcasted_iota(jnp.int32, (HEAD_PAD, HEAD_PAD), 0)
             == lax.broadcasted_iota(jnp.int32, (HEAD_PAD, HEAD_PAD), 1))[None]
    n_sq = int(math.log2(CHUNK))

    parts = _split3(wl)
    cums = []
    for c in range(nc):
        rows = slice(c * CHUNK, (c + 1) * CHUNK)
        cums.append(sum(jnp.dot(tril_b, p[rows], preferred_element_type=F32) for p in parts))
    cum_all = jnp.concatenate(cums, axis=0)

    lane = lax.broadcasted_iota(jnp.int32, (tb, HEAD_PAD), 1)
    real = lane < RWKV_HEAD_DIM
    to3 = lambda t: t.reshape(nc, CHUNK, HEAD_PAD)
    heads = range(RWKV_HEADS)
    hsl = [slice(hd * HEAD_PAD, (hd + 1) * HEAD_PAD) for hd in heads]
    kkn, bonus = [], []
    for cs in hsl:
        kk_h = kk[:, cs]
        nrm = jnp.sqrt(jnp.sum(kk_h * kk_h, axis=-1, keepdims=True))
        kkn.append(kk_h / jnp.maximum(nrm, 1e-12))
        bonus.append(jnp.sum(rkr[:, cs], axis=-1, keepdims=True) * v[:, cs])

    vc = [to3(v[:, cs]) for cs in hsl]
    zero_v = jnp.zeros_like(vc[0])
    ah, rh, last, g, bk_last = [], [], [], [], []
    for hd, cs in enumerate(hsl):
        wl3, cum = to3(wl[:, cs]), to3(cum_all[:, cs])
        rr, kc = to3(r[:, cs]), to3(k2[:, cs])
        ac, bc = to3(-kkn[hd]), to3(kkn[hd] * alr[:, cs])
        last.append(cum[:, CHUNK - 1:CHUNK, :])
        e_neg = jnp.exp(-cum)
        e_last = jnp.exp(last[hd] - cum)
        ah.append(ac * jnp.exp(cum - wl3))
        rh.append(rr * jnp.exp(cum))
        bk_last.append(jnp.concatenate([bc * e_last, kc * e_last], axis=1))
        g.append(_bmm_nt(jnp.concatenate([ah[hd], rh[hd]], axis=1),
                         jnp.concatenate([bc * e_neg, kc * e_neg], axis=1)))
    a_abk = [jnp.where(low_strict2, g[hd][:, :CHUNK], 0.0) for hd in heads]
    r_bk = [jnp.where(low_incl2, g[hd][:, CHUNK:], 0.0) for hd in heads]
    xx = [jnp.concatenate([_bmm(a_abk[hd], jnp.concatenate([zero_v, vc[hd]], axis=1)), ah[hd]],
                          axis=2) for hd in heads]
    ppad = [jnp.where(lane_c < CHUNK, a_abk[hd], 0.0) for hd in heads]
    for j in range(n_sq):
        for hd in heads:
            pj = ppad[hd][:, :, :CHUNK]
            if j + 1 < n_sq:
                res = _bmm(pj, jnp.concatenate([xx[hd], ppad[hd]], axis=2))
                xx[hd] = xx[hd] + res[:, :, :2 * HEAD_PAD]
                ppad[hd] = res[:, :, 2 * HEAD_PAD:]
            else:
                xx[hd] = xx[hd] + _bmm(pj, xx[hd])
    qm, y0, n0 = [], [], []
    for hd in heads:
        lhs = jnp.concatenate([r_bk[hd], jnp.swapaxes(bk_last[hd], 1, 2)], axis=1)
        rhs = jnp.concatenate([xx[hd], jnp.concatenate([vc[hd], zero_v], axis=2)], axis=1)
        out = _bmm(lhs, rhs)
        y0.append(out[:, :CHUNK, :HEAD_PAD])
        n0.append(out[:, CHUNK:, :HEAD_PAD])
        q = rh[hd] + out[:, :CHUNK, HEAD_PAD:]
        m = jnp.where(eye_k, jnp.exp(last[hd]), 0.0) + out[:, CHUNK:, HEAD_PAD:]
        qm.append(jnp.concatenate([q, m], axis=1).astype(BF16))

    hs = [hstate_s[hd] for hd in heads]
    ys = [[] for _ in heads]
    for c in range(nc):
        for hd in heads:
            res = jnp.dot(qm[hd][c], hs[hd].astype(BF16), preferred_element_type=F32)
            ys[hd].append(res[:CHUNK] + y0[hd][c])
            hs[hd] = res[CHUNK:] + n0[hd][c]

    for hd, cs in enumerate(hsl):
        hstate_s[hd] = hs[hd]
        y = jnp.concatenate(ys[hd], axis=0)
        mean = jnp.sum(y, axis=-1, keepdims=True) / RWKV_HEAD_DIM
        dlt = jnp.where(real, y - mean, 0.0)
        var = jnp.sum(dlt * dlt, axis=-1, keepdims=True) / RWKV_HEAD_DIM
        yn = dlt * lax.rsqrt(var + RWKV_GN_EPS) * gn_g[:, cs] + gn_b[:, cs]
        oa_ref[:, cs] = ((yn + bonus[hd]) * gate[:, cs]).astype(BF16)


def _rwkv(pa, mu, vecs, wd, wa, wg, *, tb):
    b, lp, _ = pa.shape
    tile = lambda bi, i: (bi, i, 0)
    const = lambda bi, i: (0, 0)
    halo = lambda bi, i: (bi, jnp.maximum(i * (tb // SUBLANES) - 1, 0), 0)
    return pl.pallas_call(
        _rwkv_kernel,
        grid=(b, lp // tb),
        in_specs=[
            pl.BlockSpec((None, tb, PA_COLS), tile),
            pl.BlockSpec((None, SUBLANES, PA_COLS), halo),
            pl.BlockSpec((1, PA_COLS), const),
            pl.BlockSpec((SUBLANES, RWKV_PAD), const),
            pl.BlockSpec((LORA_COLS, RWKV_PAD), const),
            pl.BlockSpec((LORA_COLS, RWKV_PAD), const),
            pl.BlockSpec((LORA_COLS, RWKV_PAD), const),
        ],
        out_specs=pl.BlockSpec((None, tb, RWKV_PAD), tile),
        out_shape=jax.ShapeDtypeStruct((b, lp, RWKV_PAD), BF16),
        scratch_shapes=[pltpu.VMEM((RWKV_HEADS, HEAD_PAD, HEAD_PAD), F32)],
        compiler_params=_params(("arbitrary", "arbitrary")),
        name="rwkv7",
    )(pa, pa, mu, vecs, wd, wa, wg)


ATTN_HEADS_PER_STEP = 4


def _attn_kernel(q_ref, k_ref, vt_ref, lamv_ref, sg_ref, o_ref, acc_s, st_s, p_s, *, lam_init):
    tq = o_ref.shape[0]
    tk = vt_ref.shape[3]
    qi = pl.program_id(2)
    heads = range(ATTN_HEADS_PER_STEP)
    hl = [slice(h * LANES, (h + 1) * LANES) for h in heads]

    def scores(h, ki):
        k = k_ref[pl.ds(pl.multiple_of(ki * tk, tk), tk), hl[h]]
        return lax.dot_general(k, q_ref[:, :, hl[h]].reshape(2 * tq, LANES),
                               (((1,), (1,)), ((), ())), preferred_element_type=F32)

    def values(h, ki, p, alpha):
        acc_s[h] = alpha * acc_s[h] + jnp.dot(vt_ref[ki, h], p, preferred_element_type=F32)

    st = [scores(h, qi) for h in heads]

    @pl.when(qi > 0)
    def _():
        for h in heads:
            st_s[h, 0] = scores(h, 0)

    kr = lax.broadcasted_iota(jnp.int32, (tk, 2 * tq), 0)
    qc = lax.broadcasted_iota(jnp.int32, (tk, 2 * tq), 1)
    qc = jnp.where(qc >= tq, qc - tq, qc)
    m0 = []
    for h in heads:
        s_h = jnp.where(kr <= qc, st[h], NEG_INF)
        m0.append(jnp.max(s_h, axis=0, keepdims=True))
        p_s[h, 1] = jnp.exp2(s_h - m0[h]).astype(BF16)
        acc_s[h] = jnp.zeros(acc_s.shape[1:], acc_s.dtype)

    def trip(k, carry, slot):
        out = []
        nxt = jnp.minimum(k + 1, qi - 1)
        prv = jnp.where(k == 0, qi, k - 1)
        for h in heads:
            st_s[h, 1 - slot] = scores(h, nxt)
        for h in heads:
            m_prev, alpha_prev = carry[h]
            s_k = st_s[h, slot]
            m_new = jnp.maximum(m_prev, jnp.max(s_k, axis=0, keepdims=True))
            alpha = jnp.exp2(m_prev - m_new)
            values(h, prv, p_s[h, 1 - slot], alpha_prev)
            p_s[h, slot] = jnp.exp2(s_k - m_new).astype(BF16)
            out.append((m_new, alpha))
        return tuple(out)

    def two_trips(j, c):
        c = trip(2 * j, c, 0)
        return lax.cond(2 * j + 1 < qi, lambda cc: trip(2 * j + 1, cc, 1), lambda cc: cc, c)

    init = tuple((m0[h], jnp.ones_like(m0[h])) for h in heads)
    carry = lax.fori_loop(0, (qi + 1) // 2, two_trips, init)
    odd = qi % 2 == 1

    @pl.when(odd)
    def _():
        for h in heads:
            values(h, qi - 1, p_s[h, 0], carry[h][1])

    @pl.when(jnp.logical_not(odd))
    def _():
        for h in heads:
            values(h, jnp.where(qi == 0, 0, qi - 1), p_s[h, 1], carry[h][1])

    lv = lamv_ref[...]
    lam = (jnp.exp(jnp.sum(lv[0:1] * lv[1:2], axis=-1, keepdims=True))
           - jnp.exp(jnp.sum(lv[2:3] * lv[3:4], axis=-1, keepdims=True)) + lam_init)
    for h in heads:
        acc = acc_s[h]
        ot = acc[:DIFF_V_DIM] / acc[DIFF_V_DIM:DIFF_V_DIM + 1]
        od = (ot[:, :tq] - lam * ot[:, tq:]).T
        ms = jnp.mean(od * od, axis=-1, keepdims=True)
        o_ref[:, hl[h]] = (od * lax.rsqrt(ms + SUBLN_EPS) * sg_ref[...]
                           * (1.0 - lam_init)).astype(BF16)


def _attention(qs, kh, vt, lamv, sg, *, tq, lam_init):
    _, b, lp, _ = qs.shape
    nq = lp // tq
    hp = ATTN_HEADS_PER_STEP
    const = lambda bi, hg, qi: (0, 0)
    return pl.pallas_call(
        functools.partial(_attn_kernel, lam_init=lam_init),
        grid=(b, DIFF_HEADS // hp, nq),
        in_specs=[
            pl.BlockSpec((2, None, tq, hp * LANES), lambda bi, hg, qi: (0, bi, qi, hg)),
            pl.BlockSpec((None, lp, hp * LANES), lambda bi, hg, qi: (bi, 0, hg)),
            pl.BlockSpec((None, nq, hp, VT_ROWS, tq), lambda bi, hg, qi: (bi, 0, hg, 0, 0)),
            pl.BlockSpec((SUBLANES, LANES), const),
            pl.BlockSpec((1, LANES), const),
        ],
        out_specs=pl.BlockSpec((None, tq, hp * LANES), lambda bi, hg, qi: (bi, qi, hg)),
        out_shape=jax.ShapeDtypeStruct((b, lp, DIFF_WIDTH), BF16),
        scratch_shapes=[pltpu.VMEM((hp, VT_ROWS, 2 * tq), F32),
                        pltpu.VMEM((hp, 2, tq, 2 * tq), F32),
                        pltpu.VMEM((hp, 2, tq, 2 * tq), BF16)],
        compiler_params=_params(("arbitrary", "arbitrary", "arbitrary")),
        name="diffattn",
    )(qs, kh, vt, lamv, sg)


def _outproj_kernel(h_ref, oa_ref, ob_ref, pc_ref, pch_ref, cw_ref, wa_ref, wb_ref, wc_ref,
                    o_ref, *, tiles_per_batch):
    tm = h_ref.shape[0]
    first = (pl.program_id(0) % tiles_per_batch) == 0
    pc = pc_ref[...]
    bg = pc[:, :CONV_WIDTH]
    z = pc[:, CONV_WIDTH:2 * CONV_WIDTH] * pc[:, 2 * CONV_WIDTH:]
    ph = pch_ref[...]
    zh = jnp.where(first, 0.0, ph[:, CONV_WIDTH:2 * CONV_WIDTH] * ph[:, 2 * CONV_WIDTH:])
    row = lax.broadcasted_iota(jnp.int32, (tm, 1), 0)
    z1 = jnp.where(row == 0, zh[SUBLANES - 1:SUBLANES], pltpu.roll(z, 1, 0))
    z2 = jnp.where(row == 0, zh[SUBLANES - 2:SUBLANES - 1],
                   jnp.where(row == 1, zh[SUBLANES - 1:SUBLANES], pltpu.roll(z, 2, 0)))
    cw = cw_ref[...]
    y = cw[0:1] * z2 + cw[1:2] * z1 + cw[2:3] * z
    oc = (bg * y).astype(BF16)
    o_ref[...] = (h_ref[...]
                  + jnp.dot(oa_ref[...], wa_ref[...], preferred_element_type=F32)
                  + jnp.dot(ob_ref[...], wb_ref[...], preferred_element_type=F32)
                  + jnp.dot(oc, wc_ref[...], preferred_element_type=F32))


def _outproj(h, oa, ob, pc, cw, wa, wb, wc, *, tm, lp):
    r, d = h.shape
    row = lambda i: (i, 0)
    const = lambda i: (0, 0)
    halo = lambda i: (jnp.maximum(i * (tm // SUBLANES) - 1, 0), 0)
    return pl.pallas_call(
        functools.partial(_outproj_kernel, tiles_per_batch=lp // tm),
        grid=(r // tm,),
        in_specs=[
            pl.BlockSpec((tm, d), row),
            pl.BlockSpec((tm, RWKV_PAD), row),
            pl.BlockSpec((tm, DIFF_WIDTH), row),
            pl.BlockSpec((tm, PC_COLS), row),
            pl.BlockSpec((SUBLANES, PC_COLS), halo),
            pl.BlockSpec((SUBLANES, CONV_WIDTH), const),
            pl.BlockSpec(wa.shape, const),
            pl.BlockSpec(wb.shape, const),
            pl.BlockSpec(wc.shape, const),
        ],
        out_specs=pl.BlockSpec((tm, d), row),
        out_shape=jax.ShapeDtypeStruct((r, d), F32),
        compiler_params=_params(("arbitrary",)),
        name="outproj",
    )(h, oa, ob, pc, pc, cw, wa, wb, wc)


def _swiglu_block(xb, wg_ref, wu_ref, wd_ref, side_work=None):
    d_ff = wg_ref.shape[1]
    acc = jnp.zeros((xb.shape[0], wd_ref.shape[1]), F32)
    for c in range(d_ff // MXU_DIM):
        cs = slice(c * MXU_DIM, (c + 1) * MXU_DIM)
        gt = jnp.dot(xb, wg_ref[:, cs], preferred_element_type=F32)
        up = jnp.dot(xb, wu_ref[:, cs], preferred_element_type=F32)
        act = (jax.nn.silu(gt) * up).astype(BF16)
        acc = acc + jnp.dot(act, wd_ref[cs, :], preferred_element_type=F32)
        if side_work is not None:
            side_work(c)
    return acc


def _rms(x, g):
    ms = jnp.mean(x * x, axis=-1, keepdims=True)
    return x * lax.rsqrt(ms + NORM_EPS) * g


def _ffn_kernel(h_ref, g_ref, wg_ref, wu_ref, wd_ref, o_ref):
    x = h_ref[...]
    xn = _rms(x, g_ref[...]).astype(BF16)
    o_ref[...] = x + _swiglu_block(xn, wg_ref, wu_ref, wd_ref)


def _dense_ffn(h, g, wg, wu, wd, *, tm):
    r, d = h.shape
    row = lambda i: (i, 0)
    const = lambda i: (0, 0)
    return pl.pallas_call(
        _ffn_kernel,
        grid=(r // tm,),
        in_specs=[
            pl.BlockSpec((tm, d), row),
            pl.BlockSpec((1, d), const),
            pl.BlockSpec(wg.shape, const),
            pl.BlockSpec(wu.shape, const),
            pl.BlockSpec(wd.shape, const),
        ],
        out_specs=pl.BlockSpec((tm, d), row),
        out_shape=jax.ShapeDtypeStruct((r, d), F32),
        compiler_params=_params(("arbitrary",)),
        name="dense_ffn",
    )(h, g, wg, wu, wd)


def _router_kernel(h_ref, g_ref, rw_ref, hn_ref, route_ref):
    tm = h_ref.shape[0]
    xn = _rms(h_ref[...], g_ref[...])
    hn_ref[...] = xn
    logits = _dot_f32(xn, rw_ref[...])
    lane = lax.broadcasted_iota(jnp.int32, (tm, LANES), 1)
    logits = jnp.where(lane < N_EXPERTS, logits, -jnp.inf)
    m1 = jnp.max(logits, axis=-1, keepdims=True)
    i1 = jnp.min(jnp.where(logits == m1, lane, LANES), axis=-1, keepdims=True)
    rest = jnp.where(lane == i1, -jnp.inf, logits)
    m2 = jnp.max(rest, axis=-1, keepdims=True)
    i2 = jnp.min(jnp.where(rest == m2, lane, LANES), axis=-1, keepdims=True)
    e2 = jnp.exp(m2 - m1)
    den = 1.0 + e2
    route_ref[...] = jnp.where(lane == 0, i1.astype(F32),
                     jnp.where(lane == 1, i2.astype(F32),
                     jnp.where(lane == 2, 1.0 / den,
                     jnp.where(lane == 3, e2 / den, 0.0))))


def _router(h, g, rw, *, tm):
    r, d = h.shape
    row = lambda i: (i, 0)
    const = lambda i: (0, 0)
    return pl.pallas_call(
        _router_kernel,
        grid=(r // tm,),
        in_specs=[pl.BlockSpec((tm, d), row), pl.BlockSpec((1, d), const),
                  pl.BlockSpec((d, LANES), const)],
        out_specs=[pl.BlockSpec((tm, d), row), pl.BlockSpec((tm, LANES), row)],
        out_shape=[jax.ShapeDtypeStruct((r, d), F32), jax.ShapeDtypeStruct((r, LANES), F32)],
        compiler_params=_params(("arbitrary",)),
        name="router",
    )(h, g, rw)


def _row_copy(src, dst, sem, src_row, dst_row):
    return pltpu.make_async_copy(src.at[pl.ds(src_row, 1), :], dst.at[pl.ds(dst_row, 1), :], sem)


DMA_UNROLL = 8
DMA_ISSUE_SLICES = 8
MOE_BUFFERS = 3


def _moe_kernel(bexp_ref, rsrc_ref, rdst_ref, hn_hbm, wg_ref, wu_ref, wd_ref, y_hbm,
                xbuf, ybuf, gsem, ssem):
    del bexp_ref
    nbuf = xbuf.shape[0]
    blk = xbuf.shape[1]
    i = pl.program_id(0)
    nb = pl.num_programs(0)
    slot = i % nbuf
    slot_prev = (i + nbuf - 1) % nbuf
    slot_next2 = (i + 2) % nbuf
    dump0 = y_hbm.shape[0] - nbuf * blk

    def for_rows(fn):
        def body(j, c):
            for u in range(DMA_UNROLL):
                fn(j * DMA_UNROLL + u)
            return c
        lax.fori_loop(0, blk // DMA_UNROLL, body, 0)

    def gather(block, s, j):
        return _row_copy(hn_hbm, xbuf.at[s], gsem.at[s], rsrc_ref[block * blk + j], j)

    def scatter(dst_row, s, j):
        return _row_copy(ybuf.at[s], y_hbm, ssem.at[s], j, dst_row)

    @pl.when(i == 0)
    def _():
        for_rows(lambda j: gather(0, 0, j).start())
        for_rows(lambda j: gather(1, 1, j).start())
        ybuf[nbuf - 1] = jnp.zeros(ybuf.shape[1:], ybuf.dtype)
        for k in range(nbuf - 1):
            zero_fill = pltpu.make_async_copy(
                ybuf.at[nbuf - 1], y_hbm.at[pl.ds(dump0 + k * blk, blk), :], ssem.at[k])
            zero_fill.start()
            zero_fill.wait()

    for_rows(lambda j: gather(0, slot, j).wait())

    @pl.when(i >= 2)
    def _():
        for_rows(lambda j: scatter(0, slot, j).wait())

    ahead = jnp.minimum(i + 2, nb - 1)
    prev = jnp.maximum(i - 1, 0)
    per = -(-blk // DMA_ISSUE_SLICES)

    def side_work(c):
        for j in range(c * per, min((c + 1) * per, blk)):
            gather(ahead, slot_next2, j).start(priority=j % 2)
            dst = jnp.where(i == 0, dump0 + (nbuf - 1) * blk + j, rdst_ref[prev * blk + j])
            scatter(dst, slot_prev, j).start(priority=(j + 1) % 2)

    ybuf[slot] = _swiglu_block(xbuf[slot].astype(BF16), wg_ref, wu_ref, wd_ref, side_work)

    @pl.when(i == nb - 1)
    def _():
        for_rows(lambda j: gather(0, (i + 1) % nbuf, j).wait())
        for_rows(lambda j: gather(0, slot_next2, j).wait())
        for_rows(lambda j: scatter(0, (i + 1) % nbuf, j).wait())
        for_rows(lambda j: scatter(0, slot_prev, j).wait())
        for_rows(lambda j: scatter(rdst_ref[i * blk + j], slot, j).start())
        for_rows(lambda j: scatter(0, slot, j).wait())


def _moe_experts(block_expert, row_src, row_dst, hn, wg, wu, wd):
    nb = block_expert.shape[0]
    r, d = hn.shape
    d_ff = wg.shape[2]
    wmap = lambda i, be, rs, rd: (be[i], 0, 0)
    grid_spec = pltpu.PrefetchScalarGridSpec(
        num_scalar_prefetch=3,
        grid=(nb,),
        in_specs=[
            pl.BlockSpec(memory_space=pl.ANY),
            pl.BlockSpec((None, d, d_ff), wmap),
            pl.BlockSpec((None, d, d_ff), wmap),
            pl.BlockSpec((None, d_ff, d), wmap),
        ],
        out_specs=pl.BlockSpec(memory_space=pl.ANY),
        scratch_shapes=[pltpu.VMEM((MOE_BUFFERS, MOE_BLOCK, d), F32),
                        pltpu.VMEM((MOE_BUFFERS, MOE_BLOCK, d), F32),
                        pltpu.SemaphoreType.DMA((MOE_BUFFERS,)),
                        pltpu.SemaphoreType.DMA((MOE_BUFFERS,))],
    )
    return pl.pallas_call(
        _moe_kernel,
        grid_spec=grid_spec,
        out_shape=jax.ShapeDtypeStruct((TOP_K * r + MOE_BUFFERS * MOE_BLOCK, d), F32),
        compiler_params=_params(("arbitrary",)),
        name="moe_experts",
    )(block_expert, row_src, row_dst, hn, wg, wu, wd)


def _combine_kernel(h_ref, route_ref, y0_ref, y1_ref, o_ref):
    route = route_ref[...]
    o_ref[...] = h_ref[...] + route[:, 2:3] * y0_ref[...] + route[:, 3:4] * y1_ref[...]


def _moe_combine(h, route, y, *, tm):
    r, d = h.shape
    nt = r // tm
    row = lambda i: (i, 0)
    return pl.pallas_call(
        _combine_kernel,
        grid=(nt,),
        in_specs=[
            pl.BlockSpec((tm, d), row),
            pl.BlockSpec((tm, LANES), row),
            pl.BlockSpec((tm, d), row),
            pl.BlockSpec((tm, d), lambda i: (i + nt, 0)),
        ],
        out_specs=pl.BlockSpec((tm, d), row),
        out_shape=jax.ShapeDtypeStruct((r, d), F32),
        compiler_params=_params(("arbitrary",)),
        name="moe_combine",
    )(h, route, y, y)


def _slot_owner_kernel(dest_ref, owner_ref):
    n_pairs = dest_ref.shape[0]
    n_slots = owner_ref.shape[0]

    def clear(i, c):
        for u in range(DMA_UNROLL):
            owner_ref[i * DMA_UNROLL + u] = -1
        return c

    lax.fori_loop(0, n_slots // DMA_UNROLL, clear, 0)

    def place(i, c):
        for u in range(DMA_UNROLL):
            p = i * DMA_UNROLL + u
            owner_ref[dest_ref[p]] = p
        return c

    lax.fori_loop(0, n_pairs // DMA_UNROLL, place, 0)


def _slot_owner(dest, n_slots):
    assert dest.shape[0] % DMA_UNROLL == 0 and n_slots % DMA_UNROLL == 0
    smem = pl.BlockSpec(memory_space=pltpu.SMEM)
    return pl.pallas_call(
        _slot_owner_kernel,
        in_specs=[smem],
        out_specs=smem,
        out_shape=jax.ShapeDtypeStruct((n_slots,), jnp.int32),
        name="slot_owner",
    )(dest)


def _routing_tables(route):
    r = route.shape[0]
    tk = r * TOP_K
    flat_e = route[:, :TOP_K].reshape(tk).astype(jnp.int32)
    onehot = (flat_e[:, None] == jnp.arange(N_EXPERTS, dtype=jnp.int32)[None, :]).astype(jnp.int32)
    csum = jnp.cumsum(onehot, axis=0)
    rank = jnp.sum(csum * onehot, axis=1) - 1
    counts = csum[-1]
    padded = (counts + MOE_BLOCK - 1) // MOE_BLOCK * MOE_BLOCK
    pend = jnp.cumsum(padded)
    pstart = pend - padded
    dest = pstart[flat_e] + rank
    nb = -(-tk // MOE_BLOCK) + N_EXPERTS
    n_slots = nb * MOE_BLOCK
    pair = _slot_owner(dest, n_slots)
    used = pair >= 0
    row_src = jnp.where(used, pair // TOP_K, 0)
    slot_id = jnp.arange(n_slots, dtype=jnp.int32)
    dump_row = tk + (slot_id // MOE_BLOCK) % MOE_BUFFERS * MOE_BLOCK + slot_id % MOE_BLOCK
    row_dst = jnp.where(used, (pair % TOP_K) * r + pair // TOP_K, dump_row)
    block_start = jnp.arange(nb, dtype=jnp.int32) * MOE_BLOCK
    block_expert = jnp.minimum(jnp.searchsorted(pend, block_start, side='right'),
                               N_EXPERTS - 1).astype(jnp.int32)
    return block_expert, row_src, row_dst


def _pad_heads(t, axis):
    axis = axis % t.ndim
    shp = t.shape
    t = t.reshape(shp[:axis] + (RWKV_HEADS, RWKV_HEAD_DIM) + shp[axis + 1:])
    pad = [(0, 0)] * t.ndim
    pad[axis + 1] = (0, HEAD_PAD - RWKV_HEAD_DIM)
    t = jnp.pad(t, pad)
    return t.reshape(shp[:axis] + (RWKV_PAD,) + shp[axis + 1:])


def _cast_kernel(x_ref, o_ref):
    o_ref[...] = x_ref[...].astype(o_ref.dtype)


def _to_bf16(w, layer):
    _, e, a, b = w.shape
    return pl.pallas_call(
        _cast_kernel,
        grid=(e, 2),
        in_specs=[pl.BlockSpec((None, None, a // 2, b), lambda i, j: (layer, i, j, 0))],
        out_specs=pl.BlockSpec((None, a // 2, b), lambda i, j: (i, j, 0)),
        out_shape=jax.ShapeDtypeStruct((e, a, b), BF16),
        compiler_params=_params(("arbitrary", "arbitrary")),
        name="to_bf16",
    )(w)


def _rope_tables(lp):
    half = ROPE_DIM // 2
    inv_freq = ROPE_THETA ** (-jnp.arange(0, ROPE_DIM, 2, dtype=F32) / ROPE_DIM)
    ang = jnp.arange(lp, dtype=F32)[:, None] * inv_freq[None, :]
    cos, sin = jnp.cos(ang), jnp.sin(ang)
    ones = jnp.ones((lp, DIFF_QK_DIM - ROPE_DIM), F32)
    zeros = jnp.zeros((lp, DIFF_QK_DIM - ROPE_DIM), F32)
    zh = jnp.zeros((lp, half), F32)
    rc = jnp.concatenate([cos, cos, ones], axis=1)
    rs1 = jnp.concatenate([-sin, zh, zeros], axis=1)
    rs2 = jnp.concatenate([zh, sin, zeros], axis=1)
    tile2 = lambda a: jnp.concatenate([a, a], axis=1)
    return tile2(rc), tile2(rs1), tile2(rs2)


def kernel(x, meta_tokens, mix_norm_g, w_in, tm_mu, tm_w0, tm_w_decay_up, tm_a0, tm_w_a_up, tm_w_g_up, tm_k_k, tm_k_a, tm_r_k, tm_gn_g, tm_gn_b, da_q_norm_g, da_k_norm_g, da_lambda_q1, da_lambda_k1, da_lambda_q2, da_lambda_k2, da_subln_g, sc_conv_w, w_out, ffn_norm_g, ffn_w_gate, ffn_w_up, ffn_w_down, router_w, moe_w_gate, moe_w_up, moe_w_down):
    b, seq, d = x.shape
    depth = w_in.shape[0]
    l = N_META + seq
    lp = -(-l // LANES) * LANES
    tm = _row_tile(lp)
    r = b * lp
    s1 = RWKV_WIDTH

    meta = jnp.broadcast_to(meta_tokens.astype(x.dtype)[None], (b, N_META, d))
    h = jnp.concatenate([meta, x, jnp.zeros((b, lp - l, d), x.dtype)], axis=1).reshape(r, d)
    rc, rs1, rs2 = _rope_tables(lp)

    for i in range(depth):
        lam_init = 0.8 - 0.6 * math.exp(-0.3 * i)
        wi = w_in[i]
        w_cat = jnp.concatenate(
            [_pad_heads(wi[:, 0:s1], 1), _pad_heads(wi[:, s1:2 * s1], 1),
             _pad_heads(wi[:, 2 * s1:3 * s1], 1), wi[:, 3 * s1:]], axis=1).astype(BF16)
        qg = jnp.tile(da_q_norm_g[i], 2)[None]
        kg = jnp.tile(da_k_norm_g[i], 2)[None]
        pa, qs, kh, vt, pc = _inproj(h, mix_norm_g[i][None], w_cat, rc, rs1, rs2, qg, kg,
                                     tm=tm, lp=lp)

        mu = tm_mu[i]
        mu_p = jnp.concatenate([_pad_heads(mu[0:s1], 0), _pad_heads(mu[s1:2 * s1], 0),
                                _pad_heads(mu[2 * s1:3 * s1], 0), mu[3 * s1:]])[None]
        vecs = jnp.stack([_pad_heads(tm_w0[i], 0), _pad_heads(tm_a0[i], 0),
                          _pad_heads(tm_k_k[i], 0), _pad_heads(tm_k_a[i], 0),
                          _pad_heads(tm_r_k[i].reshape(s1), 0), _pad_heads(tm_gn_g[i], 0),
                          _pad_heads(tm_gn_b[i], 0), jnp.zeros((RWKV_PAD,), F32)])
        zl = lambda n: jnp.zeros((n, RWKV_PAD), F32)
        wd_p = jnp.concatenate([_pad_heads(tm_w_decay_up[i], 1), zl(AAA_LORA + GATE_LORA)], axis=0)
        wa_p = jnp.concatenate([zl(DECAY_LORA), _pad_heads(tm_w_a_up[i], 1), zl(GATE_LORA)], axis=0)
        wg_p = jnp.concatenate([zl(DECAY_LORA + AAA_LORA), _pad_heads(tm_w_g_up[i], 1)], axis=0)
        oa = _rwkv(pa.reshape(b, lp, PA_COLS), mu_p, vecs, wd_p, wa_p, wg_p, tb=tm)

        lamv = jnp.zeros((SUBLANES, LANES), F32).at[0:4, :DIFF_QK_DIM].set(
            jnp.stack([da_lambda_q1[i], da_lambda_k1[i], da_lambda_q2[i], da_lambda_k2[i]]))
        ob = _attention(qs.reshape(2, b, lp, DIFF_WIDTH), kh.reshape(b, lp, DIFF_WIDTH),
                        vt.reshape(b, lp // tm, DIFF_HEADS, VT_ROWS, tm), lamv, da_subln_g[i][None],
                        tq=tm, lam_init=lam_init)

        wo = w_out[i]
        cw = jnp.zeros((SUBLANES, CONV_WIDTH), F32).at[:CONV_K].set(sc_conv_w[i])
        h = _outproj(h, oa.reshape(r, RWKV_PAD), ob.reshape(r, DIFF_WIDTH), pc, cw,
                     _pad_heads(wo[:s1], 0).astype(BF16), wo[s1:s1 + DIFF_WIDTH].astype(BF16),
                     wo[s1 + DIFF_WIDTH:].astype(BF16), tm=tm, lp=lp)

        j = i // 2
        if i % 2 == 0:
            h = _dense_ffn(h, ffn_norm_g[i][None], ffn_w_gate[j].astype(BF16),
                           ffn_w_up[j].astype(BF16), ffn_w_down[j].astype(BF16), tm=tm)
        else:
            rw = jnp.zeros((d, LANES), F32).at[:, :N_EXPERTS].set(router_w[j])
            hn, route = _router(h, ffn_norm_g[i][None], rw, tm=tm)
            block_expert, row_src, row_dst = _routing_tables(route)
            y = _moe_experts(block_expert, row_src, row_dst, hn, _to_bf16(moe_w_gate, j),
                             _to_bf16(moe_w_up, j), _to_bf16(moe_w_down, j))
            h = _moe_combine(h, route, y, tm=tm)

    return h.reshape(b, lp, d)[:, N_META:l]
```

```python
import functools
import math

import jax
import jax.numpy as jnp
from jax import lax
from jax.experimental import pallas as pl
from jax.experimental.pallas import tpu as pltpu

N_META = 16
RWKV_HEAD_DIM = 64
RWKV_HEADS = 4
RWKV_WIDTH = RWKV_HEADS * RWKV_HEAD_DIM
DECAY_LORA = 32
AAA_LORA = 32
GATE_LORA = 64
LORA_COLS = DECAY_LORA + AAA_LORA + GATE_LORA
DIFF_QK_DIM = 64
DIFF_V_DIM = 128
DIFF_HEADS = 4
DIFF_WIDTH = DIFF_HEADS * DIFF_V_DIM
CONV_WIDTH = 256
CONV_K = 3
ROPE_THETA = 500000.0
ROPE_DIM = DIFF_QK_DIM // 4
N_EXPERTS = 8
TOP_K = 2
MOE_BLOCK = 256
NORM_EPS = 1e-6
RWKV_GN_EPS = 64e-5
SUBLN_EPS = 1e-5
NEG_INF = -1e30

LANES = 128
SUBLANES = 8
MXU_DIM = 256
VMEM_LIMIT_BYTES = 56 * 1024 * 1024

HEAD_PAD = LANES
RWKV_PAD = RWKV_HEADS * HEAD_PAD
PA_COLS = 3 * RWKV_PAD + LORA_COLS
PB_COLS = 3 * DIFF_WIDTH
PC_COLS = 3 * CONV_WIDTH
VT_ROWS = DIFF_V_DIM + 16
LOG2_E = math.log2(math.e)
CHUNK = 64

F32 = jnp.float32
BF16 = jnp.bfloat16
HIGHEST = lax.Precision.HIGHEST


def _dot(a, b):
    return jnp.dot(a.astype(BF16), b.astype(BF16), preferred_element_type=F32)


def _dot_nt(a, b):
    return lax.dot_general(a.astype(BF16), b.astype(BF16), (((1,), (1,)), ((), ())),
                           preferred_element_type=F32)


def _dot_f32(a, b):
    return jnp.dot(a, b, preferred_element_type=F32, precision=HIGHEST)


def _row_tile(lp):
    for t in (384, 256, 128):
        if lp % t == 0:
            return t
    raise ValueError(f"padded length {lp} is not a multiple of {LANES}")


def _params(sem):
    return pltpu.CompilerParams(dimension_semantics=sem, vmem_limit_bytes=VMEM_LIMIT_BYTES)


def _inproj_kernel(h_ref, g_ref, w_ref, rc_ref, rs1_ref, rs2_ref, qg_ref, kg_ref,
                   pa_ref, qs_ref, kh_ref, vt_ref, pc_ref):
    tm = h_ref.shape[0]
    x = h_ref[...]
    ms = jnp.mean(x * x, axis=-1, keepdims=True)
    xn = (x * lax.rsqrt(ms + NORM_EPS) * g_ref[...]).astype(BF16)
    pa_ref[...] = jnp.dot(xn, w_ref[:, :PA_COLS], preferred_element_type=F32)
    pc_ref[...] = jnp.dot(xn, w_ref[:, PA_COLS + PB_COLS:], preferred_element_type=F32)
    pb = jnp.dot(xn, w_ref[:, PA_COLS:PA_COLS + PB_COLS], preferred_element_type=F32)

    lane = lax.broadcasted_iota(jnp.int32, (tm, LANES), 1)
    lo = lane < DIFF_QK_DIM
    rc, rs1, rs2 = rc_ref[...], rs1_ref[...], rs2_ref[...]

    def norm_rope(t, g):
        t2 = t * t
        ss_lo = jnp.sum(jnp.where(lo, t2, 0.0), axis=-1, keepdims=True)
        ss_hi = jnp.sum(jnp.where(lo, 0.0, t2), axis=-1, keepdims=True)
        inv = jnp.where(lo, lax.rsqrt(ss_lo / DIFF_QK_DIM + NORM_EPS),
                        lax.rsqrt(ss_hi / DIFF_QK_DIM + NORM_EPS))
        tn = t * inv * g
        half = ROPE_DIM // 2
        return tn * rc + pltpu.roll(tn, LANES - half, 1) * rs1 + pltpu.roll(tn, half, 1) * rs2

    for hd in range(DIFF_HEADS):
        cs = slice(hd * LANES, (hd + 1) * LANES)
        q = norm_rope(pb[:, cs], qg_ref[...]) * (DIFF_QK_DIM ** -0.5 * LOG2_E)
        qs_ref[0, :, cs] = jnp.where(lo, q, 0.0).astype(BF16)
        qs_ref[1, :, cs] = jnp.where(lo, 0.0, q).astype(BF16)
        kcs = slice(DIFF_WIDTH + hd * LANES, DIFF_WIDTH + (hd + 1) * LANES)
        kh_ref[:, cs] = norm_rope(pb[:, kcs], kg_ref[...]).astype(BF16)
        vcs = slice(2 * DIFF_WIDTH + hd * LANES, 2 * DIFF_WIDTH + (hd + 1) * LANES)
        vt_ref[0, hd, :DIFF_V_DIM] = pb[:, vcs].T.astype(BF16)
        vt_ref[0, hd, DIFF_V_DIM:] = jnp.ones((VT_ROWS - DIFF_V_DIM, tm), BF16)


def _inproj(h, g, w, rc, rs1, rs2, qg, kg, *, tm, lp):
    r, d = h.shape
    nt_b = lp // tm
    row = lambda i: (i, 0)
    const = lambda i: (0, 0)
    rope = lambda i: (i % nt_b, 0)
    return pl.pallas_call(
        _inproj_kernel,
        grid=(r // tm,),
        in_specs=[
            pl.BlockSpec((tm, d), row),
            pl.BlockSpec((1, d), const),
            pl.BlockSpec(w.shape, const),
            pl.BlockSpec((tm, LANES), rope),
            pl.BlockSpec((tm, LANES), rope),
            pl.BlockSpec((tm, LANES), rope),
            pl.BlockSpec((1, LANES), const),
            pl.BlockSpec((1, LANES), const),
        ],
        out_specs=[
            pl.BlockSpec((tm, PA_COLS), row),
            pl.BlockSpec((2, tm, DIFF_WIDTH), lambda i: (0, i, 0)),
            pl.BlockSpec((tm, DIFF_WIDTH), row),
            pl.BlockSpec((1, DIFF_HEADS, VT_ROWS, tm), lambda i: (i, 0, 0, 0)),
            pl.BlockSpec((tm, PC_COLS), row),
        ],
        out_shape=[
            jax.ShapeDtypeStruct((r, PA_COLS), F32),
            jax.ShapeDtypeStruct((2, r, DIFF_WIDTH), BF16),
            jax.ShapeDtypeStruct((r, DIFF_WIDTH), BF16),
            jax.ShapeDtypeStruct((r // tm, DIFF_HEADS, VT_ROWS, tm), BF16),
            jax.ShapeDtypeStruct((r, PC_COLS), F32),
        ],
        compiler_params=_params(("arbitrary",)),
        name="inproj",
    )(h, g, w, rc, rs1, rs2, qg, kg)


def _split3(x):
    hi = x.astype(BF16)
    r1 = x - hi.astype(F32)
    mid = r1.astype(BF16)
    lo = (r1 - mid.astype(F32)).astype(BF16)
    return hi, mid, lo


def _dot_hi(a, b):
    ah = a.astype(BF16)
    al = (a - ah.astype(F32)).astype(BF16)
    bh = b.astype(BF16)
    bl = (b - bh.astype(F32)).astype(BF16)
    d = lambda p, q: jnp.dot(p, q, preferred_element_type=F32)
    return d(ah, bh) + d(ah, bl) + d(al, bh)


def _bmm(a, b):
    return jnp.einsum('cik,ckj->cij', a.astype(BF16), b.astype(BF16), preferred_element_type=F32)


def _bmm_nt(a, b):
    return jnp.einsum('cik,cjk->cij', a.astype(BF16), b.astype(BF16), preferred_element_type=F32)


def _rwkv_kernel(pa_ref, halo_ref, mu_ref, vec_ref, wd_ref, wa_ref, wg_ref, oa_ref, hstate_s):
    tb = pa_ref.shape[0]
    nc = tb // CHUNK
    i = pl.program_id(1)

    @pl.when(i == 0)
    def _():
        hstate_s[...] = jnp.zeros_like(hstate_s)

    x = pa_ref[...]
    prev_row = jnp.where(i == 0, 0.0, halo_ref[SUBLANES - 1:SUBLANES, :])
    row = lax.broadcasted_iota(jnp.int32, (tb, 1), 0)
    prev = jnp.where(row == 0, prev_row, pltpu.roll(x, 1, 0))
    hx = x + (prev - x) * mu_ref[...]
    r = hx[:, 0:RWKV_PAD]
    k = hx[:, RWKV_PAD:2 * RWKV_PAD]
    v = hx[:, 2 * RWKV_PAD:3 * RWKV_PAD]
    lora = hx[:, 3 * RWKV_PAD:]
    w0, a0, k_k, k_a = vec_ref[0:1, :], vec_ref[1:2, :], vec_ref[2:3, :], vec_ref[3:4, :]
    r_k, gn_g, gn_b = vec_ref[4:5, :], vec_ref[5:6, :], vec_ref[6:7, :]

    w = -jax.nn.softplus(-(w0 + _dot_hi(jnp.tanh(lora), wd_ref[...]))) - 0.5
    wl = -jnp.exp(w)
    alr = jax.nn.sigmoid(a0 + _dot(lora, wa_ref[...]))
    gate = _dot(jax.nn.sigmoid(lora), wg_ref[...])
    kk = k * k_k
    k2 = k * (1.0 + (alr - 1.0) * k_a)
    rkr = r * k2 * r_k

    ri = lax.broadcasted_iota(jnp.int32, (CHUNK, CHUNK), 0)
    ci = lax.broadcasted_iota(jnp.int32, (CHUNK, CHUNK), 1)
    lane_c = lax.broadcasted_iota(jnp.int32, (CHUNK, HEAD_PAD), 1)
    row_c = lax.broadcasted_iota(jnp.int32, (CHUNK, HEAD_PAD), 0)
    col_c = jnp.where(lane_c >= CHUNK, lane_c - CHUNK, lane_c)
    low_incl2 = (col_c <= row_c)[None]
    low_strict2 = (col_c < row_c)[None]
    tril_b = jnp.where(ci <= ri, 1.0, 0.0).astype(BF16)
    eye_k = (lax.broadcasted_iota(jnp.int32, (HEAD_PAD, HEAD_PAD), 0)
             == lax.broadcasted_iota(jnp.int32, (HEAD_PAD, HEAD_PAD), 1))[None]
    n_sq = int(math.log2(CHUNK))

    parts = _split3(wl)
    cums = []
    for c in range(nc):
        rows = slice(c * CHUNK, (c + 1) * CHUNK)
        cums.append(sum(jnp.dot(tril_b, p[rows], preferred_element_type=F32) for p in parts))
    cum_all = jnp.concatenate(cums, axis=0)

    lane = lax.broadcasted_iota(jnp.int32, (tb, HEAD_PAD), 1)
    real = lane < RWKV_HEAD_DIM
    to3 = lambda t: t.reshape(nc, CHUNK, HEAD_PAD)
    heads = range(RWKV_HEADS)
    hsl = [slice(hd * HEAD_PAD, (hd + 1) * HEAD_PAD) for hd in heads]
    kkn, bonus = [], []
    for cs in hsl:
        kk_h = kk[:, cs]
        nrm = jnp.sqrt(jnp.sum(kk_h * kk_h, axis=-1, keepdims=True))
        kkn.append(kk_h / jnp.maximum(nrm, 1e-12))
        bonus.append(jnp.sum(rkr[:, cs], axis=-1, keepdims=True) * v[:, cs])

    vc = [to3(v[:, cs]) for cs in hsl]
    zero_v = jnp.zeros_like(vc[0])
    ah, rh, last, g, bk_last = [], [], [], [], []
    for hd, cs in enumerate(hsl):
        wl3, cum = to3(wl[:, cs]), to3(cum_all[:, cs])
        rr, kc = to3(r[:, cs]), to3(k2[:, cs])
        ac, bc = to3(-kkn[hd]), to3(kkn[hd] * alr[:, cs])
        last.append(cum[:, CHUNK - 1:CHUNK, :])
        e_neg = jnp.exp(-cum)
        e_last = jnp.exp(last[hd] - cum)
        ah.append(ac * jnp.exp(cum - wl3))
        rh.append(rr * jnp.exp(cum))
        bk_last.append(jnp.concatenate([bc * e_last, kc * e_last], axis=1))
        g.append(_bmm_nt(jnp.concatenate([ah[hd], rh[hd]], axis=1),
                         jnp.concatenate([bc * e_neg, kc * e_neg], axis=1)))
    a_abk = [jnp.where(low_strict2, g[hd][:, :CHUNK], 0.0) for hd in heads]
    r_bk = [jnp.where(low_incl2, g[hd][:, CHUNK:], 0.0) for hd in heads]
    xx = [jnp.concatenate([_bmm(a_abk[hd], jnp.concatenate([zero_v, vc[hd]], axis=1)), ah[hd]],
                          axis=2) for hd in heads]
    ppad = [jnp.where(lane_c < CHUNK, a_abk[hd], 0.0) for hd in heads]
    for j in range(n_sq):
        for hd in heads:
            pj = ppad[hd][:, :, :CHUNK]
            if j + 1 < n_sq:
                res = _bmm(pj, jnp.concatenate([xx[hd], ppad[hd]], axis=2))
                xx[hd] = xx[hd] + res[:, :, :2 * HEAD_PAD]
                ppad[hd] = res[:, :, 2 * HEAD_PAD:]
            else:
                xx[hd] = xx[hd] + _bmm(pj, xx[hd])
    qm, y0, n0 = [], [], []
    for hd in heads:
        lhs = jnp.concatenate([r_bk[hd], jnp.swapaxes(bk_last[hd], 1, 2)], axis=1)
        rhs = jnp.concatenate([xx[hd], jnp.concatenate([vc[hd], zero_v], axis=2)], axis=1)
        out = _bmm(lhs, rhs)
        y0.append(out[:, :CHUNK, :HEAD_PAD])
        n0.append(out[:, CHUNK:, :HEAD_PAD])
        q = rh[hd] + out[:, :CHUNK, HEAD_PAD:]
        m = jnp.where(eye_k, jnp.exp(last[hd]), 0.0) + out[:, CHUNK:, HEAD_PAD:]
        qm.append(jnp.concatenate([q, m], axis=1).astype(BF16))

    hs = [hstate_s[hd] for hd in heads]
    ys = [[] for _ in heads]
    for c in range(nc):
        for hd in heads:
            res = jnp.dot(qm[hd][c], hs[hd].astype(BF16), preferred_element_type=F32)
            ys[hd].append(res[:CHUNK] + y0[hd][c])
            hs[hd] = res[CHUNK:] + n0[hd][c]

    for hd, cs in enumerate(hsl):
        hstate_s[hd] = hs[hd]
        y = jnp.concatenate(ys[hd], axis=0)
        mean = jnp.sum(y, axis=-1, keepdims=True) / RWKV_HEAD_DIM
        dlt = jnp.where(real, y - mean, 0.0)
        var = jnp.sum(dlt * dlt, axis=-1, keepdims=True) / RWKV_HEAD_DIM
        yn = dlt * lax.rsqrt(var + RWKV_GN_EPS) * gn_g[:, cs] + gn_b[:, cs]
        oa_ref[:, cs] = ((yn + bonus[hd]) * gate[:, cs]).astype(BF16)


def _rwkv(pa, mu, vecs, wd, wa, wg, *, tb):
    b, lp, _ = pa.shape
    tile = lambda bi, i: (bi, i, 0)
    const = lambda bi, i: (0, 0)
    halo = lambda bi, i: (bi, jnp.maximum(i * (tb // SUBLANES) - 1, 0), 0)
    return pl.pallas_call(
        _rwkv_kernel,
        grid=(b, lp // tb),
        in_specs=[
            pl.BlockSpec((None, tb, PA_COLS), tile),
            pl.BlockSpec((None, SUBLANES, PA_COLS), halo),
            pl.BlockSpec((1, PA_COLS), const),
            pl.BlockSpec((SUBLANES, RWKV_PAD), const),
            pl.BlockSpec((LORA_COLS, RWKV_PAD), const),
            pl.BlockSpec((LORA_COLS, RWKV_PAD), const),
            pl.BlockSpec((LORA_COLS, RWKV_PAD), const),
        ],
        out_specs=pl.BlockSpec((None, tb, RWKV_PAD), tile),
        out_shape=jax.ShapeDtypeStruct((b, lp, RWKV_PAD), BF16),
        scratch_shapes=[pltpu.VMEM((RWKV_HEADS, HEAD_PAD, HEAD_PAD), F32)],
        compiler_params=_params(("arbitrary", "arbitrary")),
        name="rwkv7",
    )(pa, pa, mu, vecs, wd, wa, wg)


ATTN_HEADS_PER_STEP = 4


def _attn_kernel(q_ref, k_ref, vt_ref, lamv_ref, sg_ref, o_ref, acc_s, st_s, p_s, *, lam_init):
    tq = o_ref.shape[0]
    tk = vt_ref.shape[3]
    qi = pl.program_id(2)
    heads = range(ATTN_HEADS_PER_STEP)
    hl = [slice(h * LANES, (h + 1) * LANES) for h in heads]

    def scores(h, ki):
        k = k_ref[pl.ds(pl.multiple_of(ki * tk, tk), tk), hl[h]]
        return lax.dot_general(k, q_ref[:, :, hl[h]].reshape(2 * tq, LANES),
                               (((1,), (1,)), ((), ())), preferred_element_type=F32)

    def values(h, ki, p, alpha):
        acc_s[h] = alpha * acc_s[h] + jnp.dot(vt_ref[ki, h], p, preferred_element_type=F32)

    st = [scores(h, qi) for h in heads]
    for h in heads:
        st_s[h, 0] = scores(h, 0)

    kr = lax.broadcasted_iota(jnp.int32, (tk, 2 * tq), 0)
    qc = lax.broadcasted_iota(jnp.int32, (tk, 2 * tq), 1)
    qc = jnp.where(qc >= tq, qc - tq, qc)
    m0 = []
    for h in heads:
        s_h = jnp.where(kr <= qc, st[h], NEG_INF)
        m0.append(jnp.max(s_h, axis=0, keepdims=True))
        p_s[h, 1] = jnp.exp2(s_h - m0[h]).astype(BF16)
        acc_s[h] = jnp.zeros(acc_s.shape[1:], acc_s.dtype)

    def trip(k, carry, slot):
        out = []
        nxt = jnp.minimum(k + 1, qi - 1)
        prv = jnp.where(k == 0, qi, k - 1)
        for h in heads:
            st_s[h, 1 - slot] = scores(h, nxt)
        for h in heads:
            m_prev, alpha_prev = carry[h]
            s_k = st_s[h, slot]
            m_new = jnp.maximum(m_prev, jnp.max(s_k, axis=0, keepdims=True))
            alpha = jnp.exp2(m_prev - m_new)
            values(h, prv, p_s[h, 1 - slot], alpha_prev)
            p_s[h, slot] = jnp.exp2(s_k - m_new).astype(BF16)
            out.append((m_new, alpha))
        return tuple(out)

    def two_trips(j, c):
        c = trip(2 * j, c, 0)
        return lax.cond(2 * j + 1 < qi, lambda cc: trip(2 * j + 1, cc, 1), lambda cc: cc, c)

    init = tuple((m0[h], jnp.ones_like(m0[h])) for h in heads)
    carry = lax.fori_loop(0, (qi + 1) // 2, two_trips, init)
    last_slot = (qi + 1) % 2
    for h in heads:
        values(h, jnp.where(qi == 0, 0, qi - 1), p_s[h, last_slot], carry[h][1])

    lv = lamv_ref[...]
    lam = (jnp.exp(jnp.sum(lv[0:1] * lv[1:2], axis=-1, keepdims=True))
           - jnp.exp(jnp.sum(lv[2:3] * lv[3:4], axis=-1, keepdims=True)) + lam_init)
    for h in heads:
        acc = acc_s[h]
        ot = acc[:DIFF_V_DIM] / acc[DIFF_V_DIM:DIFF_V_DIM + 1]
        od = (ot[:, :tq] - lam * ot[:, tq:]).T
        ms = jnp.mean(od * od, axis=-1, keepdims=True)
        o_ref[:, hl[h]] = (od * lax.rsqrt(ms + SUBLN_EPS) * sg_ref[...]
                           * (1.0 - lam_init)).astype(BF16)


def _attention(qs, kh, vt, lamv, sg, *, tq, lam_init):
    _, b, lp, _ = qs.shape
    nq = lp // tq
    hp = ATTN_HEADS_PER_STEP
    const = lambda bi, hg, qi: (0, 0)
    return pl.pallas_call(
        functools.partial(_attn_kernel, lam_init=lam_init),
        grid=(b, DIFF_HEADS // hp, nq),
        in_specs=[
            pl.BlockSpec((2, None, tq, hp * LANES), lambda bi, hg, qi: (0, bi, qi, hg)),
            pl.BlockSpec((None, lp, hp * LANES), lambda bi, hg, qi: (bi, 0, hg)),
            pl.BlockSpec((None, nq, hp, VT_ROWS, tq), lambda bi, hg, qi: (bi, 0, hg, 0, 0)),
            pl.BlockSpec((SUBLANES, LANES), const),
            pl.BlockSpec((1, LANES), const),
        ],
        out_specs=pl.BlockSpec((None, tq, hp * LANES), lambda bi, hg, qi: (bi, qi, hg)),
        out_shape=jax.ShapeDtypeStruct((b, lp, DIFF_WIDTH), BF16),
        scratch_shapes=[pltpu.VMEM((hp, VT_ROWS, 2 * tq), F32),
                        pltpu.VMEM((hp, 2, tq, 2 * tq), F32),
                        pltpu.VMEM((hp, 2, tq, 2 * tq), BF16)],
        compiler_params=_params(("arbitrary", "arbitrary", "arbitrary")),
        name="diffattn",
    )(qs, kh, vt, lamv, sg)


def _outproj_kernel(h_ref, oa_ref, ob_ref, pc_ref, pch_ref, cw_ref, wa_ref, wb_ref, wc_ref,
                    o_ref, *, tiles_per_batch):
    tm = h_ref.shape[0]
    first = (pl.program_id(0) % tiles_per_batch) == 0
    pc = pc_ref[...]
    bg = pc[:, :CONV_WIDTH]
    z = pc[:, CONV_WIDTH:2 * CONV_WIDTH] * pc[:, 2 * CONV_WIDTH:]
    ph = pch_ref[...]
    zh = jnp.where(first, 0.0, ph[:, CONV_WIDTH:2 * CONV_WIDTH] * ph[:, 2 * CONV_WIDTH:])
    row = lax.broadcasted_iota(jnp.int32, (tm, 1), 0)
    z1 = jnp.where(row == 0, zh[SUBLANES - 1:SUBLANES], pltpu.roll(z, 1, 0))
    z2 = jnp.where(row == 0, zh[SUBLANES - 2:SUBLANES - 1],
                   jnp.where(row == 1, zh[SUBLANES - 1:SUBLANES], pltpu.roll(z, 2, 0)))
    cw = cw_ref[...]
    y = cw[0:1] * z2 + cw[1:2] * z1 + cw[2:3] * z
    oc = (bg * y).astype(BF16)
    o_ref[...] = (h_ref[...]
                  + jnp.dot(oa_ref[...], wa_ref[...], preferred_element_type=F32)
                  + jnp.dot(ob_ref[...], wb_ref[...], preferred_element_type=F32)
                  + jnp.dot(oc, wc_ref[...], preferred_element_type=F32))


def _outproj(h, oa, ob, pc, cw, wa, wb, wc, *, tm, lp):
    r, d = h.shape
    row = lambda i: (i, 0)
    const = lambda i: (0, 0)
    halo = lambda i: (jnp.maximum(i * (tm // SUBLANES) - 1, 0), 0)
    return pl.pallas_call(
        functools.partial(_outproj_kernel, tiles_per_batch=lp // tm),
        grid=(r // tm,),
        in_specs=[
            pl.BlockSpec((tm, d), row),
            pl.BlockSpec((tm, RWKV_PAD), row),
            pl.BlockSpec((tm, DIFF_WIDTH), row),
            pl.BlockSpec((tm, PC_COLS), row),
            pl.BlockSpec((SUBLANES, PC_COLS), halo),
            pl.BlockSpec((SUBLANES, CONV_WIDTH), const),
            pl.BlockSpec(wa.shape, const),
            pl.BlockSpec(wb.shape, const),
            pl.BlockSpec(wc.shape, const),
        ],
        out_specs=pl.BlockSpec((tm, d), row),
        out_shape=jax.ShapeDtypeStruct((r, d), F32),
        compiler_params=_params(("arbitrary",)),
        name="outproj",
    )(h, oa, ob, pc, pc, cw, wa, wb, wc)


def _swiglu_block(xb, wg_ref, wu_ref, wd_ref, side_work=None):
    d_ff = wg_ref.shape[1]
    acc = jnp.zeros((xb.shape[0], wd_ref.shape[1]), F32)
    for c in range(d_ff // MXU_DIM):
        cs = slice(c * MXU_DIM, (c + 1) * MXU_DIM)
        gt = jnp.dot(xb, wg_ref[:, cs], preferred_element_type=F32)
        up = jnp.dot(xb, wu_ref[:, cs], preferred_element_type=F32)
        act = (jax.nn.silu(gt) * up).astype(BF16)
        acc = acc + jnp.dot(act, wd_ref[cs, :], preferred_element_type=F32)
        if side_work is not None:
            side_work(c)
    return acc


def _rms(x, g):
    ms = jnp.mean(x * x, axis=-1, keepdims=True)
    return x * lax.rsqrt(ms + NORM_EPS) * g


def _ffn_kernel(h_ref, g_ref, wg_ref, wu_ref, wd_ref, o_ref):
    x = h_ref[...]
    xn = _rms(x, g_ref[...]).astype(BF16)
    o_ref[...] = x + _swiglu_block(xn, wg_ref, wu_ref, wd_ref)


def _dense_ffn(h, g, wg, wu, wd, *, tm):
    r, d = h.shape
    row = lambda i: (i, 0)
    const = lambda i: (0, 0)
    return pl.pallas_call(
        _ffn_kernel,
        grid=(r // tm,),
        in_specs=[
            pl.BlockSpec((tm, d), row),
            pl.BlockSpec((1, d), const),
            pl.BlockSpec(wg.shape, const),
            pl.BlockSpec(wu.shape, const),
            pl.BlockSpec(wd.shape, const),
        ],
        out_specs=pl.BlockSpec((tm, d), row),
        out_shape=jax.ShapeDtypeStruct((r, d), F32),
        compiler_params=_params(("arbitrary",)),
        name="dense_ffn",
    )(h, g, wg, wu, wd)


def _router_kernel(h_ref, g_ref, rw_ref, hn_ref, route_ref):
    tm = h_ref.shape[0]
    xn = _rms(h_ref[...], g_ref[...])
    hn_ref[...] = xn
    logits = _dot_f32(xn, rw_ref[...])
    lane = lax.broadcasted_iota(jnp.int32, (tm, LANES), 1)
    logits = jnp.where(lane < N_EXPERTS, logits, -jnp.inf)
    m1 = jnp.max(logits, axis=-1, keepdims=True)
    i1 = jnp.min(jnp.where(logits == m1, lane, LANES), axis=-1, keepdims=True)
    rest = jnp.where(lane == i1, -jnp.inf, logits)
    m2 = jnp.max(rest, axis=-1, keepdims=True)
    i2 = jnp.min(jnp.where(rest == m2, lane, LANES), axis=-1, keepdims=True)
    e2 = jnp.exp(m2 - m1)
    den = 1.0 + e2
    route_ref[...] = jnp.where(lane == 0, i1.astype(F32),
                     jnp.where(lane == 1, i2.astype(F32),
                     jnp.where(lane == 2, 1.0 / den,
                     jnp.where(lane == 3, e2 / den, 0.0))))


def _router(h, g, rw, *, tm):
    r, d = h.shape
    row = lambda i: (i, 0)
    const = lambda i: (0, 0)
    return pl.pallas_call(
        _router_kernel,
        grid=(r // tm,),
        in_specs=[pl.BlockSpec((tm, d), row), pl.BlockSpec((1, d), const),
                  pl.BlockSpec((d, LANES), const)],
        out_specs=[pl.BlockSpec((tm, d), row), pl.BlockSpec((tm, LANES), row)],
        out_shape=[jax.ShapeDtypeStruct((r, d), F32), jax.ShapeDtypeStruct((r, LANES), F32)],
        compiler_params=_params(("arbitrary",)),
        name="router",
    )(h, g, rw)


def _row_copy(src, dst, sem, src_row, dst_row):
    return pltpu.make_async_copy(src.at[pl.ds(src_row, 1), :], dst.at[pl.ds(dst_row, 1), :], sem)


DMA_UNROLL = 8
DMA_ISSUE_SLICES = 8
MOE_BUFFERS = 3


def _moe_kernel(bexp_ref, rsrc_ref, rdst_ref, hn_hbm, wg_ref, wu_ref, wd_ref, y_hbm,
                xbuf, ybuf, gsem, ssem):
    del bexp_ref
    nbuf = xbuf.shape[0]
    blk = xbuf.shape[1]
    i = pl.program_id(0)
    nb = pl.num_programs(0)
    slot = i % nbuf
    slot_prev = (i + nbuf - 1) % nbuf
    slot_next2 = (i + 2) % nbuf
    dump0 = y_hbm.shape[0] - nbuf * blk

    def for_rows(fn):
        def body(j, c):
            for u in range(DMA_UNROLL):
                fn(j * DMA_UNROLL + u)
            return c
        lax.fori_loop(0, blk // DMA_UNROLL, body, 0)

    def gather(block, s, j):
        return _row_copy(hn_hbm, xbuf.at[s], gsem.at[s], rsrc_ref[block * blk + j], j)

    def scatter(dst_row, s, j):
        return _row_copy(ybuf.at[s], y_hbm, ssem.at[s], j, dst_row)

    @pl.when(i == 0)
    def _():
        for_rows(lambda j: gather(0, 0, j).start())
        for_rows(lambda j: gather(1, 1, j).start())
        ybuf[nbuf - 1] = jnp.zeros(ybuf.shape[1:], ybuf.dtype)
        for k in range(nbuf - 1):
            zero_fill = pltpu.make_async_copy(
                ybuf.at[nbuf - 1], y_hbm.at[pl.ds(dump0 + k * blk, blk), :], ssem.at[k])
            zero_fill.start()
            zero_fill.wait()

    for_rows(lambda j: gather(0, slot, j).wait())

    @pl.when(i >= 2)
    def _():
        for_rows(lambda j: scatter(0, slot, j).wait())

    ahead = jnp.minimum(i + 2, nb - 1)
    prev = jnp.maximum(i - 1, 0)
    per = -(-blk // DMA_ISSUE_SLICES)

    def side_work(c):
        for j in range(c * per, min((c + 1) * per, blk)):
            gather(ahead, slot_next2, j).start(priority=j % 2)
            dst = jnp.where(i == 0, dump0 + (nbuf - 1) * blk + j, rdst_ref[prev * blk + j])
            scatter(dst, slot_prev, j).start(priority=(j + 1) % 2)

    ybuf[slot] = _swiglu_block(xbuf[slot].astype(BF16), wg_ref, wu_ref, wd_ref, side_work)

    @pl.when(i == nb - 1)
    def _():
        for_rows(lambda j: gather(0, (i + 1) % nbuf, j).wait())
        for_rows(lambda j: gather(0, slot_next2, j).wait())
        for_rows(lambda j: scatter(0, (i + 1) % nbuf, j).wait())
        for_rows(lambda j: scatter(0, slot_prev, j).wait())
        for_rows(lambda j: scatter(rdst_ref[i * blk + j], slot, j).start())
        for_rows(lambda j: scatter(0, slot, j).wait())


def _moe_experts(block_expert, row_src, row_dst, hn, wg, wu, wd):
    nb = block_expert.shape[0]
    r, d = hn.shape
    d_ff = wg.shape[2]
    wmap = lambda i, be, rs, rd: (be[i], 0, 0)
    grid_spec = pltpu.PrefetchScalarGridSpec(
        num_scalar_prefetch=3,
        grid=(nb,),
        in_specs=[
            pl.BlockSpec(memory_space=pl.ANY),
            pl.BlockSpec((None, d, d_ff), wmap),
            pl.BlockSpec((None, d, d_ff), wmap),
            pl.BlockSpec((None, d_ff, d), wmap),
        ],
        out_specs=pl.BlockSpec(memory_space=pl.ANY),
        scratch_shapes=[pltpu.VMEM((MOE_BUFFERS, MOE_BLOCK, d), F32),
                        pltpu.VMEM((MOE_BUFFERS, MOE_BLOCK, d), F32),
                        pltpu.SemaphoreType.DMA((MOE_BUFFERS,)),
                        pltpu.SemaphoreType.DMA((MOE_BUFFERS,))],
    )
    return pl.pallas_call(
        _moe_kernel,
        grid_spec=grid_spec,
        out_shape=jax.ShapeDtypeStruct((TOP_K * r + MOE_BUFFERS * MOE_BLOCK, d), F32),
        compiler_params=_params(("arbitrary",)),
        name="moe_experts",
    )(block_expert, row_src, row_dst, hn, wg, wu, wd)


def _combine_kernel(h_ref, route_ref, y0_ref, y1_ref, o_ref):
    route = route_ref[...]
    o_ref[...] = h_ref[...] + route[:, 2:3] * y0_ref[...] + route[:, 3:4] * y1_ref[...]


def _moe_combine(h, route, y, *, tm):
    r, d = h.shape
    nt = r // tm
    row = lambda i: (i, 0)
    return pl.pallas_call(
        _combine_kernel,
        grid=(nt,),
        in_specs=[
            pl.BlockSpec((tm, d), row),
            pl.BlockSpec((tm, LANES), row),
            pl.BlockSpec((tm, d), row),
            pl.BlockSpec((tm, d), lambda i: (i + nt, 0)),
        ],
        out_specs=pl.BlockSpec((tm, d), row),
        out_shape=jax.ShapeDtypeStruct((r, d), F32),
        compiler_params=_params(("arbitrary",)),
        name="moe_combine",
    )(h, route, y, y)


def _slot_owner_kernel(dest_ref, owner_ref):
    n_pairs = dest_ref.shape[0]
    n_slots = owner_ref.shape[0]

    def clear(i, c):
        for u in range(DMA_UNROLL):
            owner_ref[i * DMA_UNROLL + u] = -1
        return c

    lax.fori_loop(0, n_slots // DMA_UNROLL, clear, 0)

    def place(i, c):
        for u in range(DMA_UNROLL):
            p = i * DMA_UNROLL + u
            owner_ref[dest_ref[p]] = p
        return c

    lax.fori_loop(0, n_pairs // DMA_UNROLL, place, 0)


def _slot_owner(dest, n_slots):
    assert dest.shape[0] % DMA_UNROLL == 0 and n_slots % DMA_UNROLL == 0
    smem = pl.BlockSpec(memory_space=pltpu.SMEM)
    return pl.pallas_call(
        _slot_owner_kernel,
        in_specs=[smem],
        out_specs=smem,
        out_shape=jax.ShapeDtypeStruct((n_slots,), jnp.int32),
        name="slot_owner",
    )(dest)


def _routing_tables(route):
    r = route.shape[0]
    tk = r * TOP_K
    flat_e = route[:, :TOP_K].reshape(tk).astype(jnp.int32)
    onehot = (flat_e[:, None] == jnp.arange(N_EXPERTS, dtype=jnp.int32)[None, :]).astype(jnp.int32)
    csum = jnp.cumsum(onehot, axis=0)
    rank = jnp.sum(csum * onehot, axis=1) - 1
    counts = csum[-1]
    padded = (counts + MOE_BLOCK - 1) // MOE_BLOCK * MOE_BLOCK
    pend = jnp.cumsum(padded)
    pstart = pend - padded
    dest = pstart[flat_e] + rank
    nb = -(-tk // MOE_BLOCK) + N_EXPERTS
    n_slots = nb * MOE_BLOCK
    pair = _slot_owner(dest, n_slots)
    used = pair >= 0
    row_src = jnp.where(used, pair // TOP_K, 0)
    slot_id = jnp.arange(n_slots, dtype=jnp.int32)
    dump_row = tk + (slot_id // MOE_BLOCK) % MOE_BUFFERS * MOE_BLOCK + slot_id % MOE_BLOCK
    row_dst = jnp.where(used, (pair % TOP_K) * r + pair // TOP_K, dump_row)
    block_start = jnp.arange(nb, dtype=jnp.int32) * MOE_BLOCK
    block_expert = jnp.minimum(jnp.searchsorted(pend, block_start, side='right'),
                               N_EXPERTS - 1).astype(jnp.int32)
    return block_expert, row_src, row_dst


def _pad_heads(t, axis):
    axis = axis % t.ndim
    shp = t.shape
    t = t.reshape(shp[:axis] + (RWKV_HEADS, RWKV_HEAD_DIM) + shp[axis + 1:])
    pad = [(0, 0)] * t.ndim
    pad[axis + 1] = (0, HEAD_PAD - RWKV_HEAD_DIM)
    t = jnp.pad(t, pad)
    return t.reshape(shp[:axis] + (RWKV_PAD,) + shp[axis + 1:])


def _cast_kernel(x_ref, o_ref):
    o_ref[...] = x_ref[...].astype(o_ref.dtype)


def _to_bf16(w, layer):
    _, e, a, b = w.shape
    return pl.pallas_call(
        _cast_kernel,
        grid=(e, 2),
        in_specs=[pl.BlockSpec((None, None, a // 2, b), lambda i, j: (layer, i, j, 0))],
        out_specs=pl.BlockSpec((None, a // 2, b), lambda i, j: (i, j, 0)),
        out_shape=jax.ShapeDtypeStruct((e, a, b), BF16),
        compiler_params=_params(("arbitrary", "arbitrary")),
        name="to_bf16",
    )(w)


def _rope_tables(lp):
    half = ROPE_DIM // 2
    inv_freq = ROPE_THETA ** (-jnp.arange(0, ROPE_DIM, 2, dtype=F32) / ROPE_DIM)
    ang = jnp.arange(lp, dtype=F32)[:, None] * inv_freq[None, :]
    cos, sin = jnp.cos(ang), jnp.sin(ang)
    ones = jnp.ones((lp, DIFF_QK_DIM - ROPE_DIM), F32)
    zeros = jnp.zeros((lp, DIFF_QK_DIM - ROPE_DIM), F32)
    zh = jnp.zeros((lp, half), F32)
    rc = jnp.concatenate([cos, cos, ones], axis=1)
    rs1 = jnp.concatenate([-sin, zh, zeros], axis=1)
    rs2 = jnp.concatenate([zh, sin, zeros], axis=1)
    tile2 = lambda a: jnp.concatenate([a, a], axis=1)
    return tile2(rc), tile2(rs1), tile2(rs2)


def kernel(x, meta_tokens, mix_norm_g, w_in, tm_mu, tm_w0, tm_w_decay_up, tm_a0, tm_w_a_up, tm_w_g_up, tm_k_k, tm_k_a, tm_r_k, tm_gn_g, tm_gn_b, da_q_norm_g, da_k_norm_g, da_lambda_q1, da_lambda_k1, da_lambda_q2, da_lambda_k2, da_subln_g, sc_conv_w, w_out, ffn_norm_g, ffn_w_gate, ffn_w_up, ffn_w_down, router_w, moe_w_gate, moe_w_up, moe_w_down):
    b, seq, d = x.shape
    depth = w_in.shape[0]
    l = N_META + seq
    lp = -(-l // LANES) * LANES
    tm = _row_tile(lp)
    r = b * lp
    s1 = RWKV_WIDTH

    meta = jnp.broadcast_to(meta_tokens.astype(x.dtype)[None], (b, N_META, d))
    h = jnp.concatenate([meta, x, jnp.zeros((b, lp - l, d), x.dtype)], axis=1).reshape(r, d)
    rc, rs1, rs2 = _rope_tables(lp)

    for i in range(depth):
        lam_init = 0.8 - 0.6 * math.exp(-0.3 * i)
        wi = w_in[i]
        w_cat = jnp.concatenate(
            [_pad_heads(wi[:, 0:s1], 1), _pad_heads(wi[:, s1:2 * s1], 1),
             _pad_heads(wi[:, 2 * s1:3 * s1], 1), wi[:, 3 * s1:]], axis=1).astype(BF16)
        qg = jnp.tile(da_q_norm_g[i], 2)[None]
        kg = jnp.tile(da_k_norm_g[i], 2)[None]
        pa, qs, kh, vt, pc = _inproj(h, mix_norm_g[i][None], w_cat, rc, rs1, rs2, qg, kg,
                                     tm=tm, lp=lp)

        mu = tm_mu[i]
        mu_p = jnp.concatenate([_pad_heads(mu[0:s1], 0), _pad_heads(mu[s1:2 * s1], 0),
                                _pad_heads(mu[2 * s1:3 * s1], 0), mu[3 * s1:]])[None]
        vecs = jnp.stack([_pad_heads(tm_w0[i], 0), _pad_heads(tm_a0[i], 0),
                          _pad_heads(tm_k_k[i], 0), _pad_heads(tm_k_a[i], 0),
                          _pad_heads(tm_r_k[i].reshape(s1), 0), _pad_heads(tm_gn_g[i], 0),
                          _pad_heads(tm_gn_b[i], 0), jnp.zeros((RWKV_PAD,), F32)])
        zl = lambda n: jnp.zeros((n, RWKV_PAD), F32)
        wd_p = jnp.concatenate([_pad_heads(tm_w_decay_up[i], 1), zl(AAA_LORA + GATE_LORA)], axis=0)
        wa_p = jnp.concatenate([zl(DECAY_LORA), _pad_heads(tm_w_a_up[i], 1), zl(GATE_LORA)], axis=0)
        wg_p = jnp.concatenate([zl(DECAY_LORA + AAA_LORA), _pad_heads(tm_w_g_up[i], 1)], axis=0)
        oa = _rwkv(pa.reshape(b, lp, PA_COLS), mu_p, vecs, wd_p, wa_p, wg_p, tb=tm)

        lamv = jnp.zeros((SUBLANES, LANES), F32).at[0:4, :DIFF_QK_DIM].set(
            jnp.stack([da_lambda_q1[i], da_lambda_k1[i], da_lambda_q2[i], da_lambda_k2[i]]))
        ob = _attention(qs.reshape(2, b, lp, DIFF_WIDTH), kh.reshape(b, lp, DIFF_WIDTH),
                        vt.reshape(b, lp // tm, DIFF_HEADS, VT_ROWS, tm), lamv, da_subln_g[i][None],
                        tq=tm, lam_init=lam_init)

        wo = w_out[i]
        cw = jnp.zeros((SUBLANES, CONV_WIDTH), F32).at[:CONV_K].set(sc_conv_w[i])
        h = _outproj(h, oa.reshape(r, RWKV_PAD), ob.reshape(r, DIFF_WIDTH), pc, cw,
                     _pad_heads(wo[:s1], 0).astype(BF16), wo[s1:s1 + DIFF_WIDTH].astype(BF16),
                     wo[s1 + DIFF_WIDTH:].astype(BF16), tm=tm, lp=lp)

        j = i // 2
        if i % 2 == 0:
            h = _dense_ffn(h, ffn_norm_g[i][None], ffn_w_gate[j].astype(BF16),
                           ffn_w_up[j].astype(BF16), ffn_w_down[j].astype(BF16), tm=tm)
        else:
            rw = jnp.zeros((d, LANES), F32).at[:, :N_EXPERTS].set(router_w[j])
            hn, route = _router(h, ffn_norm_g[i][None], rw, tm=tm)
            block_expert, row_src, row_dst = _routing_tables(route)
            y = _moe_experts(block_expert, row_src, row_dst, hn, _to_bf16(moe_w_gate, j),
                             _to_bf16(moe_w_up, j), _to_bf16(moe_w_down, j))
            h = _moe_combine(h, route, y, tm=tm)

    return h.reshape(b, lp, d)[:, N_META:l]
```

```python
import functools
import math

import jax
import jax.numpy as jnp
from jax import lax
from jax.experimental import pallas as pl
from jax.experimental.pallas import tpu as pltpu

N_META = 16
RWKV_HEAD_DIM = 64
RWKV_HEADS = 4
RWKV_WIDTH = RWKV_HEADS * RWKV_HEAD_DIM
DECAY_LORA = 32
AAA_LORA = 32
GATE_LORA = 64
LORA_COLS = DECAY_LORA + AAA_LORA + GATE_LORA
DIFF_QK_DIM = 64
DIFF_V_DIM = 128
DIFF_HEADS = 4
DIFF_WIDTH = DIFF_HEADS * DIFF_V_DIM
CONV_WIDTH = 256
CONV_K = 3
ROPE_THETA = 500000.0
ROPE_DIM = DIFF_QK_DIM // 4
N_EXPERTS = 8
TOP_K = 2
MOE_BLOCK = 256
NORM_EPS = 1e-6
RWKV_GN_EPS = 64e-5
SUBLN_EPS = 1e-5
NEG_INF = -1e30

LANES = 128
SUBLANES = 8
MXU_DIM = 256
VMEM_LIMIT_BYTES = 56 * 1024 * 1024

HEAD_PAD = LANES
RWKV_PAD = RWKV_HEADS * HEAD_PAD
PA_COLS = 3 * RWKV_PAD + LORA_COLS
PB_COLS = 3 * DIFF_WIDTH
PC_COLS = 3 * CONV_WIDTH
VT_ROWS = DIFF_V_DIM + 16
LOG2_E = math.log2(math.e)
CHUNK = 64

F32 = jnp.float32
BF16 = jnp.bfloat16
HIGHEST = lax.Precision.HIGHEST


def _dot(a, b):
    return jnp.dot(a.astype(BF16), b.astype(BF16), preferred_element_type=F32)


def _dot_nt(a, b):
    return lax.dot_general(a.astype(BF16), b.astype(BF16), (((1,), (1,)), ((), ())),
                           preferred_element_type=F32)


def _dot_f32(a, b):
    return jnp.dot(a, b, preferred_element_type=F32, precision=HIGHEST)


def _row_tile(lp):
    for t in (384, 256, 128):
        if lp % t == 0:
            return t
    raise ValueError(f"padded length {lp} is not a multiple of {LANES}")


def _params(sem):
    return pltpu.CompilerParams(dimension_semantics=sem, vmem_limit_bytes=VMEM_LIMIT_BYTES)


def _inproj_kernel(h_ref, g_ref, w_ref, rc_ref, rs1_ref, rs2_ref, qg_ref, kg_ref,
                   pa_ref, qs_ref, kh_ref, vt_ref, pc_ref):
    tm = h_ref.shape[0]
    x = h_ref[...]
    ms = jnp.mean(x * x, axis=-1, keepdims=True)
    xn = (x * lax.rsqrt(ms + NORM_EPS) * g_ref[...]).astype(BF16)
    pa_ref[...] = jnp.dot(xn, w_ref[:, :PA_COLS], preferred_element_type=F32)
    pc_ref[...] = jnp.dot(xn, w_ref[:, PA_COLS + PB_COLS:], preferred_element_type=F32)
    pb = jnp.dot(xn, w_ref[:, PA_COLS:PA_COLS + PB_COLS], preferred_element_type=F32)

    lane = lax.broadcasted_iota(jnp.int32, (tm, LANES), 1)
    lo = lane < DIFF_QK_DIM
    rc, rs1, rs2 = rc_ref[...], rs1_ref[...], rs2_ref[...]

    def norm_rope(t, g):
        t2 = t * t
        ss_lo = jnp.sum(jnp.where(lo, t2, 0.0), axis=-1, keepdims=True)
        ss_hi = jnp.sum(jnp.where(lo, 0.0, t2), axis=-1, keepdims=True)
        inv = jnp.where(lo, lax.rsqrt(ss_lo / DIFF_QK_DIM + NORM_EPS),
                        lax.rsqrt(ss_hi / DIFF_QK_DIM + NORM_EPS))
        tn = t * inv * g
        half = ROPE_DIM // 2
        return tn * rc + pltpu.roll(tn, LANES - half, 1) * rs1 + pltpu.roll(tn, half, 1) * rs2

    for hd in range(DIFF_HEADS):
        cs = slice(hd * LANES, (hd + 1) * LANES)
        q = norm_rope(pb[:, cs], qg_ref[...]) * (DIFF_QK_DIM ** -0.5 * LOG2_E)
        qs_ref[0, :, cs] = jnp.where(lo, q, 0.0).astype(BF16)
        qs_ref[1, :, cs] = jnp.where(lo, 0.0, q).astype(BF16)
        kcs = slice(DIFF_WIDTH + hd * LANES, DIFF_WIDTH + (hd + 1) * LANES)
        kh_ref[:, cs] = norm_rope(pb[:, kcs], kg_ref[...]).astype(BF16)
        vcs = slice(2 * DIFF_WIDTH + hd * LANES, 2 * DIFF_WIDTH + (hd + 1) * LANES)
        vt_ref[0, hd, :DIFF_V_DIM] = pb[:, vcs].T.astype(BF16)
        vt_ref[0, hd, DIFF_V_DIM:] = jnp.ones((VT_ROWS - DIFF_V_DIM, tm), BF16)


def _inproj(h, g, w, rc, rs1, rs2, qg, kg, *, tm, lp):
    r, d = h.shape
    nt_b = lp // tm
    row = lambda i: (i, 0)
    const = lambda i: (0, 0)
    rope = lambda i: (i % nt_b, 0)
    return pl.pallas_call(
        _inproj_kernel,
        grid=(r // tm,),
        in_specs=[
            pl.BlockSpec((tm, d), row),
            pl.BlockSpec((1, d), const),
            pl.BlockSpec(w.shape, const),
            pl.BlockSpec((tm, LANES), rope),
            pl.BlockSpec((tm, LANES), rope),
            pl.BlockSpec((tm, LANES), rope),
            pl.BlockSpec((1, LANES), const),
            pl.BlockSpec((1, LANES), const),
        ],
        out_specs=[
            pl.BlockSpec((tm, PA_COLS), row),
            pl.BlockSpec((2, tm, DIFF_WIDTH), lambda i: (0, i, 0)),
            pl.BlockSpec((tm, DIFF_WIDTH), row),
            pl.BlockSpec((1, DIFF_HEADS, VT_ROWS, tm), lambda i: (i, 0, 0, 0)),
            pl.BlockSpec((tm, PC_COLS), row),
        ],
        out_shape=[
            jax.ShapeDtypeStruct((r, PA_COLS), F32),
            jax.ShapeDtypeStruct((2, r, DIFF_WIDTH), BF16),
            jax.ShapeDtypeStruct((r, DIFF_WIDTH), BF16),
            jax.ShapeDtypeStruct((r // tm, DIFF_HEADS, VT_ROWS, tm), BF16),
            jax.ShapeDtypeStruct((r, PC_COLS), F32),
        ],
        compiler_params=_params(("arbitrary",)),
        name="inproj",
    )(h, g, w, rc, rs1, rs2, qg, kg)


def _split3(x):
    hi = x.astype(BF16)
    r1 = x - hi.astype(F32)
    mid = r1.astype(BF16)
    lo = (r1 - mid.astype(F32)).astype(BF16)
    return hi, mid, lo


def _dot_hi(a, b):
    ah = a.astype(BF16)
    al = (a - ah.astype(F32)).astype(BF16)
    bh = b.astype(BF16)
    bl = (b - bh.astype(F32)).astype(BF16)
    d = lambda p, q: jnp.dot(p, q, preferred_element_type=F32)
    return d(ah, bh) + d(ah, bl) + d(al, bh)


def _bmm(a, b):
    return jnp.einsum('cik,ckj->cij', a.astype(BF16), b.astype(BF16), preferred_element_type=F32)


def _bmm_nt(a, b):
    return jnp.einsum('cik,cjk->cij', a.astype(BF16), b.astype(BF16), preferred_element_type=F32)


def _rwkv_kernel(pa_ref, halo_ref, mu_ref, vec_ref, wd_ref, wa_ref, wg_ref, oa_ref, hstate_s):
    tb = pa_ref.shape[0]
    nc = tb // CHUNK
    i = pl.program_id(1)

    @pl.when(i == 0)
    def _():
        hstate_s[...] = jnp.zeros_like(hstate_s)

    x = pa_ref[...]
    prev_row = jnp.where(i == 0, 0.0, halo_ref[SUBLANES - 1:SUBLANES, :])
    row = lax.broadcasted_iota(jnp.int32, (tb, 1), 0)
    prev = jnp.where(row == 0, prev_row, pltpu.roll(x, 1, 0))
    hx = x + (prev - x) * mu_ref[...]
    r = hx[:, 0:RWKV_PAD]
    k = hx[:, RWKV_PAD:2 * RWKV_PAD]
    v = hx[:, 2 * RWKV_PAD:3 * RWKV_PAD]
    lora = hx[:, 3 * RWKV_PAD:]
    w0, a0, k_k, k_a = vec_ref[0:1, :], vec_ref[1:2, :], vec_ref[2:3, :], vec_ref[3:4, :]
    r_k, gn_g, gn_b = vec_ref[4:5, :], vec_ref[5:6, :], vec_ref[6:7, :]

    w = -jax.nn.softplus(-(w0 + _dot_hi(jnp.tanh(lora), wd_ref[...]))) - 0.5
    wl = -jnp.exp(w)
    alr = jax.nn.sigmoid(a0 + _dot(lora, wa_ref[...]))
    gate = _dot(jax.nn.sigmoid(lora), wg_ref[...])
    kk = k * k_k
    k2 = k * (1.0 + (alr - 1.0) * k_a)
    rkr = r * k2 * r_k

    ri = lax.broadcasted_iota(jnp.int32, (CHUNK, CHUNK), 0)
    ci = lax.broadcasted_iota(jnp.int32, (CHUNK, CHUNK), 1)
    lane_c = lax.broadcasted_iota(jnp.int32, (CHUNK, HEAD_PAD), 1)
    row_c = lax.broadcasted_iota(jnp.int32, (CHUNK, HEAD_PAD), 0)
    col_c = jnp.where(lane_c >= CHUNK, lane_c - CHUNK, lane_c)
    low_incl2 = (col_c <= row_c)[None]
    low_strict2 = (col_c < row_c)[None]
    tril_b = jnp.where(ci <= ri, 1.0, 0.0).astype(BF16)
    eye_k = (lax.broadcasted_iota(jnp.int32, (HEAD_PAD, HEAD_PAD), 0)
             == lax.broadcasted_iota(jnp.int32, (HEAD_PAD, HEAD_PAD), 1))[None]
    n_sq = int(math.log2(CHUNK))

    parts = _split3(wl)
    cums = []
    for c in range(nc):
        rows = slice(c * CHUNK, (c + 1) * CHUNK)
        cums.append(sum(jnp.dot(tril_b, p[rows], preferred_element_type=F32) for p in parts))
    cum_all = jnp.concatenate(cums, axis=0)

    lane = lax.broadcasted_iota(jnp.int32, (tb, HEAD_PAD), 1)
    real = lane < RWKV_HEAD_DIM
    to3 = lambda t: t.reshape(nc, CHUNK, HEAD_PAD)
    heads = range(RWKV_HEADS)
    hsl = [slice(hd * HEAD_PAD, (hd + 1) * HEAD_PAD) for hd in heads]
    kkn, bonus = [], []
    for cs in hsl:
        kk_h = kk[:, cs]
        nrm = jnp.sqrt(jnp.sum(kk_h * kk_h, axis=-1, keepdims=True))
        kkn.append(kk_h / jnp.maximum(nrm, 1e-12))
        bonus.append(jnp.sum(rkr[:, cs], axis=-1, keepdims=True) * v[:, cs])

    vc = [to3(v[:, cs]) for cs in hsl]
    zero_v = jnp.zeros_like(vc[0])
    ah, rh, last, g, bk_last = [], [], [], [], []
    for hd, cs in enumerate(hsl):
        wl3, cum = to3(wl[:, cs]), to3(cum_all[:, cs])
        rr, kc = to3(r[:, cs]), to3(k2[:, cs])
        ac, bc = to3(-kkn[hd]), to3(kkn[hd] * alr[:, cs])
        last.append(cum[:, CHUNK - 1:CHUNK, :])
        e_neg = jnp.exp(-cum)
        e_last = jnp.exp(last[hd] - cum)
        ah.append(ac * jnp.exp(cum - wl3))
        rh.append(rr * jnp.exp(cum))
        bk_last.append(jnp.concatenate([bc * e_last, kc * e_last], axis=1))
        g.append(_bmm_nt(jnp.concatenate([ah[hd], rh[hd]], axis=1),
                         jnp.concatenate([bc * e_neg, kc * e_neg], axis=1)))
    a_abk = [jnp.where(low_strict2, g[hd][:, :CHUNK], 0.0) for hd in heads]
    r_bk = [jnp.where(low_incl2, g[hd][:, CHUNK:], 0.0) for hd in heads]
    xx = [jnp.concatenate([_bmm(a_abk[hd], jnp.concatenate([zero_v, vc[hd]], axis=1)), ah[hd]],
                          axis=2) for hd in heads]
    ppad = [jnp.where(lane_c < CHUNK, a_abk[hd], 0.0) for hd in heads]
    for j in range(n_sq):
        for hd in heads:
            pj = ppad[hd][:, :, :CHUNK]
            if j + 1 < n_sq:
                res = _bmm(pj, jnp.concatenate([xx[hd], ppad[hd]], axis=2))
                xx[hd] = xx[hd] + res[:, :, :2 * HEAD_PAD]
                ppad[hd] = res[:, :, 2 * HEAD_PAD:]
            else:
                xx[hd] = xx[hd] + _bmm(pj, xx[hd])
    qm, y0, n0 = [], [], []
    for hd in heads:
        lhs = jnp.concatenate([r_bk[hd], jnp.swapaxes(bk_last[hd], 1, 2)], axis=1)
        rhs = jnp.concatenate([xx[hd], jnp.concatenate([vc[hd], zero_v], axis=2)], axis=1)
        out = _bmm(lhs, rhs)
        y0.append(out[:, :CHUNK, :HEAD_PAD])
        n0.append(out[:, CHUNK:, :HEAD_PAD])
        q = rh[hd] + out[:, :CHUNK, HEAD_PAD:]
        m = jnp.where(eye_k, jnp.exp(last[hd]), 0.0) + out[:, CHUNK:, HEAD_PAD:]
        qm.append(jnp.concatenate([q, m], axis=1).astype(BF16))

    hs = [hstate_s[hd] for hd in heads]
    ys = [[] for _ in heads]
    for c in range(nc):
        for hd in heads:
            res = jnp.dot(qm[hd][c], hs[hd].astype(BF16), preferred_element_type=F32)
            ys[hd].append(res[:CHUNK] + y0[hd][c])
            hs[hd] = res[CHUNK:] + n0[hd][c]

    for hd, cs in enumerate(hsl):
        hstate_s[hd] = hs[hd]
        y = jnp.concatenate(ys[hd], axis=0)
        mean = jnp.sum(y, axis=-1, keepdims=True) / RWKV_HEAD_DIM
        dlt = jnp.where(real, y - mean, 0.0)
        var = jnp.sum(dlt * dlt, axis=-1, keepdims=True) / RWKV_HEAD_DIM
        yn = dlt * lax.rsqrt(var + RWKV_GN_EPS) * gn_g[:, cs] + gn_b[:, cs]
        oa_ref[:, cs] = ((yn + bonus[hd]) * gate[:, cs]).astype(BF16)


def _rwkv(pa, mu, vecs, wd, wa, wg, *, tb):
    b, lp, _ = pa.shape
    tile = lambda bi, i: (bi, i, 0)
    const = lambda bi, i: (0, 0)
    halo = lambda bi, i: (bi, jnp.maximum(i * (tb // SUBLANES) - 1, 0), 0)
    return pl.pallas_call(
        _rwkv_kernel,
        grid=(b, lp // tb),
        in_specs=[
            pl.BlockSpec((None, tb, PA_COLS), tile),
            pl.BlockSpec((None, SUBLANES, PA_COLS), halo),
            pl.BlockSpec((1, PA_COLS), const),
            pl.BlockSpec((SUBLANES, RWKV_PAD), const),
            pl.BlockSpec((LORA_COLS, RWKV_PAD), const),
            pl.BlockSpec((LORA_COLS, RWKV_PAD), const),
            pl.BlockSpec((LORA_COLS, RWKV_PAD), const),
        ],
        out_specs=pl.BlockSpec((None, tb, RWKV_PAD), tile),
        out_shape=jax.ShapeDtypeStruct((b, lp, RWKV_PAD), BF16),
        scratch_shapes=[pltpu.VMEM((RWKV_HEADS, HEAD_PAD, HEAD_PAD), F32)],
        compiler_params=_params(("arbitrary", "arbitrary")),
        name="rwkv7",
    )(pa, pa, mu, vecs, wd, wa, wg)


ATTN_HEADS_PER_STEP = 4


def _attn_kernel(q_ref, k_ref, vt_ref, lamv_ref, sg_ref, o_ref, acc_s, st_s, p_s, *, lam_init):
    tq = o_ref.shape[0]
    tk = vt_ref.shape[3]
    qi = pl.program_id(2)
    heads = range(ATTN_HEADS_PER_STEP)
    hl = [slice(h * LANES, (h + 1) * LANES) for h in heads]

    def scores(h, ki):
        k = k_ref[pl.ds(pl.multiple_of(ki * tk, tk), tk), hl[h]]
        return lax.dot_general(k, q_ref[:, :, hl[h]].reshape(2 * tq, LANES),
                               (((1,), (1,)), ((), ())), preferred_element_type=F32)

    def values(h, ki, p, alpha):
        acc_s[h] = alpha * acc_s[h] + jnp.dot(vt_ref[ki, h], p, preferred_element_type=F32)

    st = [scores(h, qi) for h in heads]
    for h in heads:
        st_s[h, 0] = scores(h, 0)

    kr = lax.broadcasted_iota(jnp.int32, (tk, 2 * tq), 0)
    qc = lax.broadcasted_iota(jnp.int32, (tk, 2 * tq), 1)
    qc = jnp.where(qc >= tq, qc - tq, qc)
    m0 = []
    for h in heads:
        s_h = jnp.where(kr <= qc, st[h], NEG_INF)
        m0.append(jnp.max(s_h, axis=0, keepdims=True))
        p_s[h, 1] = jnp.exp2(s_h - m0[h]).astype(BF16)
        acc_s[h] = jnp.zeros(acc_s.shape[1:], acc_s.dtype)

    def trip(k, carry, slot):
        out = []
        nxt = jnp.minimum(k + 1, qi - 1)
        prv = jnp.where(k == 0, qi, k - 1)
        for step in range(len(heads) + 1):
            if step < len(heads):
                st_s[step, 1 - slot] = scores(step, nxt)
            if step >= 1:
                h = step - 1
                m_prev, alpha_prev = carry[h]
                s_k = st_s[h, slot]
                m_new = jnp.maximum(m_prev, jnp.max(s_k, axis=0, keepdims=True))
                alpha = jnp.exp2(m_prev - m_new)
                values(h, prv, p_s[h, 1 - slot], alpha_prev)
                p_s[h, slot] = jnp.exp2(s_k - m_new).astype(BF16)
                out.append((m_new, alpha))
        return tuple(out)

    def two_trips(j, c):
        c = trip(2 * j, c, 0)
        return lax.cond(2 * j + 1 < qi, lambda cc: trip(2 * j + 1, cc, 1), lambda cc: cc, c)

    init = tuple((m0[h], jnp.ones_like(m0[h])) for h in heads)
    carry = lax.fori_loop(0, (qi + 1) // 2, two_trips, init)
    last_slot = (qi + 1) % 2
    for h in heads:
        values(h, jnp.where(qi == 0, 0, qi - 1), p_s[h, last_slot], carry[h][1])

    lv = lamv_ref[...]
    lam = (jnp.exp(jnp.sum(lv[0:1] * lv[1:2], axis=-1, keepdims=True))
           - jnp.exp(jnp.sum(lv[2:3] * lv[3:4], axis=-1, keepdims=True)) + lam_init)
    for h in heads:
        acc = acc_s[h]
        ot = acc[:DIFF_V_DIM] / acc[DIFF_V_DIM:DIFF_V_DIM + 1]
        od = (ot[:, :tq] - lam * ot[:, tq:]).T
        ms = jnp.mean(od * od, axis=-1, keepdims=True)
        o_ref[:, hl[h]] = (od * lax.rsqrt(ms + SUBLN_EPS) * sg_ref[...]
                           * (1.0 - lam_init)).astype(BF16)


def _attention(qs, kh, vt, lamv, sg, *, tq, lam_init):
    _, b, lp, _ = qs.shape
    nq = lp // tq
    hp = ATTN_HEADS_PER_STEP
    const = lambda bi, hg, qi: (0, 0)
    return pl.pallas_call(
        functools.partial(_attn_kernel, lam_init=lam_init),
        grid=(b, DIFF_HEADS // hp, nq),
        in_specs=[
            pl.BlockSpec((2, None, tq, hp * LANES), lambda bi, hg, qi: (0, bi, qi, hg)),
            pl.BlockSpec((None, lp, hp * LANES), lambda bi, hg, qi: (bi, 0, hg)),
            pl.BlockSpec((None, nq, hp, VT_ROWS, tq), lambda bi, hg, qi: (bi, 0, hg, 0, 0)),
            pl.BlockSpec((SUBLANES, LANES), const),
            pl.BlockSpec((1, LANES), const),
        ],
        out_specs=pl.BlockSpec((None, tq, hp * LANES), lambda bi, hg, qi: (bi, qi, hg)),
        out_shape=jax.ShapeDtypeStruct((b, lp, DIFF_WIDTH), BF16),
        scratch_shapes=[pltpu.VMEM((hp, VT_ROWS, 2 * tq), F32),
                        pltpu.VMEM((hp, 2, tq, 2 * tq), F32),
                        pltpu.VMEM((hp, 2, tq, 2 * tq), BF16)],
        compiler_params=_params(("arbitrary", "arbitrary", "arbitrary")),
        name="diffattn",
    )(qs, kh, vt, lamv, sg)


def _mixer_out(h_ref, oa_ref, ob_ref, pc_ref, pch_ref, cw_ref, wa_ref, wb_ref, wc_ref,
               tiles_per_batch):
    tm = h_ref.shape[0]
    first = (pl.program_id(0) % tiles_per_batch) == 0
    pc = pc_ref[...]
    bg = pc[:, :CONV_WIDTH]
    z = pc[:, CONV_WIDTH:2 * CONV_WIDTH] * pc[:, 2 * CONV_WIDTH:]
    ph = pch_ref[...]
    zh = jnp.where(first, 0.0, ph[:, CONV_WIDTH:2 * CONV_WIDTH] * ph[:, 2 * CONV_WIDTH:])
    row = lax.broadcasted_iota(jnp.int32, (tm, 1), 0)
    z1 = jnp.where(row == 0, zh[SUBLANES - 1:SUBLANES], pltpu.roll(z, 1, 0))
    z2 = jnp.where(row == 0, zh[SUBLANES - 2:SUBLANES - 1],
                   jnp.where(row == 1, zh[SUBLANES - 1:SUBLANES], pltpu.roll(z, 2, 0)))
    cw = cw_ref[...]
    y = cw[0:1] * z2 + cw[1:2] * z1 + cw[2:3] * z
    oc = (bg * y).astype(BF16)
    return (h_ref[...]
            + jnp.dot(oa_ref[...], wa_ref[...], preferred_element_type=F32)
            + jnp.dot(ob_ref[...], wb_ref[...], preferred_element_type=F32)
            + jnp.dot(oc, wc_ref[...], preferred_element_type=F32))


def _outproj_ffn_kernel(h_ref, oa_ref, ob_ref, pc_ref, pch_ref, cw_ref, wa_ref, wb_ref, wc_ref,
                        g_ref, wg_ref, wu_ref, wd_ref, o_ref, *, tiles_per_batch):
    h1 = _mixer_out(h_ref, oa_ref, ob_ref, pc_ref, pch_ref, cw_ref, wa_ref, wb_ref, wc_ref,
                    tiles_per_batch)
    o_ref[...] = h1 + _swiglu_block(_rms(h1, g_ref[...]).astype(BF16), wg_ref, wu_ref, wd_ref)


def _outproj_router_kernel(h_ref, oa_ref, ob_ref, pc_ref, pch_ref, cw_ref, wa_ref, wb_ref, wc_ref,
                           g_ref, rw_ref, o_ref, hn_ref, route_ref, *, tiles_per_batch):
    h1 = _mixer_out(h_ref, oa_ref, ob_ref, pc_ref, pch_ref, cw_ref, wa_ref, wb_ref, wc_ref,
                    tiles_per_batch)
    o_ref[...] = h1
    _route(h1, g_ref, rw_ref, hn_ref, route_ref)


def _outproj(h, oa, ob, pc, cw, wa, wb, wc, g, *, tm, lp, ffn=None, router_w=None):
    r, d = h.shape
    row = lambda i: (i, 0)
    const = lambda i: (0, 0)
    halo = lambda i: (jnp.maximum(i * (tm // SUBLANES) - 1, 0), 0)
    tail = list(ffn) if ffn is not None else [router_w]
    in_specs = [
        pl.BlockSpec((tm, d), row),
        pl.BlockSpec((tm, RWKV_PAD), row),
        pl.BlockSpec((tm, DIFF_WIDTH), row),
        pl.BlockSpec((tm, PC_COLS), row),
        pl.BlockSpec((SUBLANES, PC_COLS), halo),
        pl.BlockSpec((SUBLANES, CONV_WIDTH), const),
        pl.BlockSpec(wa.shape, const),
        pl.BlockSpec(wb.shape, const),
        pl.BlockSpec(wc.shape, const),
        pl.BlockSpec((1, d), const),
    ] + [pl.BlockSpec(t.shape, const) for t in tail]
    h_spec = pl.BlockSpec((tm, d), row)
    h_shape = jax.ShapeDtypeStruct((r, d), F32)
    if ffn is not None:
        body, name = _outproj_ffn_kernel, "outproj_ffn"
        out_specs, out_shape = h_spec, h_shape
    else:
        body, name = _outproj_router_kernel, "outproj_router"
        out_specs = [h_spec, h_spec, pl.BlockSpec((tm, LANES), row)]
        out_shape = [h_shape, h_shape, jax.ShapeDtypeStruct((r, LANES), F32)]
    return pl.pallas_call(
        functools.partial(body, tiles_per_batch=lp // tm),
        grid=(r // tm,),
        in_specs=in_specs,
        out_specs=out_specs,
        out_shape=out_shape,
        compiler_params=_params(("arbitrary",)),
        name=name,
    )(h, oa, ob, pc, pc, cw, wa, wb, wc, g, *tail)


def _swiglu_block(xb, wg_ref, wu_ref, wd_ref, side_work=None):
    d_ff = wg_ref.shape[1]
    acc = jnp.zeros((xb.shape[0], wd_ref.shape[1]), F32)
    for c in range(d_ff // MXU_DIM):
        cs = slice(c * MXU_DIM, (c + 1) * MXU_DIM)
        gt = jnp.dot(xb, wg_ref[:, cs], preferred_element_type=F32)
        up = jnp.dot(xb, wu_ref[:, cs], preferred_element_type=F32)
        act = (jax.nn.silu(gt) * up).astype(BF16)
        acc = acc + jnp.dot(act, wd_ref[cs, :], preferred_element_type=F32)
        if side_work is not None:
            side_work(c)
    return acc


def _rms(x, g):
    ms = jnp.mean(x * x, axis=-1, keepdims=True)
    return x * lax.rsqrt(ms + NORM_EPS) * g


def _route(h1, g_ref, rw_ref, hn_ref, route_ref):
    tm = h1.shape[0]
    xn = _rms(h1, g_ref[...])
    hn_ref[...] = xn
    logits = _dot_f32(xn, rw_ref[...])
    lane = lax.broadcasted_iota(jnp.int32, (tm, LANES), 1)
    logits = jnp.where(lane < N_EXPERTS, logits, -jnp.inf)
    m1 = jnp.max(logits, axis=-1, keepdims=True)
    i1 = jnp.min(jnp.where(logits == m1, lane, LANES), axis=-1, keepdims=True)
    rest = jnp.where(lane == i1, -jnp.inf, logits)
    m2 = jnp.max(rest, axis=-1, keepdims=True)
    i2 = jnp.min(jnp.where(rest == m2, lane, LANES), axis=-1, keepdims=True)
    e2 = jnp.exp(m2 - m1)
    den = 1.0 + e2
    route_ref[...] = jnp.where(lane == 0, i1.astype(F32),
                     jnp.where(lane == 1, i2.astype(F32),
                     jnp.where(lane == 2, 1.0 / den,
                     jnp.where(lane == 3, e2 / den, 0.0))))


def _row_copy(src, dst, sem, src_row, dst_row):
    return pltpu.make_async_copy(src.at[pl.ds(src_row, 1), :], dst.at[pl.ds(dst_row, 1), :], sem)


DMA_UNROLL = 8
DMA_ISSUE_SLICES = 8
MOE_BUFFERS = 3


def _moe_kernel(bexp_ref, rsrc_ref, rdst_ref, hn_hbm, wg_ref, wu_ref, wd_ref, y_hbm,
                xbuf, ybuf, gsem, ssem):
    del bexp_ref
    nbuf = xbuf.shape[0]
    blk = xbuf.shape[1]
    i = pl.program_id(0)
    nb = pl.num_programs(0)
    slot = i % nbuf
    slot_prev = (i + nbuf - 1) % nbuf
    slot_next2 = (i + 2) % nbuf
    dump0 = y_hbm.shape[0] - nbuf * blk

    def for_rows(fn):
        def body(j, c):
            for u in range(DMA_UNROLL):
                fn(j * DMA_UNROLL + u)
            return c
        lax.fori_loop(0, blk // DMA_UNROLL, body, 0)

    def gather(block, s, j):
        return _row_copy(hn_hbm, xbuf.at[s], gsem.at[s], rsrc_ref[block * blk + j], j)

    def scatter(dst_row, s, j):
        return _row_copy(ybuf.at[s], y_hbm, ssem.at[s], j, dst_row)

    @pl.when(i == 0)
    def _():
        for_rows(lambda j: gather(0, 0, j).start())
        for_rows(lambda j: gather(1, 1, j).start())
        ybuf[nbuf - 1] = jnp.zeros(ybuf.shape[1:], ybuf.dtype)
        for k in range(nbuf - 1):
            zero_fill = pltpu.make_async_copy(
                ybuf.at[nbuf - 1], y_hbm.at[pl.ds(dump0 + k * blk, blk), :], ssem.at[k])
            zero_fill.start()
            zero_fill.wait()

    for_rows(lambda j: gather(0, slot, j).wait())

    @pl.when(i >= 2)
    def _():
        for_rows(lambda j: scatter(0, slot, j).wait())

    ahead = jnp.minimum(i + 2, nb - 1)
    prev = jnp.maximum(i - 1, 0)
    per = -(-blk // DMA_ISSUE_SLICES)

    def side_work(c):
        for j in range(c * per, min((c + 1) * per, blk)):
            gather(ahead, slot_next2, j).start(priority=j % 2)
            dst = jnp.where(i == 0, dump0 + (nbuf - 1) * blk + j, rdst_ref[prev * blk + j])
            scatter(dst, slot_prev, j).start(priority=(j + 1) % 2)

    ybuf[slot] = _swiglu_block(xbuf[slot].astype(BF16), wg_ref, wu_ref, wd_ref, side_work)

    @pl.when(i == nb - 1)
    def _():
        for_rows(lambda j: gather(0, (i + 1) % nbuf, j).wait())
        for_rows(lambda j: gather(0, slot_next2, j).wait())
        for_rows(lambda j: scatter(0, (i + 1) % nbuf, j).wait())
        for_rows(lambda j: scatter(0, slot_prev, j).wait())
        for_rows(lambda j: scatter(rdst_ref[i * blk + j], slot, j).start())
        for_rows(lambda j: scatter(0, slot, j).wait())


def _moe_experts(block_expert, row_src, row_dst, hn, wg, wu, wd):
    nb = block_expert.shape[0]
    r, d = hn.shape
    d_ff = wg.shape[2]
    wmap = lambda i, be, rs, rd: (be[i], 0, 0)
    grid_spec = pltpu.PrefetchScalarGridSpec(
        num_scalar_prefetch=3,
        grid=(nb,),
        in_specs=[
            pl.BlockSpec(memory_space=pl.ANY),
            pl.BlockSpec((None, d, d_ff), wmap),
            pl.BlockSpec((None, d, d_ff), wmap),
            pl.BlockSpec((None, d_ff, d), wmap),
        ],
        out_specs=pl.BlockSpec(memory_space=pl.ANY),
        scratch_shapes=[pltpu.VMEM((MOE_BUFFERS, MOE_BLOCK, d), F32),
                        pltpu.VMEM((MOE_BUFFERS, MOE_BLOCK, d), F32),
                        pltpu.SemaphoreType.DMA((MOE_BUFFERS,)),
                        pltpu.SemaphoreType.DMA((MOE_BUFFERS,))],
    )
    return pl.pallas_call(
        _moe_kernel,
        grid_spec=grid_spec,
        out_shape=jax.ShapeDtypeStruct((TOP_K * r + MOE_BUFFERS * MOE_BLOCK, d), F32),
        compiler_params=_params(("arbitrary",)),
        name="moe_experts",
    )(block_expert, row_src, row_dst, hn, wg, wu, wd)


def _combine_kernel(h_ref, route_ref, y0_ref, y1_ref, o_ref):
    route = route_ref[...]
    o_ref[...] = h_ref[...] + route[:, 2:3] * y0_ref[...] + route[:, 3:4] * y1_ref[...]


def _moe_combine(h, route, y, *, tm):
    r, d = h.shape
    nt = r // tm
    row = lambda i: (i, 0)
    return pl.pallas_call(
        _combine_kernel,
        grid=(nt,),
        in_specs=[
            pl.BlockSpec((tm, d), row),
            pl.BlockSpec((tm, LANES), row),
            pl.BlockSpec((tm, d), row),
            pl.BlockSpec((tm, d), lambda i: (i + nt, 0)),
        ],
        out_specs=pl.BlockSpec((tm, d), row),
        out_shape=jax.ShapeDtypeStruct((r, d), F32),
        compiler_params=_params(("arbitrary",)),
        name="moe_combine",
    )(h, route, y, y)


def _slot_owner_kernel(dest_ref, owner_ref):
    n_pairs = dest_ref.shape[0]
    n_slots = owner_ref.shape[0]

    def clear(i, c):
        for u in range(DMA_UNROLL):
            owner_ref[i * DMA_UNROLL + u] = -1
        return c

    lax.fori_loop(0, n_slots // DMA_UNROLL, clear, 0)

    def place(i, c):
        for u in range(DMA_UNROLL):
            p = i * DMA_UNROLL + u
            owner_ref[dest_ref[p]] = p
        return c

    lax.fori_loop(0, n_pairs // DMA_UNROLL, place, 0)


def _slot_owner(dest, n_slots):
    assert dest.shape[0] % DMA_UNROLL == 0 and n_slots % DMA_UNROLL == 0
    smem = pl.BlockSpec(memory_space=pltpu.SMEM)
    return pl.pallas_call(
        _slot_owner_kernel,
        in_specs=[smem],
        out_specs=smem,
        out_shape=jax.ShapeDtypeStruct((n_slots,), jnp.int32),
        name="slot_owner",
    )(dest)


def _routing_tables(route):
    r = route.shape[0]
    tk = r * TOP_K
    flat_e = route[:, :TOP_K].reshape(tk).astype(jnp.int32)
    onehot = (flat_e[:, None] == jnp.arange(N_EXPERTS, dtype=jnp.int32)[None, :]).astype(jnp.int32)
    csum = jnp.cumsum(onehot, axis=0)
    rank = jnp.sum(csum * onehot, axis=1) - 1
    counts = csum[-1]
    padded = (counts + MOE_BLOCK - 1) // MOE_BLOCK * MOE_BLOCK
    pend = jnp.cumsum(padded)
    pstart = pend - padded
    dest = pstart[flat_e] + rank
    nb = -(-tk // MOE_BLOCK) + N_EXPERTS
    n_slots = nb * MOE_BLOCK
    pair = _slot_owner(dest, n_slots)
    used = pair >= 0
    row_src = jnp.where(used, pair // TOP_K, 0)
    slot_id = jnp.arange(n_slots, dtype=jnp.int32)
    dump_row = tk + (slot_id // MOE_BLOCK) % MOE_BUFFERS * MOE_BLOCK + slot_id % MOE_BLOCK
    row_dst = jnp.where(used, (pair % TOP_K) * r + pair // TOP_K, dump_row)
    block_start = jnp.arange(nb, dtype=jnp.int32) * MOE_BLOCK
    block_expert = jnp.minimum(jnp.searchsorted(pend, block_start, side='right'),
                               N_EXPERTS - 1).astype(jnp.int32)
    return block_expert, row_src, row_dst


def _pad_heads(t, axis):
    axis = axis % t.ndim
    shp = t.shape
    t = t.reshape(shp[:axis] + (RWKV_HEADS, RWKV_HEAD_DIM) + shp[axis + 1:])
    pad = [(0, 0)] * t.ndim
    pad[axis + 1] = (0, HEAD_PAD - RWKV_HEAD_DIM)
    t = jnp.pad(t, pad)
    return t.reshape(shp[:axis] + (RWKV_PAD,) + shp[axis + 1:])


def _cast_kernel(x_ref, o_ref):
    o_ref[...] = x_ref[...].astype(o_ref.dtype)


def _to_bf16(w, layer):
    _, e, a, b = w.shape
    return pl.pallas_call(
        _cast_kernel,
        grid=(e, 2),
        in_specs=[pl.BlockSpec((None, None, a // 2, b), lambda i, j: (layer, i, j, 0))],
        out_specs=pl.BlockSpec((None, a // 2, b), lambda i, j: (i, j, 0)),
        out_shape=jax.ShapeDtypeStruct((e, a, b), BF16),
        compiler_params=_params(("arbitrary", "arbitrary")),
        name="to_bf16",
    )(w)


def _rope_tables(lp):
    half = ROPE_DIM // 2
    inv_freq = ROPE_THETA ** (-jnp.arange(0, ROPE_DIM, 2, dtype=F32) / ROPE_DIM)
    ang = jnp.arange(lp, dtype=F32)[:, None] * inv_freq[None, :]
    cos, sin = jnp.cos(ang), jnp.sin(ang)
    ones = jnp.ones((lp, DIFF_QK_DIM - ROPE_DIM), F32)
    zeros = jnp.zeros((lp, DIFF_QK_DIM - ROPE_DIM), F32)
    zh = jnp.zeros((lp, half), F32)
    rc = jnp.concatenate([cos, cos, ones], axis=1)
    rs1 = jnp.concatenate([-sin, zh, zeros], axis=1)
    rs2 = jnp.concatenate([zh, sin, zeros], axis=1)
    tile2 = lambda a: jnp.concatenate([a, a], axis=1)
    return tile2(rc), tile2(rs1), tile2(rs2)


def kernel(x, meta_tokens, mix_norm_g, w_in, tm_mu, tm_w0, tm_w_decay_up, tm_a0, tm_w_a_up, tm_w_g_up, tm_k_k, tm_k_a, tm_r_k, tm_gn_g, tm_gn_b, da_q_norm_g, da_k_norm_g, da_lambda_q1, da_lambda_k1, da_lambda_q2, da_lambda_k2, da_subln_g, sc_conv_w, w_out, ffn_norm_g, ffn_w_gate, ffn_w_up, ffn_w_down, router_w, moe_w_gate, moe_w_up, moe_w_down):
    b, seq, d = x.shape
    depth = w_in.shape[0]
    l = N_META + seq
    lp = -(-l // LANES) * LANES
    tm = _row_tile(lp)
    r = b * lp
    s1 = RWKV_WIDTH

    meta = jnp.broadcast_to(meta_tokens.astype(x.dtype)[None], (b, N_META, d))
    h = jnp.concatenate([meta, x, jnp.zeros((b, lp - l, d), x.dtype)], axis=1).reshape(r, d)
    rc, rs1, rs2 = _rope_tables(lp)

    for i in range(depth):
        lam_init = 0.8 - 0.6 * math.exp(-0.3 * i)
        wi = w_in[i]
        w_cat = jnp.concatenate(
            [_pad_heads(wi[:, 0:s1], 1), _pad_heads(wi[:, s1:2 * s1], 1),
             _pad_heads(wi[:, 2 * s1:3 * s1], 1), wi[:, 3 * s1:]], axis=1).astype(BF16)
        qg = jnp.tile(da_q_norm_g[i], 2)[None]
        kg = jnp.tile(da_k_norm_g[i], 2)[None]
        pa, qs, kh, vt, pc = _inproj(h, mix_norm_g[i][None], w_cat, rc, rs1, rs2, qg, kg,
                                     tm=tm, lp=lp)

        mu = tm_mu[i]
        mu_p = jnp.concatenate([_pad_heads(mu[0:s1], 0), _pad_heads(mu[s1:2 * s1], 0),
                                _pad_heads(mu[2 * s1:3 * s1], 0), mu[3 * s1:]])[None]
        vecs = jnp.stack([_pad_heads(tm_w0[i], 0), _pad_heads(tm_a0[i], 0),
                          _pad_heads(tm_k_k[i], 0), _pad_heads(tm_k_a[i], 0),
                          _pad_heads(tm_r_k[i].reshape(s1), 0), _pad_heads(tm_gn_g[i], 0),
                          _pad_heads(tm_gn_b[i], 0), jnp.zeros((RWKV_PAD,), F32)])
        zl = lambda n: jnp.zeros((n, RWKV_PAD), F32)
        wd_p = jnp.concatenate([_pad_heads(tm_w_decay_up[i], 1), zl(AAA_LORA + GATE_LORA)], axis=0)
        wa_p = jnp.concatenate([zl(DECAY_LORA), _pad_heads(tm_w_a_up[i], 1), zl(GATE_LORA)], axis=0)
        wg_p = jnp.concatenate([zl(DECAY_LORA + AAA_LORA), _pad_heads(tm_w_g_up[i], 1)], axis=0)
        oa = _rwkv(pa.reshape(b, lp, PA_COLS), mu_p, vecs, wd_p, wa_p, wg_p, tb=tm)

        lamv = jnp.zeros((SUBLANES, LANES), F32).at[0:4, :DIFF_QK_DIM].set(
            jnp.stack([da_lambda_q1[i], da_lambda_k1[i], da_lambda_q2[i], da_lambda_k2[i]]))
        ob = _attention(qs.reshape(2, b, lp, DIFF_WIDTH), kh.reshape(b, lp, DIFF_WIDTH),
                        vt.reshape(b, lp // tm, DIFF_HEADS, VT_ROWS, tm), lamv, da_subln_g[i][None],
                        tq=tm, lam_init=lam_init)

        wo = w_out[i]
        cw = jnp.zeros((SUBLANES, CONV_WIDTH), F32).at[:CONV_K].set(sc_conv_w[i])
        mix = (h, oa.reshape(r, RWKV_PAD), ob.reshape(r, DIFF_WIDTH), pc, cw,
               _pad_heads(wo[:s1], 0).astype(BF16), wo[s1:s1 + DIFF_WIDTH].astype(BF16),
               wo[s1 + DIFF_WIDTH:].astype(BF16), ffn_norm_g[i][None])
        j = i // 2
        if i % 2 == 0:
            h = _outproj(*mix, tm=tm, lp=lp, ffn=(ffn_w_gate[j].astype(BF16),
                                                  ffn_w_up[j].astype(BF16),
                                                  ffn_w_down[j].astype(BF16)))
        else:
            rw = jnp.zeros((d, LANES), F32).at[:, :N_EXPERTS].set(router_w[j])
            h, hn, route = _outproj(*mix, tm=tm, lp=lp, router_w=rw)
            block_expert, row_src, row_dst = _routing_tables(route)
            y = _moe_experts(block_expert, row_src, row_dst, hn, _to_bf16(moe_w_gate, j),
                             _to_bf16(moe_w_up, j), _to_bf16(moe_w_down, j))
            h = _moe_combine(h, route, y, tm=tm)

    return h.reshape(b, lp, d)[:, N_META:l]
```

```python
import functools
import math

import jax
import jax.numpy as jnp
from jax import lax
from jax.experimental import pallas as pl
from jax.experimental.pallas import tpu as pltpu

N_META = 16
RWKV_HEAD_DIM = 64
RWKV_HEADS = 4
RWKV_WIDTH = RWKV_HEADS * RWKV_HEAD_DIM
DECAY_LORA = 32
AAA_LORA = 32
GATE_LORA = 64
LORA_COLS = DECAY_LORA + AAA_LORA + GATE_LORA
DIFF_QK_DIM = 64
DIFF_V_DIM = 128
DIFF_HEADS = 4
DIFF_WIDTH = DIFF_HEADS * DIFF_V_DIM
CONV_WIDTH = 256
CONV_K = 3
ROPE_THETA = 500000.0
ROPE_DIM = DIFF_QK_DIM // 4
N_EXPERTS = 8
TOP_K = 2
MOE_BLOCK = 256
NORM_EPS = 1e-6
RWKV_GN_EPS = 64e-5
SUBLN_EPS = 1e-5
NEG_INF = -1e30

LANES = 128
SUBLANES = 8
MXU_DIM = 256
VMEM_LIMIT_BYTES = 56 * 1024 * 1024

HEAD_PAD = LANES
RWKV_PAD = RWKV_HEADS * HEAD_PAD
PA_COLS = 3 * RWKV_PAD + LORA_COLS
PB_COLS = 3 * DIFF_WIDTH
PC_COLS = 3 * CONV_WIDTH
VT_ROWS = DIFF_V_DIM + 16
LOG2_E = math.log2(math.e)
CHUNK = 64

F32 = jnp.float32
BF16 = jnp.bfloat16
HIGHEST = lax.Precision.HIGHEST


def _dot(a, b):
    return jnp.dot(a.astype(BF16), b.astype(BF16), preferred_element_type=F32)


def _dot_nt(a, b):
    return lax.dot_general(a.astype(BF16), b.astype(BF16), (((1,), (1,)), ((), ())),
                           preferred_element_type=F32)


def _dot_f32(a, b):
    return jnp.dot(a, b, preferred_element_type=F32, precision=HIGHEST)


def _row_tile(lp):
    for t in (384, 256, 128):
        if lp % t == 0:
            return t
    raise ValueError(f"padded length {lp} is not a multiple of {LANES}")


def _params(sem):
    return pltpu.CompilerParams(dimension_semantics=sem, vmem_limit_bytes=VMEM_LIMIT_BYTES)


def _inproj_kernel(h_ref, g_ref, w_ref, rc_ref, rs1_ref, rs2_ref, qg_ref, kg_ref,
                   pa_ref, qs_ref, kh_ref, vt_ref, pc_ref):
    tm = h_ref.shape[0]
    x = h_ref[...]
    ms = jnp.mean(x * x, axis=-1, keepdims=True)
    xn = (x * lax.rsqrt(ms + NORM_EPS) * g_ref[...]).astype(BF16)
    pa_ref[...] = jnp.dot(xn, w_ref[:, :PA_COLS], preferred_element_type=F32)
    pc_ref[...] = jnp.dot(xn, w_ref[:, PA_COLS + PB_COLS:], preferred_element_type=F32)
    pb = jnp.dot(xn, w_ref[:, PA_COLS:PA_COLS + PB_COLS], preferred_element_type=F32)

    lane = lax.broadcasted_iota(jnp.int32, (tm, LANES), 1)
    lo = lane < DIFF_QK_DIM
    rc, rs1, rs2 = rc_ref[...], rs1_ref[...], rs2_ref[...]

    def norm_rope(t, g):
        t2 = t * t
        ss_lo = jnp.sum(jnp.where(lo, t2, 0.0), axis=-1, keepdims=True)
        ss_hi = jnp.sum(jnp.where(lo, 0.0, t2), axis=-1, keepdims=True)
        inv = jnp.where(lo, lax.rsqrt(ss_lo / DIFF_QK_DIM + NORM_EPS),
                        lax.rsqrt(ss_hi / DIFF_QK_DIM + NORM_EPS))
        tn = t * inv * g
        half = ROPE_DIM // 2
        return tn * rc + pltpu.roll(tn, LANES - half, 1) * rs1 + pltpu.roll(tn, half, 1) * rs2

    for hd in range(DIFF_HEADS):
        cs = slice(hd * LANES, (hd + 1) * LANES)
        q = norm_rope(pb[:, cs], qg_ref[...]) * (DIFF_QK_DIM ** -0.5 * LOG2_E)
        qs_ref[0, :, cs] = jnp.where(lo, q, 0.0).astype(BF16)
        qs_ref[1, :, cs] = jnp.where(lo, 0.0, q).astype(BF16)
        kcs = slice(DIFF_WIDTH + hd * LANES, DIFF_WIDTH + (hd + 1) * LANES)
        kh_ref[:, cs] = norm_rope(pb[:, kcs], kg_ref[...]).astype(BF16)
        vcs = slice(2 * DIFF_WIDTH + hd * LANES, 2 * DIFF_WIDTH + (hd + 1) * LANES)
        vt_ref[0, hd, :DIFF_V_DIM] = pb[:, vcs].T.astype(BF16)
        vt_ref[0, hd, DIFF_V_DIM:] = jnp.ones((VT_ROWS - DIFF_V_DIM, tm), BF16)


def _inproj(h, g, w, rc, rs1, rs2, qg, kg, *, tm, lp):
    r, d = h.shape
    nt_b = lp // tm
    row = lambda i: (i, 0)
    const = lambda i: (0, 0)
    rope = lambda i: (i % nt_b, 0)
    return pl.pallas_call(
        _inproj_kernel,
        grid=(r // tm,),
        in_specs=[
            pl.BlockSpec((tm, d), row),
            pl.BlockSpec((1, d), const),
            pl.BlockSpec(w.shape, const),
            pl.BlockSpec((tm, LANES), rope),
            pl.BlockSpec((tm, LANES), rope),
            pl.BlockSpec((tm, LANES), rope),
            pl.BlockSpec((1, LANES), const),
            pl.BlockSpec((1, LANES), const),
        ],
        out_specs=[
            pl.BlockSpec((tm, PA_COLS), row),
            pl.BlockSpec((2, tm, DIFF_WIDTH), lambda i: (0, i, 0)),
            pl.BlockSpec((tm, DIFF_WIDTH), row),
            pl.BlockSpec((1, DIFF_HEADS, VT_ROWS, tm), lambda i: (i, 0, 0, 0)),
            pl.BlockSpec((tm, PC_COLS), row),
        ],
        out_shape=[
            jax.ShapeDtypeStruct((r, PA_COLS), F32),
            jax.ShapeDtypeStruct((2, r, DIFF_WIDTH), BF16),
            jax.ShapeDtypeStruct((r, DIFF_WIDTH), BF16),
            jax.ShapeDtypeStruct((r // tm, DIFF_HEADS, VT_ROWS, tm), BF16),
            jax.ShapeDtypeStruct((r, PC_COLS), F32),
        ],
        compiler_params=_params(("arbitrary",)),
        name="inproj",
    )(h, g, w, rc, rs1, rs2, qg, kg)


def _split3(x):
    hi = x.astype(BF16)
    r1 = x - hi.astype(F32)
    mid = r1.astype(BF16)
    lo = (r1 - mid.astype(F32)).astype(BF16)
    return hi, mid, lo


def _dot_hi(a, b):
    ah = a.astype(BF16)
    al = (a - ah.astype(F32)).astype(BF16)
    bh = b.astype(BF16)
    bl = (b - bh.astype(F32)).astype(BF16)
    d = lambda p, q: jnp.dot(p, q, preferred_element_type=F32)
    return d(ah, bh) + d(ah, bl) + d(al, bh)


def _bmm(a, b):
    return jnp.einsum('cik,ckj->cij', a.astype(BF16), b.astype(BF16), preferred_element_type=F32)


def _bmm_nt(a, b):
    return jnp.einsum('cik,cjk->cij', a.astype(BF16), b.astype(BF16), preferred_element_type=F32)


def _rwkv_kernel(pa_ref, halo_ref, mu_ref, vec_ref, wd_ref, wa_ref, wg_ref, oa_ref, hstate_s):
    tb = pa_ref.shape[0]
    nc = tb // CHUNK
    i = pl.program_id(1)

    @pl.when(i == 0)
    def _():
        hstate_s[...] = jnp.zeros_like(hstate_s)

    x = pa_ref[...]
    prev_row = jnp.where(i == 0, 0.0, halo_ref[SUBLANES - 1:SUBLANES, :])
    row = lax.broadcasted_iota(jnp.int32, (tb, 1), 0)
    prev = jnp.where(row == 0, prev_row, pltpu.roll(x, 1, 0))
    hx = x + (prev - x) * mu_ref[...]
    r = hx[:, 0:RWKV_PAD]
    k = hx[:, RWKV_PAD:2 * RWKV_PAD]
    v = hx[:, 2 * RWKV_PAD:3 * RWKV_PAD]
    lora = hx[:, 3 * RWKV_PAD:]
    w0, a0, k_k, k_a = vec_ref[0:1, :], vec_ref[1:2, :], vec_ref[2:3, :], vec_ref[3:4, :]
    r_k, gn_g, gn_b = vec_ref[4:5, :], vec_ref[5:6, :], vec_ref[6:7, :]

    w = -jax.nn.softplus(-(w0 + _dot_hi(jnp.tanh(lora), wd_ref[...]))) - 0.5
    wl = -jnp.exp(w)
    alr = jax.nn.sigmoid(a0 + _dot(lora, wa_ref[...]))
    gate = _dot(jax.nn.sigmoid(lora), wg_ref[...])
    kk = k * k_k
    k2 = k * (1.0 + (alr - 1.0) * k_a)
    rkr = r * k2 * r_k

    ri = lax.broadcasted_iota(jnp.int32, (CHUNK, CHUNK), 0)
    ci = lax.broadcasted_iota(jnp.int32, (CHUNK, CHUNK), 1)
    lane_c = lax.broadcasted_iota(jnp.int32, (CHUNK, HEAD_PAD), 1)
    row_c = lax.broadcasted_iota(jnp.int32, (CHUNK, HEAD_PAD), 0)
    col_c = jnp.where(lane_c >= CHUNK, lane_c - CHUNK, lane_c)
    low_incl2 = (col_c <= row_c)[None]
    low_strict2 = (col_c < row_c)[None]
    tril_b = jnp.where(ci <= ri, 1.0, 0.0).astype(BF16)
    eye_k = (lax.broadcasted_iota(jnp.int32, (HEAD_PAD, HEAD_PAD), 0)
             == lax.broadcasted_iota(jnp.int32, (HEAD_PAD, HEAD_PAD), 1))[None]
    n_sq = int(math.log2(CHUNK))

    parts = _split3(wl)
    cums = []
    for c in range(nc):
        rows = slice(c * CHUNK, (c + 1) * CHUNK)
        cums.append(sum(jnp.dot(tril_b, p[rows], preferred_element_type=F32) for p in parts))
    cum_all = jnp.concatenate(cums, axis=0)

    lane = lax.broadcasted_iota(jnp.int32, (tb, HEAD_PAD), 1)
    real = lane < RWKV_HEAD_DIM
    to3 = lambda t: t.reshape(nc, CHUNK, HEAD_PAD)
    heads = range(RWKV_HEADS)
    hsl = [slice(hd * HEAD_PAD, (hd + 1) * HEAD_PAD) for hd in heads]
    kkn, bonus = [], []
    for cs in hsl:
        kk_h = kk[:, cs]
        nrm = jnp.sqrt(jnp.sum(kk_h * kk_h, axis=-1, keepdims=True))
        kkn.append(kk_h / jnp.maximum(nrm, 1e-12))
        bonus.append(jnp.sum(rkr[:, cs], axis=-1, keepdims=True) * v[:, cs])

    vc = [to3(v[:, cs]) for cs in hsl]
    zero_v = jnp.zeros_like(vc[0])
    ah, rh, last, g, bk_last = [], [], [], [], []
    for hd, cs in enumerate(hsl):
        wl3, cum = to3(wl[:, cs]), to3(cum_all[:, cs])
        rr, kc = to3(r[:, cs]), to3(k2[:, cs])
        ac, bc = to3(-kkn[hd]), to3(kkn[hd] * alr[:, cs])
        last.append(cum[:, CHUNK - 1:CHUNK, :])
        e_neg = jnp.exp(-cum)
        e_last = jnp.exp(last[hd] - cum)
        ah.append(ac * jnp.exp(cum - wl3))
        rh.append(rr * jnp.exp(cum))
        bk_last.append(jnp.concatenate([bc * e_last, kc * e_last], axis=1))
        g.append(_bmm_nt(jnp.concatenate([ah[hd], rh[hd]], axis=1),
                         jnp.concatenate([bc * e_neg, kc * e_neg], axis=1)))
    a_abk = [jnp.where(low_strict2, g[hd][:, :CHUNK], 0.0) for hd in heads]
    r_bk = [jnp.where(low_incl2, g[hd][:, CHUNK:], 0.0) for hd in heads]
    xx = [jnp.concatenate([_bmm(a_abk[hd], jnp.concatenate([zero_v, vc[hd]], axis=1)), ah[hd]],
                          axis=2) for hd in heads]
    ppad = [jnp.where(lane_c < CHUNK, a_abk[hd], 0.0) for hd in heads]
    for j in range(n_sq):
        for hd in heads:
            pj = ppad[hd][:, :, :CHUNK]
            if j + 1 < n_sq:
                res = _bmm(pj, jnp.concatenate([xx[hd], ppad[hd]], axis=2))
                xx[hd] = xx[hd] + res[:, :, :2 * HEAD_PAD]
                ppad[hd] = res[:, :, 2 * HEAD_PAD:]
            else:
                xx[hd] = xx[hd] + _bmm(pj, xx[hd])
    qm, y0, n0 = [], [], []
    for hd in heads:
        lhs = jnp.concatenate([r_bk[hd], jnp.swapaxes(bk_last[hd], 1, 2)], axis=1)
        rhs = jnp.concatenate([xx[hd], jnp.concatenate([vc[hd], zero_v], axis=2)], axis=1)
        out = _bmm(lhs, rhs)
        y0.append(out[:, :CHUNK, :HEAD_PAD])
        n0.append(out[:, CHUNK:, :HEAD_PAD])
        q = rh[hd] + out[:, :CHUNK, HEAD_PAD:]
        m = jnp.where(eye_k, jnp.exp(last[hd]), 0.0) + out[:, CHUNK:, HEAD_PAD:]
        qm.append(jnp.concatenate([q, m], axis=1).astype(BF16))

    hs = [hstate_s[hd] for hd in heads]
    ys = [[] for _ in heads]
    for c in range(nc):
        for hd in heads:
            res = jnp.dot(qm[hd][c], hs[hd].astype(BF16), preferred_element_type=F32)
            ys[hd].append(res[:CHUNK] + y0[hd][c])
            hs[hd] = res[CHUNK:] + n0[hd][c]

    for hd, cs in enumerate(hsl):
        hstate_s[hd] = hs[hd]
        y = jnp.concatenate(ys[hd], axis=0)
        mean = jnp.sum(y, axis=-1, keepdims=True) / RWKV_HEAD_DIM
        dlt = jnp.where(real, y - mean, 0.0)
        var = jnp.sum(dlt * dlt, axis=-1, keepdims=True) / RWKV_HEAD_DIM
        yn = dlt * lax.rsqrt(var + RWKV_GN_EPS) * gn_g[:, cs] + gn_b[:, cs]
        oa_ref[:, cs] = ((yn + bonus[hd]) * gate[:, cs]).astype(BF16)


def _rwkv(pa, mu, vecs, wd, wa, wg, *, tb):
    b, lp, _ = pa.shape
    tile = lambda bi, i: (bi, i, 0)
    const = lambda bi, i: (0, 0)
    halo = lambda bi, i: (bi, jnp.maximum(i * (tb // SUBLANES) - 1, 0), 0)
    return pl.pallas_call(
        _rwkv_kernel,
        grid=(b, lp // tb),
        in_specs=[
            pl.BlockSpec((None, tb, PA_COLS), tile),
            pl.BlockSpec((None, SUBLANES, PA_COLS), halo),
            pl.BlockSpec((1, PA_COLS), const),
            pl.BlockSpec((SUBLANES, RWKV_PAD), const),
            pl.BlockSpec((LORA_COLS, RWKV_PAD), const),
            pl.BlockSpec((LORA_COLS, RWKV_PAD), const),
            pl.BlockSpec((LORA_COLS, RWKV_PAD), const),
        ],
        out_specs=pl.BlockSpec((None, tb, RWKV_PAD), tile),
        out_shape=jax.ShapeDtypeStruct((b, lp, RWKV_PAD), BF16),
        scratch_shapes=[pltpu.VMEM((RWKV_HEADS, HEAD_PAD, HEAD_PAD), F32)],
        compiler_params=_params(("arbitrary", "arbitrary")),
        name="rwkv7",
    )(pa, pa, mu, vecs, wd, wa, wg)


ATTN_HEADS_PER_STEP = 4


def _attn_kernel(q_ref, k_ref, vt_ref, lamv_ref, sg_ref, o_ref, acc_s, st_s, p_s, *, lam_init):
    tq = o_ref.shape[0]
    tk = vt_ref.shape[3]
    qi = pl.program_id(2)
    heads = range(ATTN_HEADS_PER_STEP)
    hl = [slice(h * LANES, (h + 1) * LANES) for h in heads]

    def scores(h, ki):
        k = k_ref[pl.ds(pl.multiple_of(ki * tk, tk), tk), hl[h]]
        return lax.dot_general(k, q_ref[:, :, hl[h]].reshape(2 * tq, LANES),
                               (((1,), (1,)), ((), ())), preferred_element_type=F32)

    def values(h, ki, p, alpha):
        acc_s[h] = alpha * acc_s[h] + jnp.dot(vt_ref[ki, h], p, preferred_element_type=F32)

    st = [scores(h, qi) for h in heads]
    for h in heads:
        st_s[h, 0] = scores(h, 0)

    kr = lax.broadcasted_iota(jnp.int32, (tk, 2 * tq), 0)
    qc = lax.broadcasted_iota(jnp.int32, (tk, 2 * tq), 1)
    qc = jnp.where(qc >= tq, qc - tq, qc)
    m0 = []
    for h in heads:
        s_h = jnp.where(kr <= qc, st[h], NEG_INF)
        m0.append(jnp.max(s_h, axis=0, keepdims=True))
        p_s[h, 1] = jnp.exp2(s_h - m0[h]).astype(BF16)
        acc_s[h] = jnp.zeros(acc_s.shape[1:], acc_s.dtype)

    def trip(k, carry, slot):
        out = []
        nxt = jnp.minimum(k + 1, qi - 1)
        prv = jnp.where(k == 0, qi, k - 1)
        for step in range(len(heads) + 1):
            if step < len(heads):
                st_s[step, 1 - slot] = scores(step, nxt)
            if step >= 1:
                h = step - 1
                m_prev, alpha_prev = carry[h]
                s_k = st_s[h, slot]
                m_new = jnp.maximum(m_prev, jnp.max(s_k, axis=0, keepdims=True))
                alpha = jnp.exp2(m_prev - m_new)
                values(h, prv, p_s[h, 1 - slot], alpha_prev)
                p_s[h, slot] = jnp.exp2(s_k - m_new).astype(BF16)
                out.append((m_new, alpha))
        return tuple(out)

    def two_trips(j, c):
        c = trip(2 * j, c, 0)
        return lax.cond(2 * j + 1 < qi, lambda cc: trip(2 * j + 1, cc, 1), lambda cc: cc, c)

    init = tuple((m0[h], jnp.ones_like(m0[h])) for h in heads)
    carry = lax.fori_loop(0, (qi + 1) // 2, two_trips, init)
    last_slot = (qi + 1) % 2
    for h in heads:
        values(h, jnp.where(qi == 0, 0, qi - 1), p_s[h, last_slot], carry[h][1])

    lv = lamv_ref[...]
    lam = (jnp.exp(jnp.sum(lv[0:1] * lv[1:2], axis=-1, keepdims=True))
           - jnp.exp(jnp.sum(lv[2:3] * lv[3:4], axis=-1, keepdims=True)) + lam_init)
    for h in heads:
        acc = acc_s[h]
        ot = acc[:DIFF_V_DIM] / acc[DIFF_V_DIM:DIFF_V_DIM + 1]
        od = (ot[:, :tq] - lam * ot[:, tq:]).T
        ms = jnp.mean(od * od, axis=-1, keepdims=True)
        o_ref[:, hl[h]] = (od * lax.rsqrt(ms + SUBLN_EPS) * sg_ref[...]
                           * (1.0 - lam_init)).astype(BF16)


def _attention(qs, kh, vt, lamv, sg, *, tq, lam_init):
    _, b, lp, _ = qs.shape
    nq = lp // tq
    hp = ATTN_HEADS_PER_STEP
    const = lambda bi, hg, qi: (0, 0)
    return pl.pallas_call(
        functools.partial(_attn_kernel, lam_init=lam_init),
        grid=(b, DIFF_HEADS // hp, nq),
        in_specs=[
            pl.BlockSpec((2, None, tq, hp * LANES), lambda bi, hg, qi: (0, bi, qi, hg)),
            pl.BlockSpec((None, lp, hp * LANES), lambda bi, hg, qi: (bi, 0, hg)),
            pl.BlockSpec((None, nq, hp, VT_ROWS, tq), lambda bi, hg, qi: (bi, 0, hg, 0, 0)),
            pl.BlockSpec((SUBLANES, LANES), const),
            pl.BlockSpec((1, LANES), const),
        ],
        out_specs=pl.BlockSpec((None, tq, hp * LANES), lambda bi, hg, qi: (bi, qi, hg)),
        out_shape=jax.ShapeDtypeStruct((b, lp, DIFF_WIDTH), BF16),
        scratch_shapes=[pltpu.VMEM((hp, VT_ROWS, 2 * tq), F32),
                        pltpu.VMEM((hp, 2, tq, 2 * tq), F32),
                        pltpu.VMEM((hp, 2, tq, 2 * tq), BF16)],
        compiler_params=_params(("arbitrary", "arbitrary", "arbitrary")),
        name="diffattn",
    )(qs, kh, vt, lamv, sg)


def _mixer_out(h_ref, oa_ref, ob_ref, pc_ref, pch_ref, cw_ref, wa_ref, wb_ref, wc_ref,
               tiles_per_batch):
    tm = h_ref.shape[0]
    first = (pl.program_id(0) % tiles_per_batch) == 0
    pc = pc_ref[...]
    bg = pc[:, :CONV_WIDTH]
    z = pc[:, CONV_WIDTH:2 * CONV_WIDTH] * pc[:, 2 * CONV_WIDTH:]
    ph = pch_ref[...]
    zh = jnp.where(first, 0.0, ph[:, CONV_WIDTH:2 * CONV_WIDTH] * ph[:, 2 * CONV_WIDTH:])
    row = lax.broadcasted_iota(jnp.int32, (tm, 1), 0)
    z1 = jnp.where(row == 0, zh[SUBLANES - 1:SUBLANES], pltpu.roll(z, 1, 0))
    z2 = jnp.where(row == 0, zh[SUBLANES - 2:SUBLANES - 1],
                   jnp.where(row == 1, zh[SUBLANES - 1:SUBLANES], pltpu.roll(z, 2, 0)))
    cw = cw_ref[...]
    y = cw[0:1] * z2 + cw[1:2] * z1 + cw[2:3] * z
    oc = (bg * y).astype(BF16)
    return (h_ref[...]
            + jnp.dot(oa_ref[...], wa_ref[...], preferred_element_type=F32)
            + jnp.dot(ob_ref[...], wb_ref[...], preferred_element_type=F32)
            + jnp.dot(oc, wc_ref[...], preferred_element_type=F32))


def _outproj_ffn_kernel(h_ref, oa_ref, ob_ref, pc_ref, pch_ref, cw_ref, wa_ref, wb_ref, wc_ref,
                        g_ref, wg_ref, wu_ref, wd_ref, o_ref, *, tiles_per_batch):
    h1 = _mixer_out(h_ref, oa_ref, ob_ref, pc_ref, pch_ref, cw_ref, wa_ref, wb_ref, wc_ref,
                    tiles_per_batch)
    o_ref[...] = h1 + _swiglu_block(_rms(h1, g_ref[...]).astype(BF16), wg_ref, wu_ref, wd_ref)


def _outproj_kernel(h_ref, oa_ref, ob_ref, pc_ref, pch_ref, cw_ref, wa_ref, wb_ref, wc_ref,
                    o_ref, *, tiles_per_batch):
    o_ref[...] = _mixer_out(h_ref, oa_ref, ob_ref, pc_ref, pch_ref, cw_ref, wa_ref, wb_ref, wc_ref,
                            tiles_per_batch)


def _outproj(h, oa, ob, pc, cw, wa, wb, wc, *, tm, lp, ffn=None):
    r, d = h.shape
    row = lambda i: (i, 0)
    const = lambda i: (0, 0)
    halo = lambda i: (jnp.maximum(i * (tm // SUBLANES) - 1, 0), 0)
    tail = list(ffn) if ffn is not None else []
    in_specs = [
        pl.BlockSpec((tm, d), row),
        pl.BlockSpec((tm, RWKV_PAD), row),
        pl.BlockSpec((tm, DIFF_WIDTH), row),
        pl.BlockSpec((tm, PC_COLS), row),
        pl.BlockSpec((SUBLANES, PC_COLS), halo),
        pl.BlockSpec((SUBLANES, CONV_WIDTH), const),
        pl.BlockSpec(wa.shape, const),
        pl.BlockSpec(wb.shape, const),
        pl.BlockSpec(wc.shape, const),
    ] + [pl.BlockSpec(t.shape, const) for t in tail]
    body, name = (_outproj_kernel, "outproj") if ffn is None else (_outproj_ffn_kernel, "outproj_ffn")
    return pl.pallas_call(
        functools.partial(body, tiles_per_batch=lp // tm),
        grid=(r // tm,),
        in_specs=in_specs,
        out_specs=pl.BlockSpec((tm, d), row),
        out_shape=jax.ShapeDtypeStruct((r, d), F32),
        compiler_params=_params(("arbitrary",)),
        name=name,
    )(h, oa, ob, pc, pc, cw, wa, wb, wc, *tail)


def _swiglu_block(xb, wg_ref, wu_ref, wd_ref, side_work=None):
    d_ff = wg_ref.shape[1]
    acc = jnp.zeros((xb.shape[0], wd_ref.shape[1]), F32)
    for c in range(d_ff // MXU_DIM):
        cs = slice(c * MXU_DIM, (c + 1) * MXU_DIM)
        gt = jnp.dot(xb, wg_ref[:, cs], preferred_element_type=F32)
        up = jnp.dot(xb, wu_ref[:, cs], preferred_element_type=F32)
        act = (jax.nn.silu(gt) * up).astype(BF16)
        acc = acc + jnp.dot(act, wd_ref[cs, :], preferred_element_type=F32)
        if side_work is not None:
            side_work(c)
    return acc


def _rms(x, g):
    ms = jnp.mean(x * x, axis=-1, keepdims=True)
    return x * lax.rsqrt(ms + NORM_EPS) * g


def _router_kernel(h_ref, g_ref, rw_ref, hn_ref, route_ref):
    tm = h_ref.shape[0]
    xn = _rms(h_ref[...], g_ref[...])
    hn_ref[...] = xn
    logits = _dot_f32(xn, rw_ref[...])
    lane = lax.broadcasted_iota(jnp.int32, (tm, LANES), 1)
    logits = jnp.where(lane < N_EXPERTS, logits, -jnp.inf)
    m1 = jnp.max(logits, axis=-1, keepdims=True)
    i1 = jnp.min(jnp.where(logits == m1, lane, LANES), axis=-1, keepdims=True)
    rest = jnp.where(lane == i1, -jnp.inf, logits)
    m2 = jnp.max(rest, axis=-1, keepdims=True)
    i2 = jnp.min(jnp.where(rest == m2, lane, LANES), axis=-1, keepdims=True)
    e2 = jnp.exp(m2 - m1)
    den = 1.0 + e2
    route_ref[...] = jnp.where(lane == 0, i1.astype(F32),
                     jnp.where(lane == 1, i2.astype(F32),
                     jnp.where(lane == 2, 1.0 / den,
                     jnp.where(lane == 3, e2 / den, 0.0))))


def _router(h, g, rw, *, tm):
    r, d = h.shape
    row = lambda i: (i, 0)
    const = lambda i: (0, 0)
    return pl.pallas_call(
        _router_kernel,
        grid=(r // tm,),
        in_specs=[pl.BlockSpec((tm, d), row), pl.BlockSpec((1, d), const),
                  pl.BlockSpec((d, LANES), const)],
        out_specs=[pl.BlockSpec((tm, d), row), pl.BlockSpec((tm, LANES), row)],
        out_shape=[jax.ShapeDtypeStruct((r, d), F32), jax.ShapeDtypeStruct((r, LANES), F32)],
        compiler_params=_params(("arbitrary",)),
        name="router",
    )(h, g, rw)


def _row_copy(src, dst, sem, src_row, dst_row):
    return pltpu.make_async_copy(src.at[pl.ds(src_row, 1), :], dst.at[pl.ds(dst_row, 1), :], sem)


DMA_UNROLL = 8
DMA_ISSUE_SLICES = 8
MOE_BUFFERS = 3


def _moe_kernel(bexp_ref, rsrc_ref, rdst_ref, hn_hbm, wg_ref, wu_ref, wd_ref, y_hbm,
                xbuf, ybuf, gsem, ssem):
    del bexp_ref
    nbuf = xbuf.shape[0]
    blk = xbuf.shape[1]
    i = pl.program_id(0)
    nb = pl.num_programs(0)
    slot = i % nbuf
    slot_prev = (i + nbuf - 1) % nbuf
    slot_next2 = (i + 2) % nbuf
    dump0 = y_hbm.shape[0] - nbuf * blk

    def for_rows(fn):
        def body(j, c):
            for u in range(DMA_UNROLL):
                fn(j * DMA_UNROLL + u)
            return c
        lax.fori_loop(0, blk // DMA_UNROLL, body, 0)

    def gather(block, s, j):
        return _row_copy(hn_hbm, xbuf.at[s], gsem.at[s], rsrc_ref[block * blk + j], j)

    def scatter(dst_row, s, j):
        return _row_copy(ybuf.at[s], y_hbm, ssem.at[s], j, dst_row)

    @pl.when(i == 0)
    def _():
        for_rows(lambda j: gather(0, 0, j).start())
        for_rows(lambda j: gather(1, 1, j).start())
        ybuf[nbuf - 1] = jnp.zeros(ybuf.shape[1:], ybuf.dtype)
        for k in range(nbuf - 1):
            zero_fill = pltpu.make_async_copy(
                ybuf.at[nbuf - 1], y_hbm.at[pl.ds(dump0 + k * blk, blk), :], ssem.at[k])
            zero_fill.start()
            zero_fill.wait()

    for_rows(lambda j: gather(0, slot, j).wait())

    @pl.when(i >= 2)
    def _():
        for_rows(lambda j: scatter(0, slot, j).wait())

    ahead = jnp.minimum(i + 2, nb - 1)
    prev = jnp.maximum(i - 1, 0)
    per = -(-blk // DMA_ISSUE_SLICES)

    def side_work(c):
        for j in range(c * per, min((c + 1) * per, blk)):
            gather(ahead, slot_next2, j).start(priority=j % 2)
            dst = jnp.where(i == 0, dump0 + (nbuf - 1) * blk + j, rdst_ref[prev * blk + j])
            scatter(dst, slot_prev, j).start(priority=(j + 1) % 2)

    ybuf[slot] = _swiglu_block(xbuf[slot].astype(BF16), wg_ref, wu_ref, wd_ref, side_work)

    @pl.when(i == nb - 1)
    def _():
        for_rows(lambda j: gather(0, (i + 1) % nbuf, j).wait())
        for_rows(lambda j: gather(0, slot_next2, j).wait())
        for_rows(lambda j: scatter(0, (i + 1) % nbuf, j).wait())
        for_rows(lambda j: scatter(0, slot_prev, j).wait())
        for_rows(lambda j: scatter(rdst_ref[i * blk + j], slot, j).start())
        for_rows(lambda j: scatter(0, slot, j).wait())


def _moe_experts(block_expert, row_src, row_dst, hn, wg, wu, wd):
    nb = block_expert.shape[0]
    r, d = hn.shape
    d_ff = wg.shape[2]
    wmap = lambda i, be, rs, rd: (be[i], 0, 0)
    grid_spec = pltpu.PrefetchScalarGridSpec(
        num_scalar_prefetch=3,
        grid=(nb,),
        in_specs=[
            pl.BlockSpec(memory_space=pl.ANY),
            pl.BlockSpec((None, d, d_ff), wmap),
            pl.BlockSpec((None, d, d_ff), wmap),
            pl.BlockSpec((None, d_ff, d), wmap),
        ],
        out_specs=pl.BlockSpec(memory_space=pl.ANY),
        scratch_shapes=[pltpu.VMEM((MOE_BUFFERS, MOE_BLOCK, d), F32),
                        pltpu.VMEM((MOE_BUFFERS, MOE_BLOCK, d), F32),
                        pltpu.SemaphoreType.DMA((MOE_BUFFERS,)),
                        pltpu.SemaphoreType.DMA((MOE_BUFFERS,))],
    )
    return pl.pallas_call(
        _moe_kernel,
        grid_spec=grid_spec,
        out_shape=jax.ShapeDtypeStruct((TOP_K * r + MOE_BUFFERS * MOE_BLOCK, d), F32),
        compiler_params=_params(("arbitrary",)),
        name="moe_experts",
    )(block_expert, row_src, row_dst, hn, wg, wu, wd)


def _combine_kernel(h_ref, route_ref, y0_ref, y1_ref, o_ref):
    route = route_ref[...]
    o_ref[...] = h_ref[...] + route[:, 2:3] * y0_ref[...] + route[:, 3:4] * y1_ref[...]


def _moe_combine(h, route, y, *, tm):
    r, d = h.shape
    nt = r // tm
    row = lambda i: (i, 0)
    return pl.pallas_call(
        _combine_kernel,
        grid=(nt,),
        in_specs=[
            pl.BlockSpec((tm, d), row),
            pl.BlockSpec((tm, LANES), row),
            pl.BlockSpec((tm, d), row),
            pl.BlockSpec((tm, d), lambda i: (i + nt, 0)),
        ],
        out_specs=pl.BlockSpec((tm, d), row),
        out_shape=jax.ShapeDtypeStruct((r, d), F32),
        compiler_params=_params(("arbitrary",)),
        name="moe_combine",
    )(h, route, y, y)


def _slot_owner_kernel(dest_ref, owner_ref):
    n_pairs = dest_ref.shape[0]
    n_slots = owner_ref.shape[0]

    def clear(i, c):
        for u in range(DMA_UNROLL):
            owner_ref[i * DMA_UNROLL + u] = -1
        return c

    lax.fori_loop(0, n_slots // DMA_UNROLL, clear, 0)

    def place(i, c):
        for u in range(DMA_UNROLL):
            p = i * DMA_UNROLL + u
            owner_ref[dest_ref[p]] = p
        return c

    lax.fori_loop(0, n_pairs // DMA_UNROLL, place, 0)


def _slot_owner(dest, n_slots):
    assert dest.shape[0] % DMA_UNROLL == 0 and n_slots % DMA_UNROLL == 0
    smem = pl.BlockSpec(memory_space=pltpu.SMEM)
    return pl.pallas_call(
        _slot_owner_kernel,
        in_specs=[smem],
        out_specs=smem,
        out_shape=jax.ShapeDtypeStruct((n_slots,), jnp.int32),
        name="slot_owner",
    )(dest)


def _routing_tables(route):
    r = route.shape[0]
    tk = r * TOP_K
    flat_e = route[:, :TOP_K].reshape(tk).astype(jnp.int32)
    onehot = (flat_e[:, None] == jnp.arange(N_EXPERTS, dtype=jnp.int32)[None, :]).astype(jnp.int32)
    csum = jnp.cumsum(onehot, axis=0)
    rank = jnp.sum(csum * onehot, axis=1) - 1
    counts = csum[-1]
    padded = (counts + MOE_BLOCK - 1) // MOE_BLOCK * MOE_BLOCK
    pend = jnp.cumsum(padded)
    pstart = pend - padded
    dest = pstart[flat_e] + rank
    nb = -(-tk // MOE_BLOCK) + N_EXPERTS
    n_slots = nb * MOE_BLOCK
    pair = _slot_owner(dest, n_slots)
    used = pair >= 0
    row_src = jnp.where(used, pair // TOP_K, 0)
    slot_id = jnp.arange(n_slots, dtype=jnp.int32)
    dump_row = tk + (slot_id // MOE_BLOCK) % MOE_BUFFERS * MOE_BLOCK + slot_id % MOE_BLOCK
    row_dst = jnp.where(used, (pair % TOP_K) * r + pair // TOP_K, dump_row)
    block_start = jnp.arange(nb, dtype=jnp.int32) * MOE_BLOCK
    block_expert = jnp.minimum(jnp.searchsorted(pend, block_start, side='right'),
                               N_EXPERTS - 1).astype(jnp.int32)
    return block_expert, row_src, row_dst


def _pad_heads(t, axis):
    axis = axis % t.ndim
    shp = t.shape
    t = t.reshape(shp[:axis] + (RWKV_HEADS, RWKV_HEAD_DIM) + shp[axis + 1:])
    pad = [(0, 0)] * t.ndim
    pad[axis + 1] = (0, HEAD_PAD - RWKV_HEAD_DIM)
    t = jnp.pad(t, pad)
    return t.reshape(shp[:axis] + (RWKV_PAD,) + shp[axis + 1:])


def _cast_kernel(x_ref, o_ref):
    o_ref[...] = x_ref[...].astype(o_ref.dtype)


def _to_bf16(w, layer):
    _, e, a, b = w.shape
    return pl.pallas_call(
        _cast_kernel,
        grid=(e, 2),
        in_specs=[pl.BlockSpec((None, None, a // 2, b), lambda i, j: (layer, i, j, 0))],
        out_specs=pl.BlockSpec((None, a // 2, b), lambda i, j: (i, j, 0)),
        out_shape=jax.ShapeDtypeStruct((e, a, b), BF16),
        compiler_params=_params(("arbitrary", "arbitrary")),
        name="to_bf16",
    )(w)


def _rope_tables(lp):
    half = ROPE_DIM // 2
    inv_freq = ROPE_THETA ** (-jnp.arange(0, ROPE_DIM, 2, dtype=F32) / ROPE_DIM)
    ang = jnp.arange(lp, dtype=F32)[:, None] * inv_freq[None, :]
    cos, sin = jnp.cos(ang), jnp.sin(ang)
    ones = jnp.ones((lp, DIFF_QK_DIM - ROPE_DIM), F32)
    zeros = jnp.zeros((lp, DIFF_QK_DIM - ROPE_DIM), F32)
    zh = jnp.zeros((lp, half), F32)
    rc = jnp.concatenate([cos, cos, ones], axis=1)
    rs1 = jnp.concatenate([-sin, zh, zeros], axis=1)
    rs2 = jnp.concatenate([zh, sin, zeros], axis=1)
    tile2 = lambda a: jnp.concatenate([a, a], axis=1)
    return tile2(rc), tile2(rs1), tile2(rs2)


def kernel(x, meta_tokens, mix_norm_g, w_in, tm_mu, tm_w0, tm_w_decay_up, tm_a0, tm_w_a_up, tm_w_g_up, tm_k_k, tm_k_a, tm_r_k, tm_gn_g, tm_gn_b, da_q_norm_g, da_k_norm_g, da_lambda_q1, da_lambda_k1, da_lambda_q2, da_lambda_k2, da_subln_g, sc_conv_w, w_out, ffn_norm_g, ffn_w_gate, ffn_w_up, ffn_w_down, router_w, moe_w_gate, moe_w_up, moe_w_down):
    b, seq, d = x.shape
    depth = w_in.shape[0]
    l = N_META + seq
    lp = -(-l // LANES) * LANES
    tm = _row_tile(lp)
    r = b * lp
    s1 = RWKV_WIDTH

    meta = jnp.broadcast_to(meta_tokens.astype(x.dtype)[None], (b, N_META, d))
    h = jnp.concatenate([meta, x, jnp.zeros((b, lp - l, d), x.dtype)], axis=1).reshape(r, d)
    rc, rs1, rs2 = _rope_tables(lp)

    for i in range(depth):
        lam_init = 0.8 - 0.6 * math.exp(-0.3 * i)
        wi = w_in[i]
        w_cat = jnp.concatenate(
            [_pad_heads(wi[:, 0:s1], 1), _pad_heads(wi[:, s1:2 * s1], 1),
             _pad_heads(wi[:, 2 * s1:3 * s1], 1), wi[:, 3 * s1:]], axis=1).astype(BF16)
        qg = jnp.tile(da_q_norm_g[i], 2)[None]
        kg = jnp.tile(da_k_norm_g[i], 2)[None]
        pa, qs, kh, vt, pc = _inproj(h, mix_norm_g[i][None], w_cat, rc, rs1, rs2, qg, kg,
                                     tm=tm, lp=lp)

        mu = tm_mu[i]
        mu_p = jnp.concatenate([_pad_heads(mu[0:s1], 0), _pad_heads(mu[s1:2 * s1], 0),
                                _pad_heads(mu[2 * s1:3 * s1], 0), mu[3 * s1:]])[None]
        vecs = jnp.stack([_pad_heads(tm_w0[i], 0), _pad_heads(tm_a0[i], 0),
                          _pad_heads(tm_k_k[i], 0), _pad_heads(tm_k_a[i], 0),
                          _pad_heads(tm_r_k[i].reshape(s1), 0), _pad_heads(tm_gn_g[i], 0),
                          _pad_heads(tm_gn_b[i], 0), jnp.zeros((RWKV_PAD,), F32)])
        zl = lambda n: jnp.zeros((n, RWKV_PAD), F32)
        wd_p = jnp.concatenate([_pad_heads(tm_w_decay_up[i], 1), zl(AAA_LORA + GATE_LORA)], axis=0)
        wa_p = jnp.concatenate([zl(DECAY_LORA), _pad_heads(tm_w_a_up[i], 1), zl(GATE_LORA)], axis=0)
        wg_p = jnp.concatenate([zl(DECAY_LORA + AAA_LORA), _pad_heads(tm_w_g_up[i], 1)], axis=0)
        oa = _rwkv(pa.reshape(b, lp, PA_COLS), mu_p, vecs, wd_p, wa_p, wg_p, tb=tm)

        lamv = jnp.zeros((SUBLANES, LANES), F32).at[0:4, :DIFF_QK_DIM].set(
            jnp.stack([da_lambda_q1[i], da_lambda_k1[i], da_lambda_q2[i], da_lambda_k2[i]]))
        ob = _attention(qs.reshape(2, b, lp, DIFF_WIDTH), kh.reshape(b, lp, DIFF_WIDTH),
                        vt.reshape(b, lp // tm, DIFF_HEADS, VT_ROWS, tm), lamv, da_subln_g[i][None],
                        tq=tm, lam_init=lam_init)

        wo = w_out[i]
        cw = jnp.zeros((SUBLANES, CONV_WIDTH), F32).at[:CONV_K].set(sc_conv_w[i])
        mix = (h, oa.reshape(r, RWKV_PAD), ob.reshape(r, DIFF_WIDTH), pc, cw,
               _pad_heads(wo[:s1], 0).astype(BF16), wo[s1:s1 + DIFF_WIDTH].astype(BF16),
               wo[s1 + DIFF_WIDTH:].astype(BF16))
        j = i // 2
        if i % 2 == 0:
            h = _outproj(*mix, tm=tm, lp=lp,
                         ffn=(ffn_norm_g[i][None], ffn_w_gate[j].astype(BF16),
                              ffn_w_up[j].astype(BF16), ffn_w_down[j].astype(BF16)))
        else:
            h = _outproj(*mix, tm=tm, lp=lp)
            rw = jnp.zeros((d, LANES), F32).at[:, :N_EXPERTS].set(router_w[j])
            hn, route = _router(h, ffn_norm_g[i][None], rw, tm=tm)
            block_expert, row_src, row_dst = _routing_tables(route)
            y = _moe_experts(block_expert, row_src, row_dst, hn, _to_bf16(moe_w_gate, j),
                             _to_bf16(moe_w_up, j), _to_bf16(moe_w_down, j))
            h = _moe_combine(h, route, y, tm=tm)

    return h.reshape(b, lp, d)[:, N_META:l]
```

```python
import functools
import math

import jax
import jax.numpy as jnp
from jax import lax
from jax.experimental import pallas as pl
from jax.experimental.pallas import tpu as pltpu

N_META = 16
RWKV_HEAD_DIM = 64
RWKV_HEADS = 4
RWKV_WIDTH = RWKV_HEADS * RWKV_HEAD_DIM
DECAY_LORA = 32
AAA_LORA = 32
GATE_LORA = 64
LORA_COLS = DECAY_LORA + AAA_LORA + GATE_LORA
DIFF_QK_DIM = 64
DIFF_V_DIM = 128
DIFF_HEADS = 4
DIFF_WIDTH = DIFF_HEADS * DIFF_V_DIM
CONV_WIDTH = 256
CONV_K = 3
ROPE_THETA = 500000.0
ROPE_DIM = DIFF_QK_DIM // 4
N_EXPERTS = 8
TOP_K = 2
MOE_BLOCK = 256
NORM_EPS = 1e-6
RWKV_GN_EPS = 64e-5
SUBLN_EPS = 1e-5
NEG_INF = -1e30

LANES = 128
SUBLANES = 8
MXU_DIM = 256
VMEM_LIMIT_BYTES = 56 * 1024 * 1024

HEAD_PAD = LANES
RWKV_PAD = RWKV_HEADS * HEAD_PAD
PA_COLS = 3 * RWKV_PAD + LORA_COLS
PB_COLS = 3 * DIFF_WIDTH
PC_COLS = 3 * CONV_WIDTH
VT_ROWS = DIFF_V_DIM + 16
LOG2_E = math.log2(math.e)
CHUNK = 64

F32 = jnp.float32
BF16 = jnp.bfloat16
HIGHEST = lax.Precision.HIGHEST


def _dot(a, b):
    return jnp.dot(a.astype(BF16), b.astype(BF16), preferred_element_type=F32)


def _dot_nt(a, b):
    return lax.dot_general(a.astype(BF16), b.astype(BF16), (((1,), (1,)), ((), ())),
                           preferred_element_type=F32)


def _dot_f32(a, b):
    return jnp.dot(a, b, preferred_element_type=F32, precision=HIGHEST)


def _row_tile(lp):
    for t in (384, 256, 128):
        if lp % t == 0:
            return t
    raise ValueError(f"padded length {lp} is not a multiple of {LANES}")


def _params(sem):
    return pltpu.CompilerParams(dimension_semantics=sem, vmem_limit_bytes=VMEM_LIMIT_BYTES)


def _inproj_kernel(h_ref, g_ref, w_ref, rc_ref, rs1_ref, rs2_ref, qg_ref, kg_ref,
                   pa_ref, qs_ref, kh_ref, vt_ref, pc_ref):
    tm = h_ref.shape[0]
    x = h_ref[...]
    ms = jnp.mean(x * x, axis=-1, keepdims=True)
    xn = (x * lax.rsqrt(ms + NORM_EPS) * g_ref[...]).astype(BF16)
    pb = jnp.dot(xn, w_ref[:, PA_COLS:PA_COLS + PB_COLS], preferred_element_type=F32)

    lane = lax.broadcasted_iota(jnp.int32, (tm, LANES), 1)
    lo = lane < DIFF_QK_DIM
    rc, rs1, rs2 = rc_ref[...], rs1_ref[...], rs2_ref[...]

    def norm_rope(t, g):
        t2 = t * t
        ss_lo = jnp.sum(jnp.where(lo, t2, 0.0), axis=-1, keepdims=True)
        ss_hi = jnp.sum(jnp.where(lo, 0.0, t2), axis=-1, keepdims=True)
        inv = jnp.where(lo, lax.rsqrt(ss_lo / DIFF_QK_DIM + NORM_EPS),
                        lax.rsqrt(ss_hi / DIFF_QK_DIM + NORM_EPS))
        tn = t * inv * g
        half = ROPE_DIM // 2
        return tn * rc + pltpu.roll(tn, LANES - half, 1) * rs1 + pltpu.roll(tn, half, 1) * rs2

    for hd in range(DIFF_HEADS):
        if hd == 0:
            pa_ref[...] = jnp.dot(xn, w_ref[:, :PA_COLS], preferred_element_type=F32)
        if hd == DIFF_HEADS // 2:
            pc_ref[...] = jnp.dot(xn, w_ref[:, PA_COLS + PB_COLS:], preferred_element_type=F32)
        cs = slice(hd * LANES, (hd + 1) * LANES)
        q = norm_rope(pb[:, cs], qg_ref[...]) * (DIFF_QK_DIM ** -0.5 * LOG2_E)
        qs_ref[0, :, cs] = jnp.where(lo, q, 0.0).astype(BF16)
        qs_ref[1, :, cs] = jnp.where(lo, 0.0, q).astype(BF16)
        kcs = slice(DIFF_WIDTH + hd * LANES, DIFF_WIDTH + (hd + 1) * LANES)
        kh_ref[:, cs] = norm_rope(pb[:, kcs], kg_ref[...]).astype(BF16)
        vcs = slice(2 * DIFF_WIDTH + hd * LANES, 2 * DIFF_WIDTH + (hd + 1) * LANES)
        vt_ref[0, hd, :DIFF_V_DIM] = pb[:, vcs].T.astype(BF16)
        vt_ref[0, hd, DIFF_V_DIM:] = jnp.ones((VT_ROWS - DIFF_V_DIM, tm), BF16)


def _inproj(h, g, w, rc, rs1, rs2, qg, kg, *, tm, lp):
    r, d = h.shape
    nt_b = lp // tm
    row = lambda i: (i, 0)
    const = lambda i: (0, 0)
    rope = lambda i: (i % nt_b, 0)
    return pl.pallas_call(
        _inproj_kernel,
        grid=(r // tm,),
        in_specs=[
            pl.BlockSpec((tm, d), row),
            pl.BlockSpec((1, d), const),
            pl.BlockSpec(w.shape, const),
            pl.BlockSpec((tm, LANES), rope),
            pl.BlockSpec((tm, LANES), rope),
            pl.BlockSpec((tm, LANES), rope),
            pl.BlockSpec((1, LANES), const),
            pl.BlockSpec((1, LANES), const),
        ],
        out_specs=[
            pl.BlockSpec((tm, PA_COLS), row),
            pl.BlockSpec((2, tm, DIFF_WIDTH), lambda i: (0, i, 0)),
            pl.BlockSpec((tm, DIFF_WIDTH), row),
            pl.BlockSpec((1, DIFF_HEADS, VT_ROWS, tm), lambda i: (i, 0, 0, 0)),
            pl.BlockSpec((tm, PC_COLS), row),
        ],
        out_shape=[
            jax.ShapeDtypeStruct((r, PA_COLS), F32),
            jax.ShapeDtypeStruct((2, r, DIFF_WIDTH), BF16),
            jax.ShapeDtypeStruct((r, DIFF_WIDTH), BF16),
            jax.ShapeDtypeStruct((r // tm, DIFF_HEADS, VT_ROWS, tm), BF16),
            jax.ShapeDtypeStruct((r, PC_COLS), F32),
        ],
        compiler_params=_params(("arbitrary",)),
        name="inproj",
    )(h, g, w, rc, rs1, rs2, qg, kg)


def _split3(x):
    hi = x.astype(BF16)
    r1 = x - hi.astype(F32)
    mid = r1.astype(BF16)
    lo = (r1 - mid.astype(F32)).astype(BF16)
    return hi, mid, lo


def _dot_hi(a, b):
    ah = a.astype(BF16)
    al = (a - ah.astype(F32)).astype(BF16)
    bh = b.astype(BF16)
    bl = (b - bh.astype(F32)).astype(BF16)
    d = lambda p, q: jnp.dot(p, q, preferred_element_type=F32)
    return d(ah, bh) + d(ah, bl) + d(al, bh)


def _bmm(a, b):
    return jnp.einsum('cik,ckj->cij', a.astype(BF16), b.astype(BF16), preferred_element_type=F32)


def _bmm_nt(a, b):
    return jnp.einsum('cik,cjk->cij', a.astype(BF16), b.astype(BF16), preferred_element_type=F32)


def _rwkv_kernel(pa_ref, halo_ref, mu_ref, vec_ref, wd_ref, wa_ref, wg_ref, oa_ref, hstate_s):
    tb = pa_ref.shape[0]
    nc = tb // CHUNK
    i = pl.program_id(1)

    @pl.when(i == 0)
    def _():
        hstate_s[...] = jnp.zeros_like(hstate_s)

    x = pa_ref[...]
    prev_row = jnp.where(i == 0, 0.0, halo_ref[SUBLANES - 1:SUBLANES, :])
    row = lax.broadcasted_iota(jnp.int32, (tb, 1), 0)
    prev = jnp.where(row == 0, prev_row, pltpu.roll(x, 1, 0))
    hx = x + (prev - x) * mu_ref[...]
    r = hx[:, 0:RWKV_PAD]
    k = hx[:, RWKV_PAD:2 * RWKV_PAD]
    v = hx[:, 2 * RWKV_PAD:3 * RWKV_PAD]
    lora = hx[:, 3 * RWKV_PAD:]
    w0, a0, k_k, k_a = vec_ref[0:1, :], vec_ref[1:2, :], vec_ref[2:3, :], vec_ref[3:4, :]
    r_k, gn_g, gn_b = vec_ref[4:5, :], vec_ref[5:6, :], vec_ref[6:7, :]

    w = -jax.nn.softplus(-(w0 + _dot_hi(jnp.tanh(lora), wd_ref[...]))) - 0.5
    wl = -jnp.exp(w)
    alr = jax.nn.sigmoid(a0 + _dot(lora, wa_ref[...]))
    gate = _dot(jax.nn.sigmoid(lora), wg_ref[...])
    kk = k * k_k
    k2 = k * (1.0 + (alr - 1.0) * k_a)
    rkr = r * k2 * r_k

    ri = lax.broadcasted_iota(jnp.int32, (CHUNK, CHUNK), 0)
    ci = lax.broadcasted_iota(jnp.int32, (CHUNK, CHUNK), 1)
    lane_c = lax.broadcasted_iota(jnp.int32, (CHUNK, HEAD_PAD), 1)
    row_c = lax.broadcasted_iota(jnp.int32, (CHUNK, HEAD_PAD), 0)
    col_c = jnp.where(lane_c >= CHUNK, lane_c - CHUNK, lane_c)
    low_incl2 = (col_c <= row_c)[None]
    low_strict2 = (col_c < row_c)[None]
    tril_b = jnp.where(ci <= ri, 1.0, 0.0).astype(BF16)
    eye_k = (lax.broadcasted_iota(jnp.int32, (HEAD_PAD, HEAD_PAD), 0)
             == lax.broadcasted_iota(jnp.int32, (HEAD_PAD, HEAD_PAD), 1))[None]
    n_sq = int(math.log2(CHUNK))

    parts = _split3(wl)
    cums = []
    for c in range(nc):
        rows = slice(c * CHUNK, (c + 1) * CHUNK)
        cums.append(sum(jnp.dot(tril_b, p[rows], preferred_element_type=F32) for p in parts))
    cum_all = jnp.concatenate(cums, axis=0)

    lane = lax.broadcasted_iota(jnp.int32, (tb, HEAD_PAD), 1)
    real = lane < RWKV_HEAD_DIM
    to3 = lambda t: t.reshape(nc, CHUNK, HEAD_PAD)
    heads = range(RWKV_HEADS)
    hsl = [slice(hd * HEAD_PAD, (hd + 1) * HEAD_PAD) for hd in heads]
    kkn, bonus = [], []
    for cs in hsl:
        kk_h = kk[:, cs]
        nrm = jnp.sqrt(jnp.sum(kk_h * kk_h, axis=-1, keepdims=True))
        kkn.append(kk_h / jnp.maximum(nrm, 1e-12))
        bonus.append(jnp.sum(rkr[:, cs], axis=-1, keepdims=True) * v[:, cs])

    vc = [to3(v[:, cs]) for cs in hsl]
    zero_v = jnp.zeros_like(vc[0])
    ah, rh, last, g, bk_last = [], [], [], [], []
    for hd, cs in enumerate(hsl):
        wl3, cum = to3(wl[:, cs]), to3(cum_all[:, cs])
        rr, kc = to3(r[:, cs]), to3(k2[:, cs])
        ac, bc = to3(-kkn[hd]), to3(kkn[hd] * alr[:, cs])
        last.append(cum[:, CHUNK - 1:CHUNK, :])
        e_neg = jnp.exp(-cum)
        e_last = jnp.exp(last[hd] - cum)
        ah.append(ac * jnp.exp(cum - wl3))
        rh.append(rr * jnp.exp(cum))
        bk_last.append(jnp.concatenate([bc * e_last, kc * e_last], axis=1))
        g.append(_bmm_nt(jnp.concatenate([ah[hd], rh[hd]], axis=1),
                         jnp.concatenate([bc * e_neg, kc * e_neg], axis=1)))
    a_abk = [jnp.where(low_strict2, g[hd][:, :CHUNK], 0.0) for hd in heads]
    r_bk = [jnp.where(low_incl2, g[hd][:, CHUNK:], 0.0) for hd in heads]
    xx = [jnp.concatenate([_bmm(a_abk[hd], jnp.concatenate([zero_v, vc[hd]], axis=1)), ah[hd]],
                          axis=2) for hd in heads]
    ppad = [jnp.where(lane_c < CHUNK, a_abk[hd], 0.0) for hd in heads]
    for j in range(n_sq):
        for hd in heads:
            pj = ppad[hd][:, :, :CHUNK]
            if j + 1 < n_sq:
                res = _bmm(pj, jnp.concatenate([xx[hd], ppad[hd]], axis=2))
                xx[hd] = xx[hd] + res[:, :, :2 * HEAD_PAD]
                ppad[hd] = res[:, :, 2 * HEAD_PAD:]
            else:
                xx[hd] = xx[hd] + _bmm(pj, xx[hd])
    qm, y0, n0 = [], [], []
    for hd in heads:
        lhs = jnp.concatenate([r_bk[hd], jnp.swapaxes(bk_last[hd], 1, 2)], axis=1)
        rhs = jnp.concatenate([xx[hd], jnp.concatenate([vc[hd], zero_v], axis=2)], axis=1)
        out = _bmm(lhs, rhs)
        y0.append(out[:, :CHUNK, :HEAD_PAD])
        n0.append(out[:, CHUNK:, :HEAD_PAD])
        q = rh[hd] + out[:, :CHUNK, HEAD_PAD:]
        m = jnp.where(eye_k, jnp.exp(last[hd]), 0.0) + out[:, CHUNK:, HEAD_PAD:]
        qm.append(jnp.concatenate([q, m], axis=1).astype(BF16))

    hs = [hstate_s[hd] for hd in heads]
    ys = [[] for _ in heads]
    for c in range(nc):
        for hd in heads:
            res = jnp.dot(qm[hd][c], hs[hd].astype(BF16), preferred_element_type=F32)
            ys[hd].append(res[:CHUNK] + y0[hd][c])
            hs[hd] = res[CHUNK:] + n0[hd][c]

    for hd, cs in enumerate(hsl):
        hstate_s[hd] = hs[hd]
        y = jnp.concatenate(ys[hd], axis=0)
        mean = jnp.sum(y, axis=-1, keepdims=True) / RWKV_HEAD_DIM
        dlt = jnp.where(real, y - mean, 0.0)
        var = jnp.sum(dlt * dlt, axis=-1, keepdims=True) / RWKV_HEAD_DIM
        yn = dlt * lax.rsqrt(var + RWKV_GN_EPS) * gn_g[:, cs] + gn_b[:, cs]
        oa_ref[:, cs] = ((yn + bonus[hd]) * gate[:, cs]).astype(BF16)


def _rwkv(pa, mu, vecs, wd, wa, wg, *, tb):
    b, lp, _ = pa.shape
    tile = lambda bi, i: (bi, i, 0)
    const = lambda bi, i: (0, 0)
    halo = lambda bi, i: (bi, jnp.maximum(i * (tb // SUBLANES) - 1, 0), 0)
    return pl.pallas_call(
        _rwkv_kernel,
        grid=(b, lp // tb),
        in_specs=[
            pl.BlockSpec((None, tb, PA_COLS), tile),
            pl.BlockSpec((None, SUBLANES, PA_COLS), halo),
            pl.BlockSpec((1, PA_COLS), const),
            pl.BlockSpec((SUBLANES, RWKV_PAD), const),
            pl.BlockSpec((LORA_COLS, RWKV_PAD), const),
            pl.BlockSpec((LORA_COLS, RWKV_PAD), const),
            pl.BlockSpec((LORA_COLS, RWKV_PAD), const),
        ],
        out_specs=pl.BlockSpec((None, tb, RWKV_PAD), tile),
        out_shape=jax.ShapeDtypeStruct((b, lp, RWKV_PAD), BF16),
        scratch_shapes=[pltpu.VMEM((RWKV_HEADS, HEAD_PAD, HEAD_PAD), F32)],
        compiler_params=_params(("arbitrary", "arbitrary")),
        name="rwkv7",
    )(pa, pa, mu, vecs, wd, wa, wg)


ATTN_HEADS_PER_STEP = 4


def _attn_kernel(q_ref, k_ref, vt_ref, lamv_ref, sg_ref, o_ref, acc_s, st_s, p_s, *, lam_init):
    tq = o_ref.shape[0]
    tk = vt_ref.shape[3]
    qi = pl.program_id(2)
    heads = range(ATTN_HEADS_PER_STEP)
    hl = [slice(h * LANES, (h + 1) * LANES) for h in heads]

    def scores(h, ki):
        k = k_ref[pl.ds(pl.multiple_of(ki * tk, tk), tk), hl[h]]
        return lax.dot_general(k, q_ref[:, :, hl[h]].reshape(2 * tq, LANES),
                               (((1,), (1,)), ((), ())), preferred_element_type=F32)

    def values(h, ki, p, alpha):
        acc_s[h] = alpha * acc_s[h] + jnp.dot(vt_ref[ki, h], p, preferred_element_type=F32)

    st = [scores(h, qi) for h in heads]
    for h in heads:
        st_s[h, 0] = scores(h, 0)

    kr = lax.broadcasted_iota(jnp.int32, (tk, 2 * tq), 0)
    qc = lax.broadcasted_iota(jnp.int32, (tk, 2 * tq), 1)
    qc = jnp.where(qc >= tq, qc - tq, qc)
    m0 = []
    for h in heads:
        s_h = jnp.where(kr <= qc, st[h], NEG_INF)
        m0.append(jnp.max(s_h, axis=0, keepdims=True))
        p_s[h, 1] = jnp.exp2(s_h - m0[h]).astype(BF16)
        acc_s[h] = jnp.zeros(acc_s.shape[1:], acc_s.dtype)

    def trip(k, carry, slot):
        out = []
        nxt = jnp.minimum(k + 1, qi - 1)
        prv = jnp.where(k == 0, qi, k - 1)
        for step in range(len(heads) + 1):
            if step < len(heads):
                st_s[step, 1 - slot] = scores(step, nxt)
            if step >= 1:
                h = step - 1
                m_prev, alpha_prev = carry[h]
                s_k = st_s[h, slot]
                m_new = jnp.maximum(m_prev, jnp.max(s_k, axis=0, keepdims=True))
                alpha = jnp.exp2(m_prev - m_new)
                values(h, prv, p_s[h, 1 - slot], alpha_prev)
                p_s[h, slot] = jnp.exp2(s_k - m_new).astype(BF16)
                out.append((m_new, alpha))
        return tuple(out)

    def two_trips(j, c):
        c = trip(2 * j, c, 0)
        return lax.cond(2 * j + 1 < qi, lambda cc: trip(2 * j + 1, cc, 1), lambda cc: cc, c)

    init = tuple((m0[h], jnp.ones_like(m0[h])) for h in heads)
    carry = lax.fori_loop(0, (qi + 1) // 2, two_trips, init)
    last_slot = (qi + 1) % 2
    for h in heads:
        values(h, jnp.where(qi == 0, 0, qi - 1), p_s[h, last_slot], carry[h][1])

    lv = lamv_ref[...]
    lam = (jnp.exp(jnp.sum(lv[0:1] * lv[1:2], axis=-1, keepdims=True))
           - jnp.exp(jnp.sum(lv[2:3] * lv[3:4], axis=-1, keepdims=True)) + lam_init)
    for h in heads:
        acc = acc_s[h]
        ot = acc[:DIFF_V_DIM] / acc[DIFF_V_DIM:DIFF_V_DIM + 1]
        od = (ot[:, :tq] - lam * ot[:, tq:]).T
        ms = jnp.mean(od * od, axis=-1, keepdims=True)
        o_ref[:, hl[h]] = (od * lax.rsqrt(ms + SUBLN_EPS) * sg_ref[...]
                           * (1.0 - lam_init)).astype(BF16)


def _attention(qs, kh, vt, lamv, sg, *, tq, lam_init):
    _, b, lp, _ = qs.shape
    nq = lp // tq
    hp = ATTN_HEADS_PER_STEP
    const = lambda bi, hg, qi: (0, 0)
    return pl.pallas_call(
        functools.partial(_attn_kernel, lam_init=lam_init),
        grid=(b, DIFF_HEADS // hp, nq),
        in_specs=[
            pl.BlockSpec((2, None, tq, hp * LANES), lambda bi, hg, qi: (0, bi, qi, hg)),
            pl.BlockSpec((None, lp, hp * LANES), lambda bi, hg, qi: (bi, 0, hg)),
            pl.BlockSpec((None, nq, hp, VT_ROWS, tq), lambda bi, hg, qi: (bi, 0, hg, 0, 0)),
            pl.BlockSpec((SUBLANES, LANES), const),
            pl.BlockSpec((1, LANES), const),
        ],
        out_specs=pl.BlockSpec((None, tq, hp * LANES), lambda bi, hg, qi: (bi, qi, hg)),
        out_shape=jax.ShapeDtypeStruct((b, lp, DIFF_WIDTH), BF16),
        scratch_shapes=[pltpu.VMEM((hp, VT_ROWS, 2 * tq), F32),
                        pltpu.VMEM((hp, 2, tq, 2 * tq), F32),
                        pltpu.VMEM((hp, 2, tq, 2 * tq), BF16)],
        compiler_params=_params(("arbitrary", "arbitrary", "arbitrary")),
        name="diffattn",
    )(qs, kh, vt, lamv, sg)


def _mixer_out(h_ref, oa_ref, ob_ref, pc_ref, pch_ref, cw_ref, wa_ref, wb_ref, wc_ref,
               tiles_per_batch):
    tm = h_ref.shape[0]
    first = (pl.program_id(0) % tiles_per_batch) == 0
    pc = pc_ref[...]
    bg = pc[:, :CONV_WIDTH]
    z = pc[:, CONV_WIDTH:2 * CONV_WIDTH] * pc[:, 2 * CONV_WIDTH:]
    ph = pch_ref[...]
    zh = jnp.where(first, 0.0, ph[:, CONV_WIDTH:2 * CONV_WIDTH] * ph[:, 2 * CONV_WIDTH:])
    row = lax.broadcasted_iota(jnp.int32, (tm, 1), 0)
    z1 = jnp.where(row == 0, zh[SUBLANES - 1:SUBLANES], pltpu.roll(z, 1, 0))
    z2 = jnp.where(row == 0, zh[SUBLANES - 2:SUBLANES - 1],
                   jnp.where(row == 1, zh[SUBLANES - 1:SUBLANES], pltpu.roll(z, 2, 0)))
    cw = cw_ref[...]
    y = cw[0:1] * z2 + cw[1:2] * z1 + cw[2:3] * z
    oc = (bg * y).astype(BF16)
    return (h_ref[...]
            + jnp.dot(oa_ref[...], wa_ref[...], preferred_element_type=F32)
            + jnp.dot(ob_ref[...], wb_ref[...], preferred_element_type=F32)
            + jnp.dot(oc, wc_ref[...], preferred_element_type=F32))


def _outproj_ffn_kernel(h_ref, oa_ref, ob_ref, pc_ref, pch_ref, cw_ref, wa_ref, wb_ref, wc_ref,
                        g_ref, wg_ref, wu_ref, wd_ref, o_ref, *, tiles_per_batch):
    h1 = _mixer_out(h_ref, oa_ref, ob_ref, pc_ref, pch_ref, cw_ref, wa_ref, wb_ref, wc_ref,
                    tiles_per_batch)
    o_ref[...] = h1 + _swiglu_block(_rms(h1, g_ref[...]).astype(BF16), wg_ref, wu_ref, wd_ref)


def _outproj_kernel(h_ref, oa_ref, ob_ref, pc_ref, pch_ref, cw_ref, wa_ref, wb_ref, wc_ref,
                    o_ref, *, tiles_per_batch):
    o_ref[...] = _mixer_out(h_ref, oa_ref, ob_ref, pc_ref, pch_ref, cw_ref, wa_ref, wb_ref, wc_ref,
                            tiles_per_batch)


def _outproj(h, oa, ob, pc, cw, wa, wb, wc, *, tm, lp, ffn=None):
    r, d = h.shape
    row = lambda i: (i, 0)
    const = lambda i: (0, 0)
    halo = lambda i: (jnp.maximum(i * (tm // SUBLANES) - 1, 0), 0)
    tail = list(ffn) if ffn is not None else []
    in_specs = [
        pl.BlockSpec((tm, d), row),
        pl.BlockSpec((tm, RWKV_PAD), row),
        pl.BlockSpec((tm, DIFF_WIDTH), row),
        pl.BlockSpec((tm, PC_COLS), row),
        pl.BlockSpec((SUBLANES, PC_COLS), halo),
        pl.BlockSpec((SUBLANES, CONV_WIDTH), const),
        pl.BlockSpec(wa.shape, const),
        pl.BlockSpec(wb.shape, const),
        pl.BlockSpec(wc.shape, const),
    ] + [pl.BlockSpec(t.shape, const) for t in tail]
    body, name = (_outproj_kernel, "outproj") if ffn is None else (_outproj_ffn_kernel, "outproj_ffn")
    return pl.pallas_call(
        functools.partial(body, tiles_per_batch=lp // tm),
        grid=(r // tm,),
        in_specs=in_specs,
        out_specs=pl.BlockSpec((tm, d), row),
        out_shape=jax.ShapeDtypeStruct((r, d), F32),
        compiler_params=_params(("arbitrary",)),
        name=name,
    )(h, oa, ob, pc, pc, cw, wa, wb, wc, *tail)


def _swiglu_block(xb, wg_ref, wu_ref, wd_ref, side_work=None):
    d_ff = wg_ref.shape[1]
    acc = jnp.zeros((xb.shape[0], wd_ref.shape[1]), F32)
    for c in range(d_ff // MXU_DIM):
        cs = slice(c * MXU_DIM, (c + 1) * MXU_DIM)
        gt = jnp.dot(xb, wg_ref[:, cs], preferred_element_type=F32)
        up = jnp.dot(xb, wu_ref[:, cs], preferred_element_type=F32)
        act = (jax.nn.silu(gt) * up).astype(BF16)
        acc = acc + jnp.dot(act, wd_ref[cs, :], preferred_element_type=F32)
        if side_work is not None:
            side_work(c)
    return acc


def _rms(x, g):
    ms = jnp.mean(x * x, axis=-1, keepdims=True)
    return x * lax.rsqrt(ms + NORM_EPS) * g


def _router_kernel(h_ref, g_ref, rw_ref, hn_ref, route_ref):
    tm = h_ref.shape[0]
    xn = _rms(h_ref[...], g_ref[...])
    hn_ref[...] = xn
    logits = _dot_f32(xn, rw_ref[...])
    lane = lax.broadcasted_iota(jnp.int32, (tm, LANES), 1)
    logits = jnp.where(lane < N_EXPERTS, logits, -jnp.inf)
    m1 = jnp.max(logits, axis=-1, keepdims=True)
    i1 = jnp.min(jnp.where(logits == m1, lane, LANES), axis=-1, keepdims=True)
    rest = jnp.where(lane == i1, -jnp.inf, logits)
    m2 = jnp.max(rest, axis=-1, keepdims=True)
    i2 = jnp.min(jnp.where(rest == m2, lane, LANES), axis=-1, keepdims=True)
    e2 = jnp.exp(m2 - m1)
    den = 1.0 + e2
    route_ref[...] = jnp.where(lane == 0, i1.astype(F32),
                     jnp.where(lane == 1, i2.astype(F32),
                     jnp.where(lane == 2, 1.0 / den,
                     jnp.where(lane == 3, e2 / den, 0.0))))


def _router(h, g, rw, *, tm):
    r, d = h.shape
    row = lambda i: (i, 0)
    const = lambda i: (0, 0)
    return pl.pallas_call(
        _router_kernel,
        grid=(r // tm,),
        in_specs=[pl.BlockSpec((tm, d), row), pl.BlockSpec((1, d), const),
                  pl.BlockSpec((d, LANES), const)],
        out_specs=[pl.BlockSpec((tm, d), row), pl.BlockSpec((tm, LANES), row)],
        out_shape=[jax.ShapeDtypeStruct((r, d), F32), jax.ShapeDtypeStruct((r, LANES), F32)],
        compiler_params=_params(("arbitrary",)),
        name="router",
    )(h, g, rw)


def _row_copy(src, dst, sem, src_row, dst_row):
    return pltpu.make_async_copy(src.at[pl.ds(src_row, 1), :], dst.at[pl.ds(dst_row, 1), :], sem)


DMA_UNROLL = 8
DMA_ISSUE_SLICES = 8
MOE_BUFFERS = 3


def _moe_kernel(bexp_ref, rsrc_ref, rdst_ref, hn_hbm, wg_ref, wu_ref, wd_ref, y_hbm,
                xbuf, ybuf, gsem, ssem):
    del bexp_ref
    nbuf = xbuf.shape[0]
    blk = xbuf.shape[1]
    i = pl.program_id(0)
    nb = pl.num_programs(0)
    slot = i % nbuf
    slot_prev = (i + nbuf - 1) % nbuf
    slot_next2 = (i + 2) % nbuf
    dump0 = y_hbm.shape[0] - nbuf * blk

    def for_rows(fn):
        def body(j, c):
            for u in range(DMA_UNROLL):
                fn(j * DMA_UNROLL + u)
            return c
        lax.fori_loop(0, blk // DMA_UNROLL, body, 0)

    def gather(block, s, j):
        return _row_copy(hn_hbm, xbuf.at[s], gsem.at[s], rsrc_ref[block * blk + j], j)

    def scatter(dst_row, s, j):
        return _row_copy(ybuf.at[s], y_hbm, ssem.at[s], j, dst_row)

    @pl.when(i == 0)
    def _():
        for_rows(lambda j: gather(0, 0, j).start())
        for_rows(lambda j: gather(1, 1, j).start())
        ybuf[nbuf - 1] = jnp.zeros(ybuf.shape[1:], ybuf.dtype)
        for k in range(nbuf - 1):
            zero_fill = pltpu.make_async_copy(
                ybuf.at[nbuf - 1], y_hbm.at[pl.ds(dump0 + k * blk, blk), :], ssem.at[k])
            zero_fill.start()
            zero_fill.wait()

    for_rows(lambda j: gather(0, slot, j).wait())

    @pl.when(i >= 2)
    def _():
        for_rows(lambda j: scatter(0, slot, j).wait())

    ahead = jnp.minimum(i + 2, nb - 1)
    prev = jnp.maximum(i - 1, 0)
    per = -(-blk // DMA_ISSUE_SLICES)

    def side_work(c):
        for j in range(c * per, min((c + 1) * per, blk)):
            gather(ahead, slot_next2, j).start(priority=j % 2)
            dst = jnp.where(i == 0, dump0 + (nbuf - 1) * blk + j, rdst_ref[prev * blk + j])
            scatter(dst, slot_prev, j).start(priority=(j + 1) % 2)

    ybuf[slot] = _swiglu_block(xbuf[slot].astype(BF16), wg_ref, wu_ref, wd_ref, side_work)

    @pl.when(i == nb - 1)
    def _():
        for_rows(lambda j: gather(0, (i + 1) % nbuf, j).wait())
        for_rows(lambda j: gather(0, slot_next2, j).wait())
        for_rows(lambda j: scatter(0, (i + 1) % nbuf, j).wait())
        for_rows(lambda j: scatter(0, slot_prev, j).wait())
        for_rows(lambda j: scatter(rdst_ref[i * blk + j], slot, j).start())
        for_rows(lambda j: scatter(0, slot, j).wait())


def _moe_experts(block_expert, row_src, row_dst, hn, wg, wu, wd):
    nb = block_expert.shape[0]
    r, d = hn.shape
    d_ff = wg.shape[2]
    wmap = lambda i, be, rs, rd: (be[i], 0, 0)
    grid_spec = pltpu.PrefetchScalarGridSpec(
        num_scalar_prefetch=3,
        grid=(nb,),
        in_specs=[
            pl.BlockSpec(memory_space=pl.ANY),
            pl.BlockSpec((None, d, d_ff), wmap),
            pl.BlockSpec((None, d, d_ff), wmap),
            pl.BlockSpec((None, d_ff, d), wmap),
        ],
        out_specs=pl.BlockSpec(memory_space=pl.ANY),
        scratch_shapes=[pltpu.VMEM((MOE_BUFFERS, MOE_BLOCK, d), F32),
                        pltpu.VMEM((MOE_BUFFERS, MOE_BLOCK, d), F32),
                        pltpu.SemaphoreType.DMA((MOE_BUFFERS,)),
                        pltpu.SemaphoreType.DMA((MOE_BUFFERS,))],
    )
    return pl.pallas_call(
        _moe_kernel,
        grid_spec=grid_spec,
        out_shape=jax.ShapeDtypeStruct((TOP_K * r + MOE_BUFFERS * MOE_BLOCK, d), F32),
        compiler_params=_params(("arbitrary",)),
        name="moe_experts",
    )(block_expert, row_src, row_dst, hn, wg, wu, wd)


def _combine_kernel(h_ref, route_ref, y0_ref, y1_ref, o_ref):
    route = route_ref[...]
    o_ref[...] = h_ref[...] + route[:, 2:3] * y0_ref[...] + route[:, 3:4] * y1_ref[...]


def _moe_combine(h, route, y, *, tm):
    r, d = h.shape
    nt = r // tm
    row = lambda i: (i, 0)
    return pl.pallas_call(
        _combine_kernel,
        grid=(nt,),
        in_specs=[
            pl.BlockSpec((tm, d), row),
            pl.BlockSpec((tm, LANES), row),
            pl.BlockSpec((tm, d), row),
            pl.BlockSpec((tm, d), lambda i: (i + nt, 0)),
        ],
        out_specs=pl.BlockSpec((tm, d), row),
        out_shape=jax.ShapeDtypeStruct((r, d), F32),
        compiler_params=_params(("arbitrary",)),
        name="moe_combine",
    )(h, route, y, y)


def _slot_owner_kernel(dest_ref, owner_ref):
    n_pairs = dest_ref.shape[0]
    n_slots = owner_ref.shape[0]

    def clear(i, c):
        for u in range(DMA_UNROLL):
            owner_ref[i * DMA_UNROLL + u] = -1
        return c

    lax.fori_loop(0, n_slots // DMA_UNROLL, clear, 0)

    def place(i, c):
        for u in range(DMA_UNROLL):
            p = i * DMA_UNROLL + u
            owner_ref[dest_ref[p]] = p
        return c

    lax.fori_loop(0, n_pairs // DMA_UNROLL, place, 0)


def _slot_owner(dest, n_slots):
    assert dest.shape[0] % DMA_UNROLL == 0 and n_slots % DMA_UNROLL == 0
    smem = pl.BlockSpec(memory_space=pltpu.SMEM)
    return pl.pallas_call(
        _slot_owner_kernel,
        in_specs=[smem],
        out_specs=smem,
        out_shape=jax.ShapeDtypeStruct((n_slots,), jnp.int32),
        name="slot_owner",
    )(dest)


def _routing_tables(route):
    r = route.shape[0]
    tk = r * TOP_K
    flat_e = route[:, :TOP_K].reshape(tk).astype(jnp.int32)
    onehot = (flat_e[:, None] == jnp.arange(N_EXPERTS, dtype=jnp.int32)[None, :]).astype(jnp.int32)
    csum = jnp.cumsum(onehot, axis=0)
    rank = jnp.sum(csum * onehot, axis=1) - 1
    counts = csum[-1]
    padded = (counts + MOE_BLOCK - 1) // MOE_BLOCK * MOE_BLOCK
    pend = jnp.cumsum(padded)
    pstart = pend - padded
    dest = pstart[flat_e] + rank
    nb = -(-tk // MOE_BLOCK) + N_EXPERTS
    n_slots = nb * MOE_BLOCK
    pair = _slot_owner(dest, n_slots)
    used = pair >= 0
    row_src = jnp.where(used, pair // TOP_K, 0)
    slot_id = jnp.arange(n_slots, dtype=jnp.int32)
    dump_row = tk + (slot_id // MOE_BLOCK) % MOE_BUFFERS * MOE_BLOCK + slot_id % MOE_BLOCK
    row_dst = jnp.where(used, (pair % TOP_K) * r + pair // TOP_K, dump_row)
    block_start = jnp.arange(nb, dtype=jnp.int32) * MOE_BLOCK
    block_expert = jnp.minimum(jnp.searchsorted(pend, block_start, side='right'),
                               N_EXPERTS - 1).astype(jnp.int32)
    return block_expert, row_src, row_dst


def _pad_heads(t, axis):
    axis = axis % t.ndim
    shp = t.shape
    t = t.reshape(shp[:axis] + (RWKV_HEADS, RWKV_HEAD_DIM) + shp[axis + 1:])
    pad = [(0, 0)] * t.ndim
    pad[axis + 1] = (0, HEAD_PAD - RWKV_HEAD_DIM)
    t = jnp.pad(t, pad)
    return t.reshape(shp[:axis] + (RWKV_PAD,) + shp[axis + 1:])


def _cast_kernel(x_ref, o_ref):
    o_ref[...] = x_ref[...].astype(o_ref.dtype)


def _to_bf16(w, layer):
    _, e, a, b = w.shape
    return pl.pallas_call(
        _cast_kernel,
        grid=(e, 2),
        in_specs=[pl.BlockSpec((None, None, a // 2, b), lambda i, j: (layer, i, j, 0))],
        out_specs=pl.BlockSpec((None, a // 2, b), lambda i, j: (i, j, 0)),
        out_shape=jax.ShapeDtypeStruct((e, a, b), BF16),
        compiler_params=_params(("arbitrary", "arbitrary")),
        name="to_bf16",
    )(w)


def _rope_tables(lp):
    half = ROPE_DIM // 2
    inv_freq = ROPE_THETA ** (-jnp.arange(0, ROPE_DIM, 2, dtype=F32) / ROPE_DIM)
    ang = jnp.arange(lp, dtype=F32)[:, None] * inv_freq[None, :]
    cos, sin = jnp.cos(ang), jnp.sin(ang)
    ones = jnp.ones((lp, DIFF_QK_DIM - ROPE_DIM), F32)
    zeros = jnp.zeros((lp, DIFF_QK_DIM - ROPE_DIM), F32)
    zh = jnp.zeros((lp, half), F32)
    rc = jnp.concatenate([cos, cos, ones], axis=1)
    rs1 = jnp.concatenate([-sin, zh, zeros], axis=1)
    rs2 = jnp.concatenate([zh, sin, zeros], axis=1)
    tile2 = lambda a: jnp.concatenate([a, a], axis=1)
    return tile2(rc), tile2(rs1), tile2(rs2)


def kernel(x, meta_tokens, mix_norm_g, w_in, tm_mu, tm_w0, tm_w_decay_up, tm_a0, tm_w_a_up, tm_w_g_up, tm_k_k, tm_k_a, tm_r_k, tm_gn_g, tm_gn_b, da_q_norm_g, da_k_norm_g, da_lambda_q1, da_lambda_k1, da_lambda_q2, da_lambda_k2, da_subln_g, sc_conv_w, w_out, ffn_norm_g, ffn_w_gate, ffn_w_up, ffn_w_down, router_w, moe_w_gate, moe_w_up, moe_w_down):
    b, seq, d = x.shape
    depth = w_in.shape[0]
    l = N_META + seq
    lp = -(-l // LANES) * LANES
    tm = _row_tile(lp)
    r = b * lp
    s1 = RWKV_WIDTH

    meta = jnp.broadcast_to(meta_tokens.astype(x.dtype)[None], (b, N_META, d))
    h = jnp.concatenate([meta, x, jnp.zeros((b, lp - l, d), x.dtype)], axis=1).reshape(r, d)
    rc, rs1, rs2 = _rope_tables(lp)

    for i in range(depth):
        lam_init = 0.8 - 0.6 * math.exp(-0.3 * i)
        wi = w_in[i]
        w_cat = jnp.concatenate(
            [_pad_heads(wi[:, 0:s1], 1), _pad_heads(wi[:, s1:2 * s1], 1),
             _pad_heads(wi[:, 2 * s1:3 * s1], 1), wi[:, 3 * s1:]], axis=1).astype(BF16)
        qg = jnp.tile(da_q_norm_g[i], 2)[None]
        kg = jnp.tile(da_k_norm_g[i], 2)[None]
        pa, qs, kh, vt, pc = _inproj(h, mix_norm_g[i][None], w_cat, rc, rs1, rs2, qg, kg,
                                     tm=tm, lp=lp)

        mu = tm_mu[i]
        mu_p = jnp.concatenate([_pad_heads(mu[0:s1], 0), _pad_heads(mu[s1:2 * s1], 0),
                                _pad_heads(mu[2 * s1:3 * s1], 0), mu[3 * s1:]])[None]
        vecs = jnp.stack([_pad_heads(tm_w0[i], 0), _pad_heads(tm_a0[i], 0),
                          _pad_heads(tm_k_k[i], 0), _pad_heads(tm_k_a[i], 0),
                          _pad_heads(tm_r_k[i].reshape(s1), 0), _pad_heads(tm_gn_g[i], 0),
                          _pad_heads(tm_gn_b[i], 0), jnp.zeros((RWKV_PAD,), F32)])
        zl = lambda n: jnp.zeros((n, RWKV_PAD), F32)
        wd_p = jnp.concatenate([_pad_heads(tm_w_decay_up[i], 1), zl(AAA_LORA + GATE_LORA)], axis=0)
        wa_p = jnp.concatenate([zl(DECAY_LORA), _pad_heads(tm_w_a_up[i], 1), zl(GATE_LORA)], axis=0)
        wg_p = jnp.concatenate([zl(DECAY_LORA + AAA_LORA), _pad_heads(tm_w_g_up[i], 1)], axis=0)
        oa = _rwkv(pa.reshape(b, lp, PA_COLS), mu_p, vecs, wd_p, wa_p, wg_p, tb=tm)

        lamv = jnp.zeros((SUBLANES, LANES), F32).at[0:4, :DIFF_QK_DIM].set(
            jnp.stack([da_lambda_q1[i], da_lambda_k1[i], da_lambda_q2[i], da_lambda_k2[i]]))
        ob = _attention(qs.reshape(2, b, lp, DIFF_WIDTH), kh.reshape(b, lp, DIFF_WIDTH),
                        vt.reshape(b, lp // tm, DIFF_HEADS, VT_ROWS, tm), lamv, da_subln_g[i][None],
                        tq=tm, lam_init=lam_init)

        wo = w_out[i]
        cw = jnp.zeros((SUBLANES, CONV_WIDTH), F32).at[:CONV_K].set(sc_conv_w[i])
        mix = (h, oa.reshape(r, RWKV_PAD), ob.reshape(r, DIFF_WIDTH), pc, cw,
               _pad_heads(wo[:s1], 0).astype(BF16), wo[s1:s1 + DIFF_WIDTH].astype(BF16),
               wo[s1 + DIFF_WIDTH:].astype(BF16))
        j = i // 2
        if i % 2 == 0:
            h = _outproj(*mix, tm=tm, lp=lp,
                         ffn=(ffn_norm_g[i][None], ffn_w_gate[j].astype(BF16),
                              ffn_w_up[j].astype(BF16), ffn_w_down[j].astype(BF16)))
        else:
            h = _outproj(*mix, tm=tm, lp=lp)
            rw = jnp.zeros((d, LANES), F32).at[:, :N_EXPERTS].set(router_w[j])
            hn, route = _router(h, ffn_norm_g[i][None], rw, tm=tm)
            block_expert, row_src, row_dst = _routing_tables(route)
            y = _moe_experts(block_expert, row_src, row_dst, hn, _to_bf16(moe_w_gate, j),
                             _to_bf16(moe_w_up, j), _to_bf16(moe_w_down, j))
            h = _moe_combine(h, route, y, tm=tm)

    return h.reshape(b, lp, d)[:, N_META:l]
```

```python
import functools
import math

import jax
import jax.numpy as jnp
from jax import lax
from jax.experimental import pallas as pl
from jax.experimental.pallas import tpu as pltpu

N_META = 16
RWKV_HEAD_DIM = 64
RWKV_HEADS = 4
RWKV_WIDTH = RWKV_HEADS * RWKV_HEAD_DIM
DECAY_LORA = 32
AAA_LORA = 32
GATE_LORA = 64
LORA_COLS = DECAY_LORA + AAA_LORA + GATE_LORA
DIFF_QK_DIM = 64
DIFF_V_DIM = 128
DIFF_HEADS = 4
DIFF_WIDTH = DIFF_HEADS * DIFF_V_DIM
CONV_WIDTH = 256
CONV_K = 3
ROPE_THETA = 500000.0
ROPE_DIM = DIFF_QK_DIM // 4
N_EXPERTS = 8
TOP_K = 2
MOE_BLOCK = 256
NORM_EPS = 1e-6
RWKV_GN_EPS = 64e-5
SUBLN_EPS = 1e-5
NEG_INF = -1e30

LANES = 128
SUBLANES = 8
MXU_DIM = 256
VMEM_LIMIT_BYTES = 56 * 1024 * 1024

HEAD_PAD = LANES
RWKV_PAD = RWKV_HEADS * HEAD_PAD
PA_COLS = 3 * RWKV_PAD + LORA_COLS
PB_COLS = 3 * DIFF_WIDTH
PC_COLS = 3 * CONV_WIDTH
VT_ROWS = DIFF_V_DIM + 16
LOG2_E = math.log2(math.e)
CHUNK = 64

F32 = jnp.float32
BF16 = jnp.bfloat16
HIGHEST = lax.Precision.HIGHEST


def _dot(a, b):
    return jnp.dot(a.astype(BF16), b.astype(BF16), preferred_element_type=F32)


def _dot_f32(a, b):
    return jnp.dot(a, b, preferred_element_type=F32, precision=HIGHEST)


def _row_tile(lp):
    for t in (384, 256, 128):
        if lp % t == 0:
            return t
    raise ValueError(f"padded length {lp} is not a multiple of {LANES}")


def _params(sem):
    return pltpu.CompilerParams(dimension_semantics=sem, vmem_limit_bytes=VMEM_LIMIT_BYTES)


def _inproj_kernel(h_ref, g_ref, w_ref, rc_ref, rs1_ref, rs2_ref, qg_ref, kg_ref,
                   pa_ref, qs_ref, kh_ref, vt_ref, pc_ref):
    tm = h_ref.shape[0]
    x = h_ref[...]
    ms = jnp.mean(x * x, axis=-1, keepdims=True)
    xn = (x * lax.rsqrt(ms + NORM_EPS) * g_ref[...]).astype(BF16)
    pb = jnp.dot(xn, w_ref[:, PA_COLS:PA_COLS + PB_COLS], preferred_element_type=F32)

    lane = lax.broadcasted_iota(jnp.int32, (tm, LANES), 1)
    lo = lane < DIFF_QK_DIM
    rc, rs1, rs2 = rc_ref[...], rs1_ref[...], rs2_ref[...]

    def norm_rope(t, g):
        t2 = t * t
        ss_lo = jnp.sum(jnp.where(lo, t2, 0.0), axis=-1, keepdims=True)
        ss_hi = jnp.sum(jnp.where(lo, 0.0, t2), axis=-1, keepdims=True)
        inv = jnp.where(lo, lax.rsqrt(ss_lo / DIFF_QK_DIM + NORM_EPS),
                        lax.rsqrt(ss_hi / DIFF_QK_DIM + NORM_EPS))
        tn = t * inv * g
        half = ROPE_DIM // 2
        return tn * rc + pltpu.roll(tn, LANES - half, 1) * rs1 + pltpu.roll(tn, half, 1) * rs2

    for hd in range(DIFF_HEADS):
        if hd == 0:
            pa_ref[...] = jnp.dot(xn, w_ref[:, :PA_COLS], preferred_element_type=F32)
        if hd == DIFF_HEADS // 2:
            pc_ref[...] = jnp.dot(xn, w_ref[:, PA_COLS + PB_COLS:], preferred_element_type=F32)
        cs = slice(hd * LANES, (hd + 1) * LANES)
        q = norm_rope(pb[:, cs], qg_ref[...]) * (DIFF_QK_DIM ** -0.5 * LOG2_E)
        qs_ref[0, :, cs] = jnp.where(lo, q, 0.0).astype(BF16)
        qs_ref[1, :, cs] = jnp.where(lo, 0.0, q).astype(BF16)
        kcs = slice(DIFF_WIDTH + hd * LANES, DIFF_WIDTH + (hd + 1) * LANES)
        kh_ref[:, cs] = norm_rope(pb[:, kcs], kg_ref[...]).astype(BF16)
        vcs = slice(2 * DIFF_WIDTH + hd * LANES, 2 * DIFF_WIDTH + (hd + 1) * LANES)
        vt_ref[0, hd, :DIFF_V_DIM] = pb[:, vcs].T.astype(BF16)
        vt_ref[0, hd, DIFF_V_DIM:] = jnp.ones((VT_ROWS - DIFF_V_DIM, tm), BF16)


def _inproj(h, g, w, rc, rs1, rs2, qg, kg, *, tm, lp):
    r, d = h.shape
    nt_b = lp // tm
    row = lambda i: (i, 0)
    const = lambda i: (0, 0)
    rope = lambda i: (i % nt_b, 0)
    return pl.pallas_call(
        _inproj_kernel,
        grid=(r // tm,),
        in_specs=[
            pl.BlockSpec((tm, d), row),
            pl.BlockSpec((1, d), const),
            pl.BlockSpec(w.shape, const),
            pl.BlockSpec((tm, LANES), rope),
            pl.BlockSpec((tm, LANES), rope),
            pl.BlockSpec((tm, LANES), rope),
            pl.BlockSpec((1, LANES), const),
            pl.BlockSpec((1, LANES), const),
        ],
        out_specs=[
            pl.BlockSpec((tm, PA_COLS), row),
            pl.BlockSpec((2, tm, DIFF_WIDTH), lambda i: (0, i, 0)),
            pl.BlockSpec((tm, DIFF_WIDTH), row),
            pl.BlockSpec((1, DIFF_HEADS, VT_ROWS, tm), lambda i: (i, 0, 0, 0)),
            pl.BlockSpec((tm, PC_COLS), row),
        ],
        out_shape=[
            jax.ShapeDtypeStruct((r, PA_COLS), F32),
            jax.ShapeDtypeStruct((2, r, DIFF_WIDTH), BF16),
            jax.ShapeDtypeStruct((r, DIFF_WIDTH), BF16),
            jax.ShapeDtypeStruct((r // tm, DIFF_HEADS, VT_ROWS, tm), BF16),
            jax.ShapeDtypeStruct((r, PC_COLS), F32),
        ],
        compiler_params=_params(("arbitrary",)),
        name="inproj",
    )(h, g, w, rc, rs1, rs2, qg, kg)


def _split3(x):
    hi = x.astype(BF16)
    r1 = x - hi.astype(F32)
    mid = r1.astype(BF16)
    lo = (r1 - mid.astype(F32)).astype(BF16)
    return hi, mid, lo


def _dot_hi(a, b):
    ah = a.astype(BF16)
    al = (a - ah.astype(F32)).astype(BF16)
    bh = b.astype(BF16)
    bl = (b - bh.astype(F32)).astype(BF16)
    d = lambda p, q: jnp.dot(p, q, preferred_element_type=F32)
    return d(ah, bh) + d(ah, bl) + d(al, bh)


def _bmm(a, b):
    return jnp.einsum('cik,ckj->cij', a.astype(BF16), b.astype(BF16), preferred_element_type=F32)


def _bmm_nt(a, b):
    return jnp.einsum('cik,cjk->cij', a.astype(BF16), b.astype(BF16), preferred_element_type=F32)


def _rwkv_kernel(pa_ref, halo_ref, mu_ref, vec_ref, wd_ref, wa_ref, wg_ref, oa_ref, hstate_s):
    tb = pa_ref.shape[0]
    nc = tb // CHUNK
    i = pl.program_id(1)

    @pl.when(i == 0)
    def _():
        hstate_s[...] = jnp.zeros_like(hstate_s)

    x = pa_ref[...]
    prev_row = jnp.where(i == 0, 0.0, halo_ref[SUBLANES - 1:SUBLANES, :])
    row = lax.broadcasted_iota(jnp.int32, (tb, 1), 0)
    prev = jnp.where(row == 0, prev_row, pltpu.roll(x, 1, 0))
    hx = x + (prev - x) * mu_ref[...]
    r = hx[:, 0:RWKV_PAD]
    k = hx[:, RWKV_PAD:2 * RWKV_PAD]
    v = hx[:, 2 * RWKV_PAD:3 * RWKV_PAD]
    lora = hx[:, 3 * RWKV_PAD:]
    w0, a0, k_k, k_a = vec_ref[0:1, :], vec_ref[1:2, :], vec_ref[2:3, :], vec_ref[3:4, :]
    r_k, gn_g, gn_b = vec_ref[4:5, :], vec_ref[5:6, :], vec_ref[6:7, :]

    w = -jax.nn.softplus(-(w0 + _dot_hi(jnp.tanh(lora), wd_ref[...]))) - 0.5
    wl = -jnp.exp(w)
    alr = jax.nn.sigmoid(a0 + _dot(lora, wa_ref[...]))
    gate = _dot(jax.nn.sigmoid(lora), wg_ref[...])
    kk = k * k_k
    k2 = k * (1.0 + (alr - 1.0) * k_a)
    rkr = r * k2 * r_k

    ri = lax.broadcasted_iota(jnp.int32, (CHUNK, CHUNK), 0)
    ci = lax.broadcasted_iota(jnp.int32, (CHUNK, CHUNK), 1)
    lane_c = lax.broadcasted_iota(jnp.int32, (CHUNK, HEAD_PAD), 1)
    row_c = lax.broadcasted_iota(jnp.int32, (CHUNK, HEAD_PAD), 0)
    col_c = jnp.where(lane_c >= CHUNK, lane_c - CHUNK, lane_c)
    low_incl2 = (col_c <= row_c)[None]
    low_strict2 = (col_c < row_c)[None]
    tril_b = jnp.where(ci <= ri, 1.0, 0.0).astype(BF16)
    eye_k = (lax.broadcasted_iota(jnp.int32, (HEAD_PAD, HEAD_PAD), 0)
             == lax.broadcasted_iota(jnp.int32, (HEAD_PAD, HEAD_PAD), 1))[None]
    n_sq = int(math.log2(CHUNK))

    parts = _split3(wl)
    cums = []
    for c in range(nc):
        rows = slice(c * CHUNK, (c + 1) * CHUNK)
        cums.append(sum(jnp.dot(tril_b, p[rows], preferred_element_type=F32) for p in parts))
    cum_all = jnp.concatenate(cums, axis=0)

    lane = lax.broadcasted_iota(jnp.int32, (tb, HEAD_PAD), 1)
    real = lane < RWKV_HEAD_DIM
    to3 = lambda t: t.reshape(nc, CHUNK, HEAD_PAD)
    heads = range(RWKV_HEADS)
    hsl = [slice(hd * HEAD_PAD, (hd + 1) * HEAD_PAD) for hd in heads]
    vc = [to3(v[:, cs]) for cs in hsl]
    zero_v = jnp.zeros_like(vc[0])
    kkn, bonus, ah, rh, last, g, bk_last = [], [], [], [], [], [], []
    for hd, cs in enumerate(hsl):
        kk_h = kk[:, cs]
        nrm = jnp.sqrt(jnp.sum(kk_h * kk_h, axis=-1, keepdims=True))
        kkn.append(kk_h / jnp.maximum(nrm, 1e-12))
        bonus.append(jnp.sum(rkr[:, cs], axis=-1, keepdims=True) * v[:, cs])
        wl3, cum = to3(wl[:, cs]), to3(cum_all[:, cs])
        rr, kc = to3(r[:, cs]), to3(k2[:, cs])
        ac, bc = to3(-kkn[hd]), to3(kkn[hd] * alr[:, cs])
        last.append(cum[:, CHUNK - 1:CHUNK, :])
        e_neg = jnp.exp(-cum)
        e_last = jnp.exp(last[hd] - cum)
        ah.append(ac * jnp.exp(cum - wl3))
        rh.append(rr * jnp.exp(cum))
        bk_last.append(jnp.concatenate([bc * e_last, kc * e_last], axis=1))
        g.append(_bmm_nt(jnp.concatenate([ah[hd], rh[hd]], axis=1),
                         jnp.concatenate([bc * e_neg, kc * e_neg], axis=1)))
    a_abk = [jnp.where(low_strict2, g[hd][:, :CHUNK], 0.0) for hd in heads]
    r_bk = [jnp.where(low_incl2, g[hd][:, CHUNK:], 0.0) for hd in heads]
    xx = [jnp.concatenate([_bmm(a_abk[hd], jnp.concatenate([zero_v, vc[hd]], axis=1)), ah[hd]],
                          axis=2) for hd in heads]
    ppad = [jnp.where(lane_c < CHUNK, a_abk[hd], 0.0) for hd in heads]
    for j in range(n_sq):
        for hd in heads:
            pj = ppad[hd][:, :, :CHUNK]
            if j + 1 < n_sq:
                res = _bmm(pj, jnp.concatenate([xx[hd], ppad[hd]], axis=2))
                xx[hd] = xx[hd] + res[:, :, :2 * HEAD_PAD]
                ppad[hd] = res[:, :, 2 * HEAD_PAD:]
            else:
                xx[hd] = xx[hd] + _bmm(pj, xx[hd])
    qm, y0, n0 = [], [], []
    for hd in heads:
        lhs = jnp.concatenate([r_bk[hd], jnp.swapaxes(bk_last[hd], 1, 2)], axis=1)
        rhs = jnp.concatenate([xx[hd], jnp.concatenate([vc[hd], zero_v], axis=2)], axis=1)
        out = _bmm(lhs, rhs)
        y0.append(out[:, :CHUNK, :HEAD_PAD])
        n0.append(out[:, CHUNK:, :HEAD_PAD])
        q = rh[hd] + out[:, :CHUNK, HEAD_PAD:]
        m = jnp.where(eye_k, jnp.exp(last[hd]), 0.0) + out[:, CHUNK:, HEAD_PAD:]
        qm.append(jnp.concatenate([q, m], axis=1).astype(BF16))

    hs = [hstate_s[hd] for hd in heads]
    ys = [[] for _ in heads]
    for c in range(nc):
        for hd in heads:
            res = jnp.dot(qm[hd][c], hs[hd].astype(BF16), preferred_element_type=F32)
            ys[hd].append(res[:CHUNK] + y0[hd][c])
            hs[hd] = res[CHUNK:] + n0[hd][c]

    for hd, cs in enumerate(hsl):
        hstate_s[hd] = hs[hd]
        y = jnp.concatenate(ys[hd], axis=0)
        mean = jnp.sum(y, axis=-1, keepdims=True) / RWKV_HEAD_DIM
        dlt = jnp.where(real, y - mean, 0.0)
        var = jnp.sum(dlt * dlt, axis=-1, keepdims=True) / RWKV_HEAD_DIM
        yn = dlt * lax.rsqrt(var + RWKV_GN_EPS) * gn_g[:, cs] + gn_b[:, cs]
        oa_ref[:, cs] = ((yn + bonus[hd]) * gate[:, cs]).astype(BF16)


def _rwkv(pa, mu, vecs, wd, wa, wg, *, tb):
    b, lp, _ = pa.shape
    tile = lambda bi, i: (bi, i, 0)
    const = lambda bi, i: (0, 0)
    halo = lambda bi, i: (bi, jnp.maximum(i * (tb // SUBLANES) - 1, 0), 0)
    return pl.pallas_call(
        _rwkv_kernel,
        grid=(b, lp // tb),
        in_specs=[
            pl.BlockSpec((None, tb, PA_COLS), tile),
            pl.BlockSpec((None, SUBLANES, PA_COLS), halo),
            pl.BlockSpec((1, PA_COLS), const),
            pl.BlockSpec((SUBLANES, RWKV_PAD), const),
            pl.BlockSpec((LORA_COLS, RWKV_PAD), const),
            pl.BlockSpec((LORA_COLS, RWKV_PAD), const),
            pl.BlockSpec((LORA_COLS, RWKV_PAD), const),
        ],
        out_specs=pl.BlockSpec((None, tb, RWKV_PAD), tile),
        out_shape=jax.ShapeDtypeStruct((b, lp, RWKV_PAD), BF16),
        scratch_shapes=[pltpu.VMEM((RWKV_HEADS, HEAD_PAD, HEAD_PAD), F32)],
        compiler_params=_params(("arbitrary", "arbitrary")),
        name="rwkv7",
    )(pa, pa, mu, vecs, wd, wa, wg)


ATTN_HEADS_PER_STEP = 4


def _attn_kernel(q_ref, k_ref, vt_ref, lamv_ref, sg_ref, o_ref, acc_s, st_s, p_s, *, lam_init):
    tq = o_ref.shape[0]
    tk = vt_ref.shape[3]
    qi = pl.program_id(2)
    heads = range(ATTN_HEADS_PER_STEP)
    hl = [slice(h * LANES, (h + 1) * LANES) for h in heads]

    def scores(h, ki):
        k = k_ref[pl.ds(pl.multiple_of(ki * tk, tk), tk), hl[h]]
        return lax.dot_general(k, q_ref[:, :, hl[h]].reshape(2 * tq, LANES),
                               (((1,), (1,)), ((), ())), preferred_element_type=F32)

    def values(h, ki, p, alpha):
        acc_s[h] = alpha * acc_s[h] + jnp.dot(vt_ref[ki, h], p, preferred_element_type=F32)

    st = [scores(h, qi) for h in heads]
    for h in heads:
        st_s[h, 0] = scores(h, 0)

    kr = lax.broadcasted_iota(jnp.int32, (tk, 2 * tq), 0)
    qc = lax.broadcasted_iota(jnp.int32, (tk, 2 * tq), 1)
    qc = jnp.where(qc >= tq, qc - tq, qc)
    m0 = []
    for h in heads:
        s_h = jnp.where(kr <= qc, st[h], NEG_INF)
        m0.append(jnp.max(s_h, axis=0, keepdims=True))
        p_s[h, 1] = jnp.exp2(s_h - m0[h]).astype(BF16)
        acc_s[h] = jnp.zeros(acc_s.shape[1:], acc_s.dtype)

    def trip(k, carry, slot):
        out = []
        nxt = jnp.minimum(k + 1, qi - 1)
        prv = jnp.where(k == 0, qi, k - 1)
        for step in range(len(heads) + 1):
            if step < len(heads):
                st_s[step, 1 - slot] = scores(step, nxt)
            if step >= 1:
                h = step - 1
                m_prev, alpha_prev = carry[h]
                s_k = st_s[h, slot]
                m_new = jnp.maximum(m_prev, jnp.max(s_k, axis=0, keepdims=True))
                alpha = jnp.exp2(m_prev - m_new)
                values(h, prv, p_s[h, 1 - slot], alpha_prev)
                p_s[h, slot] = jnp.exp2(s_k - m_new).astype(BF16)
                out.append((m_new, alpha))
        return tuple(out)

    def two_trips(j, c):
        c = trip(2 * j, c, 0)
        return lax.cond(2 * j + 1 < qi, lambda cc: trip(2 * j + 1, cc, 1), lambda cc: cc, c)

    init = tuple((m0[h], jnp.ones_like(m0[h])) for h in heads)
    carry = lax.fori_loop(0, (qi + 1) // 2, two_trips, init)
    last_slot = (qi + 1) % 2
    for h in heads:
        values(h, jnp.where(qi == 0, 0, qi - 1), p_s[h, last_slot], carry[h][1])

    lv = lamv_ref[...]
    lam = (jnp.exp(jnp.sum(lv[0:1] * lv[1:2], axis=-1, keepdims=True))
           - jnp.exp(jnp.sum(lv[2:3] * lv[3:4], axis=-1, keepdims=True)) + lam_init)
    for h in heads:
        acc = acc_s[h]
        ot = acc[:DIFF_V_DIM] / acc[DIFF_V_DIM:DIFF_V_DIM + 1]
        od = (ot[:, :tq] - lam * ot[:, tq:]).T
        ms = jnp.mean(od * od, axis=-1, keepdims=True)
        o_ref[:, hl[h]] = (od * lax.rsqrt(ms + SUBLN_EPS) * sg_ref[...]
                           * (1.0 - lam_init)).astype(BF16)


def _attention(qs, kh, vt, lamv, sg, *, tq, lam_init):
    _, b, lp, _ = qs.shape
    nq = lp // tq
    hp = ATTN_HEADS_PER_STEP
    const = lambda bi, hg, qi: (0, 0)
    return pl.pallas_call(
        functools.partial(_attn_kernel, lam_init=lam_init),
        grid=(b, DIFF_HEADS // hp, nq),
        in_specs=[
            pl.BlockSpec((2, None, tq, hp * LANES), lambda bi, hg, qi: (0, bi, qi, hg)),
            pl.BlockSpec((None, lp, hp * LANES), lambda bi, hg, qi: (bi, 0, hg)),
            pl.BlockSpec((None, nq, hp, VT_ROWS, tq), lambda bi, hg, qi: (bi, 0, hg, 0, 0)),
            pl.BlockSpec((SUBLANES, LANES), const),
            pl.BlockSpec((1, LANES), const),
        ],
        out_specs=pl.BlockSpec((None, tq, hp * LANES), lambda bi, hg, qi: (bi, qi, hg)),
        out_shape=jax.ShapeDtypeStruct((b, lp, DIFF_WIDTH), BF16),
        scratch_shapes=[pltpu.VMEM((hp, VT_ROWS, 2 * tq), F32),
                        pltpu.VMEM((hp, 2, tq, 2 * tq), F32),
                        pltpu.VMEM((hp, 2, tq, 2 * tq), BF16)],
        compiler_params=_params(("arbitrary", "arbitrary", "arbitrary")),
        name="diffattn",
    )(qs, kh, vt, lamv, sg)


def _mixer_out(h_ref, oa_ref, ob_ref, pc_ref, pch_ref, cw_ref, wa_ref, wb_ref, wc_ref,
               tiles_per_batch):
    tm = h_ref.shape[0]
    first = (pl.program_id(0) % tiles_per_batch) == 0
    pc = pc_ref[...]
    bg = pc[:, :CONV_WIDTH]
    z = pc[:, CONV_WIDTH:2 * CONV_WIDTH] * pc[:, 2 * CONV_WIDTH:]
    ph = pch_ref[...]
    zh = jnp.where(first, 0.0, ph[:, CONV_WIDTH:2 * CONV_WIDTH] * ph[:, 2 * CONV_WIDTH:])
    row = lax.broadcasted_iota(jnp.int32, (tm, 1), 0)
    z1 = jnp.where(row == 0, zh[SUBLANES - 1:SUBLANES], pltpu.roll(z, 1, 0))
    z2 = jnp.where(row == 0, zh[SUBLANES - 2:SUBLANES - 1],
                   jnp.where(row == 1, zh[SUBLANES - 1:SUBLANES], pltpu.roll(z, 2, 0)))
    cw = cw_ref[...]
    y = cw[0:1] * z2 + cw[1:2] * z1 + cw[2:3] * z
    oc = (bg * y).astype(BF16)
    return (h_ref[...]
            + jnp.dot(oa_ref[...], wa_ref[...], preferred_element_type=F32)
            + jnp.dot(ob_ref[...], wb_ref[...], preferred_element_type=F32)
            + jnp.dot(oc, wc_ref[...], preferred_element_type=F32))


def _outproj_ffn_kernel(h_ref, oa_ref, ob_ref, pc_ref, pch_ref, cw_ref, wa_ref, wb_ref, wc_ref,
                        g_ref, wg_ref, wu_ref, wd_ref, o_ref, *, tiles_per_batch):
    h1 = _mixer_out(h_ref, oa_ref, ob_ref, pc_ref, pch_ref, cw_ref, wa_ref, wb_ref, wc_ref,
                    tiles_per_batch)
    o_ref[...] = h1 + _swiglu_block(_rms(h1, g_ref[...]).astype(BF16), wg_ref, wu_ref, wd_ref)


def _outproj_kernel(h_ref, oa_ref, ob_ref, pc_ref, pch_ref, cw_ref, wa_ref, wb_ref, wc_ref,
                    o_ref, *, tiles_per_batch):
    o_ref[...] = _mixer_out(h_ref, oa_ref, ob_ref, pc_ref, pch_ref, cw_ref, wa_ref, wb_ref, wc_ref,
                            tiles_per_batch)


def _outproj(h, oa, ob, pc, cw, wa, wb, wc, *, tm, lp, ffn=None):
    r, d = h.shape
    row = lambda i: (i, 0)
    const = lambda i: (0, 0)
    halo = lambda i: (jnp.maximum(i * (tm // SUBLANES) - 1, 0), 0)
    tail = list(ffn) if ffn is not None else []
    in_specs = [
        pl.BlockSpec((tm, d), row),
        pl.BlockSpec((tm, RWKV_PAD), row),
        pl.BlockSpec((tm, DIFF_WIDTH), row),
        pl.BlockSpec((tm, PC_COLS), row),
        pl.BlockSpec((SUBLANES, PC_COLS), halo),
        pl.BlockSpec((SUBLANES, CONV_WIDTH), const),
        pl.BlockSpec(wa.shape, const),
        pl.BlockSpec(wb.shape, const),
        pl.BlockSpec(wc.shape, const),
    ] + [pl.BlockSpec(t.shape, const) for t in tail]
    body, name = (_outproj_kernel, "outproj") if ffn is None else (_outproj_ffn_kernel, "outproj_ffn")
    return pl.pallas_call(
        functools.partial(body, tiles_per_batch=lp // tm),
        grid=(r // tm,),
        in_specs=in_specs,
        out_specs=pl.BlockSpec((tm, d), row),
        out_shape=jax.ShapeDtypeStruct((r, d), F32),
        compiler_params=_params(("arbitrary",)),
        name=name,
    )(h, oa, ob, pc, pc, cw, wa, wb, wc, *tail)


def _swiglu_block(xb, wg_ref, wu_ref, wd_ref, side_work=None):
    d_ff = wg_ref.shape[1]
    acc = jnp.zeros((xb.shape[0], wd_ref.shape[1]), F32)
    for c in range(d_ff // MXU_DIM):
        cs = slice(c * MXU_DIM, (c + 1) * MXU_DIM)
        gt = jnp.dot(xb, wg_ref[:, cs], preferred_element_type=F32)
        up = jnp.dot(xb, wu_ref[:, cs], preferred_element_type=F32)
        act = (jax.nn.silu(gt) * up).astype(BF16)
        acc = acc + jnp.dot(act, wd_ref[cs, :], preferred_element_type=F32)
        if side_work is not None:
            side_work(c)
    return acc


def _rms(x, g):
    ms = jnp.mean(x * x, axis=-1, keepdims=True)
    return x * lax.rsqrt(ms + NORM_EPS) * g


def _router_kernel(h_ref, g_ref, rw_ref, hn_ref, route_ref):
    tm = h_ref.shape[0]
    xn = _rms(h_ref[...], g_ref[...])
    hn_ref[...] = xn
    logits = _dot_f32(xn, rw_ref[...])
    lane = lax.broadcasted_iota(jnp.int32, (tm, LANES), 1)
    logits = jnp.where(lane < N_EXPERTS, logits, -jnp.inf)
    m1 = jnp.max(logits, axis=-1, keepdims=True)
    i1 = jnp.min(jnp.where(logits == m1, lane, LANES), axis=-1, keepdims=True)
    rest = jnp.where(lane == i1, -jnp.inf, logits)
    m2 = jnp.max(rest, axis=-1, keepdims=True)
    i2 = jnp.min(jnp.where(rest == m2, lane, LANES), axis=-1, keepdims=True)
    e2 = jnp.exp(m2 - m1)
    den = 1.0 + e2
    route_ref[...] = jnp.where(lane == 0, i1.astype(F32),
                     jnp.where(lane == 1, i2.astype(F32),
                     jnp.where(lane == 2, 1.0 / den,
                     jnp.where(lane == 3, e2 / den, 0.0))))


def _router(h, g, rw, *, tm):
    r, d = h.shape
    row = lambda i: (i, 0)
    const = lambda i: (0, 0)
    return pl.pallas_call(
        _router_kernel,
        grid=(r // tm,),
        in_specs=[pl.BlockSpec((tm, d), row), pl.BlockSpec((1, d), const),
                  pl.BlockSpec((d, LANES), const)],
        out_specs=[pl.BlockSpec((tm, d), row), pl.BlockSpec((tm, LANES), row)],
        out_shape=[jax.ShapeDtypeStruct((r, d), F32), jax.ShapeDtypeStruct((r, LANES), F32)],
        compiler_params=_params(("arbitrary",)),
        name="router",
    )(h, g, rw)


def _row_copy(src, dst, sem, src_row, dst_row):
    return pltpu.make_async_copy(src.at[pl.ds(src_row, 1), :], dst.at[pl.ds(dst_row, 1), :], sem)


DMA_UNROLL = 8
DMA_ISSUE_SLICES = 8
MOE_BUFFERS = 3


def _moe_kernel(bexp_ref, rsrc_ref, rdst_ref, hn_hbm, wg_ref, wu_ref, wd_ref, y_hbm,
                xbuf, ybuf, gsem, ssem):
    del bexp_ref
    nbuf = xbuf.shape[0]
    blk = xbuf.shape[1]
    i = pl.program_id(0)
    nb = pl.num_programs(0)
    slot = i % nbuf
    slot_prev = (i + nbuf - 1) % nbuf
    slot_next2 = (i + 2) % nbuf
    dump0 = y_hbm.shape[0] - nbuf * blk

    def for_rows(fn):
        def body(j, c):
            for u in range(DMA_UNROLL):
                fn(j * DMA_UNROLL + u)
            return c
        lax.fori_loop(0, blk // DMA_UNROLL, body, 0)

    def gather(block, s, j):
        return _row_copy(hn_hbm, xbuf.at[s], gsem.at[s], rsrc_ref[block * blk + j], j)

    def scatter(dst_row, s, j):
        return _row_copy(ybuf.at[s], y_hbm, ssem.at[s], j, dst_row)

    @pl.when(i == 0)
    def _():
        for_rows(lambda j: gather(0, 0, j).start())
        for_rows(lambda j: gather(1, 1, j).start())
        ybuf[nbuf - 1] = jnp.zeros(ybuf.shape[1:], ybuf.dtype)
        for k in range(nbuf - 1):
            zero_fill = pltpu.make_async_copy(
                ybuf.at[nbuf - 1], y_hbm.at[pl.ds(dump0 + k * blk, blk), :], ssem.at[k])
            zero_fill.start()
            zero_fill.wait()

    for_rows(lambda j: gather(0, slot, j).wait())

    @pl.when(i >= 2)
    def _():
        for_rows(lambda j: scatter(0, slot, j).wait())

    ahead = jnp.minimum(i + 2, nb - 1)
    prev = jnp.maximum(i - 1, 0)
    per = -(-blk // DMA_ISSUE_SLICES)

    def side_work(c):
        for j in range(c * per, min((c + 1) * per, blk)):
            gather(ahead, slot_next2, j).start(priority=j % 2)
            dst = jnp.where(i == 0, dump0 + (nbuf - 1) * blk + j, rdst_ref[prev * blk + j])
            scatter(dst, slot_prev, j).start(priority=(j + 1) % 2)

    ybuf[slot] = _swiglu_block(xbuf[slot].astype(BF16), wg_ref, wu_ref, wd_ref, side_work)

    @pl.when(i == nb - 1)
    def _():
        for_rows(lambda j: gather(0, (i + 1) % nbuf, j).wait())
        for_rows(lambda j: gather(0, slot_next2, j).wait())
        for_rows(lambda j: scatter(0, (i + 1) % nbuf, j).wait())
        for_rows(lambda j: scatter(0, slot_prev, j).wait())
        for_rows(lambda j: scatter(rdst_ref[i * blk + j], slot, j).start())
        for_rows(lambda j: scatter(0, slot, j).wait())


def _moe_experts(block_expert, row_src, row_dst, hn, wg, wu, wd):
    nb = block_expert.shape[0]
    r, d = hn.shape
    d_ff = wg.shape[2]
    wmap = lambda i, be, rs, rd: (be[i], 0, 0)
    grid_spec = pltpu.PrefetchScalarGridSpec(
        num_scalar_prefetch=3,
        grid=(nb,),
        in_specs=[
            pl.BlockSpec(memory_space=pl.ANY),
            pl.BlockSpec((None, d, d_ff), wmap),
            pl.BlockSpec((None, d, d_ff), wmap),
            pl.BlockSpec((None, d_ff, d), wmap),
        ],
        out_specs=pl.BlockSpec(memory_space=pl.ANY),
        scratch_shapes=[pltpu.VMEM((MOE_BUFFERS, MOE_BLOCK, d), F32),
                        pltpu.VMEM((MOE_BUFFERS, MOE_BLOCK, d), F32),
                        pltpu.SemaphoreType.DMA((MOE_BUFFERS,)),
                        pltpu.SemaphoreType.DMA((MOE_BUFFERS,))],
    )
    return pl.pallas_call(
        _moe_kernel,
        grid_spec=grid_spec,
        out_shape=jax.ShapeDtypeStruct((TOP_K * r + MOE_BUFFERS * MOE_BLOCK, d), F32),
        compiler_params=_params(("arbitrary",)),
        name="moe_experts",
    )(block_expert, row_src, row_dst, hn, wg, wu, wd)


def _combine_kernel(h_ref, route_ref, y0_ref, y1_ref, o_ref):
    route = route_ref[...]
    o_ref[...] = h_ref[...] + route[:, 2:3] * y0_ref[...] + route[:, 3:4] * y1_ref[...]


def _moe_combine(h, route, y, *, tm):
    r, d = h.shape
    nt = r // tm
    row = lambda i: (i, 0)
    return pl.pallas_call(
        _combine_kernel,
        grid=(nt,),
        in_specs=[
            pl.BlockSpec((tm, d), row),
            pl.BlockSpec((tm, LANES), row),
            pl.BlockSpec((tm, d), row),
            pl.BlockSpec((tm, d), lambda i: (i + nt, 0)),
        ],
        out_specs=pl.BlockSpec((tm, d), row),
        out_shape=jax.ShapeDtypeStruct((r, d), F32),
        compiler_params=_params(("arbitrary",)),
        name="moe_combine",
    )(h, route, y, y)


def _slot_owner_kernel(dest_ref, owner_ref):
    n_pairs = dest_ref.shape[0]
    n_slots = owner_ref.shape[0]

    def clear(i, c):
        for u in range(DMA_UNROLL):
            owner_ref[i * DMA_UNROLL + u] = -1
        return c

    lax.fori_loop(0, n_slots // DMA_UNROLL, clear, 0)

    def place(i, c):
        for u in range(DMA_UNROLL):
            p = i * DMA_UNROLL + u
            owner_ref[dest_ref[p]] = p
        return c

    lax.fori_loop(0, n_pairs // DMA_UNROLL, place, 0)


def _slot_owner(dest, n_slots):
    assert dest.shape[0] % DMA_UNROLL == 0 and n_slots % DMA_UNROLL == 0
    smem = pl.BlockSpec(memory_space=pltpu.SMEM)
    return pl.pallas_call(
        _slot_owner_kernel,
        in_specs=[smem],
        out_specs=smem,
        out_shape=jax.ShapeDtypeStruct((n_slots,), jnp.int32),
        name="slot_owner",
    )(dest)


def _routing_tables(route):
    r = route.shape[0]
    tk = r * TOP_K
    flat_e = route[:, :TOP_K].reshape(tk).astype(jnp.int32)
    onehot = (flat_e[:, None] == jnp.arange(N_EXPERTS, dtype=jnp.int32)[None, :]).astype(jnp.int32)
    csum = jnp.cumsum(onehot, axis=0)
    rank = jnp.sum(csum * onehot, axis=1) - 1
    counts = csum[-1]
    padded = (counts + MOE_BLOCK - 1) // MOE_BLOCK * MOE_BLOCK
    pend = jnp.cumsum(padded)
    pstart = pend - padded
    dest = pstart[flat_e] + rank
    nb = -(-tk // MOE_BLOCK) + N_EXPERTS
    n_slots = nb * MOE_BLOCK
    pair = _slot_owner(dest, n_slots)
    used = pair >= 0
    row_src = jnp.where(used, pair // TOP_K, 0)
    slot_id = jnp.arange(n_slots, dtype=jnp.int32)
    dump_row = tk + (slot_id // MOE_BLOCK) % MOE_BUFFERS * MOE_BLOCK + slot_id % MOE_BLOCK
    row_dst = jnp.where(used, (pair % TOP_K) * r + pair // TOP_K, dump_row)
    block_start = jnp.arange(nb, dtype=jnp.int32) * MOE_BLOCK
    block_expert = jnp.minimum(jnp.searchsorted(pend, block_start, side='right'),
                               N_EXPERTS - 1).astype(jnp.int32)
    return block_expert, row_src, row_dst


def _pad_heads(t, axis):
    axis = axis % t.ndim
    shp = t.shape
    t = t.reshape(shp[:axis] + (RWKV_HEADS, RWKV_HEAD_DIM) + shp[axis + 1:])
    pad = [(0, 0)] * t.ndim
    pad[axis + 1] = (0, HEAD_PAD - RWKV_HEAD_DIM)
    t = jnp.pad(t, pad)
    return t.reshape(shp[:axis] + (RWKV_PAD,) + shp[axis + 1:])


def _cast_kernel(x_ref, o_ref):
    o_ref[...] = x_ref[...].astype(o_ref.dtype)


def _to_bf16(w, layer):
    _, e, a, b = w.shape
    return pl.pallas_call(
        _cast_kernel,
        grid=(e, 2),
        in_specs=[pl.BlockSpec((None, None, a // 2, b), lambda i, j: (layer, i, j, 0))],
        out_specs=pl.BlockSpec((None, a // 2, b), lambda i, j: (i, j, 0)),
        out_shape=jax.ShapeDtypeStruct((e, a, b), BF16),
        compiler_params=_params(("arbitrary", "arbitrary")),
        name="to_bf16",
    )(w)


def _rope_tables(lp):
    half = ROPE_DIM // 2
    inv_freq = ROPE_THETA ** (-jnp.arange(0, ROPE_DIM, 2, dtype=F32) / ROPE_DIM)
    ang = jnp.arange(lp, dtype=F32)[:, None] * inv_freq[None, :]
    cos, sin = jnp.cos(ang), jnp.sin(ang)
    ones = jnp.ones((lp, DIFF_QK_DIM - ROPE_DIM), F32)
    zeros = jnp.zeros((lp, DIFF_QK_DIM - ROPE_DIM), F32)
    zh = jnp.zeros((lp, half), F32)
    rc = jnp.concatenate([cos, cos, ones], axis=1)
    rs1 = jnp.concatenate([-sin, zh, zeros], axis=1)
    rs2 = jnp.concatenate([zh, sin, zeros], axis=1)
    tile2 = lambda a: jnp.concatenate([a, a], axis=1)
    return tile2(rc), tile2(rs1), tile2(rs2)


def kernel(x, meta_tokens, mix_norm_g, w_in, tm_mu, tm_w0, tm_w_decay_up, tm_a0, tm_w_a_up, tm_w_g_up, tm_k_k, tm_k_a, tm_r_k, tm_gn_g, tm_gn_b, da_q_norm_g, da_k_norm_g, da_lambda_q1, da_lambda_k1, da_lambda_q2, da_lambda_k2, da_subln_g, sc_conv_w, w_out, ffn_norm_g, ffn_w_gate, ffn_w_up, ffn_w_down, router_w, moe_w_gate, moe_w_up, moe_w_down):
    b, seq, d = x.shape
    depth = w_in.shape[0]
    l = N_META + seq
    lp = -(-l // LANES) * LANES
    tm = _row_tile(lp)
    r = b * lp
    s1 = RWKV_WIDTH

    meta = jnp.broadcast_to(meta_tokens.astype(x.dtype)[None], (b, N_META, d))
    h = jnp.concatenate([meta, x, jnp.zeros((b, lp - l, d), x.dtype)], axis=1).reshape(r, d)
    rc, rs1, rs2 = _rope_tables(lp)

    for i in range(depth):
        lam_init = 0.8 - 0.6 * math.exp(-0.3 * i)
        wi = w_in[i]
        w_cat = jnp.concatenate(
            [_pad_heads(wi[:, 0:s1], 1), _pad_heads(wi[:, s1:2 * s1], 1),
             _pad_heads(wi[:, 2 * s1:3 * s1], 1), wi[:, 3 * s1:]], axis=1).astype(BF16)
        qg = jnp.tile(da_q_norm_g[i], 2)[None]
        kg = jnp.tile(da_k_norm_g[i], 2)[None]
        pa, qs, kh, vt, pc = _inproj(h, mix_norm_g[i][None], w_cat, rc, rs1, rs2, qg, kg,
                                     tm=tm, lp=lp)

        mu = tm_mu[i]
        mu_p = jnp.concatenate([_pad_heads(mu[0:s1], 0), _pad_heads(mu[s1:2 * s1], 0),
                                _pad_heads(mu[2 * s1:3 * s1], 0), mu[3 * s1:]])[None]
        vecs = jnp.stack([_pad_heads(tm_w0[i], 0), _pad_heads(tm_a0[i], 0),
                          _pad_heads(tm_k_k[i], 0), _pad_heads(tm_k_a[i], 0),
                          _pad_heads(tm_r_k[i].reshape(s1), 0), _pad_heads(tm_gn_g[i], 0),
                          _pad_heads(tm_gn_b[i], 0), jnp.zeros((RWKV_PAD,), F32)])
        zl = lambda n: jnp.zeros((n, RWKV_PAD), F32)
        wd_p = jnp.concatenate([_pad_heads(tm_w_decay_up[i], 1), zl(AAA_LORA + GATE_LORA)], axis=0)
        wa_p = jnp.concatenate([zl(DECAY_LORA), _pad_heads(tm_w_a_up[i], 1), zl(GATE_LORA)], axis=0)
        wg_p = jnp.concatenate([zl(DECAY_LORA + AAA_LORA), _pad_heads(tm_w_g_up[i], 1)], axis=0)
        oa = _rwkv(pa.reshape(b, lp, PA_COLS), mu_p, vecs, wd_p, wa_p, wg_p, tb=tm)

        lamv = jnp.zeros((SUBLANES, LANES), F32).at[0:4, :DIFF_QK_DIM].set(
            jnp.stack([da_lambda_q1[i], da_lambda_k1[i], da_lambda_q2[i], da_lambda_k2[i]]))
        ob = _attention(qs.reshape(2, b, lp, DIFF_WIDTH), kh.reshape(b, lp, DIFF_WIDTH),
                        vt.reshape(b, lp // tm, DIFF_HEADS, VT_ROWS, tm), lamv, da_subln_g[i][None],
                        tq=tm, lam_init=lam_init)

        wo = w_out[i]
        cw = jnp.zeros((SUBLANES, CONV_WIDTH), F32).at[:CONV_K].set(sc_conv_w[i])
        mix = (h, oa.reshape(r, RWKV_PAD), ob.reshape(r, DIFF_WIDTH), pc, cw,
               _pad_heads(wo[:s1], 0).astype(BF16), wo[s1:s1 + DIFF_WIDTH].astype(BF16),
               wo[s1 + DIFF_WIDTH:].astype(BF16))
        j = i // 2
        if i % 2 == 0:
            h = _outproj(*mix, tm=tm, lp=lp,
                         ffn=(ffn_norm_g[i][None], ffn_w_gate[j].astype(BF16),
                              ffn_w_up[j].astype(BF16), ffn_w_down[j].astype(BF16)))
        else:
            h = _outproj(*mix, tm=tm, lp=lp)
            rw = jnp.zeros((d, LANES), F32).at[:, :N_EXPERTS].set(router_w[j])
            hn, route = _router(h, ffn_norm_g[i][None], rw, tm=tm)
            block_expert, row_src, row_dst = _routing_tables(route)
            y = _moe_experts(block_expert, row_src, row_dst, hn, _to_bf16(moe_w_gate, j),
                             _to_bf16(moe_w_up, j), _to_bf16(moe_w_down, j))
            h = _moe_combine(h, route, y, tm=tm)

    return h.reshape(b, lp, d)[:, N_META:l]
```

```python
import functools
import math

import jax
import jax.numpy as jnp
from jax import lax
from jax.experimental import pallas as pl
from jax.experimental.pallas import tpu as pltpu

N_META = 16
RWKV_HEAD_DIM = 64
RWKV_HEADS = 4
RWKV_WIDTH = RWKV_HEADS * RWKV_HEAD_DIM
DECAY_LORA = 32
AAA_LORA = 32
GATE_LORA = 64
LORA_COLS = DECAY_LORA + AAA_LORA + GATE_LORA
DIFF_QK_DIM = 64
DIFF_V_DIM = 128
DIFF_HEADS = 4
DIFF_WIDTH = DIFF_HEADS * DIFF_V_DIM
CONV_WIDTH = 256
CONV_K = 3
ROPE_THETA = 500000.0
ROPE_DIM = DIFF_QK_DIM // 4
N_EXPERTS = 8
TOP_K = 2
MOE_BLOCK = 256
NORM_EPS = 1e-6
RWKV_GN_EPS = 64e-5
SUBLN_EPS = 1e-5
NEG_INF = -1e30

LANES = 128
SUBLANES = 8
MXU_DIM = 256
VMEM_LIMIT_BYTES = 56 * 1024 * 1024

HEAD_PAD = LANES
RWKV_PAD = RWKV_HEADS * HEAD_PAD
PA_COLS = 3 * RWKV_PAD + LORA_COLS
PB_COLS = 3 * DIFF_WIDTH
PC_COLS = 3 * CONV_WIDTH
VT_ROWS = DIFF_V_DIM + 16
LOG2_E = math.log2(math.e)
CHUNK = 64

F32 = jnp.float32
BF16 = jnp.bfloat16
HIGHEST = lax.Precision.HIGHEST


def _dot(a, b):
    return jnp.dot(a.astype(BF16), b.astype(BF16), preferred_element_type=F32)


def _dot_f32(a, b):
    return jnp.dot(a, b, preferred_element_type=F32, precision=HIGHEST)


def _row_tile(lp):
    for t in (384, 256, 128):
        if lp % t == 0:
            return t
    raise ValueError(f"padded length {lp} is not a multiple of {LANES}")


def _params(sem):
    return pltpu.CompilerParams(dimension_semantics=sem, vmem_limit_bytes=VMEM_LIMIT_BYTES)


def _inproj_kernel(h_ref, g_ref, w_ref, rc_ref, rs1_ref, rs2_ref, qg_ref, kg_ref,
                   pa_ref, qs_ref, kh_ref, vt_ref, pc_ref):
    tm = h_ref.shape[0]
    x = h_ref[...]
    ms = jnp.mean(x * x, axis=-1, keepdims=True)
    xn = (x * lax.rsqrt(ms + NORM_EPS) * g_ref[...]).astype(BF16)
    pb = jnp.dot(xn, w_ref[:, PA_COLS:PA_COLS + PB_COLS], preferred_element_type=F32)

    lane = lax.broadcasted_iota(jnp.int32, (tm, LANES), 1)
    lo = lane < DIFF_QK_DIM
    rc, rs1, rs2 = rc_ref[...], rs1_ref[...], rs2_ref[...]

    def norm_rope(t, g):
        t2 = t * t
        ss_lo = jnp.sum(jnp.where(lo, t2, 0.0), axis=-1, keepdims=True)
        ss_hi = jnp.sum(jnp.where(lo, 0.0, t2), axis=-1, keepdims=True)
        inv = jnp.where(lo, lax.rsqrt(ss_lo / DIFF_QK_DIM + NORM_EPS),
                        lax.rsqrt(ss_hi / DIFF_QK_DIM + NORM_EPS))
        tn = t * inv * g
        half = ROPE_DIM // 2
        return tn * rc + pltpu.roll(tn, LANES - half, 1) * rs1 + pltpu.roll(tn, half, 1) * rs2

    for hd in range(DIFF_HEADS):
        if hd == 0:
            pa_ref[...] = jnp.dot(xn, w_ref[:, :PA_COLS], preferred_element_type=F32)
        if hd == DIFF_HEADS // 2:
            pc_ref[...] = jnp.dot(xn, w_ref[:, PA_COLS + PB_COLS:], preferred_element_type=F32)
        cs = slice(hd * LANES, (hd + 1) * LANES)
        q = norm_rope(pb[:, cs], qg_ref[...]) * (DIFF_QK_DIM ** -0.5 * LOG2_E)
        qs_ref[0, :, cs] = jnp.where(lo, q, 0.0).astype(BF16)
        qs_ref[1, :, cs] = jnp.where(lo, 0.0, q).astype(BF16)
        kcs = slice(DIFF_WIDTH + hd * LANES, DIFF_WIDTH + (hd + 1) * LANES)
        kh_ref[:, cs] = norm_rope(pb[:, kcs], kg_ref[...]).astype(BF16)
        vcs = slice(2 * DIFF_WIDTH + hd * LANES, 2 * DIFF_WIDTH + (hd + 1) * LANES)
        vt_ref[0, hd, :DIFF_V_DIM] = pb[:, vcs].T.astype(BF16)
        vt_ref[0, hd, DIFF_V_DIM:] = jnp.ones((VT_ROWS - DIFF_V_DIM, tm), BF16)


def _inproj(h, g, w, rc, rs1, rs2, qg, kg, *, tm, lp):
    r, d = h.shape
    nt_b = lp // tm
    row = lambda i: (i, 0)
    const = lambda i: (0, 0)
    rope = lambda i: (i % nt_b, 0)
    return pl.pallas_call(
        _inproj_kernel,
        grid=(r // tm,),
        in_specs=[
            pl.BlockSpec((tm, d), row),
            pl.BlockSpec((1, d), const),
            pl.BlockSpec(w.shape, const),
            pl.BlockSpec((tm, LANES), rope),
            pl.BlockSpec((tm, LANES), rope),
            pl.BlockSpec((tm, LANES), rope),
            pl.BlockSpec((1, LANES), const),
            pl.BlockSpec((1, LANES), const),
        ],
        out_specs=[
            pl.BlockSpec((tm, PA_COLS), row),
            pl.BlockSpec((2, tm, DIFF_WIDTH), lambda i: (0, i, 0)),
            pl.BlockSpec((tm, DIFF_WIDTH), row),
            pl.BlockSpec((1, DIFF_HEADS, VT_ROWS, tm), lambda i: (i, 0, 0, 0)),
            pl.BlockSpec((tm, PC_COLS), row),
        ],
        out_shape=[
            jax.ShapeDtypeStruct((r, PA_COLS), F32),
            jax.ShapeDtypeStruct((2, r, DIFF_WIDTH), BF16),
            jax.ShapeDtypeStruct((r, DIFF_WIDTH), BF16),
            jax.ShapeDtypeStruct((r // tm, DIFF_HEADS, VT_ROWS, tm), BF16),
            jax.ShapeDtypeStruct((r, PC_COLS), F32),
        ],
        compiler_params=_params(("arbitrary",)),
        name="inproj",
    )(h, g, w, rc, rs1, rs2, qg, kg)


def _split3(x):
    hi = x.astype(BF16)
    r1 = x - hi.astype(F32)
    mid = r1.astype(BF16)
    lo = (r1 - mid.astype(F32)).astype(BF16)
    return hi, mid, lo


def _dot_hi(a, b):
    ah = a.astype(BF16)
    al = (a - ah.astype(F32)).astype(BF16)
    bh = b.astype(BF16)
    bl = (b - bh.astype(F32)).astype(BF16)
    d = lambda p, q: jnp.dot(p, q, preferred_element_type=F32)
    return d(ah, bh) + d(ah, bl) + d(al, bh)


def _bmm(a, b):
    return jnp.einsum('cik,ckj->cij', a.astype(BF16), b.astype(BF16), preferred_element_type=F32)


def _bmm_nt(a, b):
    return jnp.einsum('cik,cjk->cij', a.astype(BF16), b.astype(BF16), preferred_element_type=F32)


def _rwkv_kernel(pa_ref, halo_ref, mu_ref, vec_ref, wd_ref, wa_ref, wg_ref, oa_ref, hstate_s):
    tb = pa_ref.shape[0]
    nc = tb // CHUNK
    i = pl.program_id(1)

    @pl.when(i == 0)
    def _():
        hstate_s[...] = jnp.zeros_like(hstate_s)

    x = pa_ref[...]
    prev_row = jnp.where(i == 0, 0.0, halo_ref[SUBLANES - 1:SUBLANES, :])
    row = lax.broadcasted_iota(jnp.int32, (tb, 1), 0)
    prev = jnp.where(row == 0, prev_row, pltpu.roll(x, 1, 0))
    hx = x + (prev - x) * mu_ref[...]
    r = hx[:, 0:RWKV_PAD]
    k = hx[:, RWKV_PAD:2 * RWKV_PAD]
    v = hx[:, 2 * RWKV_PAD:3 * RWKV_PAD]
    lora = hx[:, 3 * RWKV_PAD:]
    w0, a0, k_k, k_a = vec_ref[0:1, :], vec_ref[1:2, :], vec_ref[2:3, :], vec_ref[3:4, :]
    r_k, gn_g, gn_b = vec_ref[4:5, :], vec_ref[5:6, :], vec_ref[6:7, :]

    w = -jax.nn.softplus(-(w0 + _dot_hi(jnp.tanh(lora), wd_ref[...]))) - 0.5
    wl = -jnp.exp(w)
    alr = jax.nn.sigmoid(a0 + _dot(lora, wa_ref[...]))
    gate = _dot(jax.nn.sigmoid(lora), wg_ref[...])
    kk = k * k_k
    k2 = k * (1.0 + (alr - 1.0) * k_a)
    rkr = r * k2 * r_k

    ri = lax.broadcasted_iota(jnp.int32, (CHUNK, CHUNK), 0)
    ci = lax.broadcasted_iota(jnp.int32, (CHUNK, CHUNK), 1)
    lane_c = lax.broadcasted_iota(jnp.int32, (CHUNK, HEAD_PAD), 1)
    row_c = lax.broadcasted_iota(jnp.int32, (CHUNK, HEAD_PAD), 0)
    col_c = jnp.where(lane_c >= CHUNK, lane_c - CHUNK, lane_c)
    low_incl2 = (col_c <= row_c)[None]
    low_strict2 = (col_c < row_c)[None]
    tril_b = jnp.where(ci <= ri, 1.0, 0.0).astype(BF16)
    eye_k = (lax.broadcasted_iota(jnp.int32, (HEAD_PAD, HEAD_PAD), 0)
             == lax.broadcasted_iota(jnp.int32, (HEAD_PAD, HEAD_PAD), 1))[None]
    n_sq = int(math.log2(CHUNK))

    parts = _split3(wl)
    cums = []
    for c in range(nc):
        rows = slice(c * CHUNK, (c + 1) * CHUNK)
        cums.append(sum(jnp.dot(tril_b, p[rows], preferred_element_type=F32) for p in parts))
    cum_all = jnp.concatenate(cums, axis=0)

    lane = lax.broadcasted_iota(jnp.int32, (tb, HEAD_PAD), 1)
    real = lane < RWKV_HEAD_DIM
    to3 = lambda t: t.reshape(nc, CHUNK, HEAD_PAD)
    heads = range(RWKV_HEADS)
    hsl = [slice(hd * HEAD_PAD, (hd + 1) * HEAD_PAD) for hd in heads]
    vc = [to3(v[:, cs]) for cs in hsl]
    zero_v = jnp.zeros_like(vc[0])
    kkn, bonus, ah, rh, last, g, bk_last = [], [], [], [], [], [], []
    for hd, cs in enumerate(hsl):
        kk_h = kk[:, cs]
        nrm = jnp.sqrt(jnp.sum(kk_h * kk_h, axis=-1, keepdims=True))
        kkn.append(kk_h / jnp.maximum(nrm, 1e-12))
        bonus.append(jnp.sum(rkr[:, cs], axis=-1, keepdims=True) * v[:, cs])
        wl3, cum = to3(wl[:, cs]), to3(cum_all[:, cs])
        rr, kc = to3(r[:, cs]), to3(k2[:, cs])
        ac, bc = to3(-kkn[hd]), to3(kkn[hd] * alr[:, cs])
        last.append(cum[:, CHUNK - 1:CHUNK, :])
        e_neg = jnp.exp(-cum)
        e_last = jnp.exp(last[hd] - cum)
        ah.append(ac * jnp.exp(cum - wl3))
        rh.append(rr * jnp.exp(cum))
        bk_last.append(jnp.concatenate([bc * e_last, kc * e_last], axis=1))
        g.append(_bmm_nt(jnp.concatenate([ah[hd], rh[hd]], axis=1),
                         jnp.concatenate([bc * e_neg, kc * e_neg], axis=1)))
    a_abk = [jnp.where(low_strict2, g[hd][:, :CHUNK], 0.0) for hd in heads]
    r_bk = [jnp.where(low_incl2, g[hd][:, CHUNK:], 0.0) for hd in heads]
    xx = [jnp.concatenate([_bmm(a_abk[hd], jnp.concatenate([zero_v, vc[hd]], axis=1)), ah[hd]],
                          axis=2) for hd in heads]
    ppad = [jnp.where(lane_c < CHUNK, a_abk[hd], 0.0) for hd in heads]
    for j in range(n_sq):
        for hd in heads:
            pj = ppad[hd][:, :, :CHUNK]
            if j + 1 < n_sq:
                res = _bmm(pj, jnp.concatenate([xx[hd], ppad[hd]], axis=2))
                xx[hd] = xx[hd] + res[:, :, :2 * HEAD_PAD]
                ppad[hd] = res[:, :, 2 * HEAD_PAD:]
            else:
                xx[hd] = xx[hd] + _bmm(pj, xx[hd])
    qm, y0, n0 = [], [], []
    for hd in heads:
        lhs = jnp.concatenate([r_bk[hd], jnp.swapaxes(bk_last[hd], 1, 2)], axis=1)
        rhs = jnp.concatenate([xx[hd], jnp.concatenate([vc[hd], zero_v], axis=2)], axis=1)
        out = _bmm(lhs, rhs)
        y0.append(out[:, :CHUNK, :HEAD_PAD])
        n0.append(out[:, CHUNK:, :HEAD_PAD])
        q = rh[hd] + out[:, :CHUNK, HEAD_PAD:]
        m = jnp.where(eye_k, jnp.exp(last[hd]), 0.0) + out[:, CHUNK:, HEAD_PAD:]
        qm.append(jnp.concatenate([q, m], axis=1).astype(BF16))

    hs = [hstate_s[hd] for hd in heads]
    ys = [[] for _ in heads]
    for c in range(nc):
        for hd in heads:
            res = jnp.dot(qm[hd][c], hs[hd].astype(BF16), preferred_element_type=F32)
            ys[hd].append(res[:CHUNK] + y0[hd][c])
            hs[hd] = res[CHUNK:] + n0[hd][c]

    for hd, cs in enumerate(hsl):
        hstate_s[hd] = hs[hd]
        y = jnp.concatenate(ys[hd], axis=0)
        mean = jnp.sum(y, axis=-1, keepdims=True) / RWKV_HEAD_DIM
        dlt = jnp.where(real, y - mean, 0.0)
        var = jnp.sum(dlt * dlt, axis=-1, keepdims=True) / RWKV_HEAD_DIM
        yn = dlt * lax.rsqrt(var + RWKV_GN_EPS) * gn_g[:, cs] + gn_b[:, cs]
        oa_ref[:, cs] = ((yn + bonus[hd]) * gate[:, cs]).astype(BF16)


def _rwkv(pa, mu, vecs, wd, wa, wg, *, tb):
    b, lp, _ = pa.shape
    tile = lambda bi, i: (bi, i, 0)
    const = lambda bi, i: (0, 0)
    halo = lambda bi, i: (bi, jnp.maximum(i * (tb // SUBLANES) - 1, 0), 0)
    return pl.pallas_call(
        _rwkv_kernel,
        grid=(b, lp // tb),
        in_specs=[
            pl.BlockSpec((None, tb, PA_COLS), tile),
            pl.BlockSpec((None, SUBLANES, PA_COLS), halo),
            pl.BlockSpec((1, PA_COLS), const),
            pl.BlockSpec((SUBLANES, RWKV_PAD), const),
            pl.BlockSpec((LORA_COLS, RWKV_PAD), const),
            pl.BlockSpec((LORA_COLS, RWKV_PAD), const),
            pl.BlockSpec((LORA_COLS, RWKV_PAD), const),
        ],
        out_specs=pl.BlockSpec((None, tb, RWKV_PAD), tile),
        out_shape=jax.ShapeDtypeStruct((b, lp, RWKV_PAD), BF16),
        scratch_shapes=[pltpu.VMEM((RWKV_HEADS, HEAD_PAD, HEAD_PAD), F32)],
        compiler_params=_params(("arbitrary", "arbitrary")),
        name="rwkv7",
    )(pa, pa, mu, vecs, wd, wa, wg)


ATTN_HEADS_PER_STEP = 4


def _attn_kernel(q_ref, k_ref, vt_ref, lamv_ref, sg_ref, o_ref, acc_s, st_s, p_s, *, lam_init):
    tq = o_ref.shape[0]
    tk = vt_ref.shape[3]
    qi = pl.program_id(2)
    heads = range(ATTN_HEADS_PER_STEP)
    hl = [slice(h * LANES, (h + 1) * LANES) for h in heads]

    def scores(h, ki):
        k = k_ref[pl.ds(pl.multiple_of(ki * tk, tk), tk), hl[h]]
        return lax.dot_general(k, q_ref[:, :, hl[h]].reshape(2 * tq, LANES),
                               (((1,), (1,)), ((), ())), preferred_element_type=F32)

    def values(h, ki, p, alpha):
        acc_s[h] = alpha * acc_s[h] + jnp.dot(vt_ref[ki, h], p, preferred_element_type=F32)

    st = [scores(h, qi) for h in heads]
    for h in heads:
        st_s[h, 0] = scores(h, 0)

    kr = lax.broadcasted_iota(jnp.int32, (tk, 2 * tq), 0)
    qc = lax.broadcasted_iota(jnp.int32, (tk, 2 * tq), 1)
    qc = jnp.where(qc >= tq, qc - tq, qc)
    m0 = []
    for h in heads:
        s_h = jnp.where(kr <= qc, st[h], NEG_INF)
        m0.append(jnp.max(s_h, axis=0, keepdims=True))
        p_s[h, 1] = jnp.exp2(s_h - m0[h]).astype(BF16)
        acc_s[h] = jnp.zeros(acc_s.shape[1:], acc_s.dtype)

    def trip(k, carry, slot):
        out = []
        nxt = jnp.minimum(k + 1, qi - 1)
        prv = jnp.where(k == 0, qi, k - 1)
        for step in range(len(heads) + 1):
            if step < len(heads):
                st_s[step, 1 - slot] = scores(step, nxt)
            if step >= 1:
                h = step - 1
                m_prev, alpha_prev = carry[h]
                s_k = st_s[h, slot]
                m_new = jnp.maximum(m_prev, jnp.max(s_k, axis=0, keepdims=True))
                alpha = jnp.exp2(m_prev - m_new)
                values(h, prv, p_s[h, 1 - slot], alpha_prev)
                p_s[h, slot] = jnp.exp2(s_k - m_new).astype(BF16)
                out.append((m_new, alpha))
        return tuple(out)

    def two_trips(j, c):
        c = trip(2 * j, c, 0)
        return lax.cond(2 * j + 1 < qi, lambda cc: trip(2 * j + 1, cc, 1), lambda cc: cc, c)

    init = tuple((m0[h], jnp.ones_like(m0[h])) for h in heads)
    carry = lax.fori_loop(0, (qi + 1) // 2, two_trips, init)
    last_slot = (qi + 1) % 2
    for h in heads:
        values(h, jnp.where(qi == 0, 0, qi - 1), p_s[h, last_slot], carry[h][1])

    lv = lamv_ref[...]
    lam = (jnp.exp(jnp.sum(lv[0:1] * lv[1:2], axis=-1, keepdims=True))
           - jnp.exp(jnp.sum(lv[2:3] * lv[3:4], axis=-1, keepdims=True)) + lam_init)
    for h in heads:
        acc = acc_s[h]
        ot = acc[:DIFF_V_DIM] / acc[DIFF_V_DIM:DIFF_V_DIM + 1]
        od = (ot[:, :tq] - lam * ot[:, tq:]).T
        ms = jnp.mean(od * od, axis=-1, keepdims=True)
        o_ref[:, hl[h]] = (od * lax.rsqrt(ms + SUBLN_EPS) * sg_ref[...]
                           * (1.0 - lam_init)).astype(BF16)


def _attention(qs, kh, vt, lamv, sg, *, tq, lam_init):
    _, b, lp, _ = qs.shape
    nq = lp // tq
    hp = ATTN_HEADS_PER_STEP
    const = lambda bi, hg, qi: (0, 0)
    return pl.pallas_call(
        functools.partial(_attn_kernel, lam_init=lam_init),
        grid=(b, DIFF_HEADS // hp, nq),
        in_specs=[
            pl.BlockSpec((2, None, tq, hp * LANES), lambda bi, hg, qi: (0, bi, qi, hg)),
            pl.BlockSpec((None, lp, hp * LANES), lambda bi, hg, qi: (bi, 0, hg)),
            pl.BlockSpec((None, nq, hp, VT_ROWS, tq), lambda bi, hg, qi: (bi, 0, hg, 0, 0)),
            pl.BlockSpec((SUBLANES, LANES), const),
            pl.BlockSpec((1, LANES), const),
        ],
        out_specs=pl.BlockSpec((None, tq, hp * LANES), lambda bi, hg, qi: (bi, qi, hg)),
        out_shape=jax.ShapeDtypeStruct((b, lp, DIFF_WIDTH), BF16),
        scratch_shapes=[pltpu.VMEM((hp, VT_ROWS, 2 * tq), F32),
                        pltpu.VMEM((hp, 2, tq, 2 * tq), F32),
                        pltpu.VMEM((hp, 2, tq, 2 * tq), BF16)],
        compiler_params=_params(("arbitrary", "arbitrary", "arbitrary")),
        name="diffattn",
    )(qs, kh, vt, lamv, sg)


def _mixer_out(h_ref, oa_ref, ob_ref, pc_ref, pch_ref, cw_ref, wa_ref, wb_ref, wc_ref,
               tiles_per_batch):
    tm = h_ref.shape[0]
    first = (pl.program_id(0) % tiles_per_batch) == 0
    pc = pc_ref[...]
    bg = pc[:, :CONV_WIDTH]
    z = pc[:, CONV_WIDTH:2 * CONV_WIDTH] * pc[:, 2 * CONV_WIDTH:]
    ph = pch_ref[...]
    zh = jnp.where(first, 0.0, ph[:, CONV_WIDTH:2 * CONV_WIDTH] * ph[:, 2 * CONV_WIDTH:])
    row = lax.broadcasted_iota(jnp.int32, (tm, 1), 0)
    z1 = jnp.where(row == 0, zh[SUBLANES - 1:SUBLANES], pltpu.roll(z, 1, 0))
    z2 = jnp.where(row == 0, zh[SUBLANES - 2:SUBLANES - 1],
                   jnp.where(row == 1, zh[SUBLANES - 1:SUBLANES], pltpu.roll(z, 2, 0)))
    cw = cw_ref[...]
    y = cw[0:1] * z2 + cw[1:2] * z1 + cw[2:3] * z
    oc = (bg * y).astype(BF16)
    return (h_ref[...]
            + jnp.dot(oa_ref[...], wa_ref[...], preferred_element_type=F32)
            + jnp.dot(ob_ref[...], wb_ref[...], preferred_element_type=F32)
            + jnp.dot(oc, wc_ref[...], preferred_element_type=F32))


def _outproj_ffn_kernel(h_ref, oa_ref, ob_ref, pc_ref, pch_ref, cw_ref, wa_ref, wb_ref, wc_ref,
                        g_ref, wg_ref, wu_ref, wd_ref, o_ref, *, tiles_per_batch):
    h1 = _mixer_out(h_ref, oa_ref, ob_ref, pc_ref, pch_ref, cw_ref, wa_ref, wb_ref, wc_ref,
                    tiles_per_batch)
    o_ref[...] = h1 + _swiglu_block(_rms(h1, g_ref[...]).astype(BF16), wg_ref, wu_ref, wd_ref)


def _outproj_kernel(h_ref, oa_ref, ob_ref, pc_ref, pch_ref, cw_ref, wa_ref, wb_ref, wc_ref,
                    o_ref, *, tiles_per_batch):
    o_ref[...] = _mixer_out(h_ref, oa_ref, ob_ref, pc_ref, pch_ref, cw_ref, wa_ref, wb_ref, wc_ref,
                            tiles_per_batch)


def _outproj(h, oa, ob, pc, cw, wa, wb, wc, *, tm, lp, ffn=None):
    r, d = h.shape
    row = lambda i: (i, 0)
    const = lambda i: (0, 0)
    halo = lambda i: (jnp.maximum(i * (tm // SUBLANES) - 1, 0), 0)
    tail = list(ffn) if ffn is not None else []
    in_specs = [
        pl.BlockSpec((tm, d), row),
        pl.BlockSpec((tm, RWKV_PAD), row),
        pl.BlockSpec((tm, DIFF_WIDTH), row),
        pl.BlockSpec((tm, PC_COLS), row),
        pl.BlockSpec((SUBLANES, PC_COLS), halo),
        pl.BlockSpec((SUBLANES, CONV_WIDTH), const),
        pl.BlockSpec(wa.shape, const),
        pl.BlockSpec(wb.shape, const),
        pl.BlockSpec(wc.shape, const),
    ] + [pl.BlockSpec(t.shape, const) for t in tail]
    body, name = (_outproj_kernel, "outproj") if ffn is None else (_outproj_ffn_kernel, "outproj_ffn")
    return pl.pallas_call(
        functools.partial(body, tiles_per_batch=lp // tm),
        grid=(r // tm,),
        in_specs=in_specs,
        out_specs=pl.BlockSpec((tm, d), row),
        out_shape=jax.ShapeDtypeStruct((r, d), F32),
        compiler_params=_params(("arbitrary",)),
        name=name,
    )(h, oa, ob, pc, pc, cw, wa, wb, wc, *tail)


def _swiglu_block(xb, wg_ref, wu_ref, wd_ref, side_work=None):
    d_ff = wg_ref.shape[1]
    acc = jnp.zeros((xb.shape[0], wd_ref.shape[1]), F32)
    for c in range(d_ff // MXU_DIM):
        cs = slice(c * MXU_DIM, (c + 1) * MXU_DIM)
        gt = jnp.dot(xb, wg_ref[:, cs], preferred_element_type=F32)
        up = jnp.dot(xb, wu_ref[:, cs], preferred_element_type=F32)
        act = (jax.nn.silu(gt) * up).astype(BF16)
        acc = acc + jnp.dot(act, wd_ref[cs, :], preferred_element_type=F32)
        if side_work is not None:
            side_work(c)
    return acc


def _rms(x, g):
    ms = jnp.mean(x * x, axis=-1, keepdims=True)
    return x * lax.rsqrt(ms + NORM_EPS) * g


def _router_kernel(h_ref, g_ref, rw_ref, hn_ref, route_ref):
    tm = h_ref.shape[0]
    xn = _rms(h_ref[...], g_ref[...])
    hn_ref[...] = xn
    logits = _dot_f32(xn, rw_ref[...])
    lane = lax.broadcasted_iota(jnp.int32, (tm, LANES), 1)
    logits = jnp.where(lane < N_EXPERTS, logits, -jnp.inf)
    m1 = jnp.max(logits, axis=-1, keepdims=True)
    i1 = jnp.min(jnp.where(logits == m1, lane, LANES), axis=-1, keepdims=True)
    rest = jnp.where(lane == i1, -jnp.inf, logits)
    m2 = jnp.max(rest, axis=-1, keepdims=True)
    i2 = jnp.min(jnp.where(rest == m2, lane, LANES), axis=-1, keepdims=True)
    e2 = jnp.exp(m2 - m1)
    den = 1.0 + e2
    route_ref[...] = jnp.where(lane == 0, i1.astype(F32),
                     jnp.where(lane == 1, i2.astype(F32),
                     jnp.where(lane == 2, 1.0 / den,
                     jnp.where(lane == 3, e2 / den, 0.0))))


def _router(h, g, rw, *, tm):
    r, d = h.shape
    row = lambda i: (i, 0)
    const = lambda i: (0, 0)
    return pl.pallas_call(
        _router_kernel,
        grid=(r // tm,),
        in_specs=[pl.BlockSpec((tm, d), row), pl.BlockSpec((1, d), const),
                  pl.BlockSpec((d, LANES), const)],
        out_specs=[pl.BlockSpec((tm, d), row), pl.BlockSpec((tm, LANES), row)],
        out_shape=[jax.ShapeDtypeStruct((r, d), F32), jax.ShapeDtypeStruct((r, LANES), F32)],
        compiler_params=_params(("arbitrary",)),
        name="router",
    )(h, g, rw)


def _row_copy(src, dst, sem, src_row, dst_row):
    return pltpu.make_async_copy(src.at[pl.ds(src_row, 1), :], dst.at[pl.ds(dst_row, 1), :], sem)


DMA_UNROLL = 8
DMA_ISSUE_SLICES = 8
MOE_BUFFERS = 3


def _moe_kernel(bexp_ref, rsrc_ref, rdst_ref, hn_hbm, wg_ref, wu_ref, wd_ref, y_hbm,
                xbuf, ybuf, gsem, ssem):
    del bexp_ref
    nbuf = xbuf.shape[0]
    blk = xbuf.shape[1]
    i = pl.program_id(0)
    nb = pl.num_programs(0)
    slot = i % nbuf
    slot_prev = (i + nbuf - 1) % nbuf
    slot_next2 = (i + 2) % nbuf
    dump0 = y_hbm.shape[0] - nbuf * blk

    def for_rows(fn):
        def body(j, c):
            for u in range(DMA_UNROLL):
                fn(j * DMA_UNROLL + u)
            return c
        lax.fori_loop(0, blk // DMA_UNROLL, body, 0)

    def gather(block, s, j):
        return _row_copy(hn_hbm, xbuf.at[s], gsem.at[s], rsrc_ref[block * blk + j], j)

    def scatter(dst_row, s, j):
        return _row_copy(ybuf.at[s], y_hbm, ssem.at[s], j, dst_row)

    @pl.when(i == 0)
    def _():
        for_rows(lambda j: gather(0, 0, j).start())
        for_rows(lambda j: gather(1, 1, j).start())
        ybuf[nbuf - 1] = jnp.zeros(ybuf.shape[1:], ybuf.dtype)
        for k in range(nbuf - 1):
            zero_fill = pltpu.make_async_copy(
                ybuf.at[nbuf - 1], y_hbm.at[pl.ds(dump0 + k * blk, blk), :], ssem.at[k])
            zero_fill.start()
            zero_fill.wait()

    for j in range(blk):
        gather(0, slot, j).wait()

    @pl.when(i >= 2)
    def _():
        for j in range(blk):
            scatter(0, slot, j).wait()

    ahead = jnp.minimum(i + 2, nb - 1)
    prev = jnp.maximum(i - 1, 0)
    per = -(-blk // DMA_ISSUE_SLICES)

    def side_work(c):
        for j in range(c * per, min((c + 1) * per, blk)):
            gather(ahead, slot_next2, j).start(priority=j % 2)
            dst = jnp.where(i == 0, dump0 + (nbuf - 1) * blk + j, rdst_ref[prev * blk + j])
            scatter(dst, slot_prev, j).start(priority=(j + 1) % 2)

    ybuf[slot] = _swiglu_block(xbuf[slot].astype(BF16), wg_ref, wu_ref, wd_ref, side_work)

    @pl.when(i == nb - 1)
    def _():
        for_rows(lambda j: gather(0, (i + 1) % nbuf, j).wait())
        for_rows(lambda j: gather(0, slot_next2, j).wait())
        for_rows(lambda j: scatter(0, (i + 1) % nbuf, j).wait())
        for_rows(lambda j: scatter(0, slot_prev, j).wait())
        for_rows(lambda j: scatter(rdst_ref[i * blk + j], slot, j).start())
        for_rows(lambda j: scatter(0, slot, j).wait())


def _moe_experts(block_expert, row_src, row_dst, hn, wg, wu, wd):
    nb = block_expert.shape[0]
    r, d = hn.shape
    d_ff = wg.shape[2]
    wmap = lambda i, be, rs, rd: (be[i], 0, 0)
    grid_spec = pltpu.PrefetchScalarGridSpec(
        num_scalar_prefetch=3,
        grid=(nb,),
        in_specs=[
            pl.BlockSpec(memory_space=pl.ANY),
            pl.BlockSpec((None, d, d_ff), wmap),
            pl.BlockSpec((None, d, d_ff), wmap),
            pl.BlockSpec((None, d_ff, d), wmap),
        ],
        out_specs=pl.BlockSpec(memory_space=pl.ANY),
        scratch_shapes=[pltpu.VMEM((MOE_BUFFERS, MOE_BLOCK, d), F32),
                        pltpu.VMEM((MOE_BUFFERS, MOE_BLOCK, d), F32),
                        pltpu.SemaphoreType.DMA((MOE_BUFFERS,)),
                        pltpu.SemaphoreType.DMA((MOE_BUFFERS,))],
    )
    return pl.pallas_call(
        _moe_kernel,
        grid_spec=grid_spec,
        out_shape=jax.ShapeDtypeStruct((TOP_K * r + MOE_BUFFERS * MOE_BLOCK, d), F32),
        compiler_params=_params(("arbitrary",)),
        name="moe_experts",
    )(block_expert, row_src, row_dst, hn, wg, wu, wd)


def _combine_kernel(h_ref, route_ref, y0_ref, y1_ref, o_ref):
    route = route_ref[...]
    o_ref[...] = h_ref[...] + route[:, 2:3] * y0_ref[...] + route[:, 3:4] * y1_ref[...]


def _moe_combine(h, route, y, *, tm):
    r, d = h.shape
    nt = r // tm
    row = lambda i: (i, 0)
    return pl.pallas_call(
        _combine_kernel,
        grid=(nt,),
        in_specs=[
            pl.BlockSpec((tm, d), row),
            pl.BlockSpec((tm, LANES), row),
            pl.BlockSpec((tm, d), row),
            pl.BlockSpec((tm, d), lambda i: (i + nt, 0)),
        ],
        out_specs=pl.BlockSpec((tm, d), row),
        out_shape=jax.ShapeDtypeStruct((r, d), F32),
        compiler_params=_params(("arbitrary",)),
        name="moe_combine",
    )(h, route, y, y)


def _slot_owner_kernel(dest_ref, owner_ref):
    n_pairs = dest_ref.shape[0]
    n_slots = owner_ref.shape[0]

    def clear(i, c):
        for u in range(DMA_UNROLL):
            owner_ref[i * DMA_UNROLL + u] = -1
        return c

    lax.fori_loop(0, n_slots // DMA_UNROLL, clear, 0)

    def place(i, c):
        for u in range(DMA_UNROLL):
            p = i * DMA_UNROLL + u
            owner_ref[dest_ref[p]] = p
        return c

    lax.fori_loop(0, n_pairs // DMA_UNROLL, place, 0)


def _slot_owner(dest, n_slots):
    assert dest.shape[0] % DMA_UNROLL == 0 and n_slots % DMA_UNROLL == 0
    smem = pl.BlockSpec(memory_space=pltpu.SMEM)
    return pl.pallas_call(
        _slot_owner_kernel,
        in_specs=[smem],
        out_specs=smem,
        out_shape=jax.ShapeDtypeStruct((n_slots,), jnp.int32),
        name="slot_owner",
    )(dest)


def _routing_tables(route):
    r = route.shape[0]
    tk = r * TOP_K
    flat_e = route[:, :TOP_K].reshape(tk).astype(jnp.int32)
    onehot = (flat_e[:, None] == jnp.arange(N_EXPERTS, dtype=jnp.int32)[None, :]).astype(jnp.int32)
    csum = jnp.cumsum(onehot, axis=0)
    rank = jnp.sum(csum * onehot, axis=1) - 1
    counts = csum[-1]
    padded = (counts + MOE_BLOCK - 1) // MOE_BLOCK * MOE_BLOCK
    pend = jnp.cumsum(padded)
    pstart = pend - padded
    dest = pstart[flat_e] + rank
    nb = -(-tk // MOE_BLOCK) + N_EXPERTS
    n_slots = nb * MOE_BLOCK
    pair = _slot_owner(dest, n_slots)
    used = pair >= 0
    row_src = jnp.where(used, pair // TOP_K, 0)
    slot_id = jnp.arange(n_slots, dtype=jnp.int32)
    dump_row = tk + (slot_id // MOE_BLOCK) % MOE_BUFFERS * MOE_BLOCK + slot_id % MOE_BLOCK
    row_dst = jnp.where(used, (pair % TOP_K) * r + pair // TOP_K, dump_row)
    block_start = jnp.arange(nb, dtype=jnp.int32) * MOE_BLOCK
    block_expert = jnp.minimum(jnp.searchsorted(pend, block_start, side='right'),
                               N_EXPERTS - 1).astype(jnp.int32)
    return block_expert, row_src, row_dst


def _pad_heads(t, axis):
    axis = axis % t.ndim
    shp = t.shape
    t = t.reshape(shp[:axis] + (RWKV_HEADS, RWKV_HEAD_DIM) + shp[axis + 1:])
    pad = [(0, 0)] * t.ndim
    pad[axis + 1] = (0, HEAD_PAD - RWKV_HEAD_DIM)
    t = jnp.pad(t, pad)
    return t.reshape(shp[:axis] + (RWKV_PAD,) + shp[axis + 1:])


def _cast_kernel(x_ref, o_ref):
    o_ref[...] = x_ref[...].astype(o_ref.dtype)


def _to_bf16(w, layer):
    _, e, a, b = w.shape
    return pl.pallas_call(
        _cast_kernel,
        grid=(e, 2),
        in_specs=[pl.BlockSpec((None, None, a // 2, b), lambda i, j: (layer, i, j, 0))],
        out_specs=pl.BlockSpec((None, a // 2, b), lambda i, j: (i, j, 0)),
        out_shape=jax.ShapeDtypeStruct((e, a, b), BF16),
        compiler_params=_params(("arbitrary", "arbitrary")),
        name="to_bf16",
    )(w)


def _rope_tables(lp):
    half = ROPE_DIM // 2
    inv_freq = ROPE_THETA ** (-jnp.arange(0, ROPE_DIM, 2, dtype=F32) / ROPE_DIM)
    ang = jnp.arange(lp, dtype=F32)[:, None] * inv_freq[None, :]
    cos, sin = jnp.cos(ang), jnp.sin(ang)
    ones = jnp.ones((lp, DIFF_QK_DIM - ROPE_DIM), F32)
    zeros = jnp.zeros((lp, DIFF_QK_DIM - ROPE_DIM), F32)
    zh = jnp.zeros((lp, half), F32)
    rc = jnp.concatenate([cos, cos, ones], axis=1)
    rs1 = jnp.concatenate([-sin, zh, zeros], axis=1)
    rs2 = jnp.concatenate([zh, sin, zeros], axis=1)
    tile2 = lambda a: jnp.concatenate([a, a], axis=1)
    return tile2(rc), tile2(rs1), tile2(rs2)


def kernel(x, meta_tokens, mix_norm_g, w_in, tm_mu, tm_w0, tm_w_decay_up, tm_a0, tm_w_a_up, tm_w_g_up, tm_k_k, tm_k_a, tm_r_k, tm_gn_g, tm_gn_b, da_q_norm_g, da_k_norm_g, da_lambda_q1, da_lambda_k1, da_lambda_q2, da_lambda_k2, da_subln_g, sc_conv_w, w_out, ffn_norm_g, ffn_w_gate, ffn_w_up, ffn_w_down, router_w, moe_w_gate, moe_w_up, moe_w_down):
    b, seq, d = x.shape
    depth = w_in.shape[0]
    l = N_META + seq
    lp = -(-l // LANES) * LANES
    tm = _row_tile(lp)
    r = b * lp
    s1 = RWKV_WIDTH

    meta = jnp.broadcast_to(meta_tokens.astype(x.dtype)[None], (b, N_META, d))
    h = jnp.concatenate([meta, x, jnp.zeros((b, lp - l, d), x.dtype)], axis=1).reshape(r, d)
    rc, rs1, rs2 = _rope_tables(lp)

    for i in range(depth):
        lam_init = 0.8 - 0.6 * math.exp(-0.3 * i)
        wi = w_in[i]
        w_cat = jnp.concatenate(
            [_pad_heads(wi[:, 0:s1], 1), _pad_heads(wi[:, s1:2 * s1], 1),
             _pad_heads(wi[:, 2 * s1:3 * s1], 1), wi[:, 3 * s1:]], axis=1).astype(BF16)
        qg = jnp.tile(da_q_norm_g[i], 2)[None]
        kg = jnp.tile(da_k_norm_g[i], 2)[None]
        pa, qs, kh, vt, pc = _inproj(h, mix_norm_g[i][None], w_cat, rc, rs1, rs2, qg, kg,
                                     tm=tm, lp=lp)

        mu = tm_mu[i]
        mu_p = jnp.concatenate([_pad_heads(mu[0:s1], 0), _pad_heads(mu[s1:2 * s1], 0),
                                _pad_heads(mu[2 * s1:3 * s1], 0), mu[3 * s1:]])[None]
        vecs = jnp.stack([_pad_heads(tm_w0[i], 0), _pad_heads(tm_a0[i], 0),
                          _pad_heads(tm_k_k[i], 0), _pad_heads(tm_k_a[i], 0),
                          _pad_heads(tm_r_k[i].reshape(s1), 0), _pad_heads(tm_gn_g[i], 0),
                          _pad_heads(tm_gn_b[i], 0), jnp.zeros((RWKV_PAD,), F32)])
        zl = lambda n: jnp.zeros((n, RWKV_PAD), F32)
        wd_p = jnp.concatenate([_pad_heads(tm_w_decay_up[i], 1), zl(AAA_LORA + GATE_LORA)], axis=0)
        wa_p = jnp.concatenate([zl(DECAY_LORA), _pad_heads(tm_w_a_up[i], 1), zl(GATE_LORA)], axis=0)
        wg_p = jnp.concatenate([zl(DECAY_LORA + AAA_LORA), _pad_heads(tm_w_g_up[i], 1)], axis=0)
        oa = _rwkv(pa.reshape(b, lp, PA_COLS), mu_p, vecs, wd_p, wa_p, wg_p, tb=tm)

        lamv = jnp.zeros((SUBLANES, LANES), F32).at[0:4, :DIFF_QK_DIM].set(
            jnp.stack([da_lambda_q1[i], da_lambda_k1[i], da_lambda_q2[i], da_lambda_k2[i]]))
        ob = _attention(qs.reshape(2, b, lp, DIFF_WIDTH), kh.reshape(b, lp, DIFF_WIDTH),
                        vt.reshape(b, lp // tm, DIFF_HEADS, VT_ROWS, tm), lamv, da_subln_g[i][None],
                        tq=tm, lam_init=lam_init)

        wo = w_out[i]
        cw = jnp.zeros((SUBLANES, CONV_WIDTH), F32).at[:CONV_K].set(sc_conv_w[i])
        mix = (h, oa.reshape(r, RWKV_PAD), ob.reshape(r, DIFF_WIDTH), pc, cw,
               _pad_heads(wo[:s1], 0).astype(BF16), wo[s1:s1 + DIFF_WIDTH].astype(BF16),
               wo[s1 + DIFF_WIDTH:].astype(BF16))
        j = i // 2
        if i % 2 == 0:
            h = _outproj(*mix, tm=tm, lp=lp,
                         ffn=(ffn_norm_g[i][None], ffn_w_gate[j].astype(BF16),
                              ffn_w_up[j].astype(BF16), ffn_w_down[j].astype(BF16)))
        else:
            h = _outproj(*mix, tm=tm, lp=lp)
            rw = jnp.zeros((d, LANES), F32).at[:, :N_EXPERTS].set(router_w[j])
            hn, route = _router(h, ffn_norm_g[i][None], rw, tm=tm)
            block_expert, row_src, row_dst = _routing_tables(route)
            y = _moe_experts(block_expert, row_src, row_dst, hn, _to_bf16(moe_w_gate, j),
                             _to_bf16(moe_w_up, j), _to_bf16(moe_w_down, j))
            h = _moe_combine(h, route, y, tm=tm)

    return h.reshape(b, lp, d)[:, N_META:l]
```
